```python
import math
import jax, jax.numpy as jnp
from jax import lax
import numpy as np

D_MODEL = 1024
BATCH = 8
SEQ = 4096
DEPTH = 4

GRID_W = 64
F32 = jnp.float32
EPS = 1e-6
HY_WIDTH = D_MODEL // 4
NA_HEAD_DIM = 64
NA_WIDTH = D_MODEL // 2
NA_HEADS = NA_WIDTH // NA_HEAD_DIM
POOL_WIDTH = D_MODEL // 4
POOL_WINDOWS = (2, 4, 8, 16)
POOL_GROUPS = len(POOL_WINDOWS)
POOL_GROUP_DIM = POOL_WIDTH // POOL_GROUPS
IN_WIDTH = 3 * HY_WIDTH + 3 * NA_WIDTH + POOL_WIDTH
SHORT_CONV = 3
FILTER_EMB = 33
FILTER_BANDS = (FILTER_EMB - 1) // 2
FILTER_HIDDEN = 64
DECAY_FAST = 0.3
DECAY_SLOW = 1.5
DECAY_TARGET = 1e-2
NA_KH_MAX = 8
NA_KW = 16
N_EXPERTS = 16
EC_CAPACITY = 2
EXPERT_FF = 2 * D_MODEL

kernel_name = 'hybrid_hyena_natten_pool_ecmoe_encoder'


def rms_norm(x, g):
    xf = x.astype(F32)
    y = xf * lax.rsqrt(jnp.mean(xf * xf, axis=-1, keepdims=True) + EPS)
    return (y * g.astype(F32)).astype(x.dtype)


def short_conv(u, w, b):
    up = jnp.pad(u, ((0, 0), (1, 1), (0, 0)))
    return up[:, :-2] * w[0] + up[:, 1:-1] * w[1] + up[:, 2:] * w[2] + b


def hyena_filter(L, w1, b1, w2, b2, w_out, freq):
    t = jnp.linspace(0.0, 1.0, L, dtype=F32)[:, None]
    ang = 2.0 * math.pi * jnp.arange(L, dtype=F32)[:, None] / L
    f = jnp.linspace(1e-4, FILTER_BANDS - 1, FILTER_BANDS, dtype=F32)[None, :]
    z = jnp.concatenate([t, jnp.cos(f * ang), -jnp.sin(f * ang)], axis=-1)
    fr = freq.astype(F32)
    h = jnp.sin(fr * (z @ w1.astype(F32) + b1.astype(F32)))
    h = jnp.sin(fr * (h @ w2.astype(F32) + b2.astype(F32)))
    h = h @ w_out.astype(F32)
    deltas = jnp.abs(jnp.linspace(math.log(DECAY_TARGET) / DECAY_FAST,
                                  math.log(DECAY_TARGET) / DECAY_SLOW, HY_WIDTH, dtype=F32))
    decay = jnp.exp(-t * deltas)
    h = h * jnp.concatenate([decay, decay], axis=-1)
    h_fwd, h_bwd = h[:, :HY_WIDTH], h[:, HY_WIDTH:]
    return jnp.concatenate([h_fwd, jnp.zeros((1, HY_WIDTH), F32), h_bwd[1:][::-1]], axis=0)


def hyena_mixer(u, sw, sb, filt, skip):
    u = short_conv(u, sw, sb)
    x0, x1, v = jnp.split(u, 3, axis=-1)
    z = (v * x1).astype(F32)
    L = z.shape[1]
    Z = jnp.fft.rfft(z, n=2 * L, axis=1)
    Hf = jnp.fft.rfft(filt, n=2 * L, axis=0)
    y = jnp.fft.irfft(Z * Hf[None], n=2 * L, axis=1)[:, :L]
    y = y + z * skip.astype(F32)
    return y.astype(u.dtype) * x0


def neighbourhood_attention(q, k, v, rpb):
    B, L, _ = q.shape
    rows = L // GRID_W
    kh = min(NA_KH_MAX, rows)
    shp = (B, rows, GRID_W, NA_HEADS, NA_HEAD_DIM)
    qg = (q * (NA_HEAD_DIM ** -0.5)).reshape(shp).transpose(1, 0, 2, 3, 4)
    kg = k.reshape(shp)
    vg = v.reshape(shp)
    cols = jnp.arange(GRID_W)
    col_start = jnp.clip(cols - NA_KW // 2, 0, GRID_W - NA_KW)
    col_idx = col_start[:, None] + jnp.arange(NA_KW)[None, :]
    dc_idx = col_idx - cols[:, None] + NA_KW - 1

    def one_row(args):
        q_row, r = args
        rs = jnp.clip(r - kh // 2, 0, rows - kh)
        k_nb = lax.dynamic_slice_in_dim(kg, rs, kh, axis=1)[:, :, col_idx]
        v_nb = lax.dynamic_slice_in_dim(vg, rs, kh, axis=1)[:, :, col_idx]
        s = jnp.einsum('bchd,bicjhd->bhcij', q_row, k_nb).astype(F32)
        dr_idx = rs + jnp.arange(kh) - r + NA_KH_MAX - 1
        bias = rpb[:, dr_idx][:, :, dc_idx].transpose(0, 2, 1, 3)
        s = s + bias.astype(F32)[None]
        p = jax.nn.softmax(s.reshape(B, NA_HEADS, GRID_W, kh * NA_KW), axis=-1)
        p = p.reshape(s.shape).astype(v_nb.dtype)
        return jnp.einsum('bhcij,bicjhd->bchd', p, v_nb)

    out = lax.map(one_row, (qg, jnp.arange(rows)))
    return out.transpose(1, 0, 2, 3, 4).reshape(B, L, NA_WIDTH)


def pool_mixer(u, pool_w, pool_scale):
    B, L, _ = u.shape
    ug = u.reshape(B, L, POOL_GROUPS, POOL_GROUP_DIM).astype(F32)
    S = jnp.concatenate([jnp.zeros((B, 1, POOL_GROUPS, POOL_GROUP_DIM), F32),
                         jnp.cumsum(ug, axis=1)], axis=1)
    t = jnp.arange(L)
    means = []
    for g, w in enumerate(POOL_WINDOWS):
        lo = jnp.clip(t - w // 2, 0, L)
        hi = jnp.clip(t + w // 2, 0, L)
        cnt = (hi - lo).astype(F32)[None, :, None]
        means.append((S[:, hi, g] - S[:, lo, g]) / cnt)
    pooled = jnp.stack(means, axis=2) - ug
    y = jnp.einsum('blgc,gcd->blgd', pooled, pool_w.astype(F32)).reshape(B, L, POOL_WIDTH)
    return (y * pool_scale.astype(F32)).astype(u.dtype)


def expert_choice_moe(h, w_router, w_gate, w_up, w_down):
    B, L, D = h.shape
    cap = EC_CAPACITY * L // N_EXPERTS
    aff = jax.nn.softmax(h.astype(F32) @ w_router.astype(F32), axis=-1)
    gate, idx = lax.top_k(aff.transpose(0, 2, 1), cap)
    xs = jax.vmap(lambda hb, ib: hb[ib])(h, idx)
    a = jnp.einsum('becd,edf->becf', xs, w_gate)
    b = jnp.einsum('becd,edf->becf', xs, w_up)
    y = jnp.einsum('becf,efd->becd', jax.nn.silu(a) * b, w_down)
    y = y * gate.astype(y.dtype)[..., None]
    return jax.vmap(lambda yb, ib: jnp.zeros((L, D), yb.dtype).at[ib.reshape(-1)].add(
        yb.reshape(-1, D)))(y, idx)


def setup_inputs(seed: int = 0) -> dict:
    key = jax.random.key(seed)
    ks = jax.random.split(key, 24)
    n = lambda k, s, sc: jax.random.normal(k, s, F32) * sc
    gain = lambda k, s: 1.0 + 0.02 * jax.random.normal(k, s, F32)
    L_ = DEPTH
    return {
        'x': jax.random.normal(ks[0], (BATCH, SEQ, D_MODEL), F32),
        'norm1_g': gain(ks[1], (L_, D_MODEL)),
        'w_in': n(ks[2], (L_, D_MODEL, IN_WIDTH), D_MODEL ** -0.5),
        'hy_short_w': n(ks[3], (L_, SHORT_CONV, 3 * HY_WIDTH), SHORT_CONV ** -0.5),
        'hy_short_b': n(ks[4], (L_, 3 * HY_WIDTH), 0.02),
        'hy_f_w1': n(ks[5], (L_, FILTER_EMB, FILTER_HIDDEN), FILTER_EMB ** -0.5),
        'hy_f_b1': n(ks[6], (L_, FILTER_HIDDEN), 0.02),
        'hy_f_w2': n(ks[7], (L_, FILTER_HIDDEN, FILTER_HIDDEN), FILTER_HIDDEN ** -0.5),
        'hy_f_b2': n(ks[8], (L_, FILTER_HIDDEN), 0.02),
        'hy_f_wout': n(ks[9], (L_, FILTER_HIDDEN, 2 * HY_WIDTH), FILTER_HIDDEN ** -0.5),
        'hy_f_freq': gain(ks[10], (L_, FILTER_HIDDEN)),
        'hy_skip': n(ks[11], (L_, HY_WIDTH), 1.0),
        'na_rpb': n(ks[12], (L_, NA_HEADS, 2 * NA_KH_MAX - 1, 2 * NA_KW - 1), 0.02),
        'pool_w': n(ks[13], (L_, POOL_GROUPS, POOL_GROUP_DIM, POOL_GROUP_DIM), POOL_GROUP_DIM ** -0.5),
        'pool_scale': 1.0 + 0.1 * jax.random.normal(ks[14], (L_, POOL_WIDTH), F32),
        'mix_norm_g': gain(ks[15], (L_, D_MODEL)),
        'w_out': n(ks[16], (L_, D_MODEL, D_MODEL), D_MODEL ** -0.5),
        'norm2_g': gain(ks[17], (L_, D_MODEL)),
        'w_router': n(ks[18], (L_, D_MODEL, N_EXPERTS), D_MODEL ** -0.5),
        'w_gate': n(ks[19], (L_, N_EXPERTS, D_MODEL, EXPERT_FF), D_MODEL ** -0.5),
        'w_up': n(ks[20], (L_, N_EXPERTS, D_MODEL, EXPERT_FF), D_MODEL ** -0.5),
        'w_down': n(ks[21], (L_, N_EXPERTS, EXPERT_FF, D_MODEL), EXPERT_FF ** -0.5),
        'final_g': gain(ks[22], (D_MODEL,)),
    }


def reference(x, norm1_g, w_in, hy_short_w, hy_short_b, hy_f_w1, hy_f_b1, hy_f_w2, hy_f_b2,
              hy_f_wout, hy_f_freq, hy_skip, na_rpb, pool_w, pool_scale, mix_norm_g, w_out,
              norm2_g, w_router, w_gate, w_up, w_down, final_g):
    L = x.shape[1]
    cuts = [3 * HY_WIDTH, 3 * HY_WIDTH + NA_WIDTH, 3 * HY_WIDTH + 2 * NA_WIDTH,
            3 * HY_WIDTH + 3 * NA_WIDTH]
    for i in range(DEPTH):
        h = rms_norm(x, norm1_g[i])
        proj = h @ w_in[i]
        hy_in, q, k, v, pool_in = jnp.split(proj, cuts, axis=-1)
        filt = hyena_filter(L, hy_f_w1[i], hy_f_b1[i], hy_f_w2[i], hy_f_b2[i],
                            hy_f_wout[i], hy_f_freq[i])
        y_hy = hyena_mixer(hy_in, hy_short_w[i], hy_short_b[i], filt, hy_skip[i])
        y_na = neighbourhood_attention(q, k, v, na_rpb[i])
        y_pool = pool_mixer(pool_in, pool_w[i], pool_scale[i])
        g = mix_norm_g[i]
        mixed = jnp.concatenate([
            rms_norm(y_hy, g[:HY_WIDTH]),
            rms_norm(y_na, g[HY_WIDTH:HY_WIDTH + NA_WIDTH]),
            rms_norm(y_pool, g[HY_WIDTH + NA_WIDTH:])], axis=-1)
        x = x + mixed @ w_out[i]
        x = x + expert_choice_moe(rms_norm(x, norm2_g[i]), w_router[i], w_gate[i],
                                  w_up[i], w_down[i])
    return rms_norm(x, final_g)
```

```python
import functools
import math

import numpy as np
import jax
import jax.numpy as jnp
from jax import lax
from jax.experimental import pallas as pl
from jax.experimental.pallas import tpu as pltpu

F32 = jnp.float32
BF16 = jnp.bfloat16
I32 = jnp.int32
EPS = 1e-6
HIGHEST = lax.Precision.HIGHEST

GRID_W = 64
NA_HEAD_DIM = 64
NA_KH_MAX = 8
NA_KW = 16
NA_GROUP_ROWS = 4
NA_KEY_ROWS = 12
POOL_WINDOWS = (2, 4, 8, 16)
POOL_PAD = 16
FILTER_EMB = 33
DECAY_FAST, DECAY_SLOW, DECAY_TARGET = 0.3, 1.5, 1e-2
EC_CAPACITY = 2
HY_B1 = 128
NEG = -1e30
LANES = 128
VMEM_LIMIT = 56 * 1024 * 1024


def _cparams(sem, vmem=VMEM_LIMIT):
    return pltpu.CompilerParams(dimension_semantics=sem, vmem_limit_bytes=vmem)


def _rms(v, g):
    return v * lax.rsqrt(jnp.mean(v * v, axis=-1, keepdims=True) + EPS) * g


def _inproj_kernel(x_ref, g_ref, w_ref, hy_ref, qkv_ref, pool_ref, *, hyw, naw):
    h = _rms(x_ref[...], g_ref[...]).astype(BF16)
    hy_ref[...] = jnp.dot(h, w_ref[:, :hyw], preferred_element_type=F32)
    qkv_ref[...] = jnp.dot(h, w_ref[:, hyw:hyw + naw], preferred_element_type=F32).astype(BF16)
    pool_ref[...] = jnp.dot(h, w_ref[:, hyw + naw:], preferred_element_type=F32)


def _inproj(x2, g, w_bf, hyw, naw, tm=512):
    n, d = x2.shape
    inw = w_bf.shape[1]
    pw = inw - hyw - naw
    return pl.pallas_call(
        functools.partial(_inproj_kernel, hyw=hyw, naw=naw),
        grid=(n // tm,),
        in_specs=[pl.BlockSpec((tm, d), lambda i: (i, 0)),
                  pl.BlockSpec((1, d), lambda i: (0, 0)),
                  pl.BlockSpec((d, inw), lambda i: (0, 0))],
        out_specs=[pl.BlockSpec((tm, hyw), lambda i: (i, 0)),
                   pl.BlockSpec((tm, naw), lambda i: (i, 0)),
                   pl.BlockSpec((tm, pw), lambda i: (i, 0))],
        out_shape=[jax.ShapeDtypeStruct((n, hyw), F32),
                   jax.ShapeDtypeStruct((n, naw), BF16),
                   jax.ShapeDtypeStruct((n, pw), F32)],
        compiler_params=_cparams(("parallel",)),
        name="inproj",
    )(x2, g, w_bf)


def _hypre_kernel(u0_ref, u1_ref, u2_ref, w0_ref, w1_ref, w2_ref, b0_ref, b1_ref, b2_ref, z_ref, x0_ref):
    seq = u0_ref.shape[1]
    row = lax.broadcasted_iota(I32, (seq, 1), 0)

    def conv(u_ref, w_ref, b_ref):
        u = u_ref[0]
        prev = jnp.where(row == 0, 0.0, pltpu.roll(u, 1, 0))
        nxt = jnp.where(row == seq - 1, 0.0, pltpu.roll(u, seq - 1, 0))
        return prev * w_ref[0:1, :] + u * w_ref[1:2, :] + nxt * w_ref[2:3, :] + b_ref[...]

    x0_ref[0] = conv(u0_ref, w0_ref, b0_ref)
    z_ref[0] = conv(u2_ref, w2_ref, b2_ref) * conv(u1_ref, w1_ref, b1_ref)


def _hypre(hy_in, sw, sb, hy):
    b, seq, _ = hy_in.shape
    nc = hy // LANES
    cb = nc
    u_spec = lambda k: pl.BlockSpec((1, seq, LANES), lambda i, j, k=k: (i, 0, k * cb + j))
    w_spec = lambda k: pl.BlockSpec((3, LANES), lambda i, j, k=k: (0, k * cb + j))
    b_spec = lambda k: pl.BlockSpec((1, LANES), lambda i, j, k=k: (0, k * cb + j))
    o_spec = pl.BlockSpec((1, seq, LANES), lambda i, j: (i, 0, j))
    return pl.pallas_call(
        _hypre_kernel,
        grid=(b, nc),
        in_specs=[u_spec(0), u_spec(1), u_spec(2), w_spec(0), w_spec(1), w_spec(2),
                  b_spec(0), b_spec(1), b_spec(2)],
        out_specs=[o_spec, o_spec],
        out_shape=[jax.ShapeDtypeStruct((b, seq, hy), F32)] * 2,
        compiler_params=_cparams(("parallel", "parallel")),
        name="hyena_pre",
    )(hy_in, hy_in, hy_in, sw, sw, sw, sb, sb, sb)


def _lmm_kernel(a_ref, x_ref, o_ref, *, cast, precision):
    x = x_ref[0]
    if cast is not None:
        x = x.astype(cast)
    o_ref[0] = jnp.dot(a_ref[...], x, preferred_element_type=F32, precision=precision).astype(o_ref.dtype)


def _lmm(a, x, tn, out_dtype, cast=None, precision=None, name="lmm"):
    g, k, n = x.shape
    m = a.shape[0]
    return pl.pallas_call(
        functools.partial(_lmm_kernel, cast=cast, precision=precision),
        grid=(g, n // tn),
        in_specs=[pl.BlockSpec((m, k), lambda i, j: (0, 0)),
                  pl.BlockSpec((1, k, tn), lambda i, j: (i, 0, j))],
        out_specs=pl.BlockSpec((1, m, tn), lambda i, j: (i, 0, j)),
        out_shape=jax.ShapeDtypeStruct((g, m, n), out_dtype),
        compiler_params=_cparams(("parallel", "parallel")),
        name=name,
    )(a, x)


def _filter_kernel(z_ref, w1_ref, b1_ref, w2_ref, b2_ref, wo_ref, fr_ref, dec_ref, o_ref):
    fr = fr_ref[...]
    h = jnp.sin(fr * (jnp.dot(z_ref[...], w1_ref[...], preferred_element_type=F32, precision=HIGHEST)
                      + b1_ref[...]))
    h = jnp.sin(fr * (jnp.dot(h, w2_ref[...], preferred_element_type=F32, precision=HIGHEST) + b2_ref[...]))
    o_ref[...] = jnp.dot(h, wo_ref[...], preferred_element_type=F32, precision=HIGHEST) * dec_ref[...]


def _filter_mlp(zemb, w1, b1, w2, b2, wo, fr, dec2, tl=512):
    seq, emb = zemb.shape
    hid = w1.shape[1]
    ow = wo.shape[1]
    full = lambda s: pl.BlockSpec(s, lambda i: (0, 0))
    return pl.pallas_call(
        _filter_kernel,
        grid=(seq // tl,),
        in_specs=[pl.BlockSpec((tl, emb), lambda i: (i, 0)), full((emb, hid)), full((1, hid)),
                  full((hid, hid)), full((1, hid)), full((hid, ow)), full((1, hid)),
                  pl.BlockSpec((tl, ow), lambda i: (i, 0))],
        out_specs=pl.BlockSpec((tl, ow), lambda i: (i, 0)),
        out_shape=jax.ShapeDtypeStruct((seq, ow), F32),
        compiler_params=_cparams(("parallel",)),
        name="hyena_filter_mlp",
    )(zemb, w1, b1, w2, b2, wo, fr, dec2)


def _hyfreq_kernel(y_ref, fb_ref, h_ref, fbi_ref, w_ref, *, nka):
    ka = pl.program_id(1)
    nb = fb_ref.shape[0] // 2

    @pl.when(ka < nka)
    def _():
        p = jnp.dot(fb_ref[...], y_ref[0, 0], preferred_element_type=F32)
        pr, pi = p[:nb], p[nb:]
        hr, hi = h_ref[0, :nb], h_ref[0, nb:]
        q = jnp.concatenate([pr * hr - pi * hi, pr * hi + pi * hr], axis=0).astype(BF16)
        w_ref[0, 0] = jnp.dot(fbi_ref[...], q, preferred_element_type=F32).astype(w_ref.dtype)

    @pl.when(ka >= nka)
    def _():
        w_ref[0, 0] = jnp.zeros(w_ref.shape[2:], w_ref.dtype)


def _hyfreq(y4, fb, hhat, fbi, nka):
    b, kap, r2, hy = y4.shape
    nb2 = fb.shape[0]
    return pl.pallas_call(
        functools.partial(_hyfreq_kernel, nka=nka),
        grid=(b, kap),
        in_specs=[pl.BlockSpec((1, 1, r2, hy), lambda i, k: (i, k, 0, 0)),
                  pl.BlockSpec((nb2, r2), lambda i, k: (0, 0)),
                  pl.BlockSpec((1, nb2, hy), lambda i, k: (k, 0, 0)),
                  pl.BlockSpec((r2, nb2), lambda i, k: (0, 0))],
        out_specs=pl.BlockSpec((1, 1, r2, hy), lambda i, k: (i, k, 0, 0)),
        out_shape=jax.ShapeDtypeStruct((b, kap, r2, hy), BF16),
        compiler_params=_cparams(("parallel", "parallel")),
        name="hyena_freq",
    )(y4, fb, hhat, fbi)


def _hyout_kernel(a_ref, w_ref, z_ref, x0_ref, sk_ref, o_ref):
    y = jnp.dot(a_ref[...], w_ref[0], preferred_element_type=F32)
    o_ref[0] = (y + z_ref[0] * sk_ref[...]) * x0_ref[0]


def _hyout(ainv, w3, z3, x03, skip_t, tn):
    b, a1, n = z3.shape
    k2 = w3.shape[1]
    blk = pl.BlockSpec((1, a1, tn), lambda i, j: (i, 0, j))
    return pl.pallas_call(
        _hyout_kernel,
        grid=(b, n // tn),
        in_specs=[pl.BlockSpec((a1, k2), lambda i, j: (0, 0)),
                  pl.BlockSpec((1, k2, tn), lambda i, j: (i, 0, j)),
                  blk, blk,
                  pl.BlockSpec((1, tn), lambda i, j: (0, 0))],
        out_specs=blk,
        out_shape=jax.ShapeDtypeStruct((b, a1, n), F32),
        compiler_params=_cparams(("parallel", "parallel")),
        name="hyena_out",
    )(ainv, w3, z3, x03, skip_t)


def _dft_tables(a1):
    a2, b2, b1 = 2 * a1, 2 * HY_B1, HY_B1
    nka = a1 + 1
    kap = -(-nka // 8) * 8
    ka = np.arange(nka)[:, None]
    def fa(na):
        ph = 2 * np.pi * ((ka * np.arange(na)[None, :]) % a2) / a2
        m = np.zeros((2 * kap, na))
        m[0:2 * nka:2] = np.cos(ph)
        m[1:2 * nka:2] = -np.sin(ph)
        return m
    kb = np.arange(b2)[:, None]
    th = 2 * np.pi * ((kb * np.arange(b2)[None, :]) % b2) / b2
    c, s = np.cos(th), np.sin(th)
    fb_full = np.block([[c, s], [-s, c]])
    fb_half = np.block([[c[:, :b1], s[:, :b1]], [-s[:, :b1], c[:, :b1]]])
    ct, st = c.T[:b1], s.T[:b1]
    fbi = np.block([[ct, -st], [st, ct]]) / b2
    ph = 2 * np.pi * ((np.arange(a1)[:, None] * np.arange(nka)[None, :]) % a2) / a2
    wgt = np.where((np.arange(nka) == 0) | (np.arange(nka) == a1), 1.0, 2.0)[None, :] / a2
    ainv = np.zeros((a1, 2 * kap))
    ainv[:, 0:2 * nka:2] = wgt * np.cos(ph)
    ainv[:, 1:2 * nka:2] = -wgt * np.sin(ph)
    f32 = lambda v: np.asarray(v, np.float32)
    return dict(nka=nka, kap=kap, fa_data=f32(fa(a1)), fa_filt=f32(fa(a2)), fb_full=f32(fb_full),
                fb_half=f32(fb_half), fbi=f32(fbi), ainv=f32(ainv))


def _hyena_filter_spectrum(seq, hy, w1, b1, w2, b2, wo, fr, tabs):
    nbands = (FILTER_EMB - 1) // 2
    t = jnp.linspace(0.0, 1.0, seq, dtype=F32)[:, None]
    ang = 2.0 * math.pi * jnp.arange(seq, dtype=F32)[:, None] / seq
    f = jnp.linspace(1e-4, nbands - 1, nbands, dtype=F32)[None, :]
    zemb = jnp.concatenate([t, jnp.cos(f * ang), -jnp.sin(f * ang)], axis=-1)
    deltas = jnp.abs(jnp.linspace(math.log(DECAY_TARGET) / DECAY_FAST,
                                  math.log(DECAY_TARGET) / DECAY_SLOW, hy, dtype=F32))
    decay = jnp.exp(-t * deltas)
    dec2 = jnp.concatenate([decay, decay], axis=-1)
    h = _filter_mlp(zemb, w1, b1[None], w2, b2[None], wo, fr[None], dec2, tl=min(512, seq))
    h_fwd, h_bwd = h[:, :hy], h[:, hy:]
    filt = jnp.concatenate([h_fwd, jnp.zeros((1, hy), F32), h_bwd[1:][::-1]], axis=0)
    a2 = 2 * seq // HY_B1
    filt3 = filt.reshape(a2, HY_B1, hy)
    h2 = jnp.concatenate([filt3, jnp.roll(filt3, 1, axis=0)], axis=1)
    kap = tabs["kap"]
    ya = _lmm(jnp.asarray(tabs["fa_filt"]), h2.reshape(1, a2, 2 * HY_B1 * hy), tn=2048, out_dtype=F32,
              precision=HIGHEST, name="hyena_filter_adft")
    ya = ya.reshape(kap, 4 * HY_B1, hy)
    return _lmm(jnp.asarray(tabs["fb_full"]), ya, tn=hy, out_dtype=F32, precision=HIGHEST,
                name="hyena_filter_bdft")


def _hyena(hy_in, sw, sb, skip, hhat, tabs):
    b, seq, hy3 = hy_in.shape
    hy = hy3 // 3
    a1 = seq // HY_B1
    kap, nka = tabs["kap"], tabs["nka"]
    z, x0 = _hypre(hy_in, sw, sb[None], hy)
    n = HY_B1 * hy
    z3 = z.reshape(b, a1, n)
    tn = 2048
    ya = _lmm(jnp.asarray(tabs["fa_data"], BF16), z3, tn=tn, out_dtype=BF16, cast=BF16, name="hyena_adft")
    y4 = ya.reshape(b, kap, 2 * HY_B1, hy)
    w4 = _hyfreq(y4, jnp.asarray(tabs["fb_half"], BF16), hhat, jnp.asarray(tabs["fbi"], BF16), nka)
    w3 = w4.reshape(b, 2 * kap, n)
    skip_t = jnp.tile(skip, tn // hy)[None]
    y = _hyout(jnp.asarray(tabs["ainv"], BF16), w3, z3, x0.reshape(b, a1, n), skip_t, tn)
    return y.reshape(b, seq, hy)


def _na_tables(rpb, rows):
    heads = rpb.shape[0]
    gr, kr_n, w, kh, kw = NA_GROUP_ROWS, NA_KEY_ROWS, GRID_W, NA_KH_MAX, NA_KW
    n_g = rows // gr
    qc = np.arange(w)
    cs = np.clip(qc - kw // 2, 0, w - kw)
    kc = np.arange(w)
    colvalid = (kc[None, :] >= cs[:, None]) & (kc[None, :] < cs[:, None] + kw)
    dc = np.clip(kc[None, :] - qc[:, None] + kw - 1, 0, 2 * kw - 2)
    out = []
    for g in (0, 1, n_g - 1):
        ks = min(max(gr * g - kh // 2, 0), rows - kr_n)
        r = gr * g + np.arange(gr)
        rs = np.clip(r - kh // 2, 0, rows - kh)
        kr = ks + np.arange(kr_n)
        rowvalid = (kr[None, :] >= rs[:, None]) & (kr[None, :] < rs[:, None] + kh)
        dr = np.clip(kr[None, :] - r[:, None] + NA_KH_MAX - 1, 0, 2 * NA_KH_MAX - 2)
        bias = rpb[:, dr][..., dc]
        bias = bias.transpose(0, 1, 3, 2, 4)
        valid = rowvalid[:, None, :, None] & colvalid[None, :, None, :]
        out.append(jnp.where(valid[None], bias.astype(F32), NEG).reshape(heads, gr * w, kr_n * w))
    return jnp.stack(out)


def _natten_kernel(q_ref, k_ref, v_ref, t_ref, o_ref, *, heads, n_g, rows):
    g = pl.program_id(1)
    tq = q_ref.shape[1]
    tk = t_ref.shape[3]
    ks = jnp.clip(NA_GROUP_ROWS * g - NA_KH_MAX // 2, 0, rows - NA_KEY_ROWS)
    kstart = pl.multiple_of(ks * GRID_W, GRID_W)
    outs = []
    for h in range(heads):
        lo = h * NA_HEAD_DIM
        q = q_ref[0, :, lo:lo + NA_HEAD_DIM] * (NA_HEAD_DIM ** -0.5)
        k = k_ref[0, pl.ds(kstart, tk), lo:lo + NA_HEAD_DIM]
        v = v_ref[0, pl.ds(kstart, tk), lo:lo + NA_HEAD_DIM]
        s = lax.dot_general(q.astype(BF16), k, (((1,), (1,)), ((), ())), preferred_element_type=F32)
        s = s + t_ref[0, h]
        m = jnp.max(s, axis=-1, keepdims=True)
        p = jnp.exp(s - m)
        l = jnp.sum(p, axis=-1, keepdims=True)
        o = jnp.dot(p.astype(BF16), v, preferred_element_type=F32)
        outs.append(o / l)
    o_ref[0] = jnp.concatenate(outs, axis=-1).astype(o_ref.dtype)


def _natten(qkv, tables, b, seq, naw):
    heads = naw // NA_HEAD_DIM
    rows = seq // GRID_W
    n_g = rows // NA_GROUP_ROWS
    tq = NA_GROUP_ROWS * GRID_W
    tk = NA_KEY_ROWS * GRID_W

    def tmap(i, g):
        return (jnp.where(g == 0, 0, jnp.where(g == n_g - 1, 2, 1)), 0, 0, 0)

    return pl.pallas_call(
        functools.partial(_natten_kernel, heads=heads, n_g=n_g, rows=rows),
        grid=(b, n_g),
        in_specs=[pl.BlockSpec((1, tq, naw), lambda i, g: (i, g, 0)),
                  pl.BlockSpec((1, seq, naw), lambda i, g: (i, 0, 1)),
                  pl.BlockSpec((1, seq, naw), lambda i, g: (i, 0, 2)),
                  pl.BlockSpec((1, heads, tq, tk), tmap)],
        out_specs=pl.BlockSpec((1, tq, naw), lambda i, g: (i, g, 0)),
        out_shape=jax.ShapeDtypeStruct((b, seq, naw), F32),
        compiler_params=_cparams(("parallel", "arbitrary")),
        name="natten",
    )(qkv, qkv, qkv, tables)


def _pool_kernel(u_ref, w_ref, sc_ref, o_ref, pad_ref, *, pw):
    seq = u_ref.shape[1]
    lp = seq + 2 * POOL_PAD
    j = pl.program_id(1)
    u = u_ref[0]
    zeros = jnp.zeros((POOL_PAD, LANES), F32)
    pad_ref[0:POOL_PAD, :] = zeros
    pad_ref[POOL_PAD + seq:lp, :] = zeros
    pad_ref[POOL_PAD:POOL_PAD + seq, :] = u
    xp = pad_ref[...]
    dn = lambda a, k: pltpu.roll(a, k, 0)
    up = lambda a, k: pltpu.roll(a, lp - k, 0)
    s2 = xp + dn(xp, 1)
    s4 = dn(s2, 1) + up(s2, 1)
    s8 = dn(s4, 2) + up(s4, 2)
    s16 = dn(s8, 4) + up(s8, 4)
    t = lax.broadcasted_iota(I32, (seq, 1), 0)
    lane = lax.broadcasted_iota(I32, (1, LANES), 1) + j * LANES
    gdim = pw // len(POOL_WINDOWS)
    grp = lane // gdim
    sums = (s2, s4, s8, s16)
    pooled = jnp.zeros((seq, LANES), F32)
    for gi, w in enumerate(POOL_WINDOWS):
        cnt = (jnp.minimum(t + w // 2, seq) - jnp.maximum(t - w // 2, 0)).astype(F32)
        mean = sums[gi][POOL_PAD:POOL_PAD + seq] / cnt
        pooled = jnp.where(grp == gi, mean, pooled)
    pooled = pooled - u
    y = jnp.dot(pooled.astype(BF16), w_ref[0], preferred_element_type=F32)
    o_ref[0] = y * sc_ref[...]


def _pool(u, pool_w, pool_scale):
    b, seq, pw = u.shape
    ng, gd, _ = pool_w.shape
    nh = pw // LANES
    per = LANES // gd
    wbd = jnp.zeros((nh, LANES, LANES), F32)
    for gi in range(ng):
        hh, k = divmod(gi, per)
        wbd = wbd.at[hh, k * gd:(k + 1) * gd, k * gd:(k + 1) * gd].set(pool_w[gi])
    return pl.pallas_call(
        functools.partial(_pool_kernel, pw=pw),
        grid=(b, nh),
        in_specs=[pl.BlockSpec((1, seq, LANES), lambda i, j: (i, 0, j)),
                  pl.BlockSpec((1, LANES, LANES), lambda i, j: (j, 0, 0)),
                  pl.BlockSpec((1, LANES), lambda i, j: (0, j))],
        out_specs=pl.BlockSpec((1, seq, LANES), lambda i, j: (i, 0, j)),
        out_shape=jax.ShapeDtypeStruct((b, seq, pw), F32),
        scratch_shapes=[pltpu.VMEM((seq + 2 * POOL_PAD, LANES), F32)],
        compiler_params=_cparams(("parallel", "parallel")),
        name="pool_mixer",
    )(u, wbd.astype(BF16), pool_scale[None])


def _outproj_kernel(x_ref, yh_ref, yn_ref, yp_ref, gm_ref, w_ref, g2_ref, wr_ref,
                    xo_ref, h_ref, aff_ref, *, hy, naw):
    gm = gm_ref[...]
    m1 = _rms(yh_ref[...], gm[:, :hy]).astype(BF16)
    m2 = _rms(yn_ref[...], gm[:, hy:hy + naw]).astype(BF16)
    m3 = _rms(yp_ref[...], gm[:, hy + naw:]).astype(BF16)
    acc = jnp.dot(m1, w_ref[:hy, :], preferred_element_type=F32)
    acc += jnp.dot(m2, w_ref[hy:hy + naw, :], preferred_element_type=F32)
    acc += jnp.dot(m3, w_ref[hy + naw:, :], preferred_element_type=F32)
    xn = x_ref[...] + acc
    xo_ref[...] = xn
    h = _rms(xn, g2_ref[...])
    h_ref[...] = h
    logits = lax.dot_general(wr_ref[...], h, (((1,), (1,)), ((), ())), preferred_element_type=F32,
                             precision=HIGHEST)
    mx = jnp.max(logits, axis=0, keepdims=True)
    ex = jnp.exp(logits - mx)
    aff_ref[...] = ex / jnp.sum(ex, axis=0, keepdims=True)


def _outproj(x2, yh, yn, yp, gm, w_bf, g2, wr_t, tm=512):
    n, d = x2.shape
    hy, naw, pw = yh.shape[1], yn.shape[1], yp.shape[1]
    e = wr_t.shape[0]
    row = lambda c: pl.BlockSpec((tm, c), lambda i: (i, 0))
    full = lambda s: pl.BlockSpec(s, lambda i: (0, 0))
    return pl.pallas_call(
        functools.partial(_outproj_kernel, hy=hy, naw=naw),
        grid=(n // tm,),
        in_specs=[row(d), row(hy), row(naw), row(pw), full((1, d)), full((d, d)), full((1, d)), full((e, d))],
        out_specs=[row(d), row(d), pl.BlockSpec((e, tm), lambda i: (0, i))],
        out_shape=[jax.ShapeDtypeStruct((n, d), F32), jax.ShapeDtypeStruct((n, d), F32),
                   jax.ShapeDtypeStruct((e, n), F32)],
        compiler_params=_cparams(("parallel",)),
        name="outproj_router",
    )(x2, yh, yn, yp, gm, w_bf, g2, wr_t)


def _cumsum_lanes(x, tri):
    r, n = x.shape
    outs = []
    off = jnp.zeros((r, 1), F32)
    for j in range(n // LANES):
        c = jnp.dot(x[:, j * LANES:(j + 1) * LANES], tri, preferred_element_type=F32) + off
        outs.append(c)
        off = c[:, LANES - 1:LANES]
    return jnp.concatenate(outs, axis=1)


def _route_kernel(aff_ref, tri_ref, idx_ref, gate_ref, key_ref, *, cap, chunk):
    aff = aff_ref[...]
    e, seq = aff.shape
    bits = pltpu.bitcast(aff, I32)
    capf = jnp.float32(cap)

    def radix(i, prefix):
        cand = prefix | jnp.left_shift(jnp.int32(1), 30 - i)
        cnt = jnp.sum((bits >= cand).astype(F32), axis=1, keepdims=True)
        return jnp.where(cnt >= capf, cand, prefix)

    tau = lax.fori_loop(0, 31, radix, jnp.zeros((e, 1), I32))
    gt = bits > tau
    eq = bits == tau
    need = capf - jnp.sum(gt.astype(F32), axis=1, keepdims=True)
    tri = tri_ref[...]
    tie_rank = _cumsum_lanes(jnp.where(eq, 1.0, 0.0).astype(BF16), tri)
    sel = gt | (eq & (tie_rank <= need))
    rank = _cumsum_lanes(jnp.where(sel, 1.0, 0.0).astype(BF16), tri)
    key_ref[...] = jnp.where(sel, rank - 1.0, -1.0)
    tpos = lax.broadcasted_iota(I32, (1, seq), 1).astype(F32)
    for ei in range(e):
        key_row = key_ref[ei:ei + 1, :]
        aff_row = aff_ref[ei:ei + 1, :]

        def chunk_body(c, carry, key_row=key_row, aff_row=aff_row, ei=ei):
            base = pl.multiple_of(c * chunk, chunk)
            slot = (lax.broadcasted_iota(I32, (chunk, 1), 0) + base).astype(F32)
            hit = key_row == slot
            idx = jnp.sum(jnp.where(hit, tpos, 0.0), axis=1, keepdims=True)
            gate = jnp.sum(jnp.where(hit, aff_row, 0.0), axis=1, keepdims=True)
            idx_ref[0, pl.ds(base, chunk), ei:ei + 1] = idx.astype(I32)
            gate_ref[0, pl.ds(base, chunk), ei:ei + 1] = gate
            return carry

        lax.fori_loop(0, cap // chunk, chunk_body, 0)


def _route(aff_t, b, seq, cap):
    e = aff_t.shape[0]
    tri = jnp.asarray(np.triu(np.ones((LANES, LANES), np.float32)), BF16)
    chunk = min(64, cap)
    idx, gate = pl.pallas_call(
        functools.partial(_route_kernel, cap=cap, chunk=chunk),
        grid=(b,),
        in_specs=[pl.BlockSpec((e, seq), lambda i: (0, i)),
                  pl.BlockSpec((LANES, LANES), lambda i: (0, 0))],
        out_specs=[pl.BlockSpec((1, cap, e), lambda i: (i, 0, 0)),
                   pl.BlockSpec((1, cap, e), lambda i: (i, 0, 0))],
        out_shape=[jax.ShapeDtypeStruct((b, cap, e), I32), jax.ShapeDtypeStruct((b, cap, e), F32)],
        scratch_shapes=[pltpu.VMEM((e, seq), F32)],
        compiler_params=_cparams(("parallel",)),
        name="ec_route",
    )(aff_t, tri)
    return idx.transpose(0, 2, 1), gate.transpose(0, 2, 1)


def _expert_kernel(rows_ref, h_hbm, wg_ref, wu_ref, wd_ref, y_ref, xbuf, xb, acc, sem, *, tm, nm, nf):
    e = pl.program_id(0)
    m = pl.program_id(1)
    f = pl.program_id(2)

    def row_copy(src_row, dst_row):
        return pltpu.make_async_copy(h_hbm.at[pl.ds(src_row, 1), :], xbuf.at[pl.ds(dst_row, 1), :], sem)

    @pl.when(f == 0)
    def _():
        base = (e * nm + m) * tm

        def issue(i, c):
            row_copy(rows_ref[base + i], i).start()
            return c

        lax.fori_loop(0, tm, issue, 0)

        def drain(i, c):
            row_copy(0, i).wait()
            return c

        lax.fori_loop(0, tm, drain, 0)
        xb[...] = xbuf[...].astype(BF16)
        acc[...] = jnp.zeros(acc.shape, F32)

    x = xb[...]
    a = jnp.dot(x, wg_ref[0].astype(BF16), preferred_element_type=F32)
    u = jnp.dot(x, wu_ref[0].astype(BF16), preferred_element_type=F32)
    hh = (a * jax.nn.sigmoid(a) * u).astype(BF16)
    acc[...] += jnp.dot(hh, wd_ref[0].astype(BF16), preferred_element_type=F32)

    @pl.when(f == nf - 1)
    def _():
        y_ref[0] = acc[...]


def _experts(rows_flat, h2, w_gate, w_up, w_down, mtot, tm=1024, tf=512):
    e, d, ff = w_gate.shape
    tm = min(tm, mtot)
    tf = min(tf, ff)
    nm, nf = mtot // tm, ff // tf
    grid_spec = pltpu.PrefetchScalarGridSpec(
        num_scalar_prefetch=1,
        grid=(e, nm, nf),
        in_specs=[pl.BlockSpec(memory_space=pl.ANY),
                  pl.BlockSpec((1, d, tf), lambda i, m, f, r: (i, 0, f)),
                  pl.BlockSpec((1, d, tf), lambda i, m, f, r: (i, 0, f)),
                  pl.BlockSpec((1, tf, d), lambda i, m, f, r: (i, f, 0))],
        out_specs=pl.BlockSpec((1, tm, d), lambda i, m, f, r: (i, m, 0)),
        scratch_shapes=[pltpu.VMEM((tm, d), F32), pltpu.VMEM((tm, d), BF16), pltpu.VMEM((tm, d), F32),
                        pltpu.SemaphoreType.DMA(())],
    )
    return pl.pallas_call(
        functools.partial(_expert_kernel, tm=tm, nm=nm, nf=nf),
        grid_spec=grid_spec,
        out_shape=jax.ShapeDtypeStruct((e, mtot, d), F32),
        compiler_params=_cparams(("arbitrary", "arbitrary", "arbitrary")),
        name="ec_experts",
    )(rows_flat, h2, w_gate, w_up, w_down)


def _combine_kernel(idx_ref, gate_ref, x_hbm, y_ref, o_ref, sem, *, ne, cap):
    b = pl.program_id(0)
    e = pl.program_id(1)

    @pl.when(e == 0)
    def _():
        cp = pltpu.make_async_copy(x_hbm.at[b], o_ref.at[0], sem)
        cp.start()
        cp.wait()

    base = (b * ne + e) * cap

    def body(i, c):
        t = idx_ref[base + i]
        g = gate_ref[base + i]
        o_ref[0, pl.ds(t, 1), :] = o_ref[0, pl.ds(t, 1), :] + g * y_ref[0, pl.ds(i, 1), :]
        return c

    lax.fori_loop(0, cap, body, 0, unroll=8)


def _combine(idx_flat, gate_flat, x3, y, cap):
    b, seq, d = x3.shape
    ne = y.shape[0]
    grid_spec = pltpu.PrefetchScalarGridSpec(
        num_scalar_prefetch=2,
        grid=(b, ne),
        in_specs=[pl.BlockSpec(memory_space=pl.ANY),
                  pl.BlockSpec((1, cap, d), lambda i, e, ix, gt: (e, i, 0))],
        out_specs=pl.BlockSpec((1, seq, d), lambda i, e, ix, gt: (i, 0, 0)),
        scratch_shapes=[pltpu.SemaphoreType.DMA(())],
    )
    return pl.pallas_call(
        functools.partial(_combine_kernel, ne=ne, cap=cap),
        grid_spec=grid_spec,
        out_shape=jax.ShapeDtypeStruct((b, seq, d), F32),
        compiler_params=_cparams(("arbitrary", "arbitrary")),
        name="ec_combine",
    )(idx_flat, gate_flat, x3, y)


def _moe(x3, h2, aff_t, w_gate, w_up, w_down):
    b, seq, d = x3.shape
    ne = w_gate.shape[0]
    cap = EC_CAPACITY * seq // ne
    idx, gate = _route(aff_t, b, seq, cap)
    rows = idx + (jnp.arange(b, dtype=I32) * seq)[:, None, None]
    rows_flat = rows.transpose(1, 0, 2).reshape(-1)
    y = _experts(rows_flat, h2, w_gate, w_up, w_down, b * cap)
    return _combine(idx.reshape(-1), gate.reshape(-1), x3, y, cap)


def _final_kernel(x_ref, g_ref, o_ref):
    o_ref[...] = _rms(x_ref[...], g_ref[...])


def _final_norm(x2, g, tm=1024):
    n, d = x2.shape
    return pl.pallas_call(
        _final_kernel,
        grid=(n // tm,),
        in_specs=[pl.BlockSpec((tm, d), lambda i: (i, 0)), pl.BlockSpec((1, d), lambda i: (0, 0))],
        out_specs=pl.BlockSpec((tm, d), lambda i: (i, 0)),
        out_shape=jax.ShapeDtypeStruct((n, d), F32),
        compiler_params=_cparams(("parallel",)),
        name="final_norm",
    )(x2, g)


def kernel(x, norm1_g, w_in, hy_short_w, hy_short_b, hy_f_w1, hy_f_b1, hy_f_w2, hy_f_b2, hy_f_wout, hy_f_freq, hy_skip, na_rpb, pool_w, pool_scale, mix_norm_g, w_out, norm2_g, w_router, w_gate, w_up, w_down, final_g):
    b, seq, d = x.shape
    depth = w_in.shape[0]
    hy = hy_skip.shape[1]
    pw = pool_scale.shape[1]
    naw = d - hy - pw
    n = b * seq
    rows = seq // GRID_W
    tabs = _dft_tables(seq // HY_B1)
    x2 = x.reshape(n, d)
    for i in range(depth):
        hy_in, qkv, pool_in = _inproj(x2, norm1_g[i][None], w_in[i].astype(BF16), 3 * hy, 3 * naw)
        hhat = _hyena_filter_spectrum(seq, hy, hy_f_w1[i], hy_f_b1[i], hy_f_w2[i], hy_f_b2[i],
                                      hy_f_wout[i], hy_f_freq[i], tabs)
        y_hy = _hyena(hy_in.reshape(b, seq, 3 * hy), hy_short_w[i], hy_short_b[i], hy_skip[i], hhat, tabs)
        y_na = _natten(qkv.reshape(b, seq, 3 * naw), _na_tables(na_rpb[i], rows), b, seq, naw)
        y_pool = _pool(pool_in.reshape(b, seq, pw), pool_w[i], pool_scale[i])
        x2, h2, aff_t = _outproj(x2, y_hy.reshape(n, hy), y_na.reshape(n, naw), y_pool.reshape(n, pw),
                                 mix_norm_g[i][None], w_out[i].astype(BF16), norm2_g[i][None],
                                 w_router[i].T)
        x2 = _moe(x2.reshape(b, seq, d), h2, aff_t, w_gate[i], w_up[i], w_down[i]).reshape(n, d)
    return _final_norm(x2, final_g[None]).reshape(b, seq, d)
```

```python
import functools
import math

import numpy as np
import jax
import jax.numpy as jnp
from jax import lax
from jax.experimental import pallas as pl
from jax.experimental.pallas import tpu as pltpu

F32 = jnp.float32
BF16 = jnp.bfloat16
I32 = jnp.int32
EPS = 1e-6
HIGHEST = lax.Precision.HIGHEST

GRID_W = 64
NA_HEAD_DIM = 64
NA_KH_MAX = 8
NA_KW = 16
NA_GROUP_ROWS = 4
NA_KEY_ROWS = 12
POOL_WINDOWS = (2, 4, 8, 16)
POOL_PAD = 16
FILTER_EMB = 33
DECAY_FAST, DECAY_SLOW, DECAY_TARGET = 0.3, 1.5, 1e-2
EC_CAPACITY = 2
HY_B1 = 128
NEG = -1e30
LANES = 128
VMEM_LIMIT = 56 * 1024 * 1024


def _cparams(sem, vmem=VMEM_LIMIT):
    return pltpu.CompilerParams(dimension_semantics=sem, vmem_limit_bytes=vmem)


def _rms(v, g):
    return v * lax.rsqrt(jnp.mean(v * v, axis=-1, keepdims=True) + EPS) * g


def _inproj_kernel(x_ref, g_ref, w_ref, hy_ref, qkv_ref, pool_ref, *, hyw, naw):
    h = _rms(x_ref[...], g_ref[...]).astype(BF16)
    hy_ref[...] = jnp.dot(h, w_ref[:, :hyw], preferred_element_type=F32)
    qkv_ref[...] = jnp.dot(h, w_ref[:, hyw:hyw + naw], preferred_element_type=F32).astype(BF16)
    pool_ref[...] = jnp.dot(h, w_ref[:, hyw + naw:], preferred_element_type=F32)


def _inproj(x2, g, w_bf, hyw, naw, tm=512):
    n, d = x2.shape
    inw = w_bf.shape[1]
    pw = inw - hyw - naw
    return pl.pallas_call(
        functools.partial(_inproj_kernel, hyw=hyw, naw=naw),
        grid=(n // tm,),
        in_specs=[pl.BlockSpec((tm, d), lambda i: (i, 0)),
                  pl.BlockSpec((1, d), lambda i: (0, 0)),
                  pl.BlockSpec((d, inw), lambda i: (0, 0))],
        out_specs=[pl.BlockSpec((tm, hyw), lambda i: (i, 0)),
                   pl.BlockSpec((tm, naw), lambda i: (i, 0)),
                   pl.BlockSpec((tm, pw), lambda i: (i, 0))],
        out_shape=[jax.ShapeDtypeStruct((n, hyw), F32),
                   jax.ShapeDtypeStruct((n, naw), BF16),
                   jax.ShapeDtypeStruct((n, pw), F32)],
        compiler_params=_cparams(("parallel",)),
        name="inproj",
    )(x2, g, w_bf)


def _hypre_kernel(u0_ref, u1_ref, u2_ref, w0_ref, w1_ref, w2_ref, b0_ref, b1_ref, b2_ref, z_ref, x0_ref):
    seq = u0_ref.shape[1]
    row = lax.broadcasted_iota(I32, (seq, 1), 0)

    def conv(u_ref, w_ref, b_ref):
        u = u_ref[0]
        prev = jnp.where(row == 0, 0.0, pltpu.roll(u, 1, 0))
        nxt = jnp.where(row == seq - 1, 0.0, pltpu.roll(u, seq - 1, 0))
        return prev * w_ref[0:1, :] + u * w_ref[1:2, :] + nxt * w_ref[2:3, :] + b_ref[...]

    x0_ref[0] = conv(u0_ref, w0_ref, b0_ref)
    z_ref[0] = conv(u2_ref, w2_ref, b2_ref) * conv(u1_ref, w1_ref, b1_ref)


def _hypre(hy_in, sw, sb, hy):
    b, seq, _ = hy_in.shape
    nc = hy // LANES
    cb = nc
    u_spec = lambda k: pl.BlockSpec((1, seq, LANES), lambda i, j, k=k: (i, 0, k * cb + j))
    w_spec = lambda k: pl.BlockSpec((3, LANES), lambda i, j, k=k: (0, k * cb + j))
    b_spec = lambda k: pl.BlockSpec((1, LANES), lambda i, j, k=k: (0, k * cb + j))
    o_spec = pl.BlockSpec((1, seq, LANES), lambda i, j: (i, 0, j))
    return pl.pallas_call(
        _hypre_kernel,
        grid=(b, nc),
        in_specs=[u_spec(0), u_spec(1), u_spec(2), w_spec(0), w_spec(1), w_spec(2),
                  b_spec(0), b_spec(1), b_spec(2)],
        out_specs=[o_spec, o_spec],
        out_shape=[jax.ShapeDtypeStruct((b, seq, hy), F32)] * 2,
        compiler_params=_cparams(("parallel", "parallel")),
        name="hyena_pre",
    )(hy_in, hy_in, hy_in, sw, sw, sw, sb, sb, sb)


def _lmm_kernel(a_ref, x_ref, o_ref, *, cast, precision):
    x = x_ref[0]
    if cast is not None:
        x = x.astype(cast)
    o_ref[0] = jnp.dot(a_ref[...], x, preferred_element_type=F32, precision=precision).astype(o_ref.dtype)


def _lmm(a, x, tn, out_dtype, cast=None, precision=None, name="lmm"):
    g, k, n = x.shape
    m = a.shape[0]
    return pl.pallas_call(
        functools.partial(_lmm_kernel, cast=cast, precision=precision),
        grid=(g, n // tn),
        in_specs=[pl.BlockSpec((m, k), lambda i, j: (0, 0)),
                  pl.BlockSpec((1, k, tn), lambda i, j: (i, 0, j))],
        out_specs=pl.BlockSpec((1, m, tn), lambda i, j: (i, 0, j)),
        out_shape=jax.ShapeDtypeStruct((g, m, n), out_dtype),
        compiler_params=_cparams(("parallel", "parallel")),
        name=name,
    )(a, x)


def _filter_kernel(z_ref, w1_ref, b1_ref, w2_ref, b2_ref, wo_ref, fr_ref, dec_ref, o_ref):
    fr = fr_ref[...]
    h = jnp.sin(fr * (jnp.dot(z_ref[...], w1_ref[...], preferred_element_type=F32, precision=HIGHEST)
                      + b1_ref[...]))
    h = jnp.sin(fr * (jnp.dot(h, w2_ref[...], preferred_element_type=F32, precision=HIGHEST) + b2_ref[...]))
    o_ref[...] = jnp.dot(h, wo_ref[...], preferred_element_type=F32, precision=HIGHEST) * dec_ref[...]


def _filter_mlp(zemb, w1, b1, w2, b2, wo, fr, dec2, tl=512):
    seq, emb = zemb.shape
    hid = w1.shape[1]
    ow = wo.shape[1]
    full = lambda s: pl.BlockSpec(s, lambda i: (0, 0))
    return pl.pallas_call(
        _filter_kernel,
        grid=(seq // tl,),
        in_specs=[pl.BlockSpec((tl, emb), lambda i: (i, 0)), full((emb, hid)), full((1, hid)),
                  full((hid, hid)), full((1, hid)), full((hid, ow)), full((1, hid)),
                  pl.BlockSpec((tl, ow), lambda i: (i, 0))],
        out_specs=pl.BlockSpec((tl, ow), lambda i: (i, 0)),
        out_shape=jax.ShapeDtypeStruct((seq, ow), F32),
        compiler_params=_cparams(("parallel",)),
        name="hyena_filter_mlp",
    )(zemb, w1, b1, w2, b2, wo, fr, dec2)


def _hyadft_kernel(fa_ref, z_ref, y_ref):
    k2, a1 = fa_ref.shape
    fa = fa_ref[...]

    def body(bi, c):
        xb = z_ref[0, pl.ds(bi, a1, stride=HY_B1), :].astype(BF16)
        y_ref[0, pl.ds(bi, k2, stride=HY_B1), :] = jnp.dot(fa, xb, preferred_element_type=F32)
        return c

    lax.fori_loop(0, HY_B1, body, 0, unroll=4)


def _hyadft(fa, z):
    b, seq, hy = z.shape
    k2, a1 = fa.shape
    return pl.pallas_call(
        _hyadft_kernel,
        grid=(b, hy // LANES),
        in_specs=[pl.BlockSpec((k2, a1), lambda i, j: (0, 0)),
                  pl.BlockSpec((1, seq, LANES), lambda i, j: (i, 0, j))],
        out_specs=pl.BlockSpec((1, k2 * HY_B1, LANES), lambda i, j: (i, 0, j)),
        out_shape=jax.ShapeDtypeStruct((b, k2 * HY_B1, hy), F32),
        compiler_params=_cparams(("parallel", "parallel")),
        name="hyena_adft",
    )(fa, z)


def _hyfreq_kernel(y_ref, fb_ref, h_ref, fbi_ref, w_ref, *, nka):
    ka = pl.program_id(0)
    nb = fb_ref.shape[0] // 2
    nbatch = y_ref.shape[0]

    @pl.when(ka < nka)
    def _():
        hr, hi = h_ref[0, :nb], h_ref[0, nb:]
        for bi in range(nbatch):
            p = jnp.dot(fb_ref[...], y_ref[bi, 0].astype(BF16), preferred_element_type=F32)
            pr, pi = p[:nb], p[nb:]
            q = jnp.concatenate([pr * hr - pi * hi, pr * hi + pi * hr], axis=0).astype(BF16)
            w_ref[bi, 0] = jnp.dot(fbi_ref[...], q, preferred_element_type=F32)

    @pl.when(ka >= nka)
    def _():
        w_ref[...] = jnp.zeros(w_ref.shape, w_ref.dtype)


def _hyfreq(y4, fb, hhat, fbi, nka):
    b, kap, r2, hy = y4.shape
    nb2 = fb.shape[0]
    return pl.pallas_call(
        functools.partial(_hyfreq_kernel, nka=nka),
        grid=(kap,),
        in_specs=[pl.BlockSpec((b, 1, r2, hy), lambda k: (0, k, 0, 0)),
                  pl.BlockSpec((nb2, r2), lambda k: (0, 0)),
                  pl.BlockSpec((1, nb2, hy), lambda k: (k, 0, 0)),
                  pl.BlockSpec((r2, nb2), lambda k: (0, 0))],
        out_specs=pl.BlockSpec((b, 1, r2, hy), lambda k: (0, k, 0, 0)),
        out_shape=jax.ShapeDtypeStruct((b, kap, r2, hy), F32),
        compiler_params=_cparams(("parallel",)),
        name="hyena_freq",
    )(y4, fb, hhat, fbi)


def _hyout_kernel(a_ref, w_ref, z_ref, x0_ref, sk_ref, o_ref):
    a1, k2 = a_ref.shape
    ainv = a_ref[...]
    skip = sk_ref[...]

    def body(bi, c):
        wb = w_ref[0, pl.ds(bi, k2, stride=HY_B1), :].astype(BF16)
        y = jnp.dot(ainv, wb, preferred_element_type=F32)
        rows = pl.ds(bi, a1, stride=HY_B1)
        o_ref[0, rows, :] = (y + z_ref[0, rows, :] * skip) * x0_ref[0, rows, :]
        return c

    lax.fori_loop(0, HY_B1, body, 0, unroll=4)


def _hyout(ainv, w3, z, x0, skip):
    b, seq, hy = z.shape
    a1, k2 = ainv.shape
    blk = pl.BlockSpec((1, seq, LANES), lambda i, j: (i, 0, j))
    return pl.pallas_call(
        _hyout_kernel,
        grid=(b, hy // LANES),
        in_specs=[pl.BlockSpec((a1, k2), lambda i, j: (0, 0)),
                  pl.BlockSpec((1, k2 * HY_B1, LANES), lambda i, j: (i, 0, j)),
                  blk, blk,
                  pl.BlockSpec((1, LANES), lambda i, j: (0, j))],
        out_specs=blk,
        out_shape=jax.ShapeDtypeStruct((b, seq, hy), F32),
        compiler_params=_cparams(("parallel", "parallel")),
        name="hyena_out",
    )(ainv, w3, z, x0, skip)


def _dft_tables(a1):
    a2, b2, b1 = 2 * a1, 2 * HY_B1, HY_B1
    nka = a1 + 1
    kap = -(-nka // 8) * 8
    ka = np.arange(nka)[:, None]
    def fa(na):
        ph = 2 * np.pi * ((ka * np.arange(na)[None, :]) % a2) / a2
        m = np.zeros((2 * kap, na))
        m[0:2 * nka:2] = np.cos(ph)
        m[1:2 * nka:2] = -np.sin(ph)
        return m
    kb = np.arange(b2)[:, None]
    th = 2 * np.pi * ((kb * np.arange(b2)[None, :]) % b2) / b2
    c, s = np.cos(th), np.sin(th)
    fb_full = np.block([[c, s], [-s, c]])
    fb_half = np.block([[c[:, :b1], s[:, :b1]], [-s[:, :b1], c[:, :b1]]])
    ct, st = c.T[:b1], s.T[:b1]
    fbi = np.block([[ct, -st], [st, ct]]) / b2
    ph = 2 * np.pi * ((np.arange(a1)[:, None] * np.arange(nka)[None, :]) % a2) / a2
    wgt = np.where((np.arange(nka) == 0) | (np.arange(nka) == a1), 1.0, 2.0)[None, :] / a2
    ainv = np.zeros((a1, 2 * kap))
    ainv[:, 0:2 * nka:2] = wgt * np.cos(ph)
    ainv[:, 1:2 * nka:2] = -wgt * np.sin(ph)
    f32 = lambda v: np.asarray(v, np.float32)
    return dict(nka=nka, kap=kap, fa_data=f32(fa(a1)), fa_filt=f32(fa(a2)), fb_full=f32(fb_full),
                fb_half=f32(fb_half), fbi=f32(fbi), ainv=f32(ainv))


def _hyena_filter_spectrum(seq, hy, w1, b1, w2, b2, wo, fr, tabs):
    nbands = (FILTER_EMB - 1) // 2
    t = jnp.linspace(0.0, 1.0, seq, dtype=F32)[:, None]
    ang = 2.0 * math.pi * jnp.arange(seq, dtype=F32)[:, None] / seq
    f = jnp.linspace(1e-4, nbands - 1, nbands, dtype=F32)[None, :]
    zemb = jnp.concatenate([t, jnp.cos(f * ang), -jnp.sin(f * ang)], axis=-1)
    deltas = jnp.abs(jnp.linspace(math.log(DECAY_TARGET) / DECAY_FAST,
                                  math.log(DECAY_TARGET) / DECAY_SLOW, hy, dtype=F32))
    decay = jnp.exp(-t * deltas)
    dec2 = jnp.concatenate([decay, decay], axis=-1)
    h = _filter_mlp(zemb, w1, b1[None], w2, b2[None], wo, fr[None], dec2, tl=min(512, seq))
    h_fwd, h_bwd = h[:, :hy], h[:, hy:]
    filt = jnp.concatenate([h_fwd, jnp.zeros((1, hy), F32), h_bwd[1:][::-1]], axis=0)
    a2 = 2 * seq // HY_B1
    filt3 = filt.reshape(a2, HY_B1, hy)
    h2 = jnp.concatenate([filt3, jnp.roll(filt3, 1, axis=0)], axis=1)
    kap = tabs["kap"]
    ya = _lmm(jnp.asarray(tabs["fa_filt"]), h2.reshape(1, a2, 2 * HY_B1 * hy), tn=2048, out_dtype=F32,
              precision=HIGHEST, name="hyena_filter_adft")
    ya = ya.reshape(kap, 4 * HY_B1, hy)
    return _lmm(jnp.asarray(tabs["fb_full"]), ya, tn=hy, out_dtype=F32, precision=HIGHEST,
                name="hyena_filter_bdft")


def _hyena(hy_in, sw, sb, skip, hhat, tabs):
    b, seq, hy3 = hy_in.shape
    hy = hy3 // 3
    a1 = seq // HY_B1
    kap, nka = tabs["kap"], tabs["nka"]
    z, x0 = _hypre(hy_in, sw, sb[None], hy)
    ya = _hyadft(jnp.asarray(tabs["fa_data"], BF16), z)
    y4 = ya.reshape(b, kap, 2 * HY_B1, hy)
    w4 = _hyfreq(y4, jnp.asarray(tabs["fb_half"], BF16), hhat, jnp.asarray(tabs["fbi"], BF16), nka)
    w3 = w4.reshape(b, 2 * kap * HY_B1, hy)
    return _hyout(jnp.asarray(tabs["ainv"], BF16), w3, z, x0, skip[None])


def _na_tables(rpb, rows):
    heads = rpb.shape[0]
    gr, kr_n, w, kh, kw = NA_GROUP_ROWS, NA_KEY_ROWS, GRID_W, NA_KH_MAX, NA_KW
    n_g = rows // gr
    qc = np.arange(w)
    cs = np.clip(qc - kw // 2, 0, w - kw)
    kc = np.arange(w)
    colvalid = (kc[None, :] >= cs[:, None]) & (kc[None, :] < cs[:, None] + kw)
    dc = np.clip(kc[None, :] - qc[:, None] + kw - 1, 0, 2 * kw - 2)
    out = []
    for g in (0, 1, n_g - 1):
        ks = min(max(gr * g - kh // 2, 0), rows - kr_n)
        r = gr * g + np.arange(gr)
        rs = np.clip(r - kh // 2, 0, rows - kh)
        kr = ks + np.arange(kr_n)
        rowvalid = (kr[None, :] >= rs[:, None]) & (kr[None, :] < rs[:, None] + kh)
        dr = np.clip(kr[None, :] - r[:, None] + NA_KH_MAX - 1, 0, 2 * NA_KH_MAX - 2)
        oh_r = np.asarray(dr[:, :, None] == np.arange(2 * NA_KH_MAX - 1), np.float32)
        oh_c = np.asarray(dc[:, :, None] == np.arange(2 * kw - 1), np.float32)
        bias = jnp.einsum("hab,gka,qcb->hgqkc", rpb.astype(F32), oh_r, oh_c, precision=HIGHEST)
        valid = rowvalid[:, None, :, None] & colvalid[None, :, None, :]
        out.append(jnp.where(valid[None], bias.astype(F32), NEG).reshape(heads, gr * w, kr_n * w))
    return jnp.stack(out)


def _natten_kernel(q_ref, k_ref, v_ref, t_ref, o_ref, *, heads, n_g, rows):
    g = pl.program_id(1)
    tq = q_ref.shape[1]
    tk = t_ref.shape[3]
    ks = jnp.clip(NA_GROUP_ROWS * g - NA_KH_MAX // 2, 0, rows - NA_KEY_ROWS)
    kstart = pl.multiple_of(ks * GRID_W, GRID_W)
    outs = []
    for h in range(heads):
        lo = h * NA_HEAD_DIM
        q = q_ref[0, :, lo:lo + NA_HEAD_DIM] * (NA_HEAD_DIM ** -0.5)
        k = k_ref[0, pl.ds(kstart, tk), lo:lo + NA_HEAD_DIM]
        v = v_ref[0, pl.ds(kstart, tk), lo:lo + NA_HEAD_DIM]
        s = lax.dot_general(q.astype(BF16), k, (((1,), (1,)), ((), ())), preferred_element_type=F32)
        s = s + t_ref[0, h]
        m = jnp.max(s, axis=-1, keepdims=True)
        p = jnp.exp(s - m)
        l = jnp.sum(p, axis=-1, keepdims=True)
        o = jnp.dot(p.astype(BF16), v, preferred_element_type=F32)
        outs.append(o / l)
    o_ref[0] = jnp.concatenate(outs, axis=-1).astype(o_ref.dtype)


def _natten(qkv, tables, b, seq, naw):
    heads = naw // NA_HEAD_DIM
    rows = seq // GRID_W
    n_g = rows // NA_GROUP_ROWS
    tq = NA_GROUP_ROWS * GRID_W
    tk = NA_KEY_ROWS * GRID_W

    def tmap(i, g):
        return (jnp.where(g == 0, 0, jnp.where(g == n_g - 1, 2, 1)), 0, 0, 0)

    return pl.pallas_call(
        functools.partial(_natten_kernel, heads=heads, n_g=n_g, rows=rows),
        grid=(b, n_g),
        in_specs=[pl.BlockSpec((1, tq, naw), lambda i, g: (i, g, 0)),
                  pl.BlockSpec((1, seq, naw), lambda i, g: (i, 0, 1)),
                  pl.BlockSpec((1, seq, naw), lambda i, g: (i, 0, 2)),
                  pl.BlockSpec((1, heads, tq, tk), tmap)],
        out_specs=pl.BlockSpec((1, tq, naw), lambda i, g: (i, g, 0)),
        out_shape=jax.ShapeDtypeStruct((b, seq, naw), F32),
        compiler_params=_cparams(("parallel", "arbitrary")),
        name="natten",
    )(qkv, qkv, qkv, tables)


def _pool_kernel(u_ref, w_ref, sc_ref, o_ref, pad_ref, *, pw):
    seq = u_ref.shape[1]
    lp = seq + 2 * POOL_PAD
    j = pl.program_id(1)
    u = u_ref[0]
    zeros = jnp.zeros((POOL_PAD, LANES), F32)
    pad_ref[0:POOL_PAD, :] = zeros
    pad_ref[POOL_PAD + seq:lp, :] = zeros
    pad_ref[POOL_PAD:POOL_PAD + seq, :] = u
    xp = pad_ref[...]
    dn = lambda a, k: pltpu.roll(a, k, 0)
    up = lambda a, k: pltpu.roll(a, lp - k, 0)
    s2 = xp + dn(xp, 1)
    s4 = dn(s2, 1) + up(s2, 1)
    s8 = dn(s4, 2) + up(s4, 2)
    s16 = dn(s8, 4) + up(s8, 4)
    t = lax.broadcasted_iota(I32, (seq, 1), 0)
    lane = lax.broadcasted_iota(I32, (1, LANES), 1) + j * LANES
    gdim = pw // len(POOL_WINDOWS)
    grp = lane // gdim
    sums = (s2, s4, s8, s16)
    pooled = jnp.zeros((seq, LANES), F32)
    for gi, w in enumerate(POOL_WINDOWS):
        cnt = (jnp.minimum(t + w // 2, seq) - jnp.maximum(t - w // 2, 0)).astype(F32)
        mean = sums[gi][POOL_PAD:POOL_PAD + seq] / cnt
        pooled = jnp.where(grp == gi, mean, pooled)
    pooled = pooled - u
    y = jnp.dot(pooled.astype(BF16), w_ref[0], preferred_element_type=F32)
    o_ref[0] = y * sc_ref[...]


def _pool(u, pool_w, pool_scale):
    b, seq, pw = u.shape
    ng, gd, _ = pool_w.shape
    nh = pw // LANES
    per = LANES // gd
    wbd = jnp.zeros((nh, LANES, LANES), F32)
    for gi in range(ng):
        hh, k = divmod(gi, per)
        wbd = wbd.at[hh, k * gd:(k + 1) * gd, k * gd:(k + 1) * gd].set(pool_w[gi])
    return pl.pallas_call(
        functools.partial(_pool_kernel, pw=pw),
        grid=(b, nh),
        in_specs=[pl.BlockSpec((1, seq, LANES), lambda i, j: (i, 0, j)),
                  pl.BlockSpec((1, LANES, LANES), lambda i, j: (j, 0, 0)),
                  pl.BlockSpec((1, LANES), lambda i, j: (0, j))],
        out_specs=pl.BlockSpec((1, seq, LANES), lambda i, j: (i, 0, j)),
        out_shape=jax.ShapeDtypeStruct((b, seq, pw), F32),
        scratch_shapes=[pltpu.VMEM((seq + 2 * POOL_PAD, LANES), F32)],
        compiler_params=_cparams(("parallel", "parallel")),
        name="pool_mixer",
    )(u, wbd.astype(BF16), pool_scale[None])


def _outproj_kernel(x_ref, yh_ref, yn_ref, yp_ref, gm_ref, w_ref, g2_ref, wr_ref,
                    xo_ref, h_ref, aff_ref, *, hy, naw):
    gm = gm_ref[...]
    m1 = _rms(yh_ref[...], gm[:, :hy]).astype(BF16)
    m2 = _rms(yn_ref[...], gm[:, hy:hy + naw]).astype(BF16)
    m3 = _rms(yp_ref[...], gm[:, hy + naw:]).astype(BF16)
    acc = jnp.dot(m1, w_ref[:hy, :], preferred_element_type=F32)
    acc += jnp.dot(m2, w_ref[hy:hy + naw, :], preferred_element_type=F32)
    acc += jnp.dot(m3, w_ref[hy + naw:, :], preferred_element_type=F32)
    xn = x_ref[...] + acc
    xo_ref[...] = xn
    h = _rms(xn, g2_ref[...])
    h_ref[...] = h
    logits = lax.dot_general(wr_ref[...], h, (((1,), (1,)), ((), ())), preferred_element_type=F32,
                             precision=HIGHEST)
    mx = jnp.max(logits, axis=0, keepdims=True)
    ex = jnp.exp(logits - mx)
    aff_ref[...] = ex / jnp.sum(ex, axis=0, keepdims=True)


def _outproj(x2, yh, yn, yp, gm, w_bf, g2, wr_t, tm=512):
    n, d = x2.shape
    hy, naw, pw = yh.shape[1], yn.shape[1], yp.shape[1]
    e = wr_t.shape[0]
    row = lambda c: pl.BlockSpec((tm, c), lambda i: (i, 0))
    full = lambda s: pl.BlockSpec(s, lambda i: (0, 0))
    return pl.pallas_call(
        functools.partial(_outproj_kernel, hy=hy, naw=naw),
        grid=(n // tm,),
        in_specs=[row(d), row(hy), row(naw), row(pw), full((1, d)), full((d, d)), full((1, d)), full((e, d))],
        out_specs=[row(d), row(d), pl.BlockSpec((e, tm), lambda i: (0, i))],
        out_shape=[jax.ShapeDtypeStruct((n, d), F32), jax.ShapeDtypeStruct((n, d), F32),
                   jax.ShapeDtypeStruct((e, n), F32)],
        compiler_params=_cparams(("parallel",)),
        name="outproj_router",
    )(x2, yh, yn, yp, gm, w_bf, g2, wr_t)


def _block_cumsum(x, tri):
    r, n = x.shape
    cls, offs = [], []
    off = jnp.zeros((r, 1), F32)
    for j in range(n // LANES):
        c = jnp.dot(x[:, j * LANES:(j + 1) * LANES], tri, preferred_element_type=F32)
        cls.append(c)
        off = off + c[:, LANES - 1:LANES]
        offs.append(off)
    return cls, offs


def _route_kernel(aff_ref, tri_ref, bci_ref, bcx_ref, idx_ref, gate_ref, blk_ref, *, cap):
    aff = aff_ref[...]
    e, seq = aff.shape
    nblk = seq // LANES
    bits = pltpu.bitcast(aff, I32)
    capf = jnp.float32(cap)

    def radix(i, prefix):
        cand = prefix | jnp.left_shift(jnp.int32(1), 30 - i)
        cnt = jnp.sum((bits >= cand).astype(F32), axis=1, keepdims=True)
        return jnp.where(cnt >= capf, cand, prefix)

    tau = lax.fori_loop(0, 31, radix, jnp.zeros((e, 1), I32))
    gt = bits > tau
    eq = bits == tau
    need = capf - jnp.sum(gt.astype(F32), axis=1, keepdims=True)
    tri = tri_ref[...]
    cls, offs = _block_cumsum(jnp.where(eq, 1.0, 0.0).astype(BF16), tri)
    tie_rank = jnp.concatenate([c if j == 0 else c + offs[j - 1] for j, c in enumerate(cls)], axis=1)
    sel = gt | (eq & (tie_rank <= need))
    self32 = jnp.where(sel, 1.0, 0.0)
    selb = self32.astype(BF16)
    cls, _ = _block_cumsum(selb, tri)
    bend = jnp.dot(selb, bci_ref[...], preferred_element_type=F32)
    bstart = jnp.dot(selb, bcx_ref[...], preferred_element_type=F32)
    for j in range(nblk):
        rows = slice(j * e, (j + 1) * e)
        lanes = slice(j * LANES, (j + 1) * LANES)
        blk_ref[0, rows, :] = cls[j]
        blk_ref[1, rows, :] = self32[:, lanes]
        blk_ref[2, rows, :] = aff[:, lanes]
    slot = lax.broadcasted_iota(I32, (cap, 1), 0).astype(F32)
    lane = lax.broadcasted_iota(I32, (1, LANES), 1).astype(F32)
    for ei in range(e):
        bs, be = bstart[ei:ei + 1, :], bend[ei:ei + 1, :]
        inblk = (bs <= slot) & (slot < be)
        local = slot - jnp.sum(jnp.where(inblk, bs, 0.0), axis=1, keepdims=True)
        jcol = jnp.sum(jnp.where(inblk, lane, 0.0), axis=1, keepdims=True)
        pick = jnp.where(inblk, 1.0, 0.0)[:, :nblk].astype(BF16)
        rows = pl.ds(ei, nblk, stride=e)
        a = blk_ref[2, rows, :]
        a_hi = a.astype(BF16)
        r1 = a - a_hi.astype(F32)
        a_mid = r1.astype(BF16)
        a_lo = (r1 - a_mid.astype(F32)).astype(BF16)
        take = lambda v: jnp.dot(pick, v, preferred_element_type=F32)
        g_cl = take(blk_ref[0, rows, :].astype(BF16))
        g_sel = take(blk_ref[1, rows, :].astype(BF16))
        g_aff = take(a_hi) + take(a_mid) + take(a_lo)
        hit = (g_cl == local + 1.0) & (g_sel > 0.5)
        idx = jcol * LANES + jnp.sum(jnp.where(hit, lane, 0.0), axis=1, keepdims=True)
        gate = jnp.sum(jnp.where(hit, g_aff, 0.0), axis=1, keepdims=True)
        idx_ref[0, :, ei:ei + 1] = idx.astype(I32)
        gate_ref[0, :, ei:ei + 1] = gate


def _route(aff_t, b, seq, cap):
    e = aff_t.shape[0]
    nblk = seq // LANES
    tri = jnp.asarray(np.triu(np.ones((LANES, LANES), np.float32)), BF16)
    tblk = np.arange(seq)[:, None] // LANES
    bci = jnp.asarray(tblk <= np.arange(LANES)[None, :], BF16)
    bcx = jnp.asarray(tblk < np.arange(LANES)[None, :], BF16)
    full = lambda s: pl.BlockSpec(s, lambda i: (0, 0))
    idx, gate = pl.pallas_call(
        functools.partial(_route_kernel, cap=cap),
        grid=(b,),
        in_specs=[pl.BlockSpec((e, seq), lambda i: (0, i)), full((LANES, LANES)),
                  full((seq, LANES)), full((seq, LANES))],
        out_specs=[pl.BlockSpec((1, cap, e), lambda i: (i, 0, 0)),
                   pl.BlockSpec((1, cap, e), lambda i: (i, 0, 0))],
        out_shape=[jax.ShapeDtypeStruct((b, cap, e), I32), jax.ShapeDtypeStruct((b, cap, e), F32)],
        scratch_shapes=[pltpu.VMEM((3, nblk * e, LANES), F32)],
        compiler_params=_cparams(("parallel",)),
        name="ec_route",
    )(aff_t, tri, bci, bcx)
    return idx.transpose(0, 2, 1), gate.transpose(0, 2, 1)


def _expert_kernel(rows_ref, h_hbm, wg_ref, wu_ref, wd_ref, y_ref, xbuf, xb, sem, *, tm, nm, nf, ne):
    e = pl.program_id(0)
    m = pl.program_id(1)
    f = pl.program_id(2)
    tile = e * nm + m
    ntiles = ne * nm
    slot = tile % 2
    chunk = tm // nf

    def row_copy(src_row, dst_slot, dst_chunk, dst_row):
        return pltpu.make_async_copy(h_hbm.at[pl.ds(src_row, 1), :],
                                     xbuf.at[dst_slot, dst_chunk, pl.ds(dst_row, 1), :], sem.at[dst_slot])

    def tile_wait(dst_slot):
        for k in range(nf):
            pltpu.make_async_copy(h_hbm.at[pl.ds(0, chunk), :], xbuf.at[dst_slot, k], sem.at[dst_slot]).wait()

    @pl.when((tile == 0) & (f == 0))
    def _():
        for k in range(nf):
            def issue(i, c, k=k):
                row_copy(rows_ref[k * chunk + i], 0, k, i).start()
                return c

            lax.fori_loop(0, chunk, issue, 0, unroll=8)

    @pl.when(f == 0)
    def _():
        tile_wait(slot)
        for k in range(nf):
            xb[k * chunk:(k + 1) * chunk, :] = xbuf[slot, k].astype(BF16)
        y_ref[0] = jnp.zeros(y_ref.shape[1:], F32)

    nxt = jnp.minimum(tile + 1, ntiles - 1)
    base = nxt * tm + f * chunk
    for i in range(chunk):
        row_copy(rows_ref[base + i], 1 - slot, f, i).start()

    x = xb[...]
    a = jnp.dot(x, wg_ref[0, 0].astype(BF16), preferred_element_type=F32)
    u = jnp.dot(x, wu_ref[0, 0].astype(BF16), preferred_element_type=F32)
    hh = (a * jax.nn.sigmoid(a) * u).astype(BF16)
    y_ref[0] += jnp.dot(hh, wd_ref[0, 0].astype(BF16), preferred_element_type=F32)

    @pl.when((tile == ntiles - 1) & (f == nf - 1))
    def _():
        tile_wait(1 - slot)


def _experts(rows_flat, h2, w_gate, w_up, w_down, layer, mtot, tm=1024, tf=512):
    _, e, d, ff = w_gate.shape
    tm = min(tm, mtot)
    tf = min(tf, ff)
    nm, nf = mtot // tm, ff // tf
    grid_spec = pltpu.PrefetchScalarGridSpec(
        num_scalar_prefetch=1,
        grid=(e, nm, nf),
        in_specs=[pl.BlockSpec(memory_space=pl.ANY),
                  pl.BlockSpec((1, 1, d, tf), lambda i, m, f, r: (layer, i, 0, f)),
                  pl.BlockSpec((1, 1, d, tf), lambda i, m, f, r: (layer, i, 0, f)),
                  pl.BlockSpec((1, 1, tf, d), lambda i, m, f, r: (layer, i, f, 0))],
        out_specs=pl.BlockSpec((1, tm, d), lambda i, m, f, r: (i, m, 0)),
        scratch_shapes=[pltpu.VMEM((2, nf, tm // nf, d), F32), pltpu.VMEM((tm, d), BF16),
                        pltpu.SemaphoreType.DMA((2,))],
    )
    return pl.pallas_call(
        functools.partial(_expert_kernel, tm=tm, nm=nm, nf=nf, ne=e),
        grid_spec=grid_spec,
        out_shape=jax.ShapeDtypeStruct((e, mtot, d), F32),
        compiler_params=_cparams(("arbitrary", "arbitrary", "arbitrary")),
        name="ec_experts",
    )(rows_flat, h2, w_gate, w_up, w_down)


def _combine_kernel(idx_ref, gate_ref, x_hbm, y_ref, o_ref, sem, *, ne, cap):
    b = pl.program_id(0)
    e = pl.program_id(1)

    @pl.when(e == 0)
    def _():
        cp = pltpu.make_async_copy(x_hbm.at[b], o_ref.at[0], sem)
        cp.start()
        cp.wait()

    base = (b * ne + e) * cap

    def body(i, c):
        t = idx_ref[base + i]
        g = gate_ref[base + i]
        o_ref[0, pl.ds(t, 1), :] = o_ref[0, pl.ds(t, 1), :] + g * y_ref[0, pl.ds(i, 1), :]
        return c

    lax.fori_loop(0, cap, body, 0, unroll=8)


def _combine(idx_flat, gate_flat, x3, y, cap):
    b, seq, d = x3.shape
    ne = y.shape[0]
    grid_spec = pltpu.PrefetchScalarGridSpec(
        num_scalar_prefetch=2,
        grid=(b, ne),
        in_specs=[pl.BlockSpec(memory_space=pl.ANY),
                  pl.BlockSpec((1, cap, d), lambda i, e, ix, gt: (e, i, 0))],
        out_specs=pl.BlockSpec((1, seq, d), lambda i, e, ix, gt: (i, 0, 0)),
        scratch_shapes=[pltpu.SemaphoreType.DMA(())],
    )
    return pl.pallas_call(
        functools.partial(_combine_kernel, ne=ne, cap=cap),
        grid_spec=grid_spec,
        out_shape=jax.ShapeDtypeStruct((b, seq, d), F32),
        compiler_params=_cparams(("arbitrary", "arbitrary")),
        name="ec_combine",
    )(idx_flat, gate_flat, x3, y)


def _moe(x3, h2, aff_t, w_gate, w_up, w_down, layer):
    b, seq, d = x3.shape
    ne = w_gate.shape[1]
    cap = EC_CAPACITY * seq // ne
    idx, gate = _route(aff_t, b, seq, cap)
    rows = idx + (jnp.arange(b, dtype=I32) * seq)[:, None, None]
    rows_flat = rows.transpose(1, 0, 2).reshape(-1)
    y = _experts(rows_flat, h2, w_gate, w_up, w_down, layer, b * cap)
    return _combine(idx.reshape(-1), gate.reshape(-1), x3, y, cap)


def _final_kernel(x_ref, g_ref, o_ref):
    o_ref[...] = _rms(x_ref[...], g_ref[...])


def _final_norm(x2, g, tm=1024):
    n, d = x2.shape
    return pl.pallas_call(
        _final_kernel,
        grid=(n // tm,),
        in_specs=[pl.BlockSpec((tm, d), lambda i: (i, 0)), pl.BlockSpec((1, d), lambda i: (0, 0))],
        out_specs=pl.BlockSpec((tm, d), lambda i: (i, 0)),
        out_shape=jax.ShapeDtypeStruct((n, d), F32),
        compiler_params=_cparams(("parallel",)),
        name="final_norm",
    )(x2, g)


def kernel(x, norm1_g, w_in, hy_short_w, hy_short_b, hy_f_w1, hy_f_b1, hy_f_w2, hy_f_b2, hy_f_wout, hy_f_freq, hy_skip, na_rpb, pool_w, pool_scale, mix_norm_g, w_out, norm2_g, w_router, w_gate, w_up, w_down, final_g):
    b, seq, d = x.shape
    depth = w_in.shape[0]
    hy = hy_skip.shape[1]
    pw = pool_scale.shape[1]
    naw = d - hy - pw
    n = b * seq
    rows = seq // GRID_W
    tabs = _dft_tables(seq // HY_B1)
    x2 = x.reshape(n, d)
    for i in range(depth):
        hy_in, qkv, pool_in = _inproj(x2, norm1_g[i][None], w_in[i].astype(BF16), 3 * hy, 3 * naw)
        hhat = _hyena_filter_spectrum(seq, hy, hy_f_w1[i], hy_f_b1[i], hy_f_w2[i], hy_f_b2[i],
                                      hy_f_wout[i], hy_f_freq[i], tabs)
        y_hy = _hyena(hy_in.reshape(b, seq, 3 * hy), hy_short_w[i], hy_short_b[i], hy_skip[i], hhat, tabs)
        y_na = _natten(qkv.reshape(b, seq, 3 * naw), _na_tables(na_rpb[i], rows), b, seq, naw)
        y_pool = _pool(pool_in.reshape(b, seq, pw), pool_w[i], pool_scale[i])
        x2, h2, aff_t = _outproj(x2, y_hy.reshape(n, hy), y_na.reshape(n, naw), y_pool.reshape(n, pw),
                                 mix_norm_g[i][None], w_out[i].astype(BF16), norm2_g[i][None],
                                 w_router[i].T)
        x2 = _moe(x2.reshape(b, seq, d), h2, aff_t, w_gate, w_up, w_down, i).reshape(n, d)
    return _final_norm(x2, final_g[None]).reshape(b, seq, d)
```

```python
import functools
import math

import numpy as np
import jax
import jax.numpy as jnp
from jax import lax
from jax.experimental import pallas as pl
from jax.experimental.pallas import tpu as pltpu

F32 = jnp.float32
BF16 = jnp.bfloat16
I32 = jnp.int32
EPS = 1e-6
HIGHEST = lax.Precision.HIGHEST

GRID_W = 64
NA_HEAD_DIM = 64
NA_KH_MAX = 8
NA_KW = 16
NA_GROUP_ROWS = 4
NA_KEY_ROWS = 12
POOL_WINDOWS = (2, 4, 8, 16)
POOL_PAD = 16
FILTER_EMB = 33
DECAY_FAST, DECAY_SLOW, DECAY_TARGET = 0.3, 1.5, 1e-2
EC_CAPACITY = 2
HY_B1 = 128
NEG = -1e30
LANES = 128
VMEM_LIMIT = 56 * 1024 * 1024


def _cparams(sem, vmem=VMEM_LIMIT):
    return pltpu.CompilerParams(dimension_semantics=sem, vmem_limit_bytes=vmem)


def _rms(v, g):
    return v * lax.rsqrt(jnp.mean(v * v, axis=-1, keepdims=True) + EPS) * g


def _inproj_kernel(x_ref, g_ref, w_ref, hy_ref, qkv_ref, pool_ref, *, hyw, naw):
    h = _rms(x_ref[...], g_ref[...]).astype(BF16)
    hy_ref[...] = jnp.dot(h, w_ref[:, :hyw], preferred_element_type=F32)
    qkv_ref[...] = jnp.dot(h, w_ref[:, hyw:hyw + naw], preferred_element_type=F32).astype(BF16)
    pool_ref[...] = jnp.dot(h, w_ref[:, hyw + naw:], preferred_element_type=F32)


def _inproj(x2, g, w_bf, hyw, naw, tm=512):
    n, d = x2.shape
    inw = w_bf.shape[1]
    pw = inw - hyw - naw
    return pl.pallas_call(
        functools.partial(_inproj_kernel, hyw=hyw, naw=naw),
        grid=(n // tm,),
        in_specs=[pl.BlockSpec((tm, d), lambda i: (i, 0)),
                  pl.BlockSpec((1, d), lambda i: (0, 0)),
                  pl.BlockSpec((d, inw), lambda i: (0, 0))],
        out_specs=[pl.BlockSpec((tm, hyw), lambda i: (i, 0)),
                   pl.BlockSpec((tm, naw), lambda i: (i, 0)),
                   pl.BlockSpec((tm, pw), lambda i: (i, 0))],
        out_shape=[jax.ShapeDtypeStruct((n, hyw), F32),
                   jax.ShapeDtypeStruct((n, naw), BF16),
                   jax.ShapeDtypeStruct((n, pw), F32)],
        compiler_params=_cparams(("parallel",)),
        name="inproj",
    )(x2, g, w_bf)


def _hypre_kernel(u0_ref, u1_ref, u2_ref, w0_ref, w1_ref, w2_ref, b0_ref, b1_ref, b2_ref, z_ref, x0_ref):
    seq = u0_ref.shape[1]
    row = lax.broadcasted_iota(I32, (seq, 1), 0)

    def conv(u_ref, w_ref, b_ref):
        u = u_ref[0]
        prev = jnp.where(row == 0, 0.0, pltpu.roll(u, 1, 0))
        nxt = jnp.where(row == seq - 1, 0.0, pltpu.roll(u, seq - 1, 0))
        return prev * w_ref[0:1, :] + u * w_ref[1:2, :] + nxt * w_ref[2:3, :] + b_ref[...]

    x0_ref[0] = conv(u0_ref, w0_ref, b0_ref)
    z_ref[0] = conv(u2_ref, w2_ref, b2_ref) * conv(u1_ref, w1_ref, b1_ref)


def _hypre(hy_in, sw, sb, hy):
    b, seq, _ = hy_in.shape
    nc = hy // LANES
    cb = nc
    u_spec = lambda k: pl.BlockSpec((1, seq, LANES), lambda i, j, k=k: (i, 0, k * cb + j))
    w_spec = lambda k: pl.BlockSpec((3, LANES), lambda i, j, k=k: (0, k * cb + j))
    b_spec = lambda k: pl.BlockSpec((1, LANES), lambda i, j, k=k: (0, k * cb + j))
    o_spec = pl.BlockSpec((1, seq, LANES), lambda i, j: (i, 0, j))
    return pl.pallas_call(
        _hypre_kernel,
        grid=(b, nc),
        in_specs=[u_spec(0), u_spec(1), u_spec(2), w_spec(0), w_spec(1), w_spec(2),
                  b_spec(0), b_spec(1), b_spec(2)],
        out_specs=[o_spec, o_spec],
        out_shape=[jax.ShapeDtypeStruct((b, seq, hy), F32)] * 2,
        compiler_params=_cparams(("parallel", "parallel")),
        name="hyena_pre",
    )(hy_in, hy_in, hy_in, sw, sw, sw, sb, sb, sb)


def _lmm_kernel(a_ref, x_ref, o_ref, *, cast, precision):
    x = x_ref[0]
    if cast is not None:
        x = x.astype(cast)
    o_ref[0] = jnp.dot(a_ref[...], x, preferred_element_type=F32, precision=precision).astype(o_ref.dtype)


def _lmm(a, x, tn, out_dtype, cast=None, precision=None, name="lmm"):
    g, k, n = x.shape
    m = a.shape[0]
    return pl.pallas_call(
        functools.partial(_lmm_kernel, cast=cast, precision=precision),
        grid=(g, n // tn),
        in_specs=[pl.BlockSpec((m, k), lambda i, j: (0, 0)),
                  pl.BlockSpec((1, k, tn), lambda i, j: (i, 0, j))],
        out_specs=pl.BlockSpec((1, m, tn), lambda i, j: (i, 0, j)),
        out_shape=jax.ShapeDtypeStruct((g, m, n), out_dtype),
        compiler_params=_cparams(("parallel", "parallel")),
        name=name,
    )(a, x)


def _filter_kernel(z_ref, w1_ref, b1_ref, w2_ref, b2_ref, wo_ref, fr_ref, dec_ref, o_ref):
    fr = fr_ref[...]
    h = jnp.sin(fr * (jnp.dot(z_ref[...], w1_ref[...], preferred_element_type=F32, precision=HIGHEST)
                      + b1_ref[...]))
    h = jnp.sin(fr * (jnp.dot(h, w2_ref[...], preferred_element_type=F32, precision=HIGHEST) + b2_ref[...]))
    o_ref[...] = jnp.dot(h, wo_ref[...], preferred_element_type=F32, precision=HIGHEST) * dec_ref[...]


def _filter_mlp(zemb, w1, b1, w2, b2, wo, fr, dec2, tl=512):
    seq, emb = zemb.shape
    hid = w1.shape[1]
    ow = wo.shape[1]
    full = lambda s: pl.BlockSpec(s, lambda i: (0, 0))
    return pl.pallas_call(
        _filter_kernel,
        grid=(seq // tl,),
        in_specs=[pl.BlockSpec((tl, emb), lambda i: (i, 0)), full((emb, hid)), full((1, hid)),
                  full((hid, hid)), full((1, hid)), full((hid, ow)), full((1, hid)),
                  pl.BlockSpec((tl, ow), lambda i: (i, 0))],
        out_specs=pl.BlockSpec((tl, ow), lambda i: (i, 0)),
        out_shape=jax.ShapeDtypeStruct((seq, ow), F32),
        compiler_params=_cparams(("parallel",)),
        name="hyena_filter_mlp",
    )(zemb, w1, b1, w2, b2, wo, fr, dec2)


def _hyadft_kernel(fa_ref, z_ref, y_ref, zt_ref, yt_ref):
    k2, a1 = fa_ref.shape
    fa = fa_ref[...]
    zt_ref[...] = pltpu.einshape("abl->bal", z_ref[0].reshape(a1, HY_B1, LANES))

    def body(i, c):
        b0 = 2 * i
        zz = jnp.concatenate([zt_ref[b0], zt_ref[b0 + 1]], axis=1).astype(BF16)
        r = jnp.dot(fa, zz, preferred_element_type=F32)
        yt_ref[b0] = r[:, :LANES]
        yt_ref[b0 + 1] = r[:, LANES:]
        return c

    lax.fori_loop(0, HY_B1 // 2, body, 0, unroll=8)
    y_ref[0] = pltpu.einshape("bkl->kbl", yt_ref[...]).reshape(k2 * HY_B1, LANES)


def _hyadft(fa, z):
    b, seq, hy = z.shape
    k2, a1 = fa.shape
    return pl.pallas_call(
        _hyadft_kernel,
        grid=(b, hy // LANES),
        in_specs=[pl.BlockSpec((k2, a1), lambda i, j: (0, 0)),
                  pl.BlockSpec((1, seq, LANES), lambda i, j: (i, 0, j))],
        out_specs=pl.BlockSpec((1, k2 * HY_B1, LANES), lambda i, j: (i, 0, j)),
        out_shape=jax.ShapeDtypeStruct((b, k2 * HY_B1, hy), F32),
        scratch_shapes=[pltpu.VMEM((HY_B1, a1, LANES), F32), pltpu.VMEM((HY_B1, k2, LANES), F32)],
        compiler_params=_cparams(("parallel", "parallel")),
        name="hyena_adft",
    )(fa, z)


def _hyfreq_kernel(y_ref, fb_ref, h_ref, fbi_ref, w_ref, *, nka):
    ka = pl.program_id(0)
    nb = fb_ref.shape[0] // 2
    nbatch = y_ref.shape[0]

    @pl.when(ka < nka)
    def _():
        hr, hi = h_ref[0, :nb], h_ref[0, nb:]
        for bi in range(nbatch):
            p = jnp.dot(fb_ref[...], y_ref[bi, 0].astype(BF16), preferred_element_type=F32)
            pr, pi = p[:nb], p[nb:]
            q = jnp.concatenate([pr * hr - pi * hi, pr * hi + pi * hr], axis=0).astype(BF16)
            w_ref[bi, 0] = jnp.dot(fbi_ref[...], q, preferred_element_type=F32)

    @pl.when(ka >= nka)
    def _():
        w_ref[...] = jnp.zeros(w_ref.shape, w_ref.dtype)


def _hyfreq(y4, fb, hhat, fbi, nka):
    b, kap, r2, hy = y4.shape
    nb2 = fb.shape[0]
    return pl.pallas_call(
        functools.partial(_hyfreq_kernel, nka=nka),
        grid=(kap,),
        in_specs=[pl.BlockSpec((b, 1, r2, hy), lambda k: (0, k, 0, 0)),
                  pl.BlockSpec((nb2, r2), lambda k: (0, 0)),
                  pl.BlockSpec((1, nb2, hy), lambda k: (k, 0, 0)),
                  pl.BlockSpec((r2, nb2), lambda k: (0, 0))],
        out_specs=pl.BlockSpec((b, 1, r2, hy), lambda k: (0, k, 0, 0)),
        out_shape=jax.ShapeDtypeStruct((b, kap, r2, hy), F32),
        compiler_params=_cparams(("parallel",)),
        name="hyena_freq",
    )(y4, fb, hhat, fbi)


def _hyout_kernel(a_ref, w_ref, z_ref, x0_ref, sk_ref, o_ref, wt_ref, zt_ref, xt_ref, ot_ref):
    a1, k2 = a_ref.shape
    ainv = a_ref[...]
    skip = sk_ref[...]
    wt_ref[...] = pltpu.einshape("kbl->bkl", w_ref[0].reshape(k2, HY_B1, LANES))
    zt_ref[...] = pltpu.einshape("abl->bal", z_ref[0].reshape(a1, HY_B1, LANES))
    xt_ref[...] = pltpu.einshape("abl->bal", x0_ref[0].reshape(a1, HY_B1, LANES))

    def body(i, c):
        b0 = 2 * i
        ww = jnp.concatenate([wt_ref[b0], wt_ref[b0 + 1]], axis=1).astype(BF16)
        y = jnp.dot(ainv, ww, preferred_element_type=F32)
        ot_ref[b0] = (y[:, :LANES] + zt_ref[b0] * skip) * xt_ref[b0]
        ot_ref[b0 + 1] = (y[:, LANES:] + zt_ref[b0 + 1] * skip) * xt_ref[b0 + 1]
        return c

    lax.fori_loop(0, HY_B1 // 2, body, 0, unroll=8)
    o_ref[0] = pltpu.einshape("bal->abl", ot_ref[...]).reshape(a1 * HY_B1, LANES)


def _hyout(ainv, w3, z, x0, skip):
    b, seq, hy = z.shape
    a1, k2 = ainv.shape
    blk = pl.BlockSpec((1, seq, LANES), lambda i, j: (i, 0, j))
    return pl.pallas_call(
        _hyout_kernel,
        grid=(b, hy // LANES),
        in_specs=[pl.BlockSpec((a1, k2), lambda i, j: (0, 0)),
                  pl.BlockSpec((1, k2 * HY_B1, LANES), lambda i, j: (i, 0, j)),
                  blk, blk,
                  pl.BlockSpec((1, LANES), lambda i, j: (0, j))],
        out_specs=blk,
        out_shape=jax.ShapeDtypeStruct((b, seq, hy), F32),
        scratch_shapes=[pltpu.VMEM((HY_B1, k2, LANES), F32)] + [pltpu.VMEM((HY_B1, a1, LANES), F32)] * 3,
        compiler_params=_cparams(("parallel", "parallel")),
        name="hyena_out",
    )(ainv, w3, z, x0, skip)


def _dft_tables(a1):
    a2, b2, b1 = 2 * a1, 2 * HY_B1, HY_B1
    nka = a1 + 1
    kap = -(-nka // 8) * 8
    ka = np.arange(nka)[:, None]
    def fa(na):
        ph = 2 * np.pi * ((ka * np.arange(na)[None, :]) % a2) / a2
        m = np.zeros((2 * kap, na))
        m[0:2 * nka:2] = np.cos(ph)
        m[1:2 * nka:2] = -np.sin(ph)
        return m
    kb = np.arange(b2)[:, None]
    th = 2 * np.pi * ((kb * np.arange(b2)[None, :]) % b2) / b2
    c, s = np.cos(th), np.sin(th)
    fb_full = np.block([[c, s], [-s, c]])
    fb_half = np.block([[c[:, :b1], s[:, :b1]], [-s[:, :b1], c[:, :b1]]])
    ct, st = c.T[:b1], s.T[:b1]
    fbi = np.block([[ct, -st], [st, ct]]) / b2
    ph = 2 * np.pi * ((np.arange(a1)[:, None] * np.arange(nka)[None, :]) % a2) / a2
    wgt = np.where((np.arange(nka) == 0) | (np.arange(nka) == a1), 1.0, 2.0)[None, :] / a2
    ainv = np.zeros((a1, 2 * kap))
    ainv[:, 0:2 * nka:2] = wgt * np.cos(ph)
    ainv[:, 1:2 * nka:2] = -wgt * np.sin(ph)
    f32 = lambda v: np.asarray(v, np.float32)
    return dict(nka=nka, kap=kap, fa_data=f32(fa(a1)), fa_filt=f32(fa(a2)), fb_full=f32(fb_full),
                fb_half=f32(fb_half), fbi=f32(fbi), ainv=f32(ainv))


def _hyena_filter_spectrum(seq, hy, w1, b1, w2, b2, wo, fr, tabs):
    nbands = (FILTER_EMB - 1) // 2
    t = jnp.linspace(0.0, 1.0, seq, dtype=F32)[:, None]
    ang = 2.0 * math.pi * jnp.arange(seq, dtype=F32)[:, None] / seq
    f = jnp.linspace(1e-4, nbands - 1, nbands, dtype=F32)[None, :]
    zemb = jnp.concatenate([t, jnp.cos(f * ang), -jnp.sin(f * ang)], axis=-1)
    deltas = jnp.abs(jnp.linspace(math.log(DECAY_TARGET) / DECAY_FAST,
                                  math.log(DECAY_TARGET) / DECAY_SLOW, hy, dtype=F32))
    decay = jnp.exp(-t * deltas)
    dec2 = jnp.concatenate([decay, decay], axis=-1)
    h = _filter_mlp(zemb, w1, b1[None], w2, b2[None], wo, fr[None], dec2, tl=min(512, seq))
    h_fwd, h_bwd = h[:, :hy], h[:, hy:]
    filt = jnp.concatenate([h_fwd, jnp.zeros((1, hy), F32), h_bwd[1:][::-1]], axis=0)
    a2 = 2 * seq // HY_B1
    filt3 = filt.reshape(a2, HY_B1, hy)
    h2 = jnp.concatenate([filt3, jnp.roll(filt3, 1, axis=0)], axis=1)
    kap = tabs["kap"]
    ya = _lmm(jnp.asarray(tabs["fa_filt"]), h2.reshape(1, a2, 2 * HY_B1 * hy), tn=2048, out_dtype=F32,
              precision=HIGHEST, name="hyena_filter_adft")
    ya = ya.reshape(kap, 4 * HY_B1, hy)
    return _lmm(jnp.asarray(tabs["fb_full"]), ya, tn=hy, out_dtype=F32, precision=HIGHEST,
                name="hyena_filter_bdft")


def _hyena(hy_in, sw, sb, skip, hhat, tabs):
    b, seq, hy3 = hy_in.shape
    hy = hy3 // 3
    a1 = seq // HY_B1
    kap, nka = tabs["kap"], tabs["nka"]
    z, x0 = _hypre(hy_in, sw, sb[None], hy)
    ya = _hyadft(jnp.asarray(tabs["fa_data"], BF16), z)
    y4 = ya.reshape(b, kap, 2 * HY_B1, hy)
    w4 = _hyfreq(y4, jnp.asarray(tabs["fb_half"], BF16), hhat, jnp.asarray(tabs["fbi"], BF16), nka)
    w3 = w4.reshape(b, 2 * kap * HY_B1, hy)
    return _hyout(jnp.asarray(tabs["ainv"], BF16), w3, z, x0, skip[None])


def _na_tables(rpb, rows):
    heads = rpb.shape[0]
    gr, kr_n, w, kh, kw = NA_GROUP_ROWS, NA_KEY_ROWS, GRID_W, NA_KH_MAX, NA_KW
    n_g = rows // gr
    qc = np.arange(w)
    cs = np.clip(qc - kw // 2, 0, w - kw)
    kc = np.arange(w)
    colvalid = (kc[None, :] >= cs[:, None]) & (kc[None, :] < cs[:, None] + kw)
    dc = np.clip(kc[None, :] - qc[:, None] + kw - 1, 0, 2 * kw - 2)
    out = []
    for g in (0, 1, n_g - 1):
        ks = min(max(gr * g - kh // 2, 0), rows - kr_n)
        r = gr * g + np.arange(gr)
        rs = np.clip(r - kh // 2, 0, rows - kh)
        kr = ks + np.arange(kr_n)
        rowvalid = (kr[None, :] >= rs[:, None]) & (kr[None, :] < rs[:, None] + kh)
        dr = np.clip(kr[None, :] - r[:, None] + NA_KH_MAX - 1, 0, 2 * NA_KH_MAX - 2)
        oh_r = np.asarray(dr[:, :, None] == np.arange(2 * NA_KH_MAX - 1), np.float32)
        oh_c = np.asarray(dc[:, :, None] == np.arange(2 * kw - 1), np.float32)
        bias = jnp.einsum("hab,gka,qcb->hgqkc", rpb.astype(F32), oh_r, oh_c, precision=HIGHEST)
        valid = rowvalid[:, None, :, None] & colvalid[None, :, None, :]
        out.append(jnp.where(valid[None], bias.astype(F32), NEG).reshape(heads, gr * w, kr_n * w))
    return jnp.stack(out)


def _natten_kernel(q_ref, k_ref, v_ref, t_ref, o_ref, *, heads, n_g, rows):
    g = pl.program_id(1)
    tq = q_ref.shape[1]
    tk = t_ref.shape[3]
    ks = jnp.clip(NA_GROUP_ROWS * g - NA_KH_MAX // 2, 0, rows - NA_KEY_ROWS)
    kstart = pl.multiple_of(ks * GRID_W, GRID_W)
    per_tile = LANES // NA_HEAD_DIM
    lane = lax.broadcasted_iota(I32, (1, LANES), 1)
    ones = jnp.ones((tk, LANES), BF16)
    for j in range(heads // per_tile):
        lanes = slice(j * LANES, (j + 1) * LANES)
        q2 = q_ref[0, :, lanes].astype(F32) * (NA_HEAD_DIM ** -0.5)
        k2 = k_ref[0, pl.ds(kstart, tk), lanes]
        vaug = jnp.concatenate([v_ref[0, pl.ds(kstart, tk), lanes], ones], axis=1)
        o2 = None
        for hh in range(per_tile):
            own = (lane >= hh * NA_HEAD_DIM) & (lane < (hh + 1) * NA_HEAD_DIM)
            qm = jnp.where(own, q2, 0.0).astype(BF16)
            s = lax.dot_general(qm, k2, (((1,), (1,)), ((), ())), preferred_element_type=F32)
            s = s + t_ref[0, j * per_tile + hh]
            m = jnp.max(s, axis=-1, keepdims=True)
            p = jnp.exp((s - m).astype(BF16))
            r = jnp.dot(p, vaug, preferred_element_type=F32)
            o = r[:, :LANES] / r[:, LANES:]
            o2 = o if o2 is None else jnp.where(own, o, o2)
        o_ref[0, :, lanes] = o2.astype(o_ref.dtype)


def _natten(qkv, tables, b, seq, naw):
    heads = naw // NA_HEAD_DIM
    rows = seq // GRID_W
    n_g = rows // NA_GROUP_ROWS
    tq = NA_GROUP_ROWS * GRID_W
    tk = NA_KEY_ROWS * GRID_W

    def tmap(i, g):
        return (jnp.where(g == 0, 0, jnp.where(g == n_g - 1, 2, 1)), 0, 0, 0)

    return pl.pallas_call(
        functools.partial(_natten_kernel, heads=heads, n_g=n_g, rows=rows),
        grid=(b, n_g),
        in_specs=[pl.BlockSpec((1, tq, naw), lambda i, g: (i, g, 0)),
                  pl.BlockSpec((1, seq, naw), lambda i, g: (i, 0, 1)),
                  pl.BlockSpec((1, seq, naw), lambda i, g: (i, 0, 2)),
                  pl.BlockSpec((1, heads, tq, tk), tmap)],
        out_specs=pl.BlockSpec((1, tq, naw), lambda i, g: (i, g, 0)),
        out_shape=jax.ShapeDtypeStruct((b, seq, naw), F32),
        compiler_params=_cparams(("parallel", "arbitrary")),
        name="natten",
    )(qkv, qkv, qkv, tables)


def _pool_kernel(u_ref, w_ref, sc_ref, o_ref, pad_ref, *, pw):
    seq = u_ref.shape[1]
    lp = seq + 2 * POOL_PAD
    j = pl.program_id(1)
    u = u_ref[0]
    zeros = jnp.zeros((POOL_PAD, LANES), F32)
    pad_ref[0:POOL_PAD, :] = zeros
    pad_ref[POOL_PAD + seq:lp, :] = zeros
    pad_ref[POOL_PAD:POOL_PAD + seq, :] = u
    xp = pad_ref[...]
    dn = lambda a, k: pltpu.roll(a, k, 0)
    up = lambda a, k: pltpu.roll(a, lp - k, 0)
    s2 = xp + dn(xp, 1)
    s4 = dn(s2, 1) + up(s2, 1)
    s8 = dn(s4, 2) + up(s4, 2)
    s16 = dn(s8, 4) + up(s8, 4)
    t = lax.broadcasted_iota(I32, (seq, 1), 0)
    lane = lax.broadcasted_iota(I32, (1, LANES), 1) + j * LANES
    gdim = pw // len(POOL_WINDOWS)
    grp = lane // gdim
    sums = (s2, s4, s8, s16)
    pooled = jnp.zeros((seq, LANES), F32)
    for gi, w in enumerate(POOL_WINDOWS):
        cnt = (jnp.minimum(t + w // 2, seq) - jnp.maximum(t - w // 2, 0)).astype(F32)
        mean = sums[gi][POOL_PAD:POOL_PAD + seq] / cnt
        pooled = jnp.where(grp == gi, mean, pooled)
    pooled = pooled - u
    y = jnp.dot(pooled.astype(BF16), w_ref[0], preferred_element_type=F32)
    o_ref[0] = y * sc_ref[...]


def _pool(u, pool_w, pool_scale):
    b, seq, pw = u.shape
    ng, gd, _ = pool_w.shape
    nh = pw // LANES
    per = LANES // gd
    wbd = jnp.zeros((nh, LANES, LANES), F32)
    for gi in range(ng):
        hh, k = divmod(gi, per)
        wbd = wbd.at[hh, k * gd:(k + 1) * gd, k * gd:(k + 1) * gd].set(pool_w[gi])
    return pl.pallas_call(
        functools.partial(_pool_kernel, pw=pw),
        grid=(b, nh),
        in_specs=[pl.BlockSpec((1, seq, LANES), lambda i, j: (i, 0, j)),
                  pl.BlockSpec((1, LANES, LANES), lambda i, j: (j, 0, 0)),
                  pl.BlockSpec((1, LANES), lambda i, j: (0, j))],
        out_specs=pl.BlockSpec((1, seq, LANES), lambda i, j: (i, 0, j)),
        out_shape=jax.ShapeDtypeStruct((b, seq, pw), F32),
        scratch_shapes=[pltpu.VMEM((seq + 2 * POOL_PAD, LANES), F32)],
        compiler_params=_cparams(("parallel", "parallel")),
        name="pool_mixer",
    )(u, wbd.astype(BF16), pool_scale[None])


def _outproj_kernel(x_ref, yh_ref, yn_ref, yp_ref, gm_ref, w_ref, g2_ref, wr_ref,
                    xo_ref, h_ref, aff_ref, *, hy, naw):
    gm = gm_ref[...]
    m1 = _rms(yh_ref[...], gm[:, :hy]).astype(BF16)
    m2 = _rms(yn_ref[...], gm[:, hy:hy + naw]).astype(BF16)
    m3 = _rms(yp_ref[...], gm[:, hy + naw:]).astype(BF16)
    acc = jnp.dot(m1, w_ref[:hy, :], preferred_element_type=F32)
    acc += jnp.dot(m2, w_ref[hy:hy + naw, :], preferred_element_type=F32)
    acc += jnp.dot(m3, w_ref[hy + naw:, :], preferred_element_type=F32)
    xn = x_ref[...] + acc
    xo_ref[...] = xn
    h = _rms(xn, g2_ref[...])
    h_ref[...] = h
    logits = lax.dot_general(wr_ref[...], h, (((1,), (1,)), ((), ())), preferred_element_type=F32,
                             precision=HIGHEST)
    mx = jnp.max(logits, axis=0, keepdims=True)
    ex = jnp.exp(logits - mx)
    aff_ref[...] = ex / jnp.sum(ex, axis=0, keepdims=True)


def _outproj(x2, yh, yn, yp, gm, w_bf, g2, wr_t, tm=512):
    n, d = x2.shape
    hy, naw, pw = yh.shape[1], yn.shape[1], yp.shape[1]
    e = wr_t.shape[0]
    row = lambda c: pl.BlockSpec((tm, c), lambda i: (i, 0))
    full = lambda s: pl.BlockSpec(s, lambda i: (0, 0))
    return pl.pallas_call(
        functools.partial(_outproj_kernel, hy=hy, naw=naw),
        grid=(n // tm,),
        in_specs=[row(d), row(hy), row(naw), row(pw), full((1, d)), full((d, d)), full((1, d)), full((e, d))],
        out_specs=[row(d), row(d), pl.BlockSpec((e, tm), lambda i: (0, i))],
        out_shape=[jax.ShapeDtypeStruct((n, d), F32), jax.ShapeDtypeStruct((n, d), F32),
                   jax.ShapeDtypeStruct((e, n), F32)],
        compiler_params=_cparams(("parallel",)),
        name="outproj_router",
    )(x2, yh, yn, yp, gm, w_bf, g2, wr_t)


def _block_cumsum(x, tri):
    r, n = x.shape
    cls, offs = [], []
    off = jnp.zeros((r, 1), F32)
    for j in range(n // LANES):
        c = jnp.dot(x[:, j * LANES:(j + 1) * LANES], tri, preferred_element_type=F32)
        cls.append(c)
        off = off + c[:, LANES - 1:LANES]
        offs.append(off)
    return cls, offs


def _route_kernel(aff_ref, tri_ref, bci_ref, bcx_ref, idx_ref, gate_ref, blk_ref, *, cap):
    aff = aff_ref[...]
    e, seq = aff.shape
    nblk = seq // LANES
    bits = pltpu.bitcast(aff, I32)
    capf = jnp.float32(cap)

    def radix(i, prefix):
        cand = prefix | jnp.left_shift(jnp.int32(1), 30 - i)
        cnt = jnp.sum((bits >= cand).astype(F32), axis=1, keepdims=True)
        return jnp.where(cnt >= capf, cand, prefix)

    tau = lax.fori_loop(0, 31, radix, jnp.zeros((e, 1), I32))
    gt = bits > tau
    eq = bits == tau
    need = capf - jnp.sum(gt.astype(F32), axis=1, keepdims=True)
    tri = tri_ref[...]
    cls, offs = _block_cumsum(jnp.where(eq, 1.0, 0.0).astype(BF16), tri)
    tie_rank = jnp.concatenate([c if j == 0 else c + offs[j - 1] for j, c in enumerate(cls)], axis=1)
    sel = gt | (eq & (tie_rank <= need))
    self32 = jnp.where(sel, 1.0, 0.0)
    selb = self32.astype(BF16)
    cls, _ = _block_cumsum(selb, tri)
    bend = jnp.dot(selb, bci_ref[...], preferred_element_type=F32)
    bstart = jnp.dot(selb, bcx_ref[...], preferred_element_type=F32)
    for j in range(nblk):
        rows = slice(j * e, (j + 1) * e)
        lanes = slice(j * LANES, (j + 1) * LANES)
        blk_ref[0, rows, :] = cls[j]
        blk_ref[1, rows, :] = self32[:, lanes]
        blk_ref[2, rows, :] = aff[:, lanes]
    slot = lax.broadcasted_iota(I32, (cap, 1), 0).astype(F32)
    lane = lax.broadcasted_iota(I32, (1, LANES), 1).astype(F32)
    for ei in range(e):
        bs, be = bstart[ei:ei + 1, :], bend[ei:ei + 1, :]
        inblk = (bs <= slot) & (slot < be)
        local = slot - jnp.sum(jnp.where(inblk, bs, 0.0), axis=1, keepdims=True)
        jcol = jnp.sum(jnp.where(inblk, lane, 0.0), axis=1, keepdims=True)
        pick = jnp.where(inblk, 1.0, 0.0)[:, :nblk].astype(BF16)
        rows = pl.ds(ei, nblk, stride=e)
        a = blk_ref[2, rows, :]
        a_hi = a.astype(BF16)
        r1 = a - a_hi.astype(F32)
        a_mid = r1.astype(BF16)
        a_lo = (r1 - a_mid.astype(F32)).astype(BF16)
        take = lambda v: jnp.dot(pick, v, preferred_element_type=F32)
        g_cl = take(blk_ref[0, rows, :].astype(BF16))
        g_sel = take(blk_ref[1, rows, :].astype(BF16))
        g_aff = take(a_hi) + take(a_mid) + take(a_lo)
        hit = (g_cl == local + 1.0) & (g_sel > 0.5)
        idx = jcol * LANES + jnp.sum(jnp.where(hit, lane, 0.0), axis=1, keepdims=True)
        gate = jnp.sum(jnp.where(hit, g_aff, 0.0), axis=1, keepdims=True)
        idx_ref[0, :, ei:ei + 1] = idx.astype(I32)
        gate_ref[0, :, ei:ei + 1] = gate


def _route(aff_t, b, seq, cap):
    e = aff_t.shape[0]
    nblk = seq // LANES
    tri = jnp.asarray(np.triu(np.ones((LANES, LANES), np.float32)), BF16)
    tblk = np.arange(seq)[:, None] // LANES
    bci = jnp.asarray(tblk <= np.arange(LANES)[None, :], BF16)
    bcx = jnp.asarray(tblk < np.arange(LANES)[None, :], BF16)
    full = lambda s: pl.BlockSpec(s, lambda i: (0, 0))
    idx, gate = pl.pallas_call(
        functools.partial(_route_kernel, cap=cap),
        grid=(b,),
        in_specs=[pl.BlockSpec((e, seq), lambda i: (0, i)), full((LANES, LANES)),
                  full((seq, LANES)), full((seq, LANES))],
        out_specs=[pl.BlockSpec((1, cap, e), lambda i: (i, 0, 0)),
                   pl.BlockSpec((1, cap, e), lambda i: (i, 0, 0))],
        out_shape=[jax.ShapeDtypeStruct((b, cap, e), I32), jax.ShapeDtypeStruct((b, cap, e), F32)],
        scratch_shapes=[pltpu.VMEM((3, nblk * e, LANES), F32)],
        compiler_params=_cparams(("parallel",)),
        name="ec_route",
    )(aff_t, tri, bci, bcx)
    return idx.transpose(0, 2, 1), gate.transpose(0, 2, 1)


def _expert_kernel(rows_ref, h_hbm, wg_ref, wu_ref, wd_ref, y_ref, xbuf, xb, sem, *, tm, nm, nf, ne):
    e = pl.program_id(0)
    m = pl.program_id(1)
    f = pl.program_id(2)
    tile = e * nm + m
    ntiles = ne * nm
    slot = tile % 2
    chunk = tm // nf

    def row_copy(src_row, dst_slot, dst_chunk, dst_row):
        return pltpu.make_async_copy(h_hbm.at[pl.ds(src_row, 1), :],
                                     xbuf.at[dst_slot, dst_chunk, pl.ds(dst_row, 1), :], sem.at[dst_slot])

    def tile_wait(dst_slot):
        for k in range(nf):
            pltpu.make_async_copy(h_hbm.at[pl.ds(0, chunk), :], xbuf.at[dst_slot, k], sem.at[dst_slot]).wait()

    @pl.when((tile == 0) & (f == 0))
    def _():
        for k in range(nf):
            def issue(i, c, k=k):
                row_copy(rows_ref[k * chunk + i], 0, k, i).start()
                return c

            lax.fori_loop(0, chunk, issue, 0, unroll=8)

    @pl.when(f == 0)
    def _():
        tile_wait(slot)
        for k in range(nf):
            xb[k * chunk:(k + 1) * chunk, :] = xbuf[slot, k].astype(BF16)
        y_ref[0] = jnp.zeros(y_ref.shape[1:], F32)

    nxt = jnp.minimum(tile + 1, ntiles - 1)
    base = nxt * tm + f * chunk
    for i in range(chunk):
        row_copy(rows_ref[base + i], 1 - slot, f, i).start()

    x = xb[...]
    a = jnp.dot(x, wg_ref[0, 0].astype(BF16), preferred_element_type=F32)
    u = jnp.dot(x, wu_ref[0, 0].astype(BF16), preferred_element_type=F32)
    hh = (a * jax.nn.sigmoid(a) * u).astype(BF16)
    y_ref[0] += jnp.dot(hh, wd_ref[0, 0].astype(BF16), preferred_element_type=F32)

    @pl.when((tile == ntiles - 1) & (f == nf - 1))
    def _():
        tile_wait(1 - slot)


def _experts(rows_flat, h2, w_gate, w_up, w_down, layer, mtot, tm=1024, tf=512):
    _, e, d, ff = w_gate.shape
    tm = min(tm, mtot)
    tf = min(tf, ff)
    nm, nf = mtot // tm, ff // tf
    grid_spec = pltpu.PrefetchScalarGridSpec(
        num_scalar_prefetch=1,
        grid=(e, nm, nf),
        in_specs=[pl.BlockSpec(memory_space=pl.ANY),
                  pl.BlockSpec((1, 1, d, tf), lambda i, m, f, r: (layer, i, 0, f)),
                  pl.BlockSpec((1, 1, d, tf), lambda i, m, f, r: (layer, i, 0, f)),
                  pl.BlockSpec((1, 1, tf, d), lambda i, m, f, r: (layer, i, f, 0))],
        out_specs=pl.BlockSpec((1, tm, d), lambda i, m, f, r: (i, m, 0)),
        scratch_shapes=[pltpu.VMEM((2, nf, tm // nf, d), F32), pltpu.VMEM((tm, d), BF16),
                        pltpu.SemaphoreType.DMA((2,))],
    )
    return pl.pallas_call(
        functools.partial(_expert_kernel, tm=tm, nm=nm, nf=nf, ne=e),
        grid_spec=grid_spec,
        out_shape=jax.ShapeDtypeStruct((e, mtot, d), F32),
        compiler_params=_cparams(("arbitrary", "arbitrary", "arbitrary")),
        name="ec_experts",
    )(rows_flat, h2, w_gate, w_up, w_down)


COMBINE_ROWS = 64
COMBINE_UNROLL = 8


def _combine_kernel(idx_ref, gate_ref, split_ref, x_ref, y_ref, *rest, ne, cap, span, final):
    if final:
        g_ref, o_ref, acc3, y3 = rest
    else:
        o_ref, acc3, y3 = rest
    b = pl.program_id(0)
    sp = pl.program_id(1)
    e = pl.program_id(2)
    d = x_ref.shape[2]
    sub = d // LANES
    rb = COMBINE_ROWS

    @pl.when(e == 0)
    def _():
        def load(c, carry):
            r0 = pl.multiple_of(c * rb, rb)
            acc3[pl.ds(r0, rb)] = x_ref[0, pl.ds(r0, rb), :].reshape(rb, sub, LANES)
            return carry

        lax.fori_loop(0, span // rb, load, 0)

    lo = split_ref[(b * ne + e) * 3 + sp]
    hi = split_ref[(b * ne + e) * 3 + sp + 1]

    def relayout(c, carry):
        r0 = pl.multiple_of(c * rb, rb)
        y3[pl.ds(r0, rb)] = y_ref[0, pl.ds(r0, rb), :].reshape(rb, sub, LANES)
        return carry

    lax.fori_loop(lo // rb, (hi + rb - 1) // rb, relayout, 0)

    base = (b * ne + e) * cap
    tok0 = sp * span

    def add_rows(first, count):
        toks = [idx_ref[base + first + u] - tok0 for u in range(count)]
        vals = [acc3[toks[u]] + gate_ref[base + first + u] * y3[first + u] for u in range(count)]
        for u in range(count):
            acc3[toks[u]] = vals[u]

    def group(k, carry):
        add_rows(lo + k * COMBINE_UNROLL, COMBINE_UNROLL)
        return carry

    ngroups = (hi - lo) // COMBINE_UNROLL
    lax.fori_loop(0, ngroups, group, 0)

    def tail(i, carry):
        add_rows(i, 1)
        return carry

    lax.fori_loop(lo + ngroups * COMBINE_UNROLL, hi, tail, 0)

    @pl.when(e == ne - 1)
    def _():
        def store(c, carry):
            r0 = pl.multiple_of(c * rb, rb)
            v = acc3[pl.ds(r0, rb)].reshape(rb, d)
            if final:
                v = _rms(v, g_ref[...])
            o_ref[0, pl.ds(r0, rb), :] = v
            return carry

        lax.fori_loop(0, span // rb, store, 0)


def _combine(idx, gate, x3, y, cap, final_g=None):
    b, seq, d = x3.shape
    ne = y.shape[0]
    span = seq // 2
    n_lower = jnp.sum((idx < span).astype(I32), axis=-1)
    split_flat = jnp.stack([jnp.zeros_like(n_lower), n_lower, jnp.full_like(n_lower, cap)], axis=-1).reshape(-1)
    final = final_g is not None
    in_specs = [pl.BlockSpec((1, span, d), lambda i, s, e, *_: (i, s, 0)),
                pl.BlockSpec((1, cap, d), lambda i, s, e, *_: (e, i, 0))]
    args = [x3, y]
    if final:
        in_specs.append(pl.BlockSpec((1, d), lambda i, s, e, *_: (0, 0)))
        args.append(final_g)
    grid_spec = pltpu.PrefetchScalarGridSpec(
        num_scalar_prefetch=3,
        grid=(b, 2, ne),
        in_specs=in_specs,
        out_specs=pl.BlockSpec((1, span, d), lambda i, s, e, *_: (i, s, 0)),
        scratch_shapes=[pltpu.VMEM((span, d // LANES, LANES), F32), pltpu.VMEM((cap, d // LANES, LANES), F32)],
    )
    return pl.pallas_call(
        functools.partial(_combine_kernel, ne=ne, cap=cap, span=span, final=final),
        grid_spec=grid_spec,
        out_shape=jax.ShapeDtypeStruct((b, seq, d), F32),
        compiler_params=_cparams(("arbitrary", "arbitrary", "arbitrary")),
        name="ec_combine",
    )(idx.reshape(-1), gate.reshape(-1), split_flat, *args)


def _moe(x3, h2, aff_t, w_gate, w_up, w_down, layer, final_g=None):
    b, seq, d = x3.shape
    ne = w_gate.shape[1]
    cap = EC_CAPACITY * seq // ne
    idx, gate = _route(aff_t, b, seq, cap)
    rows = idx + (jnp.arange(b, dtype=I32) * seq)[:, None, None]
    rows_flat = rows.transpose(1, 0, 2).reshape(-1)
    y = _experts(rows_flat, h2, w_gate, w_up, w_down, layer, b * cap)
    return _combine(idx, gate, x3, y, cap, final_g)


def kernel(x, norm1_g, w_in, hy_short_w, hy_short_b, hy_f_w1, hy_f_b1, hy_f_w2, hy_f_b2, hy_f_wout, hy_f_freq, hy_skip, na_rpb, pool_w, pool_scale, mix_norm_g, w_out, norm2_g, w_router, w_gate, w_up, w_down, final_g):
    b, seq, d = x.shape
    depth = w_in.shape[0]
    hy = hy_skip.shape[1]
    pw = pool_scale.shape[1]
    naw = d - hy - pw
    n = b * seq
    rows = seq // GRID_W
    tabs = _dft_tables(seq // HY_B1)
    x2 = x.reshape(n, d)
    for i in range(depth):
        hy_in, qkv, pool_in = _inproj(x2, norm1_g[i][None], w_in[i].astype(BF16), 3 * hy, 3 * naw)
        hhat = _hyena_filter_spectrum(seq, hy, hy_f_w1[i], hy_f_b1[i], hy_f_w2[i], hy_f_b2[i],
                                      hy_f_wout[i], hy_f_freq[i], tabs)
        y_hy = _hyena(hy_in.reshape(b, seq, 3 * hy), hy_short_w[i], hy_short_b[i], hy_skip[i], hhat, tabs)
        y_na = _natten(qkv.reshape(b, seq, 3 * naw), _na_tables(na_rpb[i], rows), b, seq, naw)
        y_pool = _pool(pool_in.reshape(b, seq, pw), pool_w[i], pool_scale[i])
        x2, h2, aff_t = _outproj(x2, y_hy.reshape(n, hy), y_na.reshape(n, naw), y_pool.reshape(n, pw),
                                 mix_norm_g[i][None], w_out[i].astype(BF16), norm2_g[i][None],
                                 w_router[i].T)
        last = final_g[None] if i == depth - 1 else None
        x2 = _moe(x2.reshape(b, seq, d), h2, aff_t, w_gate, w_up, w_down, i, last).reshape(n, d)
    return x2.reshape(b, seq, d)
```

```python
import functools
import math

import numpy as np
import jax
import jax.numpy as jnp
from jax import lax
from jax.experimental import pallas as pl
from jax.experimental.pallas import tpu as pltpu

F32 = jnp.float32
BF16 = jnp.bfloat16
I32 = jnp.int32
EPS = 1e-6
HIGHEST = lax.Precision.HIGHEST

GRID_W = 64
NA_HEAD_DIM = 64
NA_KH_MAX = 8
NA_KW = 16
NA_GROUP_ROWS = 4
NA_KEY_ROWS = 12
POOL_WINDOWS = (2, 4, 8, 16)
POOL_PAD = 16
FILTER_EMB = 33
DECAY_FAST, DECAY_SLOW, DECAY_TARGET = 0.3, 1.5, 1e-2
EC_CAPACITY = 2
HY_B1 = 128
NEG = -1e30
LANES = 128
VMEM_LIMIT = 56 * 1024 * 1024


def _cparams(sem, vmem=VMEM_LIMIT):
    return pltpu.CompilerParams(dimension_semantics=sem, vmem_limit_bytes=vmem)


def _rms(v, g):
    return v * lax.rsqrt(jnp.mean(v * v, axis=-1, keepdims=True) + EPS) * g


def _inproj_kernel(x_ref, g_ref, w_ref, hy_ref, qkv_ref, pool_ref, *, hyw, naw):
    h = _rms(x_ref[...], g_ref[...]).astype(BF16)
    hy_ref[...] = jnp.dot(h, w_ref[:, :hyw], preferred_element_type=F32)
    qkv_ref[...] = jnp.dot(h, w_ref[:, hyw:hyw + naw], preferred_element_type=F32).astype(BF16)
    pool_ref[...] = jnp.dot(h, w_ref[:, hyw + naw:], preferred_element_type=F32)


def _inproj(x2, g, w_bf, hyw, naw, tm=512):
    n, d = x2.shape
    inw = w_bf.shape[1]
    pw = inw - hyw - naw
    return pl.pallas_call(
        functools.partial(_inproj_kernel, hyw=hyw, naw=naw),
        grid=(n // tm,),
        in_specs=[pl.BlockSpec((tm, d), lambda i: (i, 0)),
                  pl.BlockSpec((1, d), lambda i: (0, 0)),
                  pl.BlockSpec((d, inw), lambda i: (0, 0))],
        out_specs=[pl.BlockSpec((tm, hyw), lambda i: (i, 0)),
                   pl.BlockSpec((tm, naw), lambda i: (i, 0)),
                   pl.BlockSpec((tm, pw), lambda i: (i, 0))],
        out_shape=[jax.ShapeDtypeStruct((n, hyw), F32),
                   jax.ShapeDtypeStruct((n, naw), BF16),
                   jax.ShapeDtypeStruct((n, pw), F32)],
        compiler_params=_cparams(("parallel",)),
        name="inproj",
    )(x2, g, w_bf)


def _hypre_kernel(u0_ref, u1_ref, u2_ref, w0_ref, w1_ref, w2_ref, b0_ref, b1_ref, b2_ref, z_ref, x0_ref):
    seq = u0_ref.shape[1]
    row = lax.broadcasted_iota(I32, (seq, 1), 0)

    def conv(u_ref, w_ref, b_ref):
        u = u_ref[0]
        prev = jnp.where(row == 0, 0.0, pltpu.roll(u, 1, 0))
        nxt = jnp.where(row == seq - 1, 0.0, pltpu.roll(u, seq - 1, 0))
        return prev * w_ref[0:1, :] + u * w_ref[1:2, :] + nxt * w_ref[2:3, :] + b_ref[...]

    x0_ref[0] = conv(u0_ref, w0_ref, b0_ref)
    z_ref[0] = conv(u2_ref, w2_ref, b2_ref) * conv(u1_ref, w1_ref, b1_ref)


def _hypre(hy_in, sw, sb, hy):
    b, seq, _ = hy_in.shape
    nc = hy // LANES
    cb = nc
    u_spec = lambda k: pl.BlockSpec((1, seq, LANES), lambda i, j, k=k: (i, 0, k * cb + j))
    w_spec = lambda k: pl.BlockSpec((3, LANES), lambda i, j, k=k: (0, k * cb + j))
    b_spec = lambda k: pl.BlockSpec((1, LANES), lambda i, j, k=k: (0, k * cb + j))
    o_spec = pl.BlockSpec((1, seq, LANES), lambda i, j: (i, 0, j))
    return pl.pallas_call(
        _hypre_kernel,
        grid=(b, nc),
        in_specs=[u_spec(0), u_spec(1), u_spec(2), w_spec(0), w_spec(1), w_spec(2),
                  b_spec(0), b_spec(1), b_spec(2)],
        out_specs=[o_spec, o_spec],
        out_shape=[jax.ShapeDtypeStruct((b, seq, hy), F32)] * 2,
        compiler_params=_cparams(("parallel", "parallel")),
        name="hyena_pre",
    )(hy_in, hy_in, hy_in, sw, sw, sw, sb, sb, sb)


def _filter_kernel(z_ref, w1_ref, b1_ref, w2_ref, b2_ref, wo_ref, fr_ref, dec_ref, o_ref):
    fr = fr_ref[0]
    h = jnp.sin(fr * (jnp.dot(z_ref[...], w1_ref[0], preferred_element_type=F32, precision=HIGHEST) + b1_ref[0]))
    h = jnp.sin(fr * (jnp.dot(h, w2_ref[0], preferred_element_type=F32, precision=HIGHEST) + b2_ref[0]))
    hw = jnp.dot(h, wo_ref[0], preferred_element_type=F32, precision=HIGHEST)
    c = dec_ref.shape[2]
    o_ref[0, 0] = hw[:, :c] * dec_ref[0]
    o_ref[0, 1] = hw[:, c:] * dec_ref[1]


def _filter_mlp(zemb, w1, b1, w2, b2, wo, fr, dec, tl=512):
    seq, emb = zemb.shape
    depth, _, hid = w1.shape
    ow = wo.shape[2]
    c = ow // 2
    lay = lambda s: pl.BlockSpec((1,) + s, lambda l, i: (l, 0, 0))
    return pl.pallas_call(
        _filter_kernel,
        grid=(depth, seq // tl),
        in_specs=[pl.BlockSpec((tl, emb), lambda l, i: (i, 0)), lay((emb, hid)), lay((1, hid)),
                  lay((hid, hid)), lay((1, hid)), lay((hid, ow)), lay((1, hid)),
                  pl.BlockSpec((2, tl, c), lambda l, i: (0, i, 0))],
        out_specs=pl.BlockSpec((1, 2, tl, c), lambda l, i: (l, 0, i, 0)),
        out_shape=jax.ShapeDtypeStruct((depth, 2, seq, c), F32),
        compiler_params=_cparams(("parallel", "parallel")),
        name="hyena_filter_mlp",
    )(zemb, w1, b1, w2, b2, wo, fr, dec)


def _filter_bdft_kernel(ff_ref, fc_ref, yf_ref, yb_ref, o_ref):
    o_ref[0, 0] = (jnp.dot(ff_ref[...], yf_ref[0, 0, 0], preferred_element_type=F32, precision=HIGHEST)
                   + jnp.dot(fc_ref[...], yb_ref[0, 0, 0], preferred_element_type=F32, precision=HIGHEST))


def _filter_bdft(ff, fc, ya5, nka):
    depth, _, kap, r, c = ya5.shape
    blk = lambda p: pl.BlockSpec((1, 1, 1, r, c), lambda l, k, p=p: (l, p, k, 0, 0))
    return pl.pallas_call(
        _filter_bdft_kernel,
        grid=(depth, nka),
        in_specs=[pl.BlockSpec((r, r), lambda l, k: (0, 0)), pl.BlockSpec((r, r), lambda l, k: (0, 0)),
                  blk(0), blk(1)],
        out_specs=pl.BlockSpec((1, 1, r, c), lambda l, k: (l, k, 0, 0)),
        out_shape=jax.ShapeDtypeStruct((depth, nka, r, c), F32),
        compiler_params=_cparams(("parallel", "parallel")),
        name="hyena_filter_bdft",
    )(ff, fc, ya5, ya5)


def _hyadft_kernel(fa_ref, z_ref, y_ref, zt_ref, yt_ref, *, cast, precision):
    k2, a1 = fa_ref.shape
    fa = fa_ref[...]
    zt_ref[...] = pltpu.einshape("abl->bal", z_ref[0].reshape(a1, HY_B1, LANES))

    def body(i, c):
        b0 = 2 * i
        zz = jnp.concatenate([zt_ref[b0], zt_ref[b0 + 1]], axis=1)
        if cast is not None:
            zz = zz.astype(cast)
        r = jnp.dot(fa, zz, preferred_element_type=F32, precision=precision)
        yt_ref[b0] = r[:, :LANES]
        yt_ref[b0 + 1] = r[:, LANES:]
        return c

    lax.fori_loop(0, HY_B1 // 2, body, 0, unroll=8)
    y_ref[0] = pltpu.einshape("bkl->kbl", yt_ref[...]).reshape(k2 * HY_B1, LANES)


def _hyadft(fa, z, cast=BF16, precision=None, name="hyena_adft"):
    b, seq, hy = z.shape
    k2, a1 = fa.shape
    return pl.pallas_call(
        functools.partial(_hyadft_kernel, cast=cast, precision=precision),
        grid=(b, hy // LANES),
        in_specs=[pl.BlockSpec((k2, a1), lambda i, j: (0, 0)),
                  pl.BlockSpec((1, seq, LANES), lambda i, j: (i, 0, j))],
        out_specs=pl.BlockSpec((1, k2 * HY_B1, LANES), lambda i, j: (i, 0, j)),
        out_shape=jax.ShapeDtypeStruct((b, k2 * HY_B1, hy), F32),
        scratch_shapes=[pltpu.VMEM((HY_B1, a1, LANES), F32), pltpu.VMEM((HY_B1, k2, LANES), F32)],
        compiler_params=_cparams(("parallel", "parallel")),
        name=name,
    )(fa, z)


def _hyfreq_kernel(y_ref, fb_ref, h_ref, fbi_ref, w_ref, *, nka):
    ka = pl.program_id(0)
    nb = fb_ref.shape[0] // 2
    nbatch = y_ref.shape[0]

    @pl.when(ka < nka)
    def _():
        hr, hi = h_ref[0, 0, :nb], h_ref[0, 0, nb:]
        for bi in range(nbatch):
            p = jnp.dot(fb_ref[...], y_ref[bi, 0].astype(BF16), preferred_element_type=F32)
            pr, pi = p[:nb], p[nb:]
            q = jnp.concatenate([pr * hr - pi * hi, pr * hi + pi * hr], axis=0).astype(BF16)
            w_ref[bi, 0] = jnp.dot(fbi_ref[...], q, preferred_element_type=F32)

    @pl.when(ka >= nka)
    def _():
        w_ref[...] = jnp.zeros(w_ref.shape, w_ref.dtype)


def _hyfreq(y4, fb, hhat, fbi, nka, layer):
    b, kap, r2, hy = y4.shape
    nb2 = fb.shape[0]
    return pl.pallas_call(
        functools.partial(_hyfreq_kernel, nka=nka),
        grid=(kap,),
        in_specs=[pl.BlockSpec((b, 1, r2, hy), lambda k: (0, k, 0, 0)),
                  pl.BlockSpec((nb2, r2), lambda k: (0, 0)),
                  pl.BlockSpec((1, 1, nb2, hy), lambda k: (layer, jnp.minimum(k, nka - 1), 0, 0)),
                  pl.BlockSpec((r2, nb2), lambda k: (0, 0))],
        out_specs=pl.BlockSpec((b, 1, r2, hy), lambda k: (0, k, 0, 0)),
        out_shape=jax.ShapeDtypeStruct((b, kap, r2, hy), F32),
        compiler_params=_cparams(("parallel",)),
        name="hyena_freq",
    )(y4, fb, hhat, fbi)


def _hyout_kernel(a_ref, w_ref, z_ref, x0_ref, sk_ref, o_ref, wt_ref, zt_ref, xt_ref, ot_ref):
    a1, k2 = a_ref.shape
    ainv = a_ref[...]
    skip = sk_ref[...]
    wt_ref[...] = pltpu.einshape("kbl->bkl", w_ref[0].reshape(k2, HY_B1, LANES))
    zt_ref[...] = pltpu.einshape("abl->bal", z_ref[0].reshape(a1, HY_B1, LANES))
    xt_ref[...] = pltpu.einshape("abl->bal", x0_ref[0].reshape(a1, HY_B1, LANES))

    def body(i, c):
        b0 = 2 * i
        ww = jnp.concatenate([wt_ref[b0], wt_ref[b0 + 1]], axis=1).astype(BF16)
        y = jnp.dot(ainv, ww, preferred_element_type=F32)
        ot_ref[b0] = (y[:, :LANES] + zt_ref[b0] * skip) * xt_ref[b0]
        ot_ref[b0 + 1] = (y[:, LANES:] + zt_ref[b0 + 1] * skip) * xt_ref[b0 + 1]
        return c

    lax.fori_loop(0, HY_B1 // 2, body, 0, unroll=8)
    o_ref[0] = pltpu.einshape("bal->abl", ot_ref[...]).reshape(a1 * HY_B1, LANES)


def _hyout(ainv, w3, z, x0, skip):
    b, seq, hy = z.shape
    a1, k2 = ainv.shape
    blk = pl.BlockSpec((1, seq, LANES), lambda i, j: (i, 0, j))
    return pl.pallas_call(
        _hyout_kernel,
        grid=(b, hy // LANES),
        in_specs=[pl.BlockSpec((a1, k2), lambda i, j: (0, 0)),
                  pl.BlockSpec((1, k2 * HY_B1, LANES), lambda i, j: (i, 0, j)),
                  blk, blk,
                  pl.BlockSpec((1, LANES), lambda i, j: (0, j))],
        out_specs=blk,
        out_shape=jax.ShapeDtypeStruct((b, seq, hy), F32),
        scratch_shapes=[pltpu.VMEM((HY_B1, k2, LANES), F32)] + [pltpu.VMEM((HY_B1, a1, LANES), F32)] * 3,
        compiler_params=_cparams(("parallel", "parallel")),
        name="hyena_out",
    )(ainv, w3, z, x0, skip)


def _dft_tables(a1):
    a2, b2, b1 = 2 * a1, 2 * HY_B1, HY_B1
    nka = a1 + 1
    kap = -(-nka // 8) * 8
    ka = np.arange(nka)[:, None]
    def fa(na):
        ph = 2 * np.pi * ((ka * np.arange(na)[None, :]) % a2) / a2
        m = np.zeros((2 * kap, na))
        m[0:2 * nka:2] = np.cos(ph)
        m[1:2 * nka:2] = -np.sin(ph)
        return m
    kb = np.arange(b2)[:, None]
    th = 2 * np.pi * ((kb * np.arange(b2)[None, :]) % b2) / b2
    c, s = np.cos(th), np.sin(th)
    fb_full = np.block([[c, s], [-s, c]])
    fb_half = np.block([[c[:, :b1], s[:, :b1]], [-s[:, :b1], c[:, :b1]]])
    ct, st = c.T[:b1], s.T[:b1]
    fbi = np.block([[ct, -st], [st, ct]]) / b2
    ph = 2 * np.pi * ((np.arange(a1)[:, None] * np.arange(nka)[None, :]) % a2) / a2
    wgt = np.where((np.arange(nka) == 0) | (np.arange(nka) == a1), 1.0, 2.0)[None, :] / a2
    ainv = np.zeros((a1, 2 * kap))
    ainv[:, 0:2 * nka:2] = wgt * np.cos(ph)
    ainv[:, 1:2 * nka:2] = -wgt * np.sin(ph)
    fa2 = fa(a2)
    fa_filt2 = np.stack([fa2[:, 0:a1], fa2[:, 1:a1 + 1]], axis=1).reshape(4 * kap, a1)
    fb_conj = np.concatenate([fb_full[:b2], -fb_full[b2:]], axis=0)
    f32 = lambda v: np.asarray(v, np.float32)
    return dict(nka=nka, kap=kap, fa_data=f32(fa(a1)), fa_filt2=f32(fa_filt2), fb_full=f32(fb_full),
                fb_conj=f32(fb_conj), fb_half=f32(fb_half), fbi=f32(fbi), ainv=f32(ainv))


def _hyena_filter_spectra(seq, hy, w1, b1, w2, b2, wo, fr, tabs):
    nbands = (FILTER_EMB - 1) // 2
    t = jnp.linspace(0.0, 1.0, seq, dtype=F32)[:, None]
    ang = 2.0 * math.pi * jnp.arange(seq, dtype=F32)[:, None] / seq
    f = jnp.linspace(1e-4, nbands - 1, nbands, dtype=F32)[None, :]
    zemb = jnp.concatenate([t, jnp.cos(f * ang), -jnp.sin(f * ang)], axis=-1)
    deltas = jnp.abs(jnp.linspace(math.log(DECAY_TARGET) / DECAY_FAST,
                                  math.log(DECAY_TARGET) / DECAY_SLOW, hy, dtype=F32))
    decay = jnp.exp(-t * deltas)
    dec = jnp.stack([decay, decay * (jnp.arange(seq) > 0)[:, None].astype(F32)])
    depth = w1.shape[0]
    h = _filter_mlp(zemb, w1, b1[:, None], w2, b2[:, None], wo, fr[:, None], dec, tl=min(512, seq))
    ya = _hyadft(jnp.asarray(tabs["fa_filt2"]), h.reshape(depth * 2, seq, hy), cast=None, precision=HIGHEST,
                 name="hyena_filter_adft")
    ya5 = ya.reshape(depth, 2, tabs["kap"], 4 * HY_B1, hy)
    return _filter_bdft(jnp.asarray(tabs["fb_full"]), jnp.asarray(tabs["fb_conj"]), ya5, tabs["nka"])


def _hyena(hy_in, sw, sb, skip, hhat, layer, tabs):
    b, seq, hy3 = hy_in.shape
    hy = hy3 // 3
    a1 = seq // HY_B1
    kap, nka = tabs["kap"], tabs["nka"]
    z, x0 = _hypre(hy_in, sw, sb[None], hy)
    ya = _hyadft(jnp.asarray(tabs["fa_data"], BF16), z)
    y4 = ya.reshape(b, kap, 2 * HY_B1, hy)
    w4 = _hyfreq(y4, jnp.asarray(tabs["fb_half"], BF16), hhat, jnp.asarray(tabs["fbi"], BF16), nka, layer)
    w3 = w4.reshape(b, 2 * kap * HY_B1, hy)
    return _hyout(jnp.asarray(tabs["ainv"], BF16), w3, z, x0, skip[None])


def _na_geometry(rows):
    gr, kr_n, kh = NA_GROUP_ROWS, NA_KEY_ROWS, NA_KH_MAX
    n_g = rows // gr
    geo = []
    for g in (0, 1, n_g - 1):
        ks = min(max(gr * g - kh // 2, 0), rows - kr_n)
        per_q = []
        for qr in range(gr):
            r = gr * g + qr
            rs = min(max(r - kh // 2, 0), rows - kh)
            per_q.append([((rs <= ks + k < rs + kh), ks + k - r + NA_KH_MAX - 1) for k in range(kr_n)])
        geo.append(per_q)
    return geo


def _na_table_kernel(r_ref, t_ref, *, geo):
    w, kw = GRID_W, NA_KW
    qc = lax.broadcasted_iota(I32, (w, 1), 0)
    lane = lax.broadcasted_iota(I32, (1, LANES), 1)
    kc = lane % w
    cs = jnp.clip(qc - kw // 2, 0, w - kw)
    colvalid = (kc >= cs) & (kc < cs + kw)
    left = lane < w
    neg = jnp.full((w, LANES), NEG, F32)
    shift = LANES - (kw - 1)

    def toeplitz(dr, lane_off):
        row = r_ref[0, 0, dr:dr + 1, :]
        if lane_off:
            row = pltpu.roll(row, lane_off, 1)
        return pltpu.roll(jnp.broadcast_to(row, (w, LANES)), shift, 1, stride=1, stride_axis=0)

    for v, per_q in enumerate(geo):
        for qr, per_k in enumerate(per_q):
            for pair in range(len(per_k) // 2):
                (ok0, dr0), (ok1, dr1) = per_k[2 * pair], per_k[2 * pair + 1]
                tile = neg
                if ok0:
                    tile = jnp.where(left & colvalid, toeplitz(dr0, 0), tile)
                if ok1:
                    tile = jnp.where((~left) & colvalid, toeplitz(dr1, w), tile)
                t_ref[0, v, 0, qr * w:(qr + 1) * w, pair * LANES:(pair + 1) * LANES] = tile


def _na_tables(rpb_all, rows):
    depth, heads, nr, nc = rpb_all.shape
    rp = jnp.pad(rpb_all.astype(F32), ((0, 0), (0, 0), (0, 16 - nr), (0, LANES - nc)))
    tq, tk = NA_GROUP_ROWS * GRID_W, NA_KEY_ROWS * GRID_W
    return pl.pallas_call(
        functools.partial(_na_table_kernel, geo=_na_geometry(rows)),
        grid=(depth, heads),
        in_specs=[pl.BlockSpec((1, 1, 16, LANES), lambda l, h: (l, h, 0, 0))],
        out_specs=pl.BlockSpec((1, 3, 1, tq, tk), lambda l, h: (l, 0, h, 0, 0)),
        out_shape=jax.ShapeDtypeStruct((depth, 3, heads, tq, tk), F32),
        compiler_params=_cparams(("parallel", "parallel")),
        name="natten_tables",
    )(rp)


def _natten_kernel(q_ref, k_ref, v_ref, t_ref, o_ref, *, heads, n_g, rows):
    g = pl.program_id(1)
    tq = q_ref.shape[1]
    tk = t_ref.shape[4]
    ks = jnp.clip(NA_GROUP_ROWS * g - NA_KH_MAX // 2, 0, rows - NA_KEY_ROWS)
    kstart = pl.multiple_of(ks * GRID_W, GRID_W)
    per_tile = LANES // NA_HEAD_DIM
    lane = lax.broadcasted_iota(I32, (1, LANES), 1)
    ones = jnp.ones((tk, LANES), BF16)
    for j in range(heads // per_tile):
        lanes = slice(j * LANES, (j + 1) * LANES)
        q2 = q_ref[0, :, lanes].astype(F32) * (NA_HEAD_DIM ** -0.5)
        k2 = k_ref[0, pl.ds(kstart, tk), lanes]
        vaug = jnp.concatenate([v_ref[0, pl.ds(kstart, tk), lanes], ones], axis=1)
        o2 = None
        for hh in range(per_tile):
            own = (lane >= hh * NA_HEAD_DIM) & (lane < (hh + 1) * NA_HEAD_DIM)
            qm = jnp.where(own, q2, 0.0).astype(BF16)
            s = lax.dot_general(qm, k2, (((1,), (1,)), ((), ())), preferred_element_type=F32)
            s = s + t_ref[0, 0, j * per_tile + hh]
            m = jnp.max(s, axis=-1, keepdims=True)
            p = jnp.exp((s - m).astype(BF16))
            r = jnp.dot(p, vaug, preferred_element_type=F32)
            o = r[:, :LANES] / r[:, LANES:]
            o2 = o if o2 is None else jnp.where(own, o, o2)
        o_ref[0, :, lanes] = o2.astype(o_ref.dtype)


def _natten(qkv, tables, layer, b, seq, naw):
    heads = naw // NA_HEAD_DIM
    rows = seq // GRID_W
    n_g = rows // NA_GROUP_ROWS
    tq = NA_GROUP_ROWS * GRID_W
    tk = NA_KEY_ROWS * GRID_W

    def tmap(i, g):
        return (layer, jnp.where(g == 0, 0, jnp.where(g == n_g - 1, 2, 1)), 0, 0, 0)

    return pl.pallas_call(
        functools.partial(_natten_kernel, heads=heads, n_g=n_g, rows=rows),
        grid=(b, n_g),
        in_specs=[pl.BlockSpec((1, tq, naw), lambda i, g: (i, g, 0)),
                  pl.BlockSpec((1, seq, naw), lambda i, g: (i, 0, 1)),
                  pl.BlockSpec((1, seq, naw), lambda i, g: (i, 0, 2)),
                  pl.BlockSpec((1, 1, heads, tq, tk), tmap)],
        out_specs=pl.BlockSpec((1, tq, naw), lambda i, g: (i, g, 0)),
        out_shape=jax.ShapeDtypeStruct((b, seq, naw), F32),
        compiler_params=_cparams(("parallel", "arbitrary")),
        name="natten",
    )(qkv, qkv, qkv, tables)


def _pool_kernel(u_ref, w_ref, sc_ref, o_ref, pad_ref, *, pw):
    seq = u_ref.shape[1]
    lp = seq + 2 * POOL_PAD
    j = pl.program_id(1)
    u = u_ref[0]
    zeros = jnp.zeros((POOL_PAD, LANES), F32)
    pad_ref[0:POOL_PAD, :] = zeros
    pad_ref[POOL_PAD + seq:lp, :] = zeros
    pad_ref[POOL_PAD:POOL_PAD + seq, :] = u
    xp = pad_ref[...]
    dn = lambda a, k: pltpu.roll(a, k, 0)
    up = lambda a, k: pltpu.roll(a, lp - k, 0)
    s2 = xp + dn(xp, 1)
    s4 = dn(s2, 1) + up(s2, 1)
    s8 = dn(s4, 2) + up(s4, 2)
    s16 = dn(s8, 4) + up(s8, 4)
    t = lax.broadcasted_iota(I32, (seq, 1), 0)
    lane = lax.broadcasted_iota(I32, (1, LANES), 1) + j * LANES
    gdim = pw // len(POOL_WINDOWS)
    grp = lane // gdim
    sums = (s2, s4, s8, s16)
    pooled = jnp.zeros((seq, LANES), F32)
    for gi, w in enumerate(POOL_WINDOWS):
        cnt = (jnp.minimum(t + w // 2, seq) - jnp.maximum(t - w // 2, 0)).astype(F32)
        mean = sums[gi][POOL_PAD:POOL_PAD + seq] / cnt
        pooled = jnp.where(grp == gi, mean, pooled)
    pooled = pooled - u
    y = jnp.dot(pooled.astype(BF16), w_ref[0], preferred_element_type=F32)
    o_ref[0] = y * sc_ref[...]


def _pool(u, pool_w, pool_scale):
    b, seq, pw = u.shape
    ng, gd, _ = pool_w.shape
    nh = pw // LANES
    per = LANES // gd
    wbd = jnp.zeros((nh, LANES, LANES), F32)
    for gi in range(ng):
        hh, k = divmod(gi, per)
        wbd = wbd.at[hh, k * gd:(k + 1) * gd, k * gd:(k + 1) * gd].set(pool_w[gi])
    return pl.pallas_call(
        functools.partial(_pool_kernel, pw=pw),
        grid=(b, nh),
        in_specs=[pl.BlockSpec((1, seq, LANES), lambda i, j: (i, 0, j)),
                  pl.BlockSpec((1, LANES, LANES), lambda i, j: (j, 0, 0)),
                  pl.BlockSpec((1, LANES), lambda i, j: (0, j))],
        out_specs=pl.BlockSpec((1, seq, LANES), lambda i, j: (i, 0, j)),
        out_shape=jax.ShapeDtypeStruct((b, seq, pw), F32),
        scratch_shapes=[pltpu.VMEM((seq + 2 * POOL_PAD, LANES), F32)],
        compiler_params=_cparams(("parallel", "parallel")),
        name="pool_mixer",
    )(u, wbd.astype(BF16), pool_scale[None])


def _outproj_kernel(x_ref, yh_ref, yn_ref, yp_ref, gm_ref, w_ref, g2_ref, wr_ref,
                    xo_ref, h_ref, aff_ref, *, hy, naw):
    gm = gm_ref[...]
    m1 = _rms(yh_ref[...], gm[:, :hy]).astype(BF16)
    m2 = _rms(yn_ref[...], gm[:, hy:hy + naw]).astype(BF16)
    m3 = _rms(yp_ref[...], gm[:, hy + naw:]).astype(BF16)
    acc = jnp.dot(m1, w_ref[:hy, :], preferred_element_type=F32)
    acc += jnp.dot(m2, w_ref[hy:hy + naw, :], preferred_element_type=F32)
    acc += jnp.dot(m3, w_ref[hy + naw:, :], preferred_element_type=F32)
    xn = x_ref[...] + acc
    xo_ref[...] = xn
    h = _rms(xn, g2_ref[...])
    h_ref[...] = h
    logits = lax.dot_general(wr_ref[...], h, (((1,), (1,)), ((), ())), preferred_element_type=F32,
                             precision=HIGHEST)
    mx = jnp.max(logits, axis=0, keepdims=True)
    ex = jnp.exp(logits - mx)
    aff_ref[...] = ex / jnp.sum(ex, axis=0, keepdims=True)


def _outproj(x2, yh, yn, yp, gm, w_bf, g2, wr_t, tm=512):
    n, d = x2.shape
    hy, naw, pw = yh.shape[1], yn.shape[1], yp.shape[1]
    e = wr_t.shape[0]
    row = lambda c: pl.BlockSpec((tm, c), lambda i: (i, 0))
    full = lambda s: pl.BlockSpec(s, lambda i: (0, 0))
    return pl.pallas_call(
        functools.partial(_outproj_kernel, hy=hy, naw=naw),
        grid=(n // tm,),
        in_specs=[row(d), row(hy), row(naw), row(pw), full((1, d)), full((d, d)), full((1, d)), full((e, d))],
        out_specs=[row(d), row(d), pl.BlockSpec((e, tm), lambda i: (0, i))],
        out_shape=[jax.ShapeDtypeStruct((n, d), F32), jax.ShapeDtypeStruct((n, d), F32),
                   jax.ShapeDtypeStruct((e, n), F32)],
        compiler_params=_cparams(("parallel",)),
        name="outproj_router",
    )(x2, yh, yn, yp, gm, w_bf, g2, wr_t)


def _block_cumsum(x, tri):
    r, n = x.shape
    cls, offs = [], []
    off = jnp.zeros((r, 1), F32)
    for j in range(n // LANES):
        c = jnp.dot(x[:, j * LANES:(j + 1) * LANES], tri, preferred_element_type=F32)
        cls.append(c)
        off = off + c[:, LANES - 1:LANES]
        offs.append(off)
    return cls, offs


def _route_kernel(aff_ref, tri_ref, bci_ref, bcx_ref, idx_ref, gate_ref, blk_ref, *, cap):
    aff = aff_ref[...]
    e, seq = aff.shape
    nblk = seq // LANES
    bits = pltpu.bitcast(aff, I32)
    capf = jnp.float32(cap)

    def radix(i, prefix):
        cand = prefix | jnp.left_shift(jnp.int32(1), 30 - i)
        cnt = jnp.sum((bits >= cand).astype(F32), axis=1, keepdims=True)
        return jnp.where(cnt >= capf, cand, prefix)

    tau = lax.fori_loop(0, 31, radix, jnp.zeros((e, 1), I32))
    gt = bits > tau
    eq = bits == tau
    need = capf - jnp.sum(gt.astype(F32), axis=1, keepdims=True)
    tri = tri_ref[...]
    cls, offs = _block_cumsum(jnp.where(eq, 1.0, 0.0).astype(BF16), tri)
    tie_rank = jnp.concatenate([c if j == 0 else c + offs[j - 1] for j, c in enumerate(cls)], axis=1)
    sel = gt | (eq & (tie_rank <= need))
    self32 = jnp.where(sel, 1.0, 0.0)
    selb = self32.astype(BF16)
    cls, _ = _block_cumsum(selb, tri)
    bend = jnp.dot(selb, bci_ref[...], preferred_element_type=F32)
    bstart = jnp.dot(selb, bcx_ref[...], preferred_element_type=F32)
    for j in range(nblk):
        rows = slice(j * e, (j + 1) * e)
        lanes = slice(j * LANES, (j + 1) * LANES)
        blk_ref[0, rows, :] = cls[j]
        blk_ref[1, rows, :] = self32[:, lanes]
        blk_ref[2, rows, :] = aff[:, lanes]
    slot = lax.broadcasted_iota(I32, (cap, 1), 0).astype(F32)
    lane = lax.broadcasted_iota(I32, (1, LANES), 1).astype(F32)
    for ei in range(e):
        bs, be = bstart[ei:ei + 1, :], bend[ei:ei + 1, :]
        inblk = (bs <= slot) & (slot < be)
        local = slot - jnp.sum(jnp.where(inblk, bs, 0.0), axis=1, keepdims=True)
        jcol = jnp.sum(jnp.where(inblk, lane, 0.0), axis=1, keepdims=True)
        pick = jnp.where(inblk, 1.0, 0.0)[:, :nblk].astype(BF16)
        rows = pl.ds(ei, nblk, stride=e)
        a = blk_ref[2, rows, :]
        a_hi = a.astype(BF16)
        r1 = a - a_hi.astype(F32)
        a_mid = r1.astype(BF16)
        a_lo = (r1 - a_mid.astype(F32)).astype(BF16)
        take = lambda v: jnp.dot(pick, v, preferred_element_type=F32)
        g_cl = take(blk_ref[0, rows, :].astype(BF16))
        g_sel = take(blk_ref[1, rows, :].astype(BF16))
        g_aff = take(a_hi) + take(a_mid) + take(a_lo)
        hit = (g_cl == local + 1.0) & (g_sel > 0.5)
        idx = jcol * LANES + jnp.sum(jnp.where(hit, lane, 0.0), axis=1, keepdims=True)
        gate = jnp.sum(jnp.where(hit, g_aff, 0.0), axis=1, keepdims=True)
        idx_ref[0, :, ei:ei + 1] = idx.astype(I32)
        gate_ref[0, :, ei:ei + 1] = gate


def _route(aff_t, b, seq, cap):
    e = aff_t.shape[0]
    nblk = seq // LANES
    tri = jnp.asarray(np.triu(np.ones((LANES, LANES), np.float32)), BF16)
    tblk = np.arange(seq)[:, None] // LANES
    bci = jnp.asarray(tblk <= np.arange(LANES)[None, :], BF16)
    bcx = jnp.asarray(tblk < np.arange(LANES)[None, :], BF16)
    full = lambda s: pl.BlockSpec(s, lambda i: (0, 0))
    idx, gate = pl.pallas_call(
        functools.partial(_route_kernel, cap=cap),
        grid=(b,),
        in_specs=[pl.BlockSpec((e, seq), lambda i: (0, i)), full((LANES, LANES)),
                  full((seq, LANES)), full((seq, LANES))],
        out_specs=[pl.BlockSpec((1, cap, e), lambda i: (i, 0, 0)),
                   pl.BlockSpec((1, cap, e), lambda i: (i, 0, 0))],
        out_shape=[jax.ShapeDtypeStruct((b, cap, e), I32), jax.ShapeDtypeStruct((b, cap, e), F32)],
        scratch_shapes=[pltpu.VMEM((3, nblk * e, LANES), F32)],
        compiler_params=_cparams(("parallel",)),
        name="ec_route",
    )(aff_t, tri, bci, bcx)
    return idx.transpose(0, 2, 1), gate.transpose(0, 2, 1)


def _expert_kernel(rows_ref, h_hbm, wg_ref, wu_ref, wd_ref, y_ref, xbuf, xb, acc, sem, *, tm, nm, nf, ne):
    e = pl.program_id(0)
    m = pl.program_id(1)
    f = pl.program_id(2)
    tile = e * nm + m
    ntiles = ne * nm
    slot = tile % 2
    chunk = tm // nf

    def row_copy(src_row, dst_slot, dst_chunk, dst_row):
        return pltpu.make_async_copy(h_hbm.at[pl.ds(src_row, 1), :],
                                     xbuf.at[dst_slot, dst_chunk, pl.ds(dst_row, 1), :], sem.at[dst_slot])

    def tile_wait(dst_slot):
        for k in range(nf):
            pltpu.make_async_copy(h_hbm.at[pl.ds(0, chunk), :], xbuf.at[dst_slot, k], sem.at[dst_slot]).wait()

    @pl.when((tile == 0) & (f == 0))
    def _():
        for k in range(nf):
            def issue(i, c, k=k):
                row_copy(rows_ref[k * chunk + i], 0, k, i).start()
                return c

            lax.fori_loop(0, chunk, issue, 0, unroll=8)

    @pl.when(f == 0)
    def _():
        tile_wait(slot)
        for k in range(nf):
            xb[k * chunk:(k + 1) * chunk, :] = xbuf[slot, k].astype(BF16)
        acc[...] = jnp.zeros(acc.shape, F32)

    nxt = jnp.minimum(tile + 1, ntiles - 1)
    base = nxt * tm + f * chunk
    for i in range(chunk):
        row_copy(rows_ref[base + i], 1 - slot, f, i).start()

    x = xb[...]
    a = jnp.dot(x, wg_ref[0, 0].astype(BF16), preferred_element_type=F32)
    u = jnp.dot(x, wu_ref[0, 0].astype(BF16), preferred_element_type=F32)
    hh = (a * jax.nn.sigmoid(a) * u).astype(BF16)
    acc[...] += jnp.dot(hh, wd_ref[0, 0].astype(BF16), preferred_element_type=F32)

    @pl.when(f == nf - 1)
    def _():
        y_ref[0] = acc[...].astype(y_ref.dtype)

    @pl.when((tile == ntiles - 1) & (f == nf - 1))
    def _():
        tile_wait(1 - slot)


def _experts(rows_flat, h2, w_gate, w_up, w_down, layer, mtot, tm=1024, tf=512):
    _, e, d, ff = w_gate.shape
    tm = min(tm, mtot)
    tf = min(tf, ff)
    nm, nf = mtot // tm, ff // tf
    grid_spec = pltpu.PrefetchScalarGridSpec(
        num_scalar_prefetch=1,
        grid=(e, nm, nf),
        in_specs=[pl.BlockSpec(memory_space=pl.ANY),
                  pl.BlockSpec((1, 1, d, tf), lambda i, m, f, r: (layer, i, 0, f)),
                  pl.BlockSpec((1, 1, d, tf), lambda i, m, f, r: (layer, i, 0, f)),
                  pl.BlockSpec((1, 1, tf, d), lambda i, m, f, r: (layer, i, f, 0))],
        out_specs=pl.BlockSpec((1, tm, d), lambda i, m, f, r: (i, m, 0)),
        scratch_shapes=[pltpu.VMEM((2, nf, tm // nf, d), F32), pltpu.VMEM((tm, d), BF16), pltpu.VMEM((tm, d), F32),
                        pltpu.SemaphoreType.DMA((2,))],
    )
    return pl.pallas_call(
        functools.partial(_expert_kernel, tm=tm, nm=nm, nf=nf, ne=e),
        grid_spec=grid_spec,
        out_shape=jax.ShapeDtypeStruct((e, mtot, d), BF16),
        compiler_params=_cparams(("arbitrary", "arbitrary", "arbitrary")),
        name="ec_experts",
    )(rows_flat, h2, w_gate, w_up, w_down)


COMBINE_ROWS = 64
COMBINE_UNROLL = 8


def _combine_kernel(idx_ref, gate_ref, split_ref, x_ref, y_ref, *rest, ne, cap, span, final):
    if final:
        g_ref, o_ref, acc3, y3 = rest
    else:
        o_ref, acc3, y3 = rest
    b = pl.program_id(0)
    sp = pl.program_id(1)
    e = pl.program_id(2)
    d = x_ref.shape[2]
    sub = d // LANES
    rb = COMBINE_ROWS

    @pl.when(e == 0)
    def _():
        def load(c, carry):
            r0 = pl.multiple_of(c * rb, rb)
            acc3[pl.ds(r0, rb)] = x_ref[0, pl.ds(r0, rb), :].reshape(rb, sub, LANES)
            return carry

        lax.fori_loop(0, span // rb, load, 0)

    lo = split_ref[(b * ne + e) * 3 + sp]
    hi = split_ref[(b * ne + e) * 3 + sp + 1]

    def relayout(c, carry):
        r0 = pl.multiple_of(c * rb, rb)
        y3[pl.ds(r0, rb)] = y_ref[0, pl.ds(r0, rb), :].astype(F32).reshape(rb, sub, LANES)
        return carry

    lax.fori_loop(lo // rb, (hi + rb - 1) // rb, relayout, 0)

    base = (b * ne + e) * cap
    tok0 = sp * span

    def add_rows(first, count):
        toks = [idx_ref[base + first + u] - tok0 for u in range(count)]
        vals = [acc3[toks[u]] + gate_ref[base + first + u] * y3[first + u] for u in range(count)]
        for u in range(count):
            acc3[toks[u]] = vals[u]

    def group(k, carry):
        add_rows(lo + k * COMBINE_UNROLL, COMBINE_UNROLL)
        return carry

    ngroups = (hi - lo) // COMBINE_UNROLL
    lax.fori_loop(0, ngroups, group, 0)

    def tail(i, carry):
        add_rows(i, 1)
        return carry

    lax.fori_loop(lo + ngroups * COMBINE_UNROLL, hi, tail, 0)

    @pl.when(e == ne - 1)
    def _():
        def store(c, carry):
            r0 = pl.multiple_of(c * rb, rb)
            v = acc3[pl.ds(r0, rb)].reshape(rb, d)
            if final:
                v = _rms(v, g_ref[...])
            o_ref[0, pl.ds(r0, rb), :] = v
            return carry

        lax.fori_loop(0, span // rb, store, 0)


def _combine(idx, gate, x3, y, cap, final_g=None):
    b, seq, d = x3.shape
    ne = y.shape[0]
    span = seq // 2
    n_lower = jnp.sum((idx < span).astype(I32), axis=-1)
    split_flat = jnp.stack([jnp.zeros_like(n_lower), n_lower, jnp.full_like(n_lower, cap)], axis=-1).reshape(-1)
    final = final_g is not None
    in_specs = [pl.BlockSpec((1, span, d), lambda i, s, e, *_: (i, s, 0)),
                pl.BlockSpec((1, cap, d), lambda i, s, e, *_: (e, i, 0))]
    args = [x3, y]
    if final:
        in_specs.append(pl.BlockSpec((1, d), lambda i, s, e, *_: (0, 0)))
        args.append(final_g)
    grid_spec = pltpu.PrefetchScalarGridSpec(
        num_scalar_prefetch=3,
        grid=(b, 2, ne),
        in_specs=in_specs,
        out_specs=pl.BlockSpec((1, span, d), lambda i, s, e, *_: (i, s, 0)),
        scratch_shapes=[pltpu.VMEM((span, d // LANES, LANES), F32), pltpu.VMEM((cap, d // LANES, LANES), F32)],
    )
    return pl.pallas_call(
        functools.partial(_combine_kernel, ne=ne, cap=cap, span=span, final=final),
        grid_spec=grid_spec,
        out_shape=jax.ShapeDtypeStruct((b, seq, d), F32),
        compiler_params=_cparams(("arbitrary", "arbitrary", "arbitrary")),
        name="ec_combine",
    )(idx.reshape(-1), gate.reshape(-1), split_flat, *args)


def _moe(x3, h2, aff_t, w_gate, w_up, w_down, layer, final_g=None):
    b, seq, d = x3.shape
    ne = w_gate.shape[1]
    cap = EC_CAPACITY * seq // ne
    idx, gate = _route(aff_t, b, seq, cap)
    rows = idx + (jnp.arange(b, dtype=I32) * seq)[:, None, None]
    rows_flat = rows.transpose(1, 0, 2).reshape(-1)
    y = _experts(rows_flat, h2, w_gate, w_up, w_down, layer, b * cap)
    return _combine(idx, gate, x3, y, cap, final_g)


def kernel(x, norm1_g, w_in, hy_short_w, hy_short_b, hy_f_w1, hy_f_b1, hy_f_w2, hy_f_b2, hy_f_wout, hy_f_freq, hy_skip, na_rpb, pool_w, pool_scale, mix_norm_g, w_out, norm2_g, w_router, w_gate, w_up, w_down, final_g):
    b, seq, d = x.shape
    depth = w_in.shape[0]
    hy = hy_skip.shape[1]
    pw = pool_scale.shape[1]
    naw = d - hy - pw
    n = b * seq
    rows = seq // GRID_W
    tabs = _dft_tables(seq // HY_B1)
    x2 = x.reshape(n, d)
    hhat = _hyena_filter_spectra(seq, hy, hy_f_w1, hy_f_b1, hy_f_w2, hy_f_b2, hy_f_wout, hy_f_freq, tabs)
    na_tables = _na_tables(na_rpb, rows)
    for i in range(depth):
        hy_in, qkv, pool_in = _inproj(x2, norm1_g[i][None], w_in[i].astype(BF16), 3 * hy, 3 * naw)
        y_hy = _hyena(hy_in.reshape(b, seq, 3 * hy), hy_short_w[i], hy_short_b[i], hy_skip[i], hhat, i, tabs)
        y_na = _natten(qkv.reshape(b, seq, 3 * naw), na_tables, i, b, seq, naw)
        y_pool = _pool(pool_in.reshape(b, seq, pw), pool_w[i], pool_scale[i])
        x2, h2, aff_t = _outproj(x2, y_hy.reshape(n, hy), y_na.reshape(n, naw), y_pool.reshape(n, pw),
                                 mix_norm_g[i][None], w_out[i].astype(BF16), norm2_g[i][None],
                                 w_router[i].T)
        last = final_g[None] if i == depth - 1 else None
        x2 = _moe(x2.reshape(b, seq, d), h2, aff_t, w_gate, w_up, w_down, i, last).reshape(n, d)
    return x2.reshape(b, seq, d)
```

```python
import functools
import math

import numpy as np
import jax
import jax.numpy as jnp
from jax import lax
from jax.experimental import pallas as pl
from jax.experimental.pallas import tpu as pltpu

F32 = jnp.float32
BF16 = jnp.bfloat16
I32 = jnp.int32
EPS = 1e-6
HIGHEST = lax.Precision.HIGHEST

GRID_W = 64
NA_HEAD_DIM = 64
NA_KH_MAX = 8
NA_KW = 16
NA_GROUP_ROWS = 4
NA_KEY_ROWS = 12
POOL_WINDOWS = (2, 4, 8, 16)
POOL_PAD = 16
FILTER_EMB = 33
DECAY_FAST, DECAY_SLOW, DECAY_TARGET = 0.3, 1.5, 1e-2
EC_CAPACITY = 2
HY_B1 = 128
NEG = -1e30
LANES = 128
VMEM_LIMIT = 56 * 1024 * 1024


def _cparams(sem, vmem=VMEM_LIMIT):
    return pltpu.CompilerParams(dimension_semantics=sem, vmem_limit_bytes=vmem)


def _rms(v, g):
    return v * lax.rsqrt(jnp.mean(v * v, axis=-1, keepdims=True) + EPS) * g


def _split_bf16(v):
    hi = v.astype(BF16)
    return hi, (v - hi.astype(F32)).astype(BF16)


def _dot_x3(a_hi, a_lo, x, dims=(((1,), (0,)), ((), ()))):
    x_hi, x_lo = _split_bf16(x)
    dg = lambda p, q: lax.dot_general(p, q, dims, preferred_element_type=F32)
    return dg(a_hi, x_hi) + dg(a_lo, x_hi) + dg(a_hi, x_lo)


def _inproj_kernel(x_ref, g_ref, w_ref, hy_ref, qkv_ref, pool_ref, *, hyw, naw):
    h = _rms(x_ref[...], g_ref[...]).astype(BF16)
    hy_ref[...] = jnp.dot(h, w_ref[:, :hyw], preferred_element_type=F32)
    qkv_ref[...] = jnp.dot(h, w_ref[:, hyw:hyw + naw], preferred_element_type=F32).astype(BF16)
    pool_ref[...] = jnp.dot(h, w_ref[:, hyw + naw:], preferred_element_type=F32)


def _inproj(x2, g, w_bf, hyw, naw, tm=512):
    n, d = x2.shape
    inw = w_bf.shape[1]
    pw = inw - hyw - naw
    return pl.pallas_call(
        functools.partial(_inproj_kernel, hyw=hyw, naw=naw),
        grid=(n // tm,),
        in_specs=[pl.BlockSpec((tm, d), lambda i: (i, 0)),
                  pl.BlockSpec((1, d), lambda i: (0, 0)),
                  pl.BlockSpec((d, inw), lambda i: (0, 0))],
        out_specs=[pl.BlockSpec((tm, hyw), lambda i: (i, 0)),
                   pl.BlockSpec((tm, naw), lambda i: (i, 0)),
                   pl.BlockSpec((tm, pw), lambda i: (i, 0))],
        out_shape=[jax.ShapeDtypeStruct((n, hyw), F32),
                   jax.ShapeDtypeStruct((n, naw), BF16),
                   jax.ShapeDtypeStruct((n, pw), F32)],
        compiler_params=_cparams(("parallel",)),
        name="inproj",
    )(x2, g, w_bf)


def _hypre_kernel(u0_ref, u1_ref, u2_ref, w0_ref, w1_ref, w2_ref, b0_ref, b1_ref, b2_ref, z_ref, x0_ref):
    seq = u0_ref.shape[1]
    row = lax.broadcasted_iota(I32, (seq, 1), 0)

    def conv(u_ref, w_ref, b_ref):
        u = u_ref[0]
        prev = jnp.where(row == 0, 0.0, pltpu.roll(u, 1, 0))
        nxt = jnp.where(row == seq - 1, 0.0, pltpu.roll(u, seq - 1, 0))
        return prev * w_ref[0:1, :] + u * w_ref[1:2, :] + nxt * w_ref[2:3, :] + b_ref[...]

    x0_ref[0] = conv(u0_ref, w0_ref, b0_ref)
    z_ref[0] = conv(u2_ref, w2_ref, b2_ref) * conv(u1_ref, w1_ref, b1_ref)


def _hypre(hy_in, sw, sb, hy):
    b, seq, _ = hy_in.shape
    nc = hy // LANES
    cb = nc
    u_spec = lambda k: pl.BlockSpec((1, seq, LANES), lambda i, j, k=k: (i, 0, k * cb + j))
    w_spec = lambda k: pl.BlockSpec((3, LANES), lambda i, j, k=k: (0, k * cb + j))
    b_spec = lambda k: pl.BlockSpec((1, LANES), lambda i, j, k=k: (0, k * cb + j))
    o_spec = pl.BlockSpec((1, seq, LANES), lambda i, j: (i, 0, j))
    return pl.pallas_call(
        _hypre_kernel,
        grid=(b, nc),
        in_specs=[u_spec(0), u_spec(1), u_spec(2), w_spec(0), w_spec(1), w_spec(2),
                  b_spec(0), b_spec(1), b_spec(2)],
        out_specs=[o_spec, o_spec],
        out_shape=[jax.ShapeDtypeStruct((b, seq, hy), F32)] * 2,
        compiler_params=_cparams(("parallel", "parallel")),
        name="hyena_pre",
    )(hy_in, hy_in, hy_in, sw, sw, sw, sb, sb, sb)


def _filter_kernel(z_ref, w1_ref, b1_ref, w2_ref, b2_ref, wo_ref, fr_ref, dec_ref, o_ref):
    fr = fr_ref[0]
    h = jnp.sin(fr * (jnp.dot(z_ref[...], w1_ref[0], preferred_element_type=F32, precision=HIGHEST) + b1_ref[0]))
    h = jnp.sin(fr * (jnp.dot(h, w2_ref[0], preferred_element_type=F32, precision=HIGHEST) + b2_ref[0]))
    hw = jnp.dot(h, wo_ref[0], preferred_element_type=F32, precision=HIGHEST)
    c = dec_ref.shape[2]
    o_ref[0, 0] = hw[:, :c] * dec_ref[0]
    o_ref[0, 1] = hw[:, c:] * dec_ref[1]


def _filter_mlp(zemb, w1, b1, w2, b2, wo, fr, dec, tl=512):
    seq, emb = zemb.shape
    depth, _, hid = w1.shape
    ow = wo.shape[2]
    c = ow // 2
    lay = lambda s: pl.BlockSpec((1,) + s, lambda l, i: (l, 0, 0))
    return pl.pallas_call(
        _filter_kernel,
        grid=(depth, seq // tl),
        in_specs=[pl.BlockSpec((tl, emb), lambda l, i: (i, 0)), lay((emb, hid)), lay((1, hid)),
                  lay((hid, hid)), lay((1, hid)), lay((hid, ow)), lay((1, hid)),
                  pl.BlockSpec((2, tl, c), lambda l, i: (0, i, 0))],
        out_specs=pl.BlockSpec((1, 2, tl, c), lambda l, i: (l, 0, i, 0)),
        out_shape=jax.ShapeDtypeStruct((depth, 2, seq, c), F32),
        compiler_params=_cparams(("parallel", "parallel")),
        name="hyena_filter_mlp",
    )(zemb, w1, b1, w2, b2, wo, fr, dec)


def _filter_bdft_kernel(ff_ref, fc_ref, yf_ref, yb_ref, o_ref):
    o_ref[0, 0] = (_dot_x3(ff_ref[0], ff_ref[1], yf_ref[0, 0, 0]) + _dot_x3(fc_ref[0], fc_ref[1], yb_ref[0, 0, 0]))


def _filter_bdft(ff, fc, ya5, nka):
    depth, _, kap, r, c = ya5.shape
    blk = lambda p: pl.BlockSpec((1, 1, 1, r, c), lambda l, k, p=p: (l, p, k, 0, 0))
    return pl.pallas_call(
        _filter_bdft_kernel,
        grid=(depth, nka),
        in_specs=[pl.BlockSpec((2, r, r), lambda l, k: (0, 0, 0)), pl.BlockSpec((2, r, r), lambda l, k: (0, 0, 0)),
                  blk(0), blk(1)],
        out_specs=pl.BlockSpec((1, 1, r, c), lambda l, k: (l, k, 0, 0)),
        out_shape=jax.ShapeDtypeStruct((depth, nka, r, c), F32),
        compiler_params=_cparams(("parallel", "parallel")),
        name="hyena_filter_bdft",
    )(ff, fc, ya5, ya5)


def _hyadft_kernel(fa_ref, z_ref, y_ref, zt_ref, yt_ref):
    pieces, k2, a1 = fa_ref.shape
    zt_ref[...] = pltpu.einshape("abl->bal", z_ref[0].reshape(a1, HY_B1, LANES))

    def body(i, c):
        b0 = 2 * i
        zz = jnp.concatenate([zt_ref[b0], zt_ref[b0 + 1]], axis=1)
        if pieces == 1:
            r = jnp.dot(fa_ref[0], zz.astype(BF16), preferred_element_type=F32)
        else:
            r = _dot_x3(fa_ref[0], fa_ref[1], zz)
        yt_ref[b0] = r[:, :LANES]
        yt_ref[b0 + 1] = r[:, LANES:]
        return c

    lax.fori_loop(0, HY_B1 // 2, body, 0, unroll=8)
    y_ref[0] = pltpu.einshape("bkl->kbl", yt_ref[...]).reshape(k2 * HY_B1, LANES)


def _hyadft(fa, z, name="hyena_adft"):
    b, seq, hy = z.shape
    pieces, k2, a1 = fa.shape
    return pl.pallas_call(
        _hyadft_kernel,
        grid=(b, hy // LANES),
        in_specs=[pl.BlockSpec((pieces, k2, a1), lambda i, j: (0, 0, 0)),
                  pl.BlockSpec((1, seq, LANES), lambda i, j: (i, 0, j))],
        out_specs=pl.BlockSpec((1, k2 * HY_B1, LANES), lambda i, j: (i, 0, j)),
        out_shape=jax.ShapeDtypeStruct((b, k2 * HY_B1, hy), F32),
        scratch_shapes=[pltpu.VMEM((HY_B1, a1, LANES), F32), pltpu.VMEM((HY_B1, k2, LANES), F32)],
        compiler_params=_cparams(("parallel", "parallel")),
        name=name,
    )(fa, z)


def _hyfreq_kernel(y_ref, fb_ref, h_ref, fbi_ref, w_ref, *, nka):
    ka = pl.program_id(0)
    nb = fb_ref.shape[0] // 2
    nbatch = y_ref.shape[0]

    @pl.when(ka < nka)
    def _():
        hr, hi = h_ref[0, 0, :nb], h_ref[0, 0, nb:]
        for bi in range(nbatch):
            p = jnp.dot(fb_ref[...], y_ref[bi, 0].astype(BF16), preferred_element_type=F32)
            pr, pi = p[:nb], p[nb:]
            q = jnp.concatenate([pr * hr - pi * hi, pr * hi + pi * hr], axis=0).astype(BF16)
            w_ref[bi, 0] = jnp.dot(fbi_ref[...], q, preferred_element_type=F32)

    @pl.when(ka >= nka)
    def _():
        w_ref[...] = jnp.zeros(w_ref.shape, w_ref.dtype)


def _hyfreq(y4, fb, hhat, fbi, nka, layer):
    b, kap, r2, hy = y4.shape
    nb2 = fb.shape[0]
    return pl.pallas_call(
        functools.partial(_hyfreq_kernel, nka=nka),
        grid=(kap,),
        in_specs=[pl.BlockSpec((b, 1, r2, hy), lambda k: (0, k, 0, 0)),
                  pl.BlockSpec((nb2, r2), lambda k: (0, 0)),
                  pl.BlockSpec((1, 1, nb2, hy), lambda k: (layer, jnp.minimum(k, nka - 1), 0, 0)),
                  pl.BlockSpec((r2, nb2), lambda k: (0, 0))],
        out_specs=pl.BlockSpec((b, 1, r2, hy), lambda k: (0, k, 0, 0)),
        out_shape=jax.ShapeDtypeStruct((b, kap, r2, hy), F32),
        compiler_params=_cparams(("parallel",)),
        name="hyena_freq",
    )(y4, fb, hhat, fbi)


def _hyout_kernel(a_ref, w_ref, z_ref, x0_ref, sk_ref, o_ref, wt_ref, zt_ref, xt_ref, ot_ref):
    a1, k2 = a_ref.shape
    ainv = a_ref[...]
    skip = sk_ref[...]
    wt_ref[...] = pltpu.einshape("kbl->bkl", w_ref[0].reshape(k2, HY_B1, LANES))
    zt_ref[...] = pltpu.einshape("abl->bal", z_ref[0].reshape(a1, HY_B1, LANES))
    xt_ref[...] = pltpu.einshape("abl->bal", x0_ref[0].reshape(a1, HY_B1, LANES))

    def body(i, c):
        b0 = 2 * i
        ww = jnp.concatenate([wt_ref[b0], wt_ref[b0 + 1]], axis=1).astype(BF16)
        y = jnp.dot(ainv, ww, preferred_element_type=F32)
        ot_ref[b0] = (y[:, :LANES] + zt_ref[b0] * skip) * xt_ref[b0]
        ot_ref[b0 + 1] = (y[:, LANES:] + zt_ref[b0 + 1] * skip) * xt_ref[b0 + 1]
        return c

    lax.fori_loop(0, HY_B1 // 2, body, 0, unroll=8)
    o_ref[0] = pltpu.einshape("bal->abl", ot_ref[...]).reshape(a1 * HY_B1, LANES)


def _hyout(ainv, w3, z, x0, skip):
    b, seq, hy = z.shape
    a1, k2 = ainv.shape
    blk = pl.BlockSpec((1, seq, LANES), lambda i, j: (i, 0, j))
    return pl.pallas_call(
        _hyout_kernel,
        grid=(b, hy // LANES),
        in_specs=[pl.BlockSpec((a1, k2), lambda i, j: (0, 0)),
                  pl.BlockSpec((1, k2 * HY_B1, LANES), lambda i, j: (i, 0, j)),
                  blk, blk,
                  pl.BlockSpec((1, LANES), lambda i, j: (0, j))],
        out_specs=blk,
        out_shape=jax.ShapeDtypeStruct((b, seq, hy), F32),
        scratch_shapes=[pltpu.VMEM((HY_B1, k2, LANES), F32)] + [pltpu.VMEM((HY_B1, a1, LANES), F32)] * 3,
        compiler_params=_cparams(("parallel", "parallel")),
        name="hyena_out",
    )(ainv, w3, z, x0, skip)


def _dft_tables(a1):
    a2, b2, b1 = 2 * a1, 2 * HY_B1, HY_B1
    nka = a1 + 1
    kap = -(-nka // 8) * 8
    ka = np.arange(nka)[:, None]
    def fa(na):
        ph = 2 * np.pi * ((ka * np.arange(na)[None, :]) % a2) / a2
        m = np.zeros((2 * kap, na))
        m[0:2 * nka:2] = np.cos(ph)
        m[1:2 * nka:2] = -np.sin(ph)
        return m
    kb = np.arange(b2)[:, None]
    th = 2 * np.pi * ((kb * np.arange(b2)[None, :]) % b2) / b2
    c, s = np.cos(th), np.sin(th)
    fb_full = np.block([[c, s], [-s, c]])
    fb_half = np.block([[c[:, :b1], s[:, :b1]], [-s[:, :b1], c[:, :b1]]])
    ct, st = c.T[:b1], s.T[:b1]
    fbi = np.block([[ct, -st], [st, ct]]) / b2
    ph = 2 * np.pi * ((np.arange(a1)[:, None] * np.arange(nka)[None, :]) % a2) / a2
    wgt = np.where((np.arange(nka) == 0) | (np.arange(nka) == a1), 1.0, 2.0)[None, :] / a2
    ainv = np.zeros((a1, 2 * kap))
    ainv[:, 0:2 * nka:2] = wgt * np.cos(ph)
    ainv[:, 1:2 * nka:2] = -wgt * np.sin(ph)
    fa2 = fa(a2)
    fa_filt2 = np.stack([fa2[:, 0:a1], fa2[:, 1:a1 + 1]], axis=1).reshape(4 * kap, a1)
    fb_conj = np.concatenate([fb_full[:b2], -fb_full[b2:]], axis=0)
    f32 = lambda v: np.asarray(v, np.float32)
    return dict(nka=nka, kap=kap, fa_data=f32(fa(a1)), fa_filt2=f32(fa_filt2), fb_full=f32(fb_full),
                fb_conj=f32(fb_conj), fb_half=f32(fb_half), fbi=f32(fbi), ainv=f32(ainv))


def _hyena_filter_spectra(seq, hy, w1, b1, w2, b2, wo, fr, tabs):
    nbands = (FILTER_EMB - 1) // 2
    t = jnp.linspace(0.0, 1.0, seq, dtype=F32)[:, None]
    ang = 2.0 * math.pi * jnp.arange(seq, dtype=F32)[:, None] / seq
    f = jnp.linspace(1e-4, nbands - 1, nbands, dtype=F32)[None, :]
    zemb = jnp.concatenate([t, jnp.cos(f * ang), -jnp.sin(f * ang)], axis=-1)
    deltas = jnp.abs(jnp.linspace(math.log(DECAY_TARGET) / DECAY_FAST,
                                  math.log(DECAY_TARGET) / DECAY_SLOW, hy, dtype=F32))
    decay = jnp.exp(-t * deltas)
    dec = jnp.stack([decay, decay * (jnp.arange(seq) > 0)[:, None].astype(F32)])
    depth = w1.shape[0]
    h = _filter_mlp(zemb, w1, b1[:, None], w2, b2[:, None], wo, fr[:, None], dec, tl=min(512, seq))
    pieces = lambda m: jnp.stack(_split_bf16(jnp.asarray(m)))
    ya = _hyadft(pieces(tabs["fa_filt2"]), h.reshape(depth * 2, seq, hy), name="hyena_filter_adft")
    ya5 = ya.reshape(depth, 2, tabs["kap"], 4 * HY_B1, hy)
    return _filter_bdft(pieces(tabs["fb_full"]), pieces(tabs["fb_conj"]), ya5, tabs["nka"])


def _hyena(hy_in, sw, sb, skip, hhat, layer, tabs):
    b, seq, hy3 = hy_in.shape
    hy = hy3 // 3
    a1 = seq // HY_B1
    kap, nka = tabs["kap"], tabs["nka"]
    z, x0 = _hypre(hy_in, sw, sb[None], hy)
    ya = _hyadft(jnp.asarray(tabs["fa_data"], BF16)[None], z)
    y4 = ya.reshape(b, kap, 2 * HY_B1, hy)
    w4 = _hyfreq(y4, jnp.asarray(tabs["fb_half"], BF16), hhat, jnp.asarray(tabs["fbi"], BF16), nka, layer)
    w3 = w4.reshape(b, 2 * kap * HY_B1, hy)
    return _hyout(jnp.asarray(tabs["ainv"], BF16), w3, z, x0, skip[None])


def _na_geometry(rows):
    gr, kr_n, kh = NA_GROUP_ROWS, NA_KEY_ROWS, NA_KH_MAX
    n_g = rows // gr
    geo = []
    for g in (0, 1, n_g - 1):
        ks = min(max(gr * g - kh // 2, 0), rows - kr_n)
        per_q = []
        for qr in range(gr):
            r = gr * g + qr
            rs = min(max(r - kh // 2, 0), rows - kh)
            per_q.append([((rs <= ks + k < rs + kh), ks + k - r + NA_KH_MAX - 1) for k in range(kr_n)])
        geo.append(per_q)
    return geo


def _na_table_kernel(r_ref, t_ref, *, geo):
    w, kw = GRID_W, NA_KW
    qc = lax.broadcasted_iota(I32, (w, 1), 0)
    lane = lax.broadcasted_iota(I32, (1, LANES), 1)
    kc = lane % w
    cs = jnp.clip(qc - kw // 2, 0, w - kw)
    colvalid = (kc >= cs) & (kc < cs + kw)
    left = lane < w
    neg = jnp.full((w, LANES), NEG, F32)
    shift = LANES - (kw - 1)

    def toeplitz(dr, lane_off):
        row = r_ref[0, 0, dr:dr + 1, :]
        if lane_off:
            row = pltpu.roll(row, lane_off, 1)
        return pltpu.roll(jnp.broadcast_to(row, (w, LANES)), shift, 1, stride=1, stride_axis=0)

    for v, per_q in enumerate(geo):
        for qr, per_k in enumerate(per_q):
            for pair in range(len(per_k) // 2):
                (ok0, dr0), (ok1, dr1) = per_k[2 * pair], per_k[2 * pair + 1]
                tile = neg
                if ok0:
                    tile = jnp.where(left & colvalid, toeplitz(dr0, 0), tile)
                if ok1:
                    tile = jnp.where((~left) & colvalid, toeplitz(dr1, w), tile)
                t_ref[0, v, 0, qr * w:(qr + 1) * w, pair * LANES:(pair + 1) * LANES] = tile


def _na_tables(rpb_all, rows):
    depth, heads, nr, nc = rpb_all.shape
    rp = jnp.pad(rpb_all.astype(F32), ((0, 0), (0, 0), (0, 16 - nr), (0, LANES - nc)))
    tq, tk = NA_GROUP_ROWS * GRID_W, NA_KEY_ROWS * GRID_W
    return pl.pallas_call(
        functools.partial(_na_table_kernel, geo=_na_geometry(rows)),
        grid=(depth, heads),
        in_specs=[pl.BlockSpec((1, 1, 16, LANES), lambda l, h: (l, h, 0, 0))],
        out_specs=pl.BlockSpec((1, 3, 1, tq, tk), lambda l, h: (l, 0, h, 0, 0)),
        out_shape=jax.ShapeDtypeStruct((depth, 3, heads, tq, tk), F32),
        compiler_params=_cparams(("parallel", "parallel")),
        name="natten_tables",
    )(rp)


def _natten_kernel(q_ref, k_ref, v_ref, t_ref, o_ref, *, heads, n_g, rows):
    g = pl.program_id(1)
    tq = q_ref.shape[1]
    tk = t_ref.shape[4]
    ks = jnp.clip(NA_GROUP_ROWS * g - NA_KH_MAX // 2, 0, rows - NA_KEY_ROWS)
    kstart = pl.multiple_of(ks * GRID_W, GRID_W)
    per_tile = LANES // NA_HEAD_DIM
    lane = lax.broadcasted_iota(I32, (1, LANES), 1)
    ones = jnp.ones((tk, LANES), BF16)
    for j in range(heads // per_tile):
        lanes = slice(j * LANES, (j + 1) * LANES)
        q2 = q_ref[0, :, lanes].astype(F32) * (NA_HEAD_DIM ** -0.5)
        k2 = k_ref[0, pl.ds(kstart, tk), lanes]
        vaug = jnp.concatenate([v_ref[0, pl.ds(kstart, tk), lanes], ones], axis=1)
        o2 = None
        for hh in range(per_tile):
            own = (lane >= hh * NA_HEAD_DIM) & (lane < (hh + 1) * NA_HEAD_DIM)
            qm = jnp.where(own, q2, 0.0).astype(BF16)
            s = lax.dot_general(qm, k2, (((1,), (1,)), ((), ())), preferred_element_type=F32)
            s = s + t_ref[0, 0, j * per_tile + hh]
            m = jnp.max(s, axis=-1, keepdims=True)
            p = jnp.exp((s - m).astype(BF16))
            r = jnp.dot(p, vaug, preferred_element_type=F32)
            o = r[:, :LANES] / r[:, LANES:]
            o2 = o if o2 is None else jnp.where(own, o, o2)
        o_ref[0, :, lanes] = o2.astype(o_ref.dtype)


def _natten(qkv, tables, layer, b, seq, naw):
    heads = naw // NA_HEAD_DIM
    rows = seq // GRID_W
    n_g = rows // NA_GROUP_ROWS
    tq = NA_GROUP_ROWS * GRID_W
    tk = NA_KEY_ROWS * GRID_W

    def tmap(i, g):
        return (layer, jnp.where(g == 0, 0, jnp.where(g == n_g - 1, 2, 1)), 0, 0, 0)

    return pl.pallas_call(
        functools.partial(_natten_kernel, heads=heads, n_g=n_g, rows=rows),
        grid=(b, n_g),
        in_specs=[pl.BlockSpec((1, tq, naw), lambda i, g: (i, g, 0)),
                  pl.BlockSpec((1, seq, naw), lambda i, g: (i, 0, 1)),
                  pl.BlockSpec((1, seq, naw), lambda i, g: (i, 0, 2)),
                  pl.BlockSpec((1, 1, heads, tq, tk), tmap)],
        out_specs=pl.BlockSpec((1, tq, naw), lambda i, g: (i, g, 0)),
        out_shape=jax.ShapeDtypeStruct((b, seq, naw), F32),
        compiler_params=_cparams(("parallel", "arbitrary")),
        name="natten",
    )(qkv, qkv, qkv, tables)


def _pool_kernel(u_ref, w_ref, sc_ref, o_ref, pad_ref, *, pw):
    seq = u_ref.shape[1]
    lp = seq + 2 * POOL_PAD
    j = pl.program_id(1)
    u = u_ref[0]
    zeros = jnp.zeros((POOL_PAD, LANES), F32)
    pad_ref[0:POOL_PAD, :] = zeros
    pad_ref[POOL_PAD + seq:lp, :] = zeros
    pad_ref[POOL_PAD:POOL_PAD + seq, :] = u
    xp = pad_ref[...]
    dn = lambda a, k: pltpu.roll(a, k, 0)
    up = lambda a, k: pltpu.roll(a, lp - k, 0)
    s2 = xp + dn(xp, 1)
    s4 = dn(s2, 1) + up(s2, 1)
    s8 = dn(s4, 2) + up(s4, 2)
    s16 = dn(s8, 4) + up(s8, 4)
    t = lax.broadcasted_iota(I32, (seq, 1), 0)
    lane = lax.broadcasted_iota(I32, (1, LANES), 1) + j * LANES
    gdim = pw // len(POOL_WINDOWS)
    grp = lane // gdim
    sums = (s2, s4, s8, s16)
    pooled = jnp.zeros((seq, LANES), F32)
    for gi, w in enumerate(POOL_WINDOWS):
        cnt = (jnp.minimum(t + w // 2, seq) - jnp.maximum(t - w // 2, 0)).astype(F32)
        mean = sums[gi][POOL_PAD:POOL_PAD + seq] / cnt
        pooled = jnp.where(grp == gi, mean, pooled)
    pooled = pooled - u
    y = jnp.dot(pooled.astype(BF16), w_ref[0], preferred_element_type=F32)
    o_ref[0] = y * sc_ref[...]


def _pool(u, pool_w, pool_scale):
    b, seq, pw = u.shape
    ng, gd, _ = pool_w.shape
    nh = pw // LANES
    per = LANES // gd
    wbd = jnp.zeros((nh, LANES, LANES), F32)
    for gi in range(ng):
        hh, k = divmod(gi, per)
        wbd = wbd.at[hh, k * gd:(k + 1) * gd, k * gd:(k + 1) * gd].set(pool_w[gi])
    return pl.pallas_call(
        functools.partial(_pool_kernel, pw=pw),
        grid=(b, nh),
        in_specs=[pl.BlockSpec((1, seq, LANES), lambda i, j: (i, 0, j)),
                  pl.BlockSpec((1, LANES, LANES), lambda i, j: (j, 0, 0)),
                  pl.BlockSpec((1, LANES), lambda i, j: (0, j))],
        out_specs=pl.BlockSpec((1, seq, LANES), lambda i, j: (i, 0, j)),
        out_shape=jax.ShapeDtypeStruct((b, seq, pw), F32),
        scratch_shapes=[pltpu.VMEM((seq + 2 * POOL_PAD, LANES), F32)],
        compiler_params=_cparams(("parallel", "parallel")),
        name="pool_mixer",
    )(u, wbd.astype(BF16), pool_scale[None])


def _outproj_kernel(x_ref, yh_ref, yn_ref, yp_ref, gm_ref, w_ref, g2_ref, wr_ref,
                    xo_ref, h_ref, aff_ref, *, hy, naw):
    gm = gm_ref[...]
    m1 = _rms(yh_ref[...], gm[:, :hy]).astype(BF16)
    m2 = _rms(yn_ref[...], gm[:, hy:hy + naw]).astype(BF16)
    m3 = _rms(yp_ref[...], gm[:, hy + naw:]).astype(BF16)
    acc = jnp.dot(m1, w_ref[:hy, :], preferred_element_type=F32)
    acc += jnp.dot(m2, w_ref[hy:hy + naw, :], preferred_element_type=F32)
    acc += jnp.dot(m3, w_ref[hy + naw:, :], preferred_element_type=F32)
    xn = x_ref[...] + acc
    xo_ref[...] = xn
    h = _rms(xn, g2_ref[...])
    h_ref[...] = h
    logits = _dot_x3(wr_ref[0], wr_ref[1], h, (((1,), (1,)), ((), ())))
    mx = jnp.max(logits, axis=0, keepdims=True)
    ex = jnp.exp(logits - mx)
    aff_ref[...] = ex / jnp.sum(ex, axis=0, keepdims=True)


def _outproj(x2, yh, yn, yp, gm, w_bf, g2, wr_t, tm=512):
    n, d = x2.shape
    hy, naw, pw = yh.shape[1], yn.shape[1], yp.shape[1]
    e = wr_t.shape[1]
    row = lambda c: pl.BlockSpec((tm, c), lambda i: (i, 0))
    full = lambda s: pl.BlockSpec(s, lambda i: (0, 0))
    return pl.pallas_call(
        functools.partial(_outproj_kernel, hy=hy, naw=naw),
        grid=(n // tm,),
        in_specs=[row(d), row(hy), row(naw), row(pw), full((1, d)), full((d, d)), full((1, d)),
                  pl.BlockSpec((2, e, d), lambda i: (0, 0, 0))],
        out_specs=[row(d), row(d), pl.BlockSpec((e, tm), lambda i: (0, i))],
        out_shape=[jax.ShapeDtypeStruct((n, d), F32), jax.ShapeDtypeStruct((n, d), F32),
                   jax.ShapeDtypeStruct((e, n), F32)],
        compiler_params=_cparams(("parallel",)),
        name="outproj_router",
    )(x2, yh, yn, yp, gm, w_bf, g2, wr_t)


def _block_cumsum(x, tri):
    r, n = x.shape
    cls, offs = [], []
    off = jnp.zeros((r, 1), F32)
    for j in range(n // LANES):
        c = jnp.dot(x[:, j * LANES:(j + 1) * LANES], tri, preferred_element_type=F32)
        cls.append(c)
        off = off + c[:, LANES - 1:LANES]
        offs.append(off)
    return cls, offs


def _route_kernel(aff_ref, tri_ref, bci_ref, bcx_ref, idx_ref, gate_ref, blk_ref, *, cap):
    aff = aff_ref[...]
    e, seq = aff.shape
    nblk = seq // LANES
    bits = pltpu.bitcast(aff, I32)
    capf = jnp.float32(cap)

    def radix(i, prefix):
        cand = prefix | jnp.left_shift(jnp.int32(1), 30 - i)
        cnt = jnp.sum((bits >= cand).astype(F32), axis=1, keepdims=True)
        return jnp.where(cnt >= capf, cand, prefix)

    tau = lax.fori_loop(0, 31, radix, jnp.zeros((e, 1), I32))
    gt = bits > tau
    eq = bits == tau
    need = capf - jnp.sum(gt.astype(F32), axis=1, keepdims=True)
    tri = tri_ref[...]
    cls, offs = _block_cumsum(jnp.where(eq, 1.0, 0.0).astype(BF16), tri)
    tie_rank = jnp.concatenate([c if j == 0 else c + offs[j - 1] for j, c in enumerate(cls)], axis=1)
    sel = gt | (eq & (tie_rank <= need))
    self32 = jnp.where(sel, 1.0, 0.0)
    selb = self32.astype(BF16)
    cls, _ = _block_cumsum(selb, tri)
    bend = jnp.dot(selb, bci_ref[...], preferred_element_type=F32)
    bstart = jnp.dot(selb, bcx_ref[...], preferred_element_type=F32)
    for j in range(nblk):
        rows = slice(j * e, (j + 1) * e)
        lanes = slice(j * LANES, (j + 1) * LANES)
        blk_ref[0, rows, :] = cls[j]
        blk_ref[1, rows, :] = self32[:, lanes]
        blk_ref[2, rows, :] = aff[:, lanes]
    slot = lax.broadcasted_iota(I32, (cap, 1), 0).astype(F32)
    lane = lax.broadcasted_iota(I32, (1, LANES), 1).astype(F32)
    for ei in range(e):
        bs, be = bstart[ei:ei + 1, :], bend[ei:ei + 1, :]
        inblk = (bs <= slot) & (slot < be)
        local = slot - jnp.sum(jnp.where(inblk, bs, 0.0), axis=1, keepdims=True)
        jcol = jnp.sum(jnp.where(inblk, lane, 0.0), axis=1, keepdims=True)
        pick = jnp.where(inblk, 1.0, 0.0)[:, :nblk].astype(BF16)
        rows = pl.ds(ei, nblk, stride=e)
        a = blk_ref[2, rows, :]
        a_hi = a.astype(BF16)
        r1 = a - a_hi.astype(F32)
        a_mid = r1.astype(BF16)
        a_lo = (r1 - a_mid.astype(F32)).astype(BF16)
        take = lambda v: jnp.dot(pick, v, preferred_element_type=F32)
        g_cl = take(blk_ref[0, rows, :].astype(BF16))
        g_sel = take(blk_ref[1, rows, :].astype(BF16))
        g_aff = take(a_hi) + take(a_mid) + take(a_lo)
        hit = (g_cl == local + 1.0) & (g_sel > 0.5)
        idx = jcol * LANES + jnp.sum(jnp.where(hit, lane, 0.0), axis=1, keepdims=True)
        gate = jnp.sum(jnp.where(hit, g_aff, 0.0), axis=1, keepdims=True)
        idx_ref[0, :, ei:ei + 1] = idx.astype(I32)
        gate_ref[0, :, ei:ei + 1] = gate


def _route(aff_t, b, seq, cap):
    e = aff_t.shape[0]
    nblk = seq // LANES
    tri = jnp.asarray(np.triu(np.ones((LANES, LANES), np.float32)), BF16)
    tblk = np.arange(seq)[:, None] // LANES
    bci = jnp.asarray(tblk <= np.arange(LANES)[None, :], BF16)
    bcx = jnp.asarray(tblk < np.arange(LANES)[None, :], BF16)
    full = lambda s: pl.BlockSpec(s, lambda i: (0, 0))
    idx, gate = pl.pallas_call(
        functools.partial(_route_kernel, cap=cap),
        grid=(b,),
        in_specs=[pl.BlockSpec((e, seq), lambda i: (0, i)), full((LANES, LANES)),
                  full((seq, LANES)), full((seq, LANES))],
        out_specs=[pl.BlockSpec((1, cap, e), lambda i: (i, 0, 0)),
                   pl.BlockSpec((1, cap, e), lambda i: (i, 0, 0))],
        out_shape=[jax.ShapeDtypeStruct((b, cap, e), I32), jax.ShapeDtypeStruct((b, cap, e), F32)],
        scratch_shapes=[pltpu.VMEM((3, nblk * e, LANES), F32)],
        compiler_params=_cparams(("parallel",)),
        name="ec_route",
    )(aff_t, tri, bci, bcx)
    return idx.transpose(0, 2, 1), gate


def _expert_kernel(rows_ref, h_hbm, wg_ref, wu_ref, wd_ref, y_ref, xbuf, xb, acc, sem, *, tm, nm, nf, ne):
    e = pl.program_id(0)
    m = pl.program_id(1)
    f = pl.program_id(2)
    tile = e * nm + m
    ntiles = ne * nm
    slot = tile % 2
    chunk = tm // nf

    def row_copy(src_row, dst_slot, dst_chunk, dst_row):
        return pltpu.make_async_copy(h_hbm.at[pl.ds(src_row, 1), :],
                                     xbuf.at[dst_slot, dst_chunk, pl.ds(dst_row, 1), :], sem.at[dst_slot])

    def tile_wait(dst_slot):
        for k in range(nf):
            pltpu.make_async_copy(h_hbm.at[pl.ds(0, chunk), :], xbuf.at[dst_slot, k], sem.at[dst_slot]).wait()

    @pl.when((tile == 0) & (f == 0))
    def _():
        for k in range(nf):
            def issue(i, c, k=k):
                row_copy(rows_ref[k * chunk + i], 0, k, i).start()
                return c

            lax.fori_loop(0, chunk, issue, 0, unroll=8)

    @pl.when(f == 0)
    def _():
        tile_wait(slot)
        for k in range(nf):
            xb[k * chunk:(k + 1) * chunk, :] = xbuf[slot, k].astype(BF16)
        acc[...] = jnp.zeros(acc.shape, F32)

    nxt = jnp.minimum(tile + 1, ntiles - 1)
    base = nxt * tm + f * chunk
    for i in range(chunk):
        row_copy(rows_ref[base + i], 1 - slot, f, i).start()

    x = xb[...]
    a = jnp.dot(x, wg_ref[0, 0].astype(BF16), preferred_element_type=F32)
    u = jnp.dot(x, wu_ref[0, 0].astype(BF16), preferred_element_type=F32)
    hh = (a * jax.nn.sigmoid(a) * u).astype(BF16)
    acc[...] += jnp.dot(hh, wd_ref[0, 0].astype(BF16), preferred_element_type=F32)

    @pl.when(f == nf - 1)
    def _():
        y_ref[0] = acc[...].astype(y_ref.dtype)

    @pl.when((tile == ntiles - 1) & (f == nf - 1))
    def _():
        tile_wait(1 - slot)


def _experts(rows_flat, h2, w_gate, w_up, w_down, layer, mtot, tm=1024, tf=512):
    _, e, d, ff = w_gate.shape
    tm = min(tm, mtot)
    tf = min(tf, ff)
    nm, nf = mtot // tm, ff // tf
    grid_spec = pltpu.PrefetchScalarGridSpec(
        num_scalar_prefetch=1,
        grid=(e, nm, nf),
        in_specs=[pl.BlockSpec(memory_space=pl.ANY),
                  pl.BlockSpec((1, 1, d, tf), lambda i, m, f, r: (layer, i, 0, f)),
                  pl.BlockSpec((1, 1, d, tf), lambda i, m, f, r: (layer, i, 0, f)),
                  pl.BlockSpec((1, 1, tf, d), lambda i, m, f, r: (layer, i, f, 0))],
        out_specs=pl.BlockSpec((1, tm, d), lambda i, m, f, r: (i, m, 0)),
        scratch_shapes=[pltpu.VMEM((2, nf, tm // nf, d), F32), pltpu.VMEM((tm, d), BF16), pltpu.VMEM((tm, d), F32),
                        pltpu.SemaphoreType.DMA((2,))],
    )
    return pl.pallas_call(
        functools.partial(_expert_kernel, tm=tm, nm=nm, nf=nf, ne=e),
        grid_spec=grid_spec,
        out_shape=jax.ShapeDtypeStruct((e, mtot, d), BF16),
        compiler_params=_cparams(("arbitrary", "arbitrary", "arbitrary")),
        name="ec_experts",
    )(rows_flat, h2, w_gate, w_up, w_down)


COMBINE_ROWS = 64
COMBINE_UNROLL = 8


def _combine_kernel(idx_ref, split_ref, x_ref, y_ref, gate_ref, *rest, ne, cap, span, final):
    if final:
        g_ref, o_ref, acc3, y3, gl = rest
    else:
        o_ref, acc3, y3, gl = rest
    b = pl.program_id(0)
    sp = pl.program_id(1)
    e = pl.program_id(2)
    d = x_ref.shape[2]
    sub = d // LANES
    rb = COMBINE_ROWS

    @pl.when(e == 0)
    def _():
        def load(c, carry):
            r0 = pl.multiple_of(c * rb, rb)
            acc3[pl.ds(pl.multiple_of(r0 * sub, rb * sub), rb * sub), :] = (
                x_ref[0, pl.ds(r0, rb), :].reshape(rb * sub, LANES))
            return carry

        lax.fori_loop(0, span // rb, load, 0)

    lo = split_ref[(b * ne + e) * 3 + sp]
    hi = split_ref[(b * ne + e) * 3 + sp + 1]

    def relayout(c, carry):
        r0 = pl.multiple_of(c * rb, rb)
        y3[pl.ds(r0, rb)] = y_ref[0, pl.ds(r0, rb), :].astype(F32).reshape(rb, sub, LANES)
        return carry

    lax.fori_loop(lo // rb, (hi + rb - 1) // rb, relayout, 0)

    col = lax.broadcasted_iota(I32, (1, gate_ref.shape[2]), 1)
    gl[...] = jnp.broadcast_to(jnp.sum(jnp.where(col == e, gate_ref[0], 0.0), axis=1, keepdims=True), gl.shape)

    base = (b * ne + e) * cap
    nu = COMBINE_UNROLL

    def tokens(first, count):
        return tuple(idx_ref[base + first + u] for u in range(count))

    def add_rows(first, toks):
        rows = [pl.ds(pl.multiple_of(t, sub), sub) for t in toks]
        vals = [acc3[r, :] + gl[pl.ds(first + u, 1), :] * y3[first + u] for u, r in enumerate(rows)]
        for r, v in zip(rows, vals):
            acc3[r, :] = v

    def group(k, toks):
        first = lo + k * nu
        nxt = tokens(jnp.minimum(first + nu, cap - nu), nu)
        add_rows(first, toks)
        return nxt

    ngroups = (hi - lo) // nu
    lax.fori_loop(0, ngroups, group, tokens(jnp.minimum(lo, cap - nu), nu))

    def tail(i, carry):
        add_rows(i, tokens(i, 1))
        return carry

    lax.fori_loop(lo + ngroups * nu, hi, tail, 0)

    @pl.when(e == ne - 1)
    def _():
        def store(c, carry):
            r0 = pl.multiple_of(c * rb, rb)
            v = acc3[pl.ds(pl.multiple_of(r0 * sub, rb * sub), rb * sub), :].reshape(rb, d)
            if final:
                v = _rms(v, g_ref[...])
            o_ref[0, pl.ds(r0, rb), :] = v
            return carry

        lax.fori_loop(0, span // rb, store, 0)


def _combine(idx, gate, x3, y, cap, final_g=None):
    b, seq, d = x3.shape
    ne = y.shape[0]
    span = seq // 2
    n_lower = jnp.sum((idx < span).astype(I32), axis=-1)
    split_flat = jnp.stack([jnp.zeros_like(n_lower), n_lower, jnp.full_like(n_lower, cap)], axis=-1).reshape(-1)
    final = final_g is not None
    in_specs = [pl.BlockSpec((1, span, d), lambda i, s, e, *_: (i, s, 0)),
                pl.BlockSpec((1, cap, d), lambda i, s, e, *_: (e, i, 0)),
                pl.BlockSpec((1, cap, ne), lambda i, s, e, *_: (i, 0, 0))]
    args = [x3, y, gate]
    if final:
        in_specs.append(pl.BlockSpec((1, d), lambda i, s, e, *_: (0, 0)))
        args.append(final_g)
    grid_spec = pltpu.PrefetchScalarGridSpec(
        num_scalar_prefetch=2,
        grid=(b, 2, ne),
        in_specs=in_specs,
        out_specs=pl.BlockSpec((1, span, d), lambda i, s, e, *_: (i, s, 0)),
        scratch_shapes=[pltpu.VMEM((span * (d // LANES), LANES), F32), pltpu.VMEM((cap, d // LANES, LANES), F32),
                        pltpu.VMEM((cap, LANES), F32)],
    )
    return pl.pallas_call(
        functools.partial(_combine_kernel, ne=ne, cap=cap, span=span, final=final),
        grid_spec=grid_spec,
        out_shape=jax.ShapeDtypeStruct((b, seq, d), F32),
        compiler_params=_cparams(("arbitrary", "arbitrary", "arbitrary")),
        name="ec_combine",
    )(((idx % span) * (d // LANES)).reshape(-1), split_flat, *args)


def _moe(x3, h2, aff_t, w_gate, w_up, w_down, layer, final_g=None):
    b, seq, d = x3.shape
    ne = w_gate.shape[1]
    cap = EC_CAPACITY * seq // ne
    idx, gate = _route(aff_t, b, seq, cap)
    rows = idx + (jnp.arange(b, dtype=I32) * seq)[:, None, None]
    rows_flat = rows.transpose(1, 0, 2).reshape(-1)
    y = _experts(rows_flat, h2, w_gate, w_up, w_down, layer, b * cap)
    return _combine(idx, gate, x3, y, cap, final_g)


def kernel(x, norm1_g, w_in, hy_short_w, hy_short_b, hy_f_w1, hy_f_b1, hy_f_w2, hy_f_b2, hy_f_wout, hy_f_freq, hy_skip, na_rpb, pool_w, pool_scale, mix_norm_g, w_out, norm2_g, w_router, w_gate, w_up, w_down, final_g):
    b, seq, d = x.shape
    depth = w_in.shape[0]
    hy = hy_skip.shape[1]
    pw = pool_scale.shape[1]
    naw = d - hy - pw
    n = b * seq
    rows = seq // GRID_W
    tabs = _dft_tables(seq // HY_B1)
    x2 = x.reshape(n, d)
    hhat = _hyena_filter_spectra(seq, hy, hy_f_w1, hy_f_b1, hy_f_w2, hy_f_b2, hy_f_wout, hy_f_freq, tabs)
    na_tables = _na_tables(na_rpb, rows)
    for i in range(depth):
        hy_in, qkv, pool_in = _inproj(x2, norm1_g[i][None], w_in[i].astype(BF16), 3 * hy, 3 * naw)
        y_hy = _hyena(hy_in.reshape(b, seq, 3 * hy), hy_short_w[i], hy_short_b[i], hy_skip[i], hhat, i, tabs)
        y_na = _natten(qkv.reshape(b, seq, 3 * naw), na_tables, i, b, seq, naw)
        y_pool = _pool(pool_in.reshape(b, seq, pw), pool_w[i], pool_scale[i])
        x2, h2, aff_t = _outproj(x2, y_hy.reshape(n, hy), y_na.reshape(n, naw), y_pool.reshape(n, pw),
                                 mix_norm_g[i][None], w_out[i].astype(BF16), norm2_g[i][None],
                                 jnp.stack(_split_bf16(w_router[i].T)))
        last = final_g[None] if i == depth - 1 else None
        x2 = _moe(x2.reshape(b, seq, d), h2, aff_t, w_gate, w_up, w_down, i, last).reshape(n, d)
    return x2.reshape(b, seq, d)
```

```python
import functools
import math

import numpy as np
import jax
import jax.numpy as jnp
from jax import lax
from jax.experimental import pallas as pl
from jax.experimental.pallas import tpu as pltpu

F32 = jnp.float32
BF16 = jnp.bfloat16
I32 = jnp.int32
EPS = 1e-6
HIGHEST = lax.Precision.HIGHEST

GRID_W = 64
NA_HEAD_DIM = 64
NA_KH_MAX = 8
NA_KW = 16
NA_GROUP_ROWS = 4
NA_KEY_ROWS = 12
POOL_WINDOWS = (2, 4, 8, 16)
POOL_PAD = 16
FILTER_EMB = 33
DECAY_FAST, DECAY_SLOW, DECAY_TARGET = 0.3, 1.5, 1e-2
EC_CAPACITY = 2
HY_B1 = 128
NEG = -1e30
LANES = 128
VMEM_LIMIT = 56 * 1024 * 1024


def _cparams(sem, vmem=VMEM_LIMIT):
    return pltpu.CompilerParams(dimension_semantics=sem, vmem_limit_bytes=vmem)


def _rms(v, g):
    return v * lax.rsqrt(jnp.mean(v * v, axis=-1, keepdims=True) + EPS) * g


def _split_bf16(v):
    hi = v.astype(BF16)
    return hi, (v - hi.astype(F32)).astype(BF16)


def _dot_x3(a_hi, a_lo, x, dims=(((1,), (0,)), ((), ()))):
    x_hi, x_lo = _split_bf16(x)
    dg = lambda p, q: lax.dot_general(p, q, dims, preferred_element_type=F32)
    return dg(a_hi, x_hi) + dg(a_lo, x_hi) + dg(a_hi, x_lo)


def _inproj_kernel(x_ref, g_ref, w_ref, hy_ref, qkv_ref, pool_ref, *, hyw, naw):
    h = _rms(x_ref[...], g_ref[...]).astype(BF16)
    hy_ref[...] = jnp.dot(h, w_ref[:, :hyw], preferred_element_type=F32)
    qkv_ref[...] = jnp.dot(h, w_ref[:, hyw:hyw + naw], preferred_element_type=F32).astype(BF16)
    pool_ref[...] = jnp.dot(h, w_ref[:, hyw + naw:], preferred_element_type=F32)


def _inproj(x2, g, w_bf, hyw, naw, tm=512):
    n, d = x2.shape
    inw = w_bf.shape[1]
    pw = inw - hyw - naw
    return pl.pallas_call(
        functools.partial(_inproj_kernel, hyw=hyw, naw=naw),
        grid=(n // tm,),
        in_specs=[pl.BlockSpec((tm, d), lambda i: (i, 0)),
                  pl.BlockSpec((1, d), lambda i: (0, 0)),
                  pl.BlockSpec((d, inw), lambda i: (0, 0))],
        out_specs=[pl.BlockSpec((tm, hyw), lambda i: (i, 0)),
                   pl.BlockSpec((tm, naw), lambda i: (i, 0)),
                   pl.BlockSpec((tm, pw), lambda i: (i, 0))],
        out_shape=[jax.ShapeDtypeStruct((n, hyw), F32),
                   jax.ShapeDtypeStruct((n, naw), BF16),
                   jax.ShapeDtypeStruct((n, pw), F32)],
        compiler_params=_cparams(("parallel",)),
        name="inproj",
    )(x2, g, w_bf)


def _hypre_kernel(u0_ref, u1_ref, u2_ref, w0_ref, w1_ref, w2_ref, b0_ref, b1_ref, b2_ref, z_ref, x0_ref):
    seq = u0_ref.shape[1]
    row = lax.broadcasted_iota(I32, (seq, 1), 0)

    def conv(u_ref, w_ref, b_ref):
        u = u_ref[0]
        prev = jnp.where(row == 0, 0.0, pltpu.roll(u, 1, 0))
        nxt = jnp.where(row == seq - 1, 0.0, pltpu.roll(u, seq - 1, 0))
        return prev * w_ref[0:1, :] + u * w_ref[1:2, :] + nxt * w_ref[2:3, :] + b_ref[...]

    x0_ref[0] = conv(u0_ref, w0_ref, b0_ref)
    z_ref[0] = conv(u2_ref, w2_ref, b2_ref) * conv(u1_ref, w1_ref, b1_ref)


def _hypre(hy_in, sw, sb, hy):
    b, seq, _ = hy_in.shape
    nc = hy // LANES
    cb = nc
    u_spec = lambda k: pl.BlockSpec((1, seq, LANES), lambda i, j, k=k: (i, 0, k * cb + j))
    w_spec = lambda k: pl.BlockSpec((3, LANES), lambda i, j, k=k: (0, k * cb + j))
    b_spec = lambda k: pl.BlockSpec((1, LANES), lambda i, j, k=k: (0, k * cb + j))
    o_spec = pl.BlockSpec((1, seq, LANES), lambda i, j: (i, 0, j))
    return pl.pallas_call(
        _hypre_kernel,
        grid=(b, nc),
        in_specs=[u_spec(0), u_spec(1), u_spec(2), w_spec(0), w_spec(1), w_spec(2),
                  b_spec(0), b_spec(1), b_spec(2)],
        out_specs=[o_spec, o_spec],
        out_shape=[jax.ShapeDtypeStruct((b, seq, hy), F32)] * 2,
        compiler_params=_cparams(("parallel", "parallel")),
        name="hyena_pre",
    )(hy_in, hy_in, hy_in, sw, sw, sw, sb, sb, sb)


def _filter_kernel(z_ref, w1_ref, b1_ref, w2_ref, b2_ref, wo_ref, fr_ref, dec_ref, o_ref):
    fr = fr_ref[0]
    h = jnp.sin(fr * (jnp.dot(z_ref[...], w1_ref[0], preferred_element_type=F32, precision=HIGHEST) + b1_ref[0]))
    h = jnp.sin(fr * (jnp.dot(h, w2_ref[0], preferred_element_type=F32, precision=HIGHEST) + b2_ref[0]))
    hw = jnp.dot(h, wo_ref[0], preferred_element_type=F32, precision=HIGHEST)
    c = dec_ref.shape[2]
    o_ref[0, 0] = hw[:, :c] * dec_ref[0]
    o_ref[0, 1] = hw[:, c:] * dec_ref[1]


def _filter_mlp(zemb, w1, b1, w2, b2, wo, fr, dec, tl=512):
    seq, emb = zemb.shape
    depth, _, hid = w1.shape
    ow = wo.shape[2]
    c = ow // 2
    lay = lambda s: pl.BlockSpec((1,) + s, lambda l, i: (l, 0, 0))
    return pl.pallas_call(
        _filter_kernel,
        grid=(depth, seq // tl),
        in_specs=[pl.BlockSpec((tl, emb), lambda l, i: (i, 0)), lay((emb, hid)), lay((1, hid)),
                  lay((hid, hid)), lay((1, hid)), lay((hid, ow)), lay((1, hid)),
                  pl.BlockSpec((2, tl, c), lambda l, i: (0, i, 0))],
        out_specs=pl.BlockSpec((1, 2, tl, c), lambda l, i: (l, 0, i, 0)),
        out_shape=jax.ShapeDtypeStruct((depth, 2, seq, c), F32),
        compiler_params=_cparams(("parallel", "parallel")),
        name="hyena_filter_mlp",
    )(zemb, w1, b1, w2, b2, wo, fr, dec)


def _filter_bdft_kernel(ff_ref, fc_ref, yf_ref, yb_ref, o_ref):
    o_ref[0, 0] = (_dot_x3(ff_ref[0], ff_ref[1], yf_ref[0, 0, 0]) + _dot_x3(fc_ref[0], fc_ref[1], yb_ref[0, 0, 0]))


def _filter_bdft(ff, fc, ya5, nka):
    depth, _, kap, r, c = ya5.shape
    blk = lambda p: pl.BlockSpec((1, 1, 1, r, c), lambda l, k, p=p: (l, p, k, 0, 0))
    return pl.pallas_call(
        _filter_bdft_kernel,
        grid=(depth, nka),
        in_specs=[pl.BlockSpec((2, r, r), lambda l, k: (0, 0, 0)), pl.BlockSpec((2, r, r), lambda l, k: (0, 0, 0)),
                  blk(0), blk(1)],
        out_specs=pl.BlockSpec((1, 1, r, c), lambda l, k: (l, k, 0, 0)),
        out_shape=jax.ShapeDtypeStruct((depth, nka, r, c), F32),
        compiler_params=_cparams(("parallel", "parallel")),
        name="hyena_filter_bdft",
    )(ff, fc, ya5, ya5)


def _hyadft_kernel(fa_ref, z_ref, y_ref, zt_ref, yt_ref):
    pieces, k2, a1 = fa_ref.shape
    zt_ref[...] = pltpu.einshape("abl->bal", z_ref[0].reshape(a1, HY_B1, LANES))

    def body(i, c):
        b0 = 2 * i
        zz = jnp.concatenate([zt_ref[b0], zt_ref[b0 + 1]], axis=1)
        if pieces == 1:
            r = jnp.dot(fa_ref[0], zz.astype(BF16), preferred_element_type=F32)
        else:
            r = _dot_x3(fa_ref[0], fa_ref[1], zz)
        yt_ref[b0] = r[:, :LANES]
        yt_ref[b0 + 1] = r[:, LANES:]
        return c

    lax.fori_loop(0, HY_B1 // 2, body, 0, unroll=8)
    y_ref[0] = pltpu.einshape("bkl->kbl", yt_ref[...]).reshape(k2 * HY_B1, LANES).astype(y_ref.dtype)


def _hyadft(fa, z, out_dtype=F32, name="hyena_adft"):
    b, seq, hy = z.shape
    pieces, k2, a1 = fa.shape
    return pl.pallas_call(
        _hyadft_kernel,
        grid=(b, hy // LANES),
        in_specs=[pl.BlockSpec((pieces, k2, a1), lambda i, j: (0, 0, 0)),
                  pl.BlockSpec((1, seq, LANES), lambda i, j: (i, 0, j))],
        out_specs=pl.BlockSpec((1, k2 * HY_B1, LANES), lambda i, j: (i, 0, j)),
        out_shape=jax.ShapeDtypeStruct((b, k2 * HY_B1, hy), out_dtype),
        scratch_shapes=[pltpu.VMEM((HY_B1, a1, LANES), F32), pltpu.VMEM((HY_B1, k2, LANES), F32)],
        compiler_params=_cparams(("parallel", "parallel")),
        name=name,
    )(fa, z)


def _hyfreq_kernel(y_ref, fb_ref, h_ref, fbi_ref, w_ref, *, nka):
    ka = pl.program_id(0)
    nb = fb_ref.shape[0] // 2
    nbatch = y_ref.shape[0]

    @pl.when(ka < nka)
    def _():
        hr, hi = h_ref[0, 0, :nb], h_ref[0, 0, nb:]
        for bi in range(nbatch):
            p = jnp.dot(fb_ref[...], y_ref[bi, 0].astype(BF16), preferred_element_type=F32)
            pr, pi = p[:nb], p[nb:]
            q = jnp.concatenate([pr * hr - pi * hi, pr * hi + pi * hr], axis=0).astype(BF16)
            w_ref[bi, 0] = jnp.dot(fbi_ref[...], q, preferred_element_type=F32).astype(w_ref.dtype)

    @pl.when(ka >= nka)
    def _():
        w_ref[...] = jnp.zeros(w_ref.shape, w_ref.dtype)


def _hyfreq(y4, fb, hhat, fbi, nka, layer):
    b, kap, r2, hy = y4.shape
    nb2 = fb.shape[0]
    return pl.pallas_call(
        functools.partial(_hyfreq_kernel, nka=nka),
        grid=(kap,),
        in_specs=[pl.BlockSpec((b, 1, r2, hy), lambda k: (0, k, 0, 0)),
                  pl.BlockSpec((nb2, r2), lambda k: (0, 0)),
                  pl.BlockSpec((1, 1, nb2, hy), lambda k: (layer, jnp.minimum(k, nka - 1), 0, 0)),
                  pl.BlockSpec((r2, nb2), lambda k: (0, 0))],
        out_specs=pl.BlockSpec((b, 1, r2, hy), lambda k: (0, k, 0, 0)),
        out_shape=jax.ShapeDtypeStruct((b, kap, r2, hy), BF16),
        compiler_params=_cparams(("parallel",)),
        name="hyena_freq",
    )(y4, fb, hhat, fbi)


def _hyout_kernel(a_ref, w_ref, z_ref, x0_ref, sk_ref, o_ref, wt_ref, zt_ref, xt_ref, ot_ref):
    a1, k2 = a_ref.shape
    ainv = a_ref[...]
    skip = sk_ref[...]
    wt_ref[...] = pltpu.einshape("kbl->bkl", w_ref[0].astype(F32).reshape(k2, HY_B1, LANES))
    zt_ref[...] = pltpu.einshape("abl->bal", z_ref[0].reshape(a1, HY_B1, LANES))
    xt_ref[...] = pltpu.einshape("abl->bal", x0_ref[0].reshape(a1, HY_B1, LANES))

    def body(i, c):
        b0 = 2 * i
        ww = jnp.concatenate([wt_ref[b0], wt_ref[b0 + 1]], axis=1).astype(BF16)
        y = jnp.dot(ainv, ww, preferred_element_type=F32)
        ot_ref[b0] = (y[:, :LANES] + zt_ref[b0] * skip) * xt_ref[b0]
        ot_ref[b0 + 1] = (y[:, LANES:] + zt_ref[b0 + 1] * skip) * xt_ref[b0 + 1]
        return c

    lax.fori_loop(0, HY_B1 // 2, body, 0, unroll=8)
    o_ref[0] = pltpu.einshape("bal->abl", ot_ref[...]).reshape(a1 * HY_B1, LANES)


def _hyout(ainv, w3, z, x0, skip):
    b, seq, hy = z.shape
    a1, k2 = ainv.shape
    blk = pl.BlockSpec((1, seq, LANES), lambda i, j: (i, 0, j))
    return pl.pallas_call(
        _hyout_kernel,
        grid=(b, hy // LANES),
        in_specs=[pl.BlockSpec((a1, k2), lambda i, j: (0, 0)),
                  pl.BlockSpec((1, k2 * HY_B1, LANES), lambda i, j: (i, 0, j)),
                  blk, blk,
                  pl.BlockSpec((1, LANES), lambda i, j: (0, j))],
        out_specs=blk,
        out_shape=jax.ShapeDtypeStruct((b, seq, hy), F32),
        scratch_shapes=[pltpu.VMEM((HY_B1, k2, LANES), F32)] + [pltpu.VMEM((HY_B1, a1, LANES), F32)] * 3,
        compiler_params=_cparams(("parallel", "parallel")),
        name="hyena_out",
    )(ainv, w3, z, x0, skip)


def _dft_tables(a1):
    a2, b2, b1 = 2 * a1, 2 * HY_B1, HY_B1
    nka = a1 + 1
    kap = -(-nka // 8) * 8
    ka = np.arange(nka)[:, None]
    def fa(na):
        ph = 2 * np.pi * ((ka * np.arange(na)[None, :]) % a2) / a2
        m = np.zeros((2 * kap, na))
        m[0:2 * nka:2] = np.cos(ph)
        m[1:2 * nka:2] = -np.sin(ph)
        return m
    kb = np.arange(b2)[:, None]
    th = 2 * np.pi * ((kb * np.arange(b2)[None, :]) % b2) / b2
    c, s = np.cos(th), np.sin(th)
    fb_full = np.block([[c, s], [-s, c]])
    fb_half = np.block([[c[:, :b1], s[:, :b1]], [-s[:, :b1], c[:, :b1]]])
    ct, st = c.T[:b1], s.T[:b1]
    fbi = np.block([[ct, -st], [st, ct]]) / b2
    ph = 2 * np.pi * ((np.arange(a1)[:, None] * np.arange(nka)[None, :]) % a2) / a2
    wgt = np.where((np.arange(nka) == 0) | (np.arange(nka) == a1), 1.0, 2.0)[None, :] / a2
    ainv = np.zeros((a1, 2 * kap))
    ainv[:, 0:2 * nka:2] = wgt * np.cos(ph)
    ainv[:, 1:2 * nka:2] = -wgt * np.sin(ph)
    fa2 = fa(a2)
    fa_filt2 = np.stack([fa2[:, 0:a1], fa2[:, 1:a1 + 1]], axis=1).reshape(4 * kap, a1)
    fb_conj = np.concatenate([fb_full[:b2], -fb_full[b2:]], axis=0)
    f32 = lambda v: np.asarray(v, np.float32)
    return dict(nka=nka, kap=kap, fa_data=f32(fa(a1)), fa_filt2=f32(fa_filt2), fb_full=f32(fb_full),
                fb_conj=f32(fb_conj), fb_half=f32(fb_half), fbi=f32(fbi), ainv=f32(ainv))


def _hyena_filter_spectra(seq, hy, w1, b1, w2, b2, wo, fr, tabs):
    nbands = (FILTER_EMB - 1) // 2
    t = jnp.linspace(0.0, 1.0, seq, dtype=F32)[:, None]
    ang = 2.0 * math.pi * jnp.arange(seq, dtype=F32)[:, None] / seq
    f = jnp.linspace(1e-4, nbands - 1, nbands, dtype=F32)[None, :]
    zemb = jnp.concatenate([t, jnp.cos(f * ang), -jnp.sin(f * ang)], axis=-1)
    deltas = jnp.abs(jnp.linspace(math.log(DECAY_TARGET) / DECAY_FAST,
                                  math.log(DECAY_TARGET) / DECAY_SLOW, hy, dtype=F32))
    decay = jnp.exp(-t * deltas)
    dec = jnp.stack([decay, decay * (jnp.arange(seq) > 0)[:, None].astype(F32)])
    depth = w1.shape[0]
    h = _filter_mlp(zemb, w1, b1[:, None], w2, b2[:, None], wo, fr[:, None], dec, tl=min(512, seq))
    pieces = lambda m: jnp.stack(_split_bf16(jnp.asarray(m)))
    ya = _hyadft(pieces(tabs["fa_filt2"]), h.reshape(depth * 2, seq, hy), name="hyena_filter_adft")
    ya5 = ya.reshape(depth, 2, tabs["kap"], 4 * HY_B1, hy)
    return _filter_bdft(pieces(tabs["fb_full"]), pieces(tabs["fb_conj"]), ya5, tabs["nka"])


def _hyena(hy_in, sw, sb, skip, hhat, layer, tabs):
    b, seq, hy3 = hy_in.shape
    hy = hy3 // 3
    a1 = seq // HY_B1
    kap, nka = tabs["kap"], tabs["nka"]
    z, x0 = _hypre(hy_in, sw, sb[None], hy)
    ya = _hyadft(jnp.asarray(tabs["fa_data"], BF16)[None], z, out_dtype=BF16)
    y4 = ya.reshape(b, kap, 2 * HY_B1, hy)
    w4 = _hyfreq(y4, jnp.asarray(tabs["fb_half"], BF16), hhat, jnp.asarray(tabs["fbi"], BF16), nka, layer)
    w3 = w4.reshape(b, 2 * kap * HY_B1, hy)
    return _hyout(jnp.asarray(tabs["ainv"], BF16), w3, z, x0, skip[None])


def _na_geometry(rows):
    gr, kr_n, kh = NA_GROUP_ROWS, NA_KEY_ROWS, NA_KH_MAX
    n_g = rows // gr
    geo = []
    for g in (0, 1, n_g - 1):
        ks = min(max(gr * g - kh // 2, 0), rows - kr_n)
        per_q = []
        for qr in range(gr):
            r = gr * g + qr
            rs = min(max(r - kh // 2, 0), rows - kh)
            per_q.append([((rs <= ks + k < rs + kh), ks + k - r + NA_KH_MAX - 1) for k in range(kr_n)])
        geo.append(per_q)
    return geo


def _na_table_kernel(r_ref, t_ref, *, geo):
    w, kw = GRID_W, NA_KW
    qc = lax.broadcasted_iota(I32, (w, 1), 0)
    lane = lax.broadcasted_iota(I32, (1, LANES), 1)
    kc = lane % w
    cs = jnp.clip(qc - kw // 2, 0, w - kw)
    colvalid = (kc >= cs) & (kc < cs + kw)
    left = lane < w
    neg = jnp.full((w, LANES), NEG, F32)
    shift = LANES - (kw - 1)

    def toeplitz(dr, lane_off):
        row = r_ref[0, 0, dr:dr + 1, :]
        if lane_off:
            row = pltpu.roll(row, lane_off, 1)
        return pltpu.roll(jnp.broadcast_to(row, (w, LANES)), shift, 1, stride=1, stride_axis=0)

    for v, per_q in enumerate(geo):
        for qr, per_k in enumerate(per_q):
            for pair in range(len(per_k) // 2):
                (ok0, dr0), (ok1, dr1) = per_k[2 * pair], per_k[2 * pair + 1]
                tile = neg
                if ok0:
                    tile = jnp.where(left & colvalid, toeplitz(dr0, 0), tile)
                if ok1:
                    tile = jnp.where((~left) & colvalid, toeplitz(dr1, w), tile)
                t_ref[0, v, 0, qr * w:(qr + 1) * w, pair * LANES:(pair + 1) * LANES] = tile


def _na_tables(rpb_all, rows):
    depth, heads, nr, nc = rpb_all.shape
    rp = jnp.pad(rpb_all.astype(F32), ((0, 0), (0, 0), (0, 16 - nr), (0, LANES - nc)))
    tq, tk = NA_GROUP_ROWS * GRID_W, NA_KEY_ROWS * GRID_W
    return pl.pallas_call(
        functools.partial(_na_table_kernel, geo=_na_geometry(rows)),
        grid=(depth, heads),
        in_specs=[pl.BlockSpec((1, 1, 16, LANES), lambda l, h: (l, h, 0, 0))],
        out_specs=pl.BlockSpec((1, 3, 1, tq, tk), lambda l, h: (l, 0, h, 0, 0)),
        out_shape=jax.ShapeDtypeStruct((depth, 3, heads, tq, tk), F32),
        compiler_params=_cparams(("parallel", "parallel")),
        name="natten_tables",
    )(rp)


def _natten_kernel(q_ref, k_ref, v_ref, t_ref, o_ref, *, heads, n_g, rows):
    g = pl.program_id(1)
    tq = q_ref.shape[1]
    tk = t_ref.shape[4]
    ks = jnp.clip(NA_GROUP_ROWS * g - NA_KH_MAX // 2, 0, rows - NA_KEY_ROWS)
    kstart = pl.multiple_of(ks * GRID_W, GRID_W)
    per_tile = LANES // NA_HEAD_DIM
    lane = lax.broadcasted_iota(I32, (1, LANES), 1)
    ones = jnp.ones((tk, LANES), BF16)
    for j in range(heads // per_tile):
        lanes = slice(j * LANES, (j + 1) * LANES)
        q2 = q_ref[0, :, lanes].astype(F32) * (NA_HEAD_DIM ** -0.5)
        k2 = k_ref[0, pl.ds(kstart, tk), lanes]
        vaug = jnp.concatenate([v_ref[0, pl.ds(kstart, tk), lanes], ones], axis=1)
        o2 = None
        for hh in range(per_tile):
            own = (lane >= hh * NA_HEAD_DIM) & (lane < (hh + 1) * NA_HEAD_DIM)
            qm = jnp.where(own, q2, 0.0).astype(BF16)
            s = lax.dot_general(qm, k2, (((1,), (1,)), ((), ())), preferred_element_type=F32)
            s = s + t_ref[0, 0, j * per_tile + hh]
            m = jnp.max(s, axis=-1, keepdims=True)
            p = jnp.exp((s - m).astype(BF16))
            r = jnp.dot(p, vaug, preferred_element_type=F32)
            o = r[:, :LANES] / r[:, LANES:]
            o2 = o if o2 is None else jnp.where(own, o, o2)
        o_ref[0, :, lanes] = o2.astype(o_ref.dtype)


def _natten(qkv, tables, layer, b, seq, naw):
    heads = naw // NA_HEAD_DIM
    rows = seq // GRID_W
    n_g = rows // NA_GROUP_ROWS
    tq = NA_GROUP_ROWS * GRID_W
    tk = NA_KEY_ROWS * GRID_W

    def tmap(i, g):
        return (layer, jnp.where(g == 0, 0, jnp.where(g == n_g - 1, 2, 1)), 0, 0, 0)

    return pl.pallas_call(
        functools.partial(_natten_kernel, heads=heads, n_g=n_g, rows=rows),
        grid=(b, n_g),
        in_specs=[pl.BlockSpec((1, tq, naw), lambda i, g: (i, g, 0)),
                  pl.BlockSpec((1, seq, naw), lambda i, g: (i, 0, 1)),
                  pl.BlockSpec((1, seq, naw), lambda i, g: (i, 0, 2)),
                  pl.BlockSpec((1, 1, heads, tq, tk), tmap)],
        out_specs=pl.BlockSpec((1, tq, naw), lambda i, g: (i, g, 0)),
        out_shape=jax.ShapeDtypeStruct((b, seq, naw), F32),
        compiler_params=_cparams(("parallel", "arbitrary")),
        name="natten",
    )(qkv, qkv, qkv, tables)


def _pool_kernel(u_ref, w_ref, sc_ref, o_ref, pad_ref, *, pw):
    seq = u_ref.shape[1]
    lp = seq + 2 * POOL_PAD
    j = pl.program_id(1)
    u = u_ref[0]
    zeros = jnp.zeros((POOL_PAD, LANES), F32)
    pad_ref[0:POOL_PAD, :] = zeros
    pad_ref[POOL_PAD + seq:lp, :] = zeros
    pad_ref[POOL_PAD:POOL_PAD + seq, :] = u
    xp = pad_ref[...]
    dn = lambda a, k: pltpu.roll(a, k, 0)
    up = lambda a, k: pltpu.roll(a, lp - k, 0)
    s2 = xp + dn(xp, 1)
    s4 = dn(s2, 1) + up(s2, 1)
    s8 = dn(s4, 2) + up(s4, 2)
    s16 = dn(s8, 4) + up(s8, 4)
    t = lax.broadcasted_iota(I32, (seq, 1), 0)
    lane = lax.broadcasted_iota(I32, (1, LANES), 1) + j * LANES
    gdim = pw // len(POOL_WINDOWS)
    grp = lane // gdim
    sums = (s2, s4, s8, s16)
    pooled = jnp.zeros((seq, LANES), F32)
    for gi, w in enumerate(POOL_WINDOWS):
        cnt = (jnp.minimum(t + w // 2, seq) - jnp.maximum(t - w // 2, 0)).astype(F32)
        mean = sums[gi][POOL_PAD:POOL_PAD + seq] / cnt
        pooled = jnp.where(grp == gi, mean, pooled)
    pooled = pooled - u
    y = jnp.dot(pooled.astype(BF16), w_ref[0], preferred_element_type=F32)
    o_ref[0] = y * sc_ref[...]


def _pool(u, pool_w, pool_scale):
    b, seq, pw = u.shape
    ng, gd, _ = pool_w.shape
    nh = pw // LANES
    per = LANES // gd
    wbd = jnp.zeros((nh, LANES, LANES), F32)
    for gi in range(ng):
        hh, k = divmod(gi, per)
        wbd = wbd.at[hh, k * gd:(k + 1) * gd, k * gd:(k + 1) * gd].set(pool_w[gi])
    return pl.pallas_call(
        functools.partial(_pool_kernel, pw=pw),
        grid=(b, nh),
        in_specs=[pl.BlockSpec((1, seq, LANES), lambda i, j: (i, 0, j)),
                  pl.BlockSpec((1, LANES, LANES), lambda i, j: (j, 0, 0)),
                  pl.BlockSpec((1, LANES), lambda i, j: (0, j))],
        out_specs=pl.BlockSpec((1, seq, LANES), lambda i, j: (i, 0, j)),
        out_shape=jax.ShapeDtypeStruct((b, seq, pw), F32),
        scratch_shapes=[pltpu.VMEM((seq + 2 * POOL_PAD, LANES), F32)],
        compiler_params=_cparams(("parallel", "parallel")),
        name="pool_mixer",
    )(u, wbd.astype(BF16), pool_scale[None])


def _outproj_kernel(x_ref, yh_ref, yn_ref, yp_ref, gm_ref, w_ref, g2_ref, wr_ref,
                    xo_ref, h_ref, aff_ref, *, hy, naw):
    gm = gm_ref[...]
    m1 = _rms(yh_ref[...], gm[:, :hy]).astype(BF16)
    m2 = _rms(yn_ref[...], gm[:, hy:hy + naw]).astype(BF16)
    m3 = _rms(yp_ref[...], gm[:, hy + naw:]).astype(BF16)
    acc = jnp.dot(m1, w_ref[:hy, :], preferred_element_type=F32)
    acc += jnp.dot(m2, w_ref[hy:hy + naw, :], preferred_element_type=F32)
    acc += jnp.dot(m3, w_ref[hy + naw:, :], preferred_element_type=F32)
    xn = x_ref[...] + acc
    xo_ref[...] = xn
    h = _rms(xn, g2_ref[...])
    h_ref[...] = h
    logits = _dot_x3(wr_ref[0], wr_ref[1], h, (((1,), (1,)), ((), ())))
    mx = jnp.max(logits, axis=0, keepdims=True)
    ex = jnp.exp(logits - mx)
    aff_ref[...] = ex / jnp.sum(ex, axis=0, keepdims=True)


def _outproj(x2, yh, yn, yp, gm, w_bf, g2, wr_t, tm=512):
    n, d = x2.shape
    hy, naw, pw = yh.shape[1], yn.shape[1], yp.shape[1]
    e = wr_t.shape[1]
    row = lambda c: pl.BlockSpec((tm, c), lambda i: (i, 0))
    full = lambda s: pl.BlockSpec(s, lambda i: (0, 0))
    return pl.pallas_call(
        functools.partial(_outproj_kernel, hy=hy, naw=naw),
        grid=(n // tm,),
        in_specs=[row(d), row(hy), row(naw), row(pw), full((1, d)), full((d, d)), full((1, d)),
                  pl.BlockSpec((2, e, d), lambda i: (0, 0, 0))],
        out_specs=[row(d), row(d), pl.BlockSpec((e, tm), lambda i: (0, i))],
        out_shape=[jax.ShapeDtypeStruct((n, d), F32), jax.ShapeDtypeStruct((n, d), F32),
                   jax.ShapeDtypeStruct((e, n), F32)],
        compiler_params=_cparams(("parallel",)),
        name="outproj_router",
    )(x2, yh, yn, yp, gm, w_bf, g2, wr_t)


def _block_cumsum(x, tri):
    r, n = x.shape
    cls, offs = [], []
    off = jnp.zeros((r, 1), F32)
    for j in range(n // LANES):
        c = jnp.dot(x[:, j * LANES:(j + 1) * LANES], tri, preferred_element_type=F32)
        cls.append(c)
        off = off + c[:, LANES - 1:LANES]
        offs.append(off)
    return cls, offs


def _route_kernel(aff_ref, tri_ref, bci_ref, bcx_ref, idx_ref, gate_ref, blk_ref, *, cap):
    aff = aff_ref[...]
    e, seq = aff.shape
    nblk = seq // LANES
    bits = pltpu.bitcast(aff, I32)
    capf = jnp.float32(cap)

    def radix(i, prefix):
        cand = prefix | jnp.left_shift(jnp.int32(1), 30 - i)
        cnt = jnp.sum((bits >= cand).astype(F32), axis=1, keepdims=True)
        return jnp.where(cnt >= capf, cand, prefix)

    tau = lax.fori_loop(0, 31, radix, jnp.zeros((e, 1), I32))
    gt = bits > tau
    eq = bits == tau
    need = capf - jnp.sum(gt.astype(F32), axis=1, keepdims=True)
    tri = tri_ref[...]
    cls, offs = _block_cumsum(jnp.where(eq, 1.0, 0.0).astype(BF16), tri)
    tie_rank = jnp.concatenate([c if j == 0 else c + offs[j - 1] for j, c in enumerate(cls)], axis=1)
    sel = gt | (eq & (tie_rank <= need))
    self32 = jnp.where(sel, 1.0, 0.0)
    selb = self32.astype(BF16)
    cls, _ = _block_cumsum(selb, tri)
    bend = jnp.dot(selb, bci_ref[...], preferred_element_type=F32)
    bstart = jnp.dot(selb, bcx_ref[...], preferred_element_type=F32)
    for j in range(nblk):
        rows = slice(j * e, (j + 1) * e)
        lanes = slice(j * LANES, (j + 1) * LANES)
        blk_ref[0, rows, :] = cls[j]
        blk_ref[1, rows, :] = self32[:, lanes]
        blk_ref[2, rows, :] = aff[:, lanes]
    slot = lax.broadcasted_iota(I32, (cap, 1), 0).astype(F32)
    lane = lax.broadcasted_iota(I32, (1, LANES), 1).astype(F32)
    for ei in range(e):
        bs, be = bstart[ei:ei + 1, :], bend[ei:ei + 1, :]
        inblk = (bs <= slot) & (slot < be)
        local = slot - jnp.sum(jnp.where(inblk, bs, 0.0), axis=1, keepdims=True)
        jcol = jnp.sum(jnp.where(inblk, lane, 0.0), axis=1, keepdims=True)
        pick = jnp.where(inblk, 1.0, 0.0)[:, :nblk].astype(BF16)
        rows = pl.ds(ei, nblk, stride=e)
        a = blk_ref[2, rows, :]
        a_hi = a.astype(BF16)
        r1 = a - a_hi.astype(F32)
        a_mid = r1.astype(BF16)
        a_lo = (r1 - a_mid.astype(F32)).astype(BF16)
        take = lambda v: jnp.dot(pick, v, preferred_element_type=F32)
        g_cl = take(blk_ref[0, rows, :].astype(BF16))
        g_sel = take(blk_ref[1, rows, :].astype(BF16))
        g_aff = take(a_hi) + take(a_mid) + take(a_lo)
        hit = (g_cl == local + 1.0) & (g_sel > 0.5)
        idx = jcol * LANES + jnp.sum(jnp.where(hit, lane, 0.0), axis=1, keepdims=True)
        gate = jnp.sum(jnp.where(hit, g_aff, 0.0), axis=1, keepdims=True)
        idx_ref[0, :, ei:ei + 1] = idx.astype(I32)
        gate_ref[0, ei] = jnp.broadcast_to(gate, (cap, LANES))


def _route(aff_t, b, seq, cap):
    e = aff_t.shape[0]
    nblk = seq // LANES
    tri = jnp.asarray(np.triu(np.ones((LANES, LANES), np.float32)), BF16)
    tblk = np.arange(seq)[:, None] // LANES
    bci = jnp.asarray(tblk <= np.arange(LANES)[None, :], BF16)
    bcx = jnp.asarray(tblk < np.arange(LANES)[None, :], BF16)
    full = lambda s: pl.BlockSpec(s, lambda i: (0, 0))
    idx, gate = pl.pallas_call(
        functools.partial(_route_kernel, cap=cap),
        grid=(b,),
        in_specs=[pl.BlockSpec((e, seq), lambda i: (0, i)), full((LANES, LANES)),
                  full((seq, LANES)), full((seq, LANES))],
        out_specs=[pl.BlockSpec((1, cap, e), lambda i: (i, 0, 0)),
                   pl.BlockSpec((1, e, cap, LANES), lambda i: (i, 0, 0, 0))],
        out_shape=[jax.ShapeDtypeStruct((b, cap, e), I32), jax.ShapeDtypeStruct((b, e, cap, LANES), F32)],
        scratch_shapes=[pltpu.VMEM((3, nblk * e, LANES), F32)],
        compiler_params=_cparams(("parallel",)),
        name="ec_route",
    )(aff_t, tri, bci, bcx)
    return idx.transpose(0, 2, 1), gate


def _expert_kernel(rows_ref, h_hbm, wg_ref, wu_ref, wd_ref, y_ref, xbuf, xb, acc, sem, *, tm, nm, nf, ne):
    e = pl.program_id(0)
    m = pl.program_id(1)
    f = pl.program_id(2)
    tile = e * nm + m
    ntiles = ne * nm
    slot = tile % 2
    chunk = tm // nf

    def row_copy(src_row, dst_slot, dst_chunk, dst_row):
        return pltpu.make_async_copy(h_hbm.at[pl.ds(src_row, 1), :],
                                     xbuf.at[dst_slot, dst_chunk, pl.ds(dst_row, 1), :], sem.at[dst_slot])

    def tile_wait(dst_slot):
        for k in range(nf):
            pltpu.make_async_copy(h_hbm.at[pl.ds(0, chunk), :], xbuf.at[dst_slot, k], sem.at[dst_slot]).wait()

    @pl.when((tile == 0) & (f == 0))
    def _():
        for k in range(nf):
            def issue(i, c, k=k):
                row_copy(rows_ref[k * chunk + i], 0, k, i).start()
                return c

            lax.fori_loop(0, chunk, issue, 0, unroll=8)

    @pl.when(f == 0)
    def _():
        tile_wait(slot)
        for k in range(nf):
            xb[k * chunk:(k + 1) * chunk, :] = xbuf[slot, k].astype(BF16)
        acc[...] = jnp.zeros(acc.shape, F32)

    nxt = jnp.minimum(tile + 1, ntiles - 1)
    base = nxt * tm + f * chunk
    for i in range(chunk):
        row_copy(rows_ref[base + i], 1 - slot, f, i).start()

    x = xb[...]
    a = jnp.dot(x, wg_ref[0, 0].astype(BF16), preferred_element_type=F32)
    u = jnp.dot(x, wu_ref[0, 0].astype(BF16), preferred_element_type=F32)
    hh = (a * jax.nn.sigmoid(a) * u).astype(BF16)
    acc[...] += jnp.dot(hh, wd_ref[0, 0].astype(BF16), preferred_element_type=F32)

    @pl.when(f == nf - 1)
    def _():
        y_ref[0] = acc[...].astype(y_ref.dtype)

    @pl.when((tile == ntiles - 1) & (f == nf - 1))
    def _():
        tile_wait(1 - slot)


def _experts(rows_flat, h2, w_gate, w_up, w_down, layer, mtot, tm=1024, tf=512):
    _, e, d, ff = w_gate.shape
    tm = min(tm, mtot)
    tf = min(tf, ff)
    nm, nf = mtot // tm, ff // tf
    grid_spec = pltpu.PrefetchScalarGridSpec(
        num_scalar_prefetch=1,
        grid=(e, nm, nf),
        in_specs=[pl.BlockSpec(memory_space=pl.ANY),
                  pl.BlockSpec((1, 1, d, tf), lambda i, m, f, r: (layer, i, 0, f)),
                  pl.BlockSpec((1, 1, d, tf), lambda i, m, f, r: (layer, i, 0, f)),
                  pl.BlockSpec((1, 1, tf, d), lambda i, m, f, r: (layer, i, f, 0))],
        out_specs=pl.BlockSpec((1, tm, d), lambda i, m, f, r: (i, m, 0)),
        scratch_shapes=[pltpu.VMEM((2, nf, tm // nf, d), F32), pltpu.VMEM((tm, d), BF16), pltpu.VMEM((tm, d), F32),
                        pltpu.SemaphoreType.DMA((2,))],
    )
    return pl.pallas_call(
        functools.partial(_expert_kernel, tm=tm, nm=nm, nf=nf, ne=e),
        grid_spec=grid_spec,
        out_shape=jax.ShapeDtypeStruct((e, mtot, d), BF16),
        compiler_params=_cparams(("arbitrary", "arbitrary", "arbitrary")),
        name="ec_experts",
    )(rows_flat, h2, w_gate, w_up, w_down)


COMBINE_ROWS = 64
COMBINE_UNROLL = 8


def _combine_kernel(idx_ref, split_ref, x_ref, y_ref, gl, *rest, ne, cap, span, final):
    if final:
        g_ref, o_ref, acc3, y3 = rest
    else:
        o_ref, acc3, y3 = rest
    b = pl.program_id(0)
    sp = pl.program_id(1)
    e = pl.program_id(2)
    d = x_ref.shape[2]
    sub = d // LANES
    rb = COMBINE_ROWS

    @pl.when(e == 0)
    def _():
        def load(c, carry):
            r0 = pl.multiple_of(c * rb, rb)
            acc3[pl.ds(pl.multiple_of(r0 * sub, rb * sub), rb * sub), :] = (
                x_ref[0, pl.ds(r0, rb), :].reshape(rb * sub, LANES))
            return carry

        lax.fori_loop(0, span // rb, load, 0)

    lo = split_ref[(b * ne + e) * 3 + sp]
    hi = split_ref[(b * ne + e) * 3 + sp + 1]

    def relayout(c, carry):
        r0 = pl.multiple_of(c * rb, rb)
        y3[pl.ds(r0, rb)] = y_ref[0, pl.ds(r0, rb), :].astype(F32).reshape(rb, sub, LANES)
        return carry

    lax.fori_loop(lo // rb, (hi + rb - 1) // rb, relayout, 0)

    base = (b * ne + e) * cap
    nu = COMBINE_UNROLL

    def tokens(first, count):
        return tuple(idx_ref[base + first + u] for u in range(count))

    def add_rows(first, toks):
        rows = [pl.ds(pl.multiple_of(t, sub), sub) for t in toks]
        vals = [acc3[r, :] + gl[0, 0, pl.ds(first + u, 1), :] * y3[first + u] for u, r in enumerate(rows)]
        for r, v in zip(rows, vals):
            acc3[r, :] = v

    def group(k, toks):
        first = lo + k * nu
        nxt = tokens(jnp.minimum(first + nu, cap - nu), nu)
        add_rows(first, toks)
        return nxt

    ngroups = (hi - lo) // nu
    lax.fori_loop(0, ngroups, group, tokens(jnp.minimum(lo, cap - nu), nu))

    def tail(i, carry):
        add_rows(i, tokens(i, 1))
        return carry

    lax.fori_loop(lo + ngroups * nu, hi, tail, 0)

    @pl.when(e == ne - 1)
    def _():
        def store(c, carry):
            r0 = pl.multiple_of(c * rb, rb)
            v = acc3[pl.ds(pl.multiple_of(r0 * sub, rb * sub), rb * sub), :].reshape(rb, d)
            if final:
                v = _rms(v, g_ref[...])
            o_ref[0, pl.ds(r0, rb), :] = v
            return carry

        lax.fori_loop(0, span // rb, store, 0)


def _combine(idx, gate, x3, y, cap, final_g=None):
    b, seq, d = x3.shape
    ne = y.shape[0]
    span = seq // 2
    n_lower = jnp.sum((idx < span).astype(I32), axis=-1)
    split_flat = jnp.stack([jnp.zeros_like(n_lower), n_lower, jnp.full_like(n_lower, cap)], axis=-1).reshape(-1)
    final = final_g is not None
    in_specs = [pl.BlockSpec((1, span, d), lambda i, s, e, *_: (i, s, 0)),
                pl.BlockSpec((1, cap, d), lambda i, s, e, *_: (e, i, 0)),
                pl.BlockSpec((1, 1, cap, LANES), lambda i, s, e, *_: (i, e, 0, 0))]
    args = [x3, y, gate]
    if final:
        in_specs.append(pl.BlockSpec((1, d), lambda i, s, e, *_: (0, 0)))
        args.append(final_g)
    grid_spec = pltpu.PrefetchScalarGridSpec(
        num_scalar_prefetch=2,
        grid=(b, 2, ne),
        in_specs=in_specs,
        out_specs=pl.BlockSpec((1, span, d), lambda i, s, e, *_: (i, s, 0)),
        scratch_shapes=[pltpu.VMEM((span * (d // LANES), LANES), F32), pltpu.VMEM((cap, d // LANES, LANES), F32)],
    )
    return pl.pallas_call(
        functools.partial(_combine_kernel, ne=ne, cap=cap, span=span, final=final),
        grid_spec=grid_spec,
        out_shape=jax.ShapeDtypeStruct((b, seq, d), F32),
        compiler_params=_cparams(("arbitrary", "arbitrary", "arbitrary")),
        name="ec_combine",
    )(((idx % span) * (d // LANES)).reshape(-1), split_flat, *args)


def _moe(x3, h2, aff_t, w_gate, w_up, w_down, layer, final_g=None):
    b, seq, d = x3.shape
    ne = w_gate.shape[1]
    cap = EC_CAPACITY * seq // ne
    idx, gate = _route(aff_t, b, seq, cap)
    rows = idx + (jnp.arange(b, dtype=I32) * seq)[:, None, None]
    rows_flat = rows.transpose(1, 0, 2).reshape(-1)
    y = _experts(rows_flat, h2, w_gate, w_up, w_down, layer, b * cap)
    return _combine(idx, gate, x3, y, cap, final_g)


def kernel(x, norm1_g, w_in, hy_short_w, hy_short_b, hy_f_w1, hy_f_b1, hy_f_w2, hy_f_b2, hy_f_wout, hy_f_freq, hy_skip, na_rpb, pool_w, pool_scale, mix_norm_g, w_out, norm2_g, w_router, w_gate, w_up, w_down, final_g):
    b, seq, d = x.shape
    depth = w_in.shape[0]
    hy = hy_skip.shape[1]
    pw = pool_scale.shape[1]
    naw = d - hy - pw
    n = b * seq
    rows = seq // GRID_W
    tabs = _dft_tables(seq // HY_B1)
    x2 = x.reshape(n, d)
    hhat = _hyena_filter_spectra(seq, hy, hy_f_w1, hy_f_b1, hy_f_w2, hy_f_b2, hy_f_wout, hy_f_freq, tabs)
    na_tables = _na_tables(na_rpb, rows)
    for i in range(depth):
        hy_in, qkv, pool_in = _inproj(x2, norm1_g[i][None], w_in[i].astype(BF16), 3 * hy, 3 * naw)
        y_hy = _hyena(hy_in.reshape(b, seq, 3 * hy), hy_short_w[i], hy_short_b[i], hy_skip[i], hhat, i, tabs)
        y_na = _natten(qkv.reshape(b, seq, 3 * naw), na_tables, i, b, seq, naw)
        y_pool = _pool(pool_in.reshape(b, seq, pw), pool_w[i], pool_scale[i])
        x2, h2, aff_t = _outproj(x2, y_hy.reshape(n, hy), y_na.reshape(n, naw), y_pool.reshape(n, pw),
                                 mix_norm_g[i][None], w_out[i].astype(BF16), norm2_g[i][None],
                                 jnp.stack(_split_bf16(w_router[i].T)))
        last = final_g[None] if i == depth - 1 else None
        x2 = _moe(x2.reshape(b, seq, d), h2, aff_t, w_gate, w_up, w_down, i, last).reshape(n, d)
    return x2.reshape(b, seq, d)
```

```python
import functools
import math

import numpy as np
import jax
import jax.numpy as jnp
from jax import lax
from jax.experimental import pallas as pl
from jax.experimental.pallas import tpu as pltpu

F32 = jnp.float32
BF16 = jnp.bfloat16
I32 = jnp.int32
EPS = 1e-6
HIGHEST = lax.Precision.HIGHEST

GRID_W = 64
NA_HEAD_DIM = 64
NA_KH_MAX = 8
NA_KW = 16
NA_GROUP_ROWS = 4
NA_KEY_ROWS = 12
POOL_WINDOWS = (2, 4, 8, 16)
POOL_PAD = 16
FILTER_EMB = 33
DECAY_FAST, DECAY_SLOW, DECAY_TARGET = 0.3, 1.5, 1e-2
EC_CAPACITY = 2
HY_B1 = 128
NEG = -1e30
LANES = 128
VMEM_LIMIT = 56 * 1024 * 1024


def _cparams(sem, vmem=VMEM_LIMIT):
    return pltpu.CompilerParams(dimension_semantics=sem, vmem_limit_bytes=vmem)


def _rms(v, g):
    return v * lax.rsqrt(jnp.mean(v * v, axis=-1, keepdims=True) + EPS) * g


def _split_bf16(v):
    hi = v.astype(BF16)
    return hi, (v - hi.astype(F32)).astype(BF16)


def _dot_x3(a_hi, a_lo, x, dims=(((1,), (0,)), ((), ()))):
    x_hi, x_lo = _split_bf16(x)
    dg = lambda p, q: lax.dot_general(p, q, dims, preferred_element_type=F32)
    return dg(a_hi, x_hi) + dg(a_lo, x_hi) + dg(a_hi, x_lo)


def _inproj_kernel(x_ref, g_ref, w_ref, hy_ref, qkv_ref, pool_ref, *, hyw, naw):
    h = _rms(x_ref[...], g_ref[...]).astype(BF16)
    hy_ref[...] = jnp.dot(h, w_ref[:, :hyw], preferred_element_type=F32)
    qkv_ref[...] = jnp.dot(h, w_ref[:, hyw:hyw + naw], preferred_element_type=F32).astype(BF16)
    pool_ref[...] = jnp.dot(h, w_ref[:, hyw + naw:], preferred_element_type=F32)


def _inproj(x2, g, w_bf, hyw, naw, tm=512):
    n, d = x2.shape
    inw = w_bf.shape[1]
    pw = inw - hyw - naw
    return pl.pallas_call(
        functools.partial(_inproj_kernel, hyw=hyw, naw=naw),
        grid=(n // tm,),
        in_specs=[pl.BlockSpec((tm, d), lambda i: (i, 0)),
                  pl.BlockSpec((1, d), lambda i: (0, 0)),
                  pl.BlockSpec((d, inw), lambda i: (0, 0))],
        out_specs=[pl.BlockSpec((tm, hyw), lambda i: (i, 0)),
                   pl.BlockSpec((tm, naw), lambda i: (i, 0)),
                   pl.BlockSpec((tm, pw), lambda i: (i, 0))],
        out_shape=[jax.ShapeDtypeStruct((n, hyw), F32),
                   jax.ShapeDtypeStruct((n, naw), BF16),
                   jax.ShapeDtypeStruct((n, pw), F32)],
        compiler_params=_cparams(("parallel",)),
        name="inproj",
    )(x2, g, w_bf)


def _hypre_kernel(u0_ref, u1_ref, u2_ref, w0_ref, w1_ref, w2_ref, b0_ref, b1_ref, b2_ref, z_ref, x0_ref):
    seq = u0_ref.shape[1]
    row = lax.broadcasted_iota(I32, (seq, 1), 0)

    def conv(u_ref, w_ref, b_ref):
        u = u_ref[0]
        prev = jnp.where(row == 0, 0.0, pltpu.roll(u, 1, 0))
        nxt = jnp.where(row == seq - 1, 0.0, pltpu.roll(u, seq - 1, 0))
        return prev * w_ref[0:1, :] + u * w_ref[1:2, :] + nxt * w_ref[2:3, :] + b_ref[...]

    x0_ref[0] = conv(u0_ref, w0_ref, b0_ref)
    z_ref[0] = conv(u2_ref, w2_ref, b2_ref) * conv(u1_ref, w1_ref, b1_ref)


def _hypre(hy_in, sw, sb, hy):
    b, seq, _ = hy_in.shape
    nc = hy // LANES
    cb = nc
    u_spec = lambda k: pl.BlockSpec((1, seq, LANES), lambda i, j, k=k: (i, 0, k * cb + j))
    w_spec = lambda k: pl.BlockSpec((3, LANES), lambda i, j, k=k: (0, k * cb + j))
    b_spec = lambda k: pl.BlockSpec((1, LANES), lambda i, j, k=k: (0, k * cb + j))
    o_spec = pl.BlockSpec((1, seq, LANES), lambda i, j: (i, 0, j))
    return pl.pallas_call(
        _hypre_kernel,
        grid=(b, nc),
        in_specs=[u_spec(0), u_spec(1), u_spec(2), w_spec(0), w_spec(1), w_spec(2),
                  b_spec(0), b_spec(1), b_spec(2)],
        out_specs=[o_spec, o_spec],
        out_shape=[jax.ShapeDtypeStruct((b, seq, hy), F32)] * 2,
        compiler_params=_cparams(("parallel", "parallel")),
        name="hyena_pre",
    )(hy_in, hy_in, hy_in, sw, sw, sw, sb, sb, sb)


def _filter_kernel(z_ref, w1_ref, b1_ref, w2_ref, b2_ref, wo_ref, fr_ref, dec_ref, o_ref):
    fr = fr_ref[0]
    h = jnp.sin(fr * (jnp.dot(z_ref[...], w1_ref[0], preferred_element_type=F32, precision=HIGHEST) + b1_ref[0]))
    h = jnp.sin(fr * (jnp.dot(h, w2_ref[0], preferred_element_type=F32, precision=HIGHEST) + b2_ref[0]))
    hw = jnp.dot(h, wo_ref[0], preferred_element_type=F32, precision=HIGHEST)
    c = dec_ref.shape[2]
    o_ref[0, 0] = hw[:, :c] * dec_ref[0]
    o_ref[0, 1] = hw[:, c:] * dec_ref[1]


def _filter_mlp(zemb, w1, b1, w2, b2, wo, fr, dec, tl=512):
    seq, emb = zemb.shape
    depth, _, hid = w1.shape
    ow = wo.shape[2]
    c = ow // 2
    lay = lambda s: pl.BlockSpec((1,) + s, lambda l, i: (l, 0, 0))
    return pl.pallas_call(
        _filter_kernel,
        grid=(depth, seq // tl),
        in_specs=[pl.BlockSpec((tl, emb), lambda l, i: (i, 0)), lay((emb, hid)), lay((1, hid)),
                  lay((hid, hid)), lay((1, hid)), lay((hid, ow)), lay((1, hid)),
                  pl.BlockSpec((2, tl, c), lambda l, i: (0, i, 0))],
        out_specs=pl.BlockSpec((1, 2, tl, c), lambda l, i: (l, 0, i, 0)),
        out_shape=jax.ShapeDtypeStruct((depth, 2, seq, c), F32),
        compiler_params=_cparams(("parallel", "parallel")),
        name="hyena_filter_mlp",
    )(zemb, w1, b1, w2, b2, wo, fr, dec)


def _filter_bdft_kernel(ff_ref, fc_ref, yf_ref, yb_ref, o_ref):
    o_ref[0, 0] = (_dot_x3(ff_ref[0], ff_ref[1], yf_ref[0, 0, 0]) + _dot_x3(fc_ref[0], fc_ref[1], yb_ref[0, 0, 0]))


def _filter_bdft(ff, fc, ya5, nka):
    depth, _, kap, r, c = ya5.shape
    blk = lambda p: pl.BlockSpec((1, 1, 1, r, c), lambda l, k, p=p: (l, p, k, 0, 0))
    return pl.pallas_call(
        _filter_bdft_kernel,
        grid=(depth, nka),
        in_specs=[pl.BlockSpec((2, r, r), lambda l, k: (0, 0, 0)), pl.BlockSpec((2, r, r), lambda l, k: (0, 0, 0)),
                  blk(0), blk(1)],
        out_specs=pl.BlockSpec((1, 1, r, c), lambda l, k: (l, k, 0, 0)),
        out_shape=jax.ShapeDtypeStruct((depth, nka, r, c), F32),
        compiler_params=_cparams(("parallel", "parallel")),
        name="hyena_filter_bdft",
    )(ff, fc, ya5, ya5)


def _hyadft_kernel(fa_ref, z_ref, y_ref, zt_ref, yt_ref):
    pieces, k2, a1 = fa_ref.shape
    zt_ref[...] = pltpu.einshape("abl->bal", z_ref[0].reshape(a1, HY_B1, LANES))

    def body(i, c):
        b0 = 2 * i
        zz = jnp.concatenate([zt_ref[b0], zt_ref[b0 + 1]], axis=1)
        if pieces == 1:
            r = jnp.dot(fa_ref[0], zz.astype(BF16), preferred_element_type=F32)
        else:
            r = _dot_x3(fa_ref[0], fa_ref[1], zz)
        yt_ref[b0] = r[:, :LANES]
        yt_ref[b0 + 1] = r[:, LANES:]
        return c

    lax.fori_loop(0, HY_B1 // 2, body, 0, unroll=8)
    y_ref[0] = pltpu.einshape("bkl->kbl", yt_ref[...]).reshape(k2 * HY_B1, LANES).astype(y_ref.dtype)


def _hyadft(fa, z, out_dtype=F32, name="hyena_adft"):
    b, seq, hy = z.shape
    pieces, k2, a1 = fa.shape
    return pl.pallas_call(
        _hyadft_kernel,
        grid=(b, hy // LANES),
        in_specs=[pl.BlockSpec((pieces, k2, a1), lambda i, j: (0, 0, 0)),
                  pl.BlockSpec((1, seq, LANES), lambda i, j: (i, 0, j))],
        out_specs=pl.BlockSpec((1, k2 * HY_B1, LANES), lambda i, j: (i, 0, j)),
        out_shape=jax.ShapeDtypeStruct((b, k2 * HY_B1, hy), out_dtype),
        scratch_shapes=[pltpu.VMEM((HY_B1, a1, LANES), F32), pltpu.VMEM((HY_B1, k2, LANES), F32)],
        compiler_params=_cparams(("parallel", "parallel")),
        name=name,
    )(fa, z)


def _hyfreq_kernel(y_ref, fb_ref, h_ref, fbi_ref, w_ref, *, nka):
    ka = pl.program_id(0)
    nb = fb_ref.shape[0] // 2
    nbatch = y_ref.shape[0]

    @pl.when(ka < nka)
    def _():
        hr, hi = h_ref[0, 0, :nb], h_ref[0, 0, nb:]
        for bi in range(nbatch):
            p = jnp.dot(fb_ref[...], y_ref[bi, 0].astype(BF16), preferred_element_type=F32)
            pr, pi = p[:nb], p[nb:]
            q = jnp.concatenate([pr * hr - pi * hi, pr * hi + pi * hr], axis=0).astype(BF16)
            w_ref[bi, 0] = jnp.dot(fbi_ref[...], q, preferred_element_type=F32).astype(w_ref.dtype)

    @pl.when(ka >= nka)
    def _():
        w_ref[...] = jnp.zeros(w_ref.shape, w_ref.dtype)


def _hyfreq(y4, fb, hhat, fbi, nka, layer):
    b, kap, r2, hy = y4.shape
    nb2 = fb.shape[0]
    return pl.pallas_call(
        functools.partial(_hyfreq_kernel, nka=nka),
        grid=(kap,),
        in_specs=[pl.BlockSpec((b, 1, r2, hy), lambda k: (0, k, 0, 0)),
                  pl.BlockSpec((nb2, r2), lambda k: (0, 0)),
                  pl.BlockSpec((1, 1, nb2, hy), lambda k: (layer, jnp.minimum(k, nka - 1), 0, 0)),
                  pl.BlockSpec((r2, nb2), lambda k: (0, 0))],
        out_specs=pl.BlockSpec((b, 1, r2, hy), lambda k: (0, k, 0, 0)),
        out_shape=jax.ShapeDtypeStruct((b, kap, r2, hy), BF16),
        compiler_params=_cparams(("parallel",)),
        name="hyena_freq",
    )(y4, fb, hhat, fbi)


def _hyout_kernel(a_ref, w_ref, z_ref, x0_ref, sk_ref, o_ref, wt_ref, zt_ref, xt_ref, ot_ref):
    a1, k2 = a_ref.shape
    ainv = a_ref[...]
    skip = sk_ref[...]
    wt_ref[...] = pltpu.einshape("kbl->bkl", w_ref[0].astype(F32).reshape(k2, HY_B1, LANES))
    zt_ref[...] = pltpu.einshape("abl->bal", z_ref[0].reshape(a1, HY_B1, LANES))
    xt_ref[...] = pltpu.einshape("abl->bal", x0_ref[0].reshape(a1, HY_B1, LANES))

    def body(i, c):
        b0 = 2 * i
        ww = jnp.concatenate([wt_ref[b0], wt_ref[b0 + 1]], axis=1).astype(BF16)
        y = jnp.dot(ainv, ww, preferred_element_type=F32)
        ot_ref[b0] = (y[:, :LANES] + zt_ref[b0] * skip) * xt_ref[b0]
        ot_ref[b0 + 1] = (y[:, LANES:] + zt_ref[b0 + 1] * skip) * xt_ref[b0 + 1]
        return c

    lax.fori_loop(0, HY_B1 // 2, body, 0, unroll=8)
    o_ref[0] = pltpu.einshape("bal->abl", ot_ref[...]).reshape(a1 * HY_B1, LANES)


def _hyout(ainv, w3, z, x0, skip):
    b, seq, hy = z.shape
    a1, k2 = ainv.shape
    blk = pl.BlockSpec((1, seq, LANES), lambda i, j: (i, 0, j))
    return pl.pallas_call(
        _hyout_kernel,
        grid=(b, hy // LANES),
        in_specs=[pl.BlockSpec((a1, k2), lambda i, j: (0, 0)),
                  pl.BlockSpec((1, k2 * HY_B1, LANES), lambda i, j: (i, 0, j)),
                  blk, blk,
                  pl.BlockSpec((1, LANES), lambda i, j: (0, j))],
        out_specs=blk,
        out_shape=jax.ShapeDtypeStruct((b, seq, hy), F32),
        scratch_shapes=[pltpu.VMEM((HY_B1, k2, LANES), F32)] + [pltpu.VMEM((HY_B1, a1, LANES), F32)] * 3,
        compiler_params=_cparams(("parallel", "parallel")),
        name="hyena_out",
    )(ainv, w3, z, x0, skip)


def _dft_tables(a1):
    a2, b2, b1 = 2 * a1, 2 * HY_B1, HY_B1
    nka = a1 + 1
    kap = -(-nka // 8) * 8
    ka = np.arange(nka)[:, None]
    def fa(na):
        ph = 2 * np.pi * ((ka * np.arange(na)[None, :]) % a2) / a2
        m = np.zeros((2 * kap, na))
        m[0:2 * nka:2] = np.cos(ph)
        m[1:2 * nka:2] = -np.sin(ph)
        return m
    kb = np.arange(b2)[:, None]
    th = 2 * np.pi * ((kb * np.arange(b2)[None, :]) % b2) / b2
    c, s = np.cos(th), np.sin(th)
    fb_full = np.block([[c, s], [-s, c]])
    fb_half = np.block([[c[:, :b1], s[:, :b1]], [-s[:, :b1], c[:, :b1]]])
    ct, st = c.T[:b1], s.T[:b1]
    fbi = np.block([[ct, -st], [st, ct]]) / b2
    ph = 2 * np.pi * ((np.arange(a1)[:, None] * np.arange(nka)[None, :]) % a2) / a2
    wgt = np.where((np.arange(nka) == 0) | (np.arange(nka) == a1), 1.0, 2.0)[None, :] / a2
    ainv = np.zeros((a1, 2 * kap))
    ainv[:, 0:2 * nka:2] = wgt * np.cos(ph)
    ainv[:, 1:2 * nka:2] = -wgt * np.sin(ph)
    fa2 = fa(a2)
    fa_filt2 = np.stack([fa2[:, 0:a1], fa2[:, 1:a1 + 1]], axis=1).reshape(4 * kap, a1)
    fb_conj = np.concatenate([fb_full[:b2], -fb_full[b2:]], axis=0)
    f32 = lambda v: np.asarray(v, np.float32)
    return dict(nka=nka, kap=kap, fa_data=f32(fa(a1)), fa_filt2=f32(fa_filt2), fb_full=f32(fb_full),
                fb_conj=f32(fb_conj), fb_half=f32(fb_half), fbi=f32(fbi), ainv=f32(ainv))


def _hyena_filter_spectra(seq, hy, w1, b1, w2, b2, wo, fr, tabs):
    nbands = (FILTER_EMB - 1) // 2
    t = jnp.linspace(0.0, 1.0, seq, dtype=F32)[:, None]
    ang = 2.0 * math.pi * jnp.arange(seq, dtype=F32)[:, None] / seq
    f = jnp.linspace(1e-4, nbands - 1, nbands, dtype=F32)[None, :]
    zemb = jnp.concatenate([t, jnp.cos(f * ang), -jnp.sin(f * ang)], axis=-1)
    deltas = jnp.abs(jnp.linspace(math.log(DECAY_TARGET) / DECAY_FAST,
                                  math.log(DECAY_TARGET) / DECAY_SLOW, hy, dtype=F32))
    decay = jnp.exp(-t * deltas)
    dec = jnp.stack([decay, decay * (jnp.arange(seq) > 0)[:, None].astype(F32)])
    depth = w1.shape[0]
    h = _filter_mlp(zemb, w1, b1[:, None], w2, b2[:, None], wo, fr[:, None], dec, tl=min(512, seq))
    pieces = lambda m: jnp.stack(_split_bf16(jnp.asarray(m)))
    ya = _hyadft(pieces(tabs["fa_filt2"]), h.reshape(depth * 2, seq, hy), name="hyena_filter_adft")
    ya5 = ya.reshape(depth, 2, tabs["kap"], 4 * HY_B1, hy)
    return _filter_bdft(pieces(tabs["fb_full"]), pieces(tabs["fb_conj"]), ya5, tabs["nka"])


def _hyena(hy_in, sw, sb, skip, hhat, layer, tabs):
    b, seq, hy3 = hy_in.shape
    hy = hy3 // 3
    a1 = seq // HY_B1
    kap, nka = tabs["kap"], tabs["nka"]
    z, x0 = _hypre(hy_in, sw, sb[None], hy)
    ya = _hyadft(jnp.asarray(tabs["fa_data"], BF16)[None], z, out_dtype=BF16)
    y4 = ya.reshape(b, kap, 2 * HY_B1, hy)
    w4 = _hyfreq(y4, jnp.asarray(tabs["fb_half"], BF16), hhat, jnp.asarray(tabs["fbi"], BF16), nka, layer)
    w3 = w4.reshape(b, 2 * kap * HY_B1, hy)
    return _hyout(jnp.asarray(tabs["ainv"], BF16), w3, z, x0, skip[None])


def _na_geometry(rows):
    gr, kr_n, kh = NA_GROUP_ROWS, NA_KEY_ROWS, NA_KH_MAX
    n_g = rows // gr
    geo = []
    for g in (0, 1, n_g - 1):
        ks = min(max(gr * g - kh // 2, 0), rows - kr_n)
        per_q = []
        for qr in range(gr):
            r = gr * g + qr
            rs = min(max(r - kh // 2, 0), rows - kh)
            per_q.append([((rs <= ks + k < rs + kh), ks + k - r + NA_KH_MAX - 1) for k in range(kr_n)])
        geo.append(per_q)
    return geo


def _na_table_kernel(r_ref, t_ref, *, geo):
    w, kw = GRID_W, NA_KW
    qc = lax.broadcasted_iota(I32, (w, 1), 0)
    lane = lax.broadcasted_iota(I32, (1, LANES), 1)
    kc = lane % w
    cs = jnp.clip(qc - kw // 2, 0, w - kw)
    colvalid = (kc >= cs) & (kc < cs + kw)
    left = lane < w
    neg = jnp.full((w, LANES), NEG, F32)
    shift = LANES - (kw - 1)

    def toeplitz(dr, lane_off):
        row = r_ref[0, 0, dr:dr + 1, :]
        if lane_off:
            row = pltpu.roll(row, lane_off, 1)
        return pltpu.roll(jnp.broadcast_to(row, (w, LANES)), shift, 1, stride=1, stride_axis=0)

    for v, per_q in enumerate(geo):
        for qr, per_k in enumerate(per_q):
            for pair in range(len(per_k) // 2):
                (ok0, dr0), (ok1, dr1) = per_k[2 * pair], per_k[2 * pair + 1]
                tile = neg
                if ok0:
                    tile = jnp.where(left & colvalid, toeplitz(dr0, 0), tile)
                if ok1:
                    tile = jnp.where((~left) & colvalid, toeplitz(dr1, w), tile)
                t_ref[0, v, 0, qr * w:(qr + 1) * w, pair * LANES:(pair + 1) * LANES] = tile


def _na_tables(rpb_all, rows):
    depth, heads, nr, nc = rpb_all.shape
    rp = jnp.pad(rpb_all.astype(F32), ((0, 0), (0, 0), (0, 16 - nr), (0, LANES - nc)))
    tq, tk = NA_GROUP_ROWS * GRID_W, NA_KEY_ROWS * GRID_W
    return pl.pallas_call(
        functools.partial(_na_table_kernel, geo=_na_geometry(rows)),
        grid=(depth, heads),
        in_specs=[pl.BlockSpec((1, 1, 16, LANES), lambda l, h: (l, h, 0, 0))],
        out_specs=pl.BlockSpec((1, 3, 1, tq, tk), lambda l, h: (l, 0, h, 0, 0)),
        out_shape=jax.ShapeDtypeStruct((depth, 3, heads, tq, tk), F32),
        compiler_params=_cparams(("parallel", "parallel")),
        name="natten_tables",
    )(rp)


def _natten_kernel(q_ref, k_ref, v_ref, t_ref, o_ref, *, heads, n_g, rows):
    g = pl.program_id(1)
    tq = q_ref.shape[1]
    tk = t_ref.shape[4]
    ks = jnp.clip(NA_GROUP_ROWS * g - NA_KH_MAX // 2, 0, rows - NA_KEY_ROWS)
    kstart = pl.multiple_of(ks * GRID_W, GRID_W)
    per_tile = LANES // NA_HEAD_DIM
    lane = lax.broadcasted_iota(I32, (1, LANES), 1)
    ones = jnp.ones((tk, LANES), BF16)
    for j in range(heads // per_tile):
        lanes = slice(j * LANES, (j + 1) * LANES)
        q2 = q_ref[0, :, lanes].astype(F32) * (NA_HEAD_DIM ** -0.5)
        k2 = k_ref[0, pl.ds(kstart, tk), lanes]
        vaug = jnp.concatenate([v_ref[0, pl.ds(kstart, tk), lanes], ones], axis=1)
        o2 = None
        for hh in range(per_tile):
            own = (lane >= hh * NA_HEAD_DIM) & (lane < (hh + 1) * NA_HEAD_DIM)
            qm = jnp.where(own, q2, 0.0).astype(BF16)
            s = lax.dot_general(qm, k2, (((1,), (1,)), ((), ())), preferred_element_type=F32)
            s = s + t_ref[0, 0, j * per_tile + hh]
            m = jnp.max(s, axis=-1, keepdims=True)
            p = jnp.exp((s - m).astype(BF16))
            r = jnp.dot(p, vaug, preferred_element_type=F32)
            o = r[:, :LANES] / r[:, LANES:]
            o2 = o if o2 is None else jnp.where(own, o, o2)
        o_ref[0, :, lanes] = o2.astype(o_ref.dtype)


def _natten(qkv, tables, layer, b, seq, naw):
    heads = naw // NA_HEAD_DIM
    rows = seq // GRID_W
    n_g = rows // NA_GROUP_ROWS
    tq = NA_GROUP_ROWS * GRID_W
    tk = NA_KEY_ROWS * GRID_W

    def tmap(i, g):
        return (layer, jnp.where(g == 0, 0, jnp.where(g == n_g - 1, 2, 1)), 0, 0, 0)

    return pl.pallas_call(
        functools.partial(_natten_kernel, heads=heads, n_g=n_g, rows=rows),
        grid=(b, n_g),
        in_specs=[pl.BlockSpec((1, tq, naw), lambda i, g: (i, g, 0)),
                  pl.BlockSpec((1, seq, naw), lambda i, g: (i, 0, 1)),
                  pl.BlockSpec((1, seq, naw), lambda i, g: (i, 0, 2)),
                  pl.BlockSpec((1, 1, heads, tq, tk), tmap)],
        out_specs=pl.BlockSpec((1, tq, naw), lambda i, g: (i, g, 0)),
        out_shape=jax.ShapeDtypeStruct((b, seq, naw), F32),
        compiler_params=_cparams(("parallel", "arbitrary")),
        name="natten",
    )(qkv, qkv, qkv, tables)


def _pool_kernel(u_ref, w_ref, sc_ref, o_ref, pad_ref, *, pw):
    seq = u_ref.shape[1]
    lp = seq + 2 * POOL_PAD
    j = pl.program_id(1)
    u = u_ref[0]
    zeros = jnp.zeros((POOL_PAD, LANES), F32)
    pad_ref[0:POOL_PAD, :] = zeros
    pad_ref[POOL_PAD + seq:lp, :] = zeros
    pad_ref[POOL_PAD:POOL_PAD + seq, :] = u
    xp = pad_ref[...]
    dn = lambda a, k: pltpu.roll(a, k, 0)
    up = lambda a, k: pltpu.roll(a, lp - k, 0)
    s2 = xp + dn(xp, 1)
    s4 = dn(s2, 1) + up(s2, 1)
    s8 = dn(s4, 2) + up(s4, 2)
    s16 = dn(s8, 4) + up(s8, 4)
    t = lax.broadcasted_iota(I32, (seq, 1), 0)
    lane = lax.broadcasted_iota(I32, (1, LANES), 1) + j * LANES
    gdim = pw // len(POOL_WINDOWS)
    grp = lane // gdim
    sums = (s2, s4, s8, s16)
    pooled = jnp.zeros((seq, LANES), F32)
    for gi, w in enumerate(POOL_WINDOWS):
        cnt = (jnp.minimum(t + w // 2, seq) - jnp.maximum(t - w // 2, 0)).astype(F32)
        mean = sums[gi][POOL_PAD:POOL_PAD + seq] / cnt
        pooled = jnp.where(grp == gi, mean, pooled)
    pooled = pooled - u
    y = jnp.dot(pooled.astype(BF16), w_ref[0], preferred_element_type=F32)
    o_ref[0] = y * sc_ref[...]


def _pool(u, pool_w, pool_scale):
    b, seq, pw = u.shape
    ng, gd, _ = pool_w.shape
    nh = pw // LANES
    per = LANES // gd
    wbd = jnp.zeros((nh, LANES, LANES), F32)
    for gi in range(ng):
        hh, k = divmod(gi, per)
        wbd = wbd.at[hh, k * gd:(k + 1) * gd, k * gd:(k + 1) * gd].set(pool_w[gi])
    return pl.pallas_call(
        functools.partial(_pool_kernel, pw=pw),
        grid=(b, nh),
        in_specs=[pl.BlockSpec((1, seq, LANES), lambda i, j: (i, 0, j)),
                  pl.BlockSpec((1, LANES, LANES), lambda i, j: (j, 0, 0)),
                  pl.BlockSpec((1, LANES), lambda i, j: (0, j))],
        out_specs=pl.BlockSpec((1, seq, LANES), lambda i, j: (i, 0, j)),
        out_shape=jax.ShapeDtypeStruct((b, seq, pw), F32),
        scratch_shapes=[pltpu.VMEM((seq + 2 * POOL_PAD, LANES), F32)],
        compiler_params=_cparams(("parallel", "parallel")),
        name="pool_mixer",
    )(u, wbd.astype(BF16), pool_scale[None])


def _outproj_kernel(x_ref, yh_ref, yn_ref, yp_ref, gm_ref, w_ref, g2_ref, wr_ref,
                    xo_ref, h_ref, aff_ref, *, hy, naw):
    gm = gm_ref[...]
    m1 = _rms(yh_ref[...], gm[:, :hy]).astype(BF16)
    m2 = _rms(yn_ref[...], gm[:, hy:hy + naw]).astype(BF16)
    m3 = _rms(yp_ref[...], gm[:, hy + naw:]).astype(BF16)
    acc = jnp.dot(m1, w_ref[:hy, :], preferred_element_type=F32)
    acc += jnp.dot(m2, w_ref[hy:hy + naw, :], preferred_element_type=F32)
    acc += jnp.dot(m3, w_ref[hy + naw:, :], preferred_element_type=F32)
    xn = x_ref[...] + acc
    xo_ref[...] = xn
    h = _rms(xn, g2_ref[...])
    h_ref[...] = h.reshape(h_ref.shape)
    logits = _dot_x3(wr_ref[0], wr_ref[1], h, (((1,), (1,)), ((), ())))
    mx = jnp.max(logits, axis=0, keepdims=True)
    ex = jnp.exp(logits - mx)
    aff_ref[...] = ex / jnp.sum(ex, axis=0, keepdims=True)


def _outproj(x2, yh, yn, yp, gm, w_bf, g2, wr_t, tm=512):
    n, d = x2.shape
    hy, naw, pw = yh.shape[1], yn.shape[1], yp.shape[1]
    e = wr_t.shape[1]
    row = lambda c: pl.BlockSpec((tm, c), lambda i: (i, 0))
    full = lambda s: pl.BlockSpec(s, lambda i: (0, 0))
    return pl.pallas_call(
        functools.partial(_outproj_kernel, hy=hy, naw=naw),
        grid=(n // tm,),
        in_specs=[row(d), row(hy), row(naw), row(pw), full((1, d)), full((d, d)), full((1, d)),
                  pl.BlockSpec((2, e, d), lambda i: (0, 0, 0))],
        out_specs=[row(d), pl.BlockSpec((tm, d // LANES, LANES), lambda i: (i, 0, 0)),
                   pl.BlockSpec((e, tm), lambda i: (0, i))],
        out_shape=[jax.ShapeDtypeStruct((n, d), F32), jax.ShapeDtypeStruct((n, d // LANES, LANES), F32),
                   jax.ShapeDtypeStruct((e, n), F32)],
        compiler_params=_cparams(("parallel",)),
        name="outproj_router",
    )(x2, yh, yn, yp, gm, w_bf, g2, wr_t)


def _block_cumsum(x, tri):
    r, n = x.shape
    cls, offs = [], []
    off = jnp.zeros((r, 1), F32)
    for j in range(n // LANES):
        c = jnp.dot(x[:, j * LANES:(j + 1) * LANES], tri, preferred_element_type=F32)
        cls.append(c)
        off = off + c[:, LANES - 1:LANES]
        offs.append(off)
    return cls, offs


def _route_kernel(aff_ref, tri_ref, bci_ref, bcx_ref, idx_ref, gate_ref, blk_ref, *, cap):
    aff = aff_ref[...]
    e, seq = aff.shape
    nblk = seq // LANES
    bits = pltpu.bitcast(aff, I32)
    capf = jnp.float32(cap)

    def radix(i, prefix):
        cand = prefix | jnp.left_shift(jnp.int32(1), 30 - i)
        cnt = jnp.sum((bits >= cand).astype(F32), axis=1, keepdims=True)
        return jnp.where(cnt >= capf, cand, prefix)

    tau = lax.fori_loop(0, 31, radix, jnp.zeros((e, 1), I32))
    gt = bits > tau
    eq = bits == tau
    need = capf - jnp.sum(gt.astype(F32), axis=1, keepdims=True)
    tri = tri_ref[...]
    cls, offs = _block_cumsum(jnp.where(eq, 1.0, 0.0).astype(BF16), tri)
    tie_rank = jnp.concatenate([c if j == 0 else c + offs[j - 1] for j, c in enumerate(cls)], axis=1)
    sel = gt | (eq & (tie_rank <= need))
    self32 = jnp.where(sel, 1.0, 0.0)
    selb = self32.astype(BF16)
    cls, _ = _block_cumsum(selb, tri)
    bend = jnp.dot(selb, bci_ref[...], preferred_element_type=F32)
    bstart = jnp.dot(selb, bcx_ref[...], preferred_element_type=F32)
    for j in range(nblk):
        rows = slice(j * e, (j + 1) * e)
        lanes = slice(j * LANES, (j + 1) * LANES)
        blk_ref[0, rows, :] = cls[j]
        blk_ref[1, rows, :] = self32[:, lanes]
        blk_ref[2, rows, :] = aff[:, lanes]
    slot = lax.broadcasted_iota(I32, (cap, 1), 0).astype(F32)
    lane = lax.broadcasted_iota(I32, (1, LANES), 1).astype(F32)
    for ei in range(e):
        bs, be = bstart[ei:ei + 1, :], bend[ei:ei + 1, :]
        inblk = (bs <= slot) & (slot < be)
        local = slot - jnp.sum(jnp.where(inblk, bs, 0.0), axis=1, keepdims=True)
        jcol = jnp.sum(jnp.where(inblk, lane, 0.0), axis=1, keepdims=True)
        pick = jnp.where(inblk, 1.0, 0.0)[:, :nblk].astype(BF16)
        rows = pl.ds(ei, nblk, stride=e)
        a = blk_ref[2, rows, :]
        a_hi = a.astype(BF16)
        r1 = a - a_hi.astype(F32)
        a_mid = r1.astype(BF16)
        a_lo = (r1 - a_mid.astype(F32)).astype(BF16)
        take = lambda v: jnp.dot(pick, v, preferred_element_type=F32)
        g_cl = take(blk_ref[0, rows, :].astype(BF16))
        g_sel = take(blk_ref[1, rows, :].astype(BF16))
        g_aff = take(a_hi) + take(a_mid) + take(a_lo)
        hit = (g_cl == local + 1.0) & (g_sel > 0.5)
        idx = jcol * LANES + jnp.sum(jnp.where(hit, lane, 0.0), axis=1, keepdims=True)
        gate = jnp.sum(jnp.where(hit, g_aff, 0.0), axis=1, keepdims=True)
        idx_ref[0, :, ei:ei + 1] = idx.astype(I32)
        gate_ref[0, ei] = jnp.broadcast_to(gate, (cap, LANES))


def _route(aff_t, b, seq, cap):
    e = aff_t.shape[0]
    nblk = seq // LANES
    tri = jnp.asarray(np.triu(np.ones((LANES, LANES), np.float32)), BF16)
    tblk = np.arange(seq)[:, None] // LANES
    bci = jnp.asarray(tblk <= np.arange(LANES)[None, :], BF16)
    bcx = jnp.asarray(tblk < np.arange(LANES)[None, :], BF16)
    full = lambda s: pl.BlockSpec(s, lambda i: (0, 0))
    idx, gate = pl.pallas_call(
        functools.partial(_route_kernel, cap=cap),
        grid=(b,),
        in_specs=[pl.BlockSpec((e, seq), lambda i: (0, i)), full((LANES, LANES)),
                  full((seq, LANES)), full((seq, LANES))],
        out_specs=[pl.BlockSpec((1, cap, e), lambda i: (i, 0, 0)),
                   pl.BlockSpec((1, e, cap, LANES), lambda i: (i, 0, 0, 0))],
        out_shape=[jax.ShapeDtypeStruct((b, cap, e), I32), jax.ShapeDtypeStruct((b, e, cap, LANES), F32)],
        scratch_shapes=[pltpu.VMEM((3, nblk * e, LANES), F32)],
        compiler_params=_cparams(("parallel",)),
        name="ec_route",
    )(aff_t, tri, bci, bcx)
    return idx.transpose(0, 2, 1), gate


def _expert_kernel(rows_ref, h_hbm, wg_ref, wu_ref, wd_ref, y_ref, xbuf, xb, acc, sem, *, tm, nm, nf, ne):
    e = pl.program_id(0)
    m = pl.program_id(1)
    f = pl.program_id(2)
    tile = e * nm + m
    ntiles = ne * nm
    slot = tile % 2
    chunk = tm // nf

    def row_copy(src_row, dst_slot, dst_chunk, dst_row):
        return pltpu.make_async_copy(h_hbm.at[pl.ds(src_row, 1)],
                                     xbuf.at[dst_slot, dst_chunk, pl.ds(dst_row, 1)], sem.at[dst_slot])

    def tile_wait(dst_slot):
        for k in range(nf):
            pltpu.make_async_copy(h_hbm.at[pl.ds(0, chunk)], xbuf.at[dst_slot, k], sem.at[dst_slot]).wait()

    @pl.when((tile == 0) & (f == 0))
    def _():
        for k in range(nf):
            def issue(i, c, k=k):
                row_copy(rows_ref[k * chunk + i], 0, k, i).start()
                return c

            lax.fori_loop(0, chunk, issue, 0, unroll=8)

    @pl.when(f == 0)
    def _():
        tile_wait(slot)
        rb = min(EXPERT_RELAYOUT_ROWS, chunk)
        for k in range(nf):
            def to_rows(c, carry, k=k):
                r0 = pl.multiple_of(c * rb, rb)
                xb[pl.ds(k * chunk + r0, rb), :] = xbuf[slot, k, pl.ds(r0, rb)].reshape(rb, xb.shape[1]).astype(BF16)
                return carry

            lax.fori_loop(0, chunk // rb, to_rows, 0)
        acc[...] = jnp.zeros(acc.shape, F32)

    nxt = jnp.minimum(tile + 1, ntiles - 1)
    base = nxt * tm + f * chunk
    for i in range(chunk):
        row_copy(rows_ref[base + i], 1 - slot, f, i).start()

    x = xb[...]
    a = jnp.dot(x, wg_ref[0, 0].astype(BF16), preferred_element_type=F32)
    u = jnp.dot(x, wu_ref[0, 0].astype(BF16), preferred_element_type=F32)
    hh = (a * jax.nn.sigmoid(a) * u).astype(BF16)
    acc[...] += jnp.dot(hh, wd_ref[0, 0].astype(BF16), preferred_element_type=F32)

    @pl.when(f == nf - 1)
    def _():
        y_ref[0] = acc[...].astype(y_ref.dtype)

    @pl.when((tile == ntiles - 1) & (f == nf - 1))
    def _():
        tile_wait(1 - slot)


EXPERT_RELAYOUT_ROWS = 64


def _experts(rows_flat, h3, w_gate, w_up, w_down, layer, mtot, tm=1024, tf=512):
    _, e, d, ff = w_gate.shape
    tm = min(tm, mtot)
    tf = min(tf, ff)
    nm, nf = mtot // tm, ff // tf
    grid_spec = pltpu.PrefetchScalarGridSpec(
        num_scalar_prefetch=1,
        grid=(e, nm, nf),
        in_specs=[pl.BlockSpec(memory_space=pl.ANY),
                  pl.BlockSpec((1, 1, d, tf), lambda i, m, f, r: (layer, i, 0, f)),
                  pl.BlockSpec((1, 1, d, tf), lambda i, m, f, r: (layer, i, 0, f)),
                  pl.BlockSpec((1, 1, tf, d), lambda i, m, f, r: (layer, i, f, 0))],
        out_specs=pl.BlockSpec((1, tm, d), lambda i, m, f, r: (i, m, 0)),
        scratch_shapes=[pltpu.VMEM((2, nf, tm // nf, d // LANES, LANES), F32), pltpu.VMEM((tm, d), BF16),
                        pltpu.VMEM((tm, d), F32),
                        pltpu.SemaphoreType.DMA((2,))],
    )
    return pl.pallas_call(
        functools.partial(_expert_kernel, tm=tm, nm=nm, nf=nf, ne=e),
        grid_spec=grid_spec,
        out_shape=jax.ShapeDtypeStruct((e, mtot, d), BF16),
        compiler_params=_cparams(("arbitrary", "arbitrary", "arbitrary")),
        name="ec_experts",
    )(rows_flat, h3, w_gate, w_up, w_down)


COMBINE_ROWS = 64
COMBINE_UNROLL = 8


def _combine_kernel(idx_ref, split_ref, x_ref, y_ref, gl, *rest, ne, cap, span, final):
    if final:
        g_ref, o_ref, acc3, y3 = rest
    else:
        o_ref, acc3, y3 = rest
    b = pl.program_id(0)
    sp = pl.program_id(1)
    e = pl.program_id(2)
    d = x_ref.shape[2]
    sub = d // LANES
    rb = COMBINE_ROWS

    @pl.when(e == 0)
    def _():
        def load(c, carry):
            r0 = pl.multiple_of(c * rb, rb)
            acc3[pl.ds(pl.multiple_of(r0 * sub, rb * sub), rb * sub), :] = (
                x_ref[0, pl.ds(r0, rb), :].reshape(rb * sub, LANES))
            return carry

        lax.fori_loop(0, span // rb, load, 0)

    lo = split_ref[(b * ne + e) * 3 + sp]
    hi = split_ref[(b * ne + e) * 3 + sp + 1]

    def relayout(c, carry):
        r0 = pl.multiple_of(c * rb, rb)
        y3[pl.ds(r0, rb)] = y_ref[0, pl.ds(r0, rb), :].astype(F32).reshape(rb, sub, LANES)
        return carry

    lax.fori_loop(lo // rb, (hi + rb - 1) // rb, relayout, 0)

    base = (b * ne + e) * cap
    nu = COMBINE_UNROLL

    def tokens(first, count):
        return tuple(idx_ref[base + first + u] for u in range(count))

    def add_rows(first, toks):
        rows = [pl.ds(pl.multiple_of(t, sub), sub) for t in toks]
        vals = [acc3[r, :] + gl[0, 0, pl.ds(first + u, 1), :] * y3[first + u] for u, r in enumerate(rows)]
        for r, v in zip(rows, vals):
            acc3[r, :] = v

    def group(k, toks):
        first = lo + k * nu
        nxt = tokens(jnp.minimum(first + nu, cap - nu), nu)
        add_rows(first, toks)
        return nxt

    ngroups = (hi - lo) // nu
    lax.fori_loop(0, ngroups, group, tokens(jnp.minimum(lo, cap - nu), nu))

    def tail(i, carry):
        add_rows(i, tokens(i, 1))
        return carry

    lax.fori_loop(lo + ngroups * nu, hi, tail, 0)

    @pl.when(e == ne - 1)
    def _():
        def store(c, carry):
            r0 = pl.multiple_of(c * rb, rb)
            v = acc3[pl.ds(pl.multiple_of(r0 * sub, rb * sub), rb * sub), :].reshape(rb, d)
            if final:
                v = _rms(v, g_ref[...])
            o_ref[0, pl.ds(r0, rb), :] = v
            return carry

        lax.fori_loop(0, span // rb, store, 0)


def _combine(idx, gate, x3, y, cap, final_g=None):
    b, seq, d = x3.shape
    ne = y.shape[0]
    span = seq // 2
    n_lower = jnp.sum((idx < span).astype(I32), axis=-1)
    split_flat = jnp.stack([jnp.zeros_like(n_lower), n_lower, jnp.full_like(n_lower, cap)], axis=-1).reshape(-1)
    final = final_g is not None
    in_specs = [pl.BlockSpec((1, span, d), lambda i, s, e, *_: (i, s, 0)),
                pl.BlockSpec((1, cap, d), lambda i, s, e, *_: (e, i, 0)),
                pl.BlockSpec((1, 1, cap, LANES), lambda i, s, e, *_: (i, e, 0, 0))]
    args = [x3, y, gate]
    if final:
        in_specs.append(pl.BlockSpec((1, d), lambda i, s, e, *_: (0, 0)))
        args.append(final_g)
    grid_spec = pltpu.PrefetchScalarGridSpec(
        num_scalar_prefetch=2,
        grid=(b, 2, ne),
        in_specs=in_specs,
        out_specs=pl.BlockSpec((1, span, d), lambda i, s, e, *_: (i, s, 0)),
        scratch_shapes=[pltpu.VMEM((span * (d // LANES), LANES), F32), pltpu.VMEM((cap, d // LANES, LANES), F32)],
    )
    return pl.pallas_call(
        functools.partial(_combine_kernel, ne=ne, cap=cap, span=span, final=final),
        grid_spec=grid_spec,
        out_shape=jax.ShapeDtypeStruct((b, seq, d), F32),
        compiler_params=_cparams(("arbitrary", "arbitrary", "arbitrary")),
        name="ec_combine",
    )(((idx % span) * (d // LANES)).reshape(-1), split_flat, *args)


def _moe(x3, h2, aff_t, w_gate, w_up, w_down, layer, final_g=None):
    b, seq, d = x3.shape
    ne = w_gate.shape[1]
    cap = EC_CAPACITY * seq // ne
    idx, gate = _route(aff_t, b, seq, cap)
    rows = idx + (jnp.arange(b, dtype=I32) * seq)[:, None, None]
    rows_flat = rows.transpose(1, 0, 2).reshape(-1)
    y = _experts(rows_flat, h2, w_gate, w_up, w_down, layer, b * cap)
    return _combine(idx, gate, x3, y, cap, final_g)


def kernel(x, norm1_g, w_in, hy_short_w, hy_short_b, hy_f_w1, hy_f_b1, hy_f_w2, hy_f_b2, hy_f_wout, hy_f_freq, hy_skip, na_rpb, pool_w, pool_scale, mix_norm_g, w_out, norm2_g, w_router, w_gate, w_up, w_down, final_g):
    b, seq, d = x.shape
    depth = w_in.shape[0]
    hy = hy_skip.shape[1]
    pw = pool_scale.shape[1]
    naw = d - hy - pw
    n = b * seq
    rows = seq // GRID_W
    tabs = _dft_tables(seq // HY_B1)
    x2 = x.reshape(n, d)
    hhat = _hyena_filter_spectra(seq, hy, hy_f_w1, hy_f_b1, hy_f_w2, hy_f_b2, hy_f_wout, hy_f_freq, tabs)
    na_tables = _na_tables(na_rpb, rows)
    for i in range(depth):
        hy_in, qkv, pool_in = _inproj(x2, norm1_g[i][None], w_in[i].astype(BF16), 3 * hy, 3 * naw)
        y_hy = _hyena(hy_in.reshape(b, seq, 3 * hy), hy_short_w[i], hy_short_b[i], hy_skip[i], hhat, i, tabs)
        y_na = _natten(qkv.reshape(b, seq, 3 * naw), na_tables, i, b, seq, naw)
        y_pool = _pool(pool_in.reshape(b, seq, pw), pool_w[i], pool_scale[i])
        x2, h2, aff_t = _outproj(x2, y_hy.reshape(n, hy), y_na.reshape(n, naw), y_pool.reshape(n, pw),
                                 mix_norm_g[i][None], w_out[i].astype(BF16), norm2_g[i][None],
                                 jnp.stack(_split_bf16(w_router[i].T)))
        last = final_g[None] if i == depth - 1 else None
        x2 = _moe(x2.reshape(b, seq, d), h2, aff_t, w_gate, w_up, w_down, i, last).reshape(n, d)
    return x2.reshape(b, seq, d)
```

```python
import functools
import math

import numpy as np
import jax
import jax.numpy as jnp
from jax import lax
from jax.experimental import pallas as pl
from jax.experimental.pallas import tpu as pltpu

F32 = jnp.float32
BF16 = jnp.bfloat16
I32 = jnp.int32
EPS = 1e-6
HIGHEST = lax.Precision.HIGHEST

GRID_W = 64
NA_HEAD_DIM = 64
NA_KH_MAX = 8
NA_KW = 16
NA_GROUP_ROWS = 4
NA_KEY_ROWS = 12
POOL_WINDOWS = (2, 4, 8, 16)
POOL_PAD = 16
FILTER_EMB = 33
DECAY_FAST, DECAY_SLOW, DECAY_TARGET = 0.3, 1.5, 1e-2
EC_CAPACITY = 2
HY_B1 = 128
NEG = -1e30
LANES = 128
VMEM_LIMIT = 56 * 1024 * 1024


def _cparams(sem, vmem=VMEM_LIMIT):
    return pltpu.CompilerParams(dimension_semantics=sem, vmem_limit_bytes=vmem)


def _rms(v, g):
    return v * lax.rsqrt(jnp.mean(v * v, axis=-1, keepdims=True) + EPS) * g


def _split_bf16(v):
    hi = v.astype(BF16)
    return hi, (v - hi.astype(F32)).astype(BF16)


def _dot_x3(a_hi, a_lo, x, dims=(((1,), (0,)), ((), ()))):
    x_hi, x_lo = _split_bf16(x)
    dg = lambda p, q: lax.dot_general(p, q, dims, preferred_element_type=F32)
    return dg(a_hi, x_hi) + dg(a_lo, x_hi) + dg(a_hi, x_lo)


HALO = 16


def _inproj_kernel(x_ref, xp_ref, xn_ref, g_ref, w_ref, sw_ref, sb_ref, z_ref, x0_ref, qkv_ref, pool_ref,
                   *, hyw, naw, seq):
    g = g_ref[...]
    tm = x_ref.shape[0]
    h = _rms(x_ref[...], g).astype(BF16)
    hall = jnp.concatenate([_rms(xp_ref[...], g).astype(BF16), h, _rms(xn_ref[...], g).astype(BF16)], axis=0)
    u = jnp.dot(hall, w_ref[:, :hyw], preferred_element_type=F32)
    t = lax.rem(pl.program_id(0) * tm + lax.broadcasted_iota(I32, (tm, 1), 0), seq)
    prev = jnp.where(t == 0, 0.0, u[HALO - 1:HALO - 1 + tm])
    nxt = jnp.where(t == seq - 1, 0.0, u[HALO + 1:HALO + 1 + tm])
    conv = prev * sw_ref[0:1, :] + u[HALO:HALO + tm] * sw_ref[1:2, :] + nxt * sw_ref[2:3, :] + sb_ref[...]
    hy = hyw // 3
    x0_ref[...] = conv[:, :hy]
    z_ref[...] = conv[:, 2 * hy:] * conv[:, hy:2 * hy]
    qkv_ref[...] = jnp.dot(h, w_ref[:, hyw:hyw + naw], preferred_element_type=F32).astype(BF16)
    pool_ref[...] = jnp.dot(h, w_ref[:, hyw + naw:], preferred_element_type=F32)


def _inproj(x2, g, w_bf, sw, sb, hyw, naw, seq, tm=512):
    n, d = x2.shape
    inw = w_bf.shape[1]
    pw = inw - hyw - naw
    hy = hyw // 3
    per = tm // HALO
    last = n // HALO - 1
    return pl.pallas_call(
        functools.partial(_inproj_kernel, hyw=hyw, naw=naw, seq=seq),
        grid=(n // tm,),
        in_specs=[pl.BlockSpec((tm, d), lambda i: (i, 0)),
                  pl.BlockSpec((HALO, d), lambda i: (jnp.maximum(i * per - 1, 0), 0)),
                  pl.BlockSpec((HALO, d), lambda i: (jnp.minimum((i + 1) * per, last), 0)),
                  pl.BlockSpec((1, d), lambda i: (0, 0)),
                  pl.BlockSpec((d, inw), lambda i: (0, 0)),
                  pl.BlockSpec((3, hyw), lambda i: (0, 0)),
                  pl.BlockSpec((1, hyw), lambda i: (0, 0))],
        out_specs=[pl.BlockSpec((tm, hy), lambda i: (i, 0)),
                   pl.BlockSpec((tm, hy), lambda i: (i, 0)),
                   pl.BlockSpec((tm, naw), lambda i: (i, 0)),
                   pl.BlockSpec((tm, pw), lambda i: (i, 0))],
        out_shape=[jax.ShapeDtypeStruct((n, hy), F32),
                   jax.ShapeDtypeStruct((n, hy), F32),
                   jax.ShapeDtypeStruct((n, naw), BF16),
                   jax.ShapeDtypeStruct((n, pw), F32)],
        compiler_params=_cparams(("parallel",)),
        name="inproj",
    )(x2, x2, x2, g, w_bf, sw, sb)


def _filter_kernel(z_ref, w1_ref, b1_ref, w2_ref, b2_ref, wo_ref, fr_ref, dec_ref, o_ref):
    fr = fr_ref[0]
    h = jnp.sin(fr * (jnp.dot(z_ref[...], w1_ref[0], preferred_element_type=F32, precision=HIGHEST) + b1_ref[0]))
    h = jnp.sin(fr * (jnp.dot(h, w2_ref[0], preferred_element_type=F32, precision=HIGHEST) + b2_ref[0]))
    hw = jnp.dot(h, wo_ref[0], preferred_element_type=F32, precision=HIGHEST)
    c = dec_ref.shape[2]
    o_ref[0, 0] = hw[:, :c] * dec_ref[0]
    o_ref[0, 1] = hw[:, c:] * dec_ref[1]


def _filter_mlp(zemb, w1, b1, w2, b2, wo, fr, dec, tl=512):
    seq, emb = zemb.shape
    depth, _, hid = w1.shape
    ow = wo.shape[2]
    c = ow // 2
    lay = lambda s: pl.BlockSpec((1,) + s, lambda l, i: (l, 0, 0))
    return pl.pallas_call(
        _filter_kernel,
        grid=(depth, seq // tl),
        in_specs=[pl.BlockSpec((tl, emb), lambda l, i: (i, 0)), lay((emb, hid)), lay((1, hid)),
                  lay((hid, hid)), lay((1, hid)), lay((hid, ow)), lay((1, hid)),
                  pl.BlockSpec((2, tl, c), lambda l, i: (0, i, 0))],
        out_specs=pl.BlockSpec((1, 2, tl, c), lambda l, i: (l, 0, i, 0)),
        out_shape=jax.ShapeDtypeStruct((depth, 2, seq, c), F32),
        compiler_params=_cparams(("parallel", "parallel")),
        name="hyena_filter_mlp",
    )(zemb, w1, b1, w2, b2, wo, fr, dec)


def _filter_bdft_kernel(ff_ref, fc_ref, yf_ref, yb_ref, o_ref):
    o_ref[0, 0] = (_dot_x3(ff_ref[0], ff_ref[1], yf_ref[0, 0, 0]) + _dot_x3(fc_ref[0], fc_ref[1], yb_ref[0, 0, 0]))


def _filter_bdft(ff, fc, ya5, nka):
    depth, _, kap, r, c = ya5.shape
    blk = lambda p: pl.BlockSpec((1, 1, 1, r, c), lambda l, k, p=p: (l, p, k, 0, 0))
    return pl.pallas_call(
        _filter_bdft_kernel,
        grid=(depth, nka),
        in_specs=[pl.BlockSpec((2, r, r), lambda l, k: (0, 0, 0)), pl.BlockSpec((2, r, r), lambda l, k: (0, 0, 0)),
                  blk(0), blk(1)],
        out_specs=pl.BlockSpec((1, 1, r, c), lambda l, k: (l, k, 0, 0)),
        out_shape=jax.ShapeDtypeStruct((depth, nka, r, c), F32),
        compiler_params=_cparams(("parallel", "parallel")),
        name="hyena_filter_bdft",
    )(ff, fc, ya5, ya5)


def _hyadft_kernel(fa_ref, z_ref, y_ref, zt_ref, yt_ref):
    pieces, k2, a1 = fa_ref.shape
    zt_ref[...] = pltpu.einshape("abl->bal", z_ref[0].reshape(a1, HY_B1, LANES))

    def body(i, c):
        b0 = 2 * i
        zz = jnp.concatenate([zt_ref[b0], zt_ref[b0 + 1]], axis=1)
        if pieces == 1:
            r = jnp.dot(fa_ref[0], zz.astype(BF16), preferred_element_type=F32)
        else:
            r = _dot_x3(fa_ref[0], fa_ref[1], zz)
        yt_ref[b0] = r[:, :LANES]
        yt_ref[b0 + 1] = r[:, LANES:]
        return c

    lax.fori_loop(0, HY_B1 // 2, body, 0, unroll=8)
    y_ref[0] = pltpu.einshape("bkl->kbl", yt_ref[...]).reshape(k2 * HY_B1, LANES).astype(y_ref.dtype)


def _hyadft(fa, z, out_dtype=F32, name="hyena_adft"):
    b, seq, hy = z.shape
    pieces, k2, a1 = fa.shape
    return pl.pallas_call(
        _hyadft_kernel,
        grid=(b, hy // LANES),
        in_specs=[pl.BlockSpec((pieces, k2, a1), lambda i, j: (0, 0, 0)),
                  pl.BlockSpec((1, seq, LANES), lambda i, j: (i, 0, j))],
        out_specs=pl.BlockSpec((1, k2 * HY_B1, LANES), lambda i, j: (i, 0, j)),
        out_shape=jax.ShapeDtypeStruct((b, k2 * HY_B1, hy), out_dtype),
        scratch_shapes=[pltpu.VMEM((HY_B1, a1, LANES), F32), pltpu.VMEM((HY_B1, k2, LANES), F32)],
        compiler_params=_cparams(("parallel", "parallel")),
        name=name,
    )(fa, z)


def _hyfreq_kernel(y_ref, fb_ref, h_ref, fbi_ref, w_ref, *, nka):
    ka = pl.program_id(0)
    nb = fb_ref.shape[0] // 2
    nbatch = y_ref.shape[0]

    @pl.when(ka < nka)
    def _():
        hr, hi = h_ref[0, 0, :nb], h_ref[0, 0, nb:]
        for bi in range(nbatch):
            p = jnp.dot(fb_ref[...], y_ref[bi, 0].astype(BF16), preferred_element_type=F32)
            pr, pi = p[:nb], p[nb:]
            q = jnp.concatenate([pr * hr - pi * hi, pr * hi + pi * hr], axis=0).astype(BF16)
            w_ref[bi, 0] = jnp.dot(fbi_ref[...], q, preferred_element_type=F32).astype(w_ref.dtype)

    @pl.when(ka >= nka)
    def _():
        w_ref[...] = jnp.zeros(w_ref.shape, w_ref.dtype)


def _hyfreq(y4, fb, hhat, fbi, nka, layer):
    b, kap, r2, hy = y4.shape
    nb2 = fb.shape[0]
    return pl.pallas_call(
        functools.partial(_hyfreq_kernel, nka=nka),
        grid=(kap,),
        in_specs=[pl.BlockSpec((b, 1, r2, hy), lambda k: (0, k, 0, 0)),
                  pl.BlockSpec((nb2, r2), lambda k: (0, 0)),
                  pl.BlockSpec((1, 1, nb2, hy), lambda k: (layer, jnp.minimum(k, nka - 1), 0, 0)),
                  pl.BlockSpec((r2, nb2), lambda k: (0, 0))],
        out_specs=pl.BlockSpec((b, 1, r2, hy), lambda k: (0, k, 0, 0)),
        out_shape=jax.ShapeDtypeStruct((b, kap, r2, hy), BF16),
        compiler_params=_cparams(("parallel",)),
        name="hyena_freq",
    )(y4, fb, hhat, fbi)


def _hyout_kernel(a_ref, w_ref, z_ref, x0_ref, sk_ref, o_ref, wt_ref, zt_ref, xt_ref, ot_ref):
    a1, k2 = a_ref.shape
    ainv = a_ref[...]
    skip = sk_ref[...]
    wt_ref[...] = pltpu.einshape("kbl->bkl", w_ref[0].astype(F32).reshape(k2, HY_B1, LANES))
    zt_ref[...] = pltpu.einshape("abl->bal", z_ref[0].reshape(a1, HY_B1, LANES))
    xt_ref[...] = pltpu.einshape("abl->bal", x0_ref[0].reshape(a1, HY_B1, LANES))

    def body(i, c):
        b0 = 2 * i
        ww = jnp.concatenate([wt_ref[b0], wt_ref[b0 + 1]], axis=1).astype(BF16)
        y = jnp.dot(ainv, ww, preferred_element_type=F32)
        ot_ref[b0] = (y[:, :LANES] + zt_ref[b0] * skip) * xt_ref[b0]
        ot_ref[b0 + 1] = (y[:, LANES:] + zt_ref[b0 + 1] * skip) * xt_ref[b0 + 1]
        return c

    lax.fori_loop(0, HY_B1 // 2, body, 0, unroll=8)
    o_ref[0] = pltpu.einshape("bal->abl", ot_ref[...]).reshape(a1 * HY_B1, LANES)


def _hyout(ainv, w3, z, x0, skip):
    b, seq, hy = z.shape
    a1, k2 = ainv.shape
    blk = pl.BlockSpec((1, seq, LANES), lambda i, j: (i, 0, j))
    return pl.pallas_call(
        _hyout_kernel,
        grid=(b, hy // LANES),
        in_specs=[pl.BlockSpec((a1, k2), lambda i, j: (0, 0)),
                  pl.BlockSpec((1, k2 * HY_B1, LANES), lambda i, j: (i, 0, j)),
                  blk, blk,
                  pl.BlockSpec((1, LANES), lambda i, j: (0, j))],
        out_specs=blk,
        out_shape=jax.ShapeDtypeStruct((b, seq, hy), F32),
        scratch_shapes=[pltpu.VMEM((HY_B1, k2, LANES), F32)] + [pltpu.VMEM((HY_B1, a1, LANES), F32)] * 3,
        compiler_params=_cparams(("parallel", "parallel")),
        name="hyena_out",
    )(ainv, w3, z, x0, skip)


def _dft_tables(a1):
    a2, b2, b1 = 2 * a1, 2 * HY_B1, HY_B1
    nka = a1 + 1
    kap = -(-nka // 8) * 8
    ka = np.arange(nka)[:, None]
    def fa(na):
        ph = 2 * np.pi * ((ka * np.arange(na)[None, :]) % a2) / a2
        m = np.zeros((2 * kap, na))
        m[0:2 * nka:2] = np.cos(ph)
        m[1:2 * nka:2] = -np.sin(ph)
        return m
    kb = np.arange(b2)[:, None]
    th = 2 * np.pi * ((kb * np.arange(b2)[None, :]) % b2) / b2
    c, s = np.cos(th), np.sin(th)
    fb_full = np.block([[c, s], [-s, c]])
    fb_half = np.block([[c[:, :b1], s[:, :b1]], [-s[:, :b1], c[:, :b1]]])
    ct, st = c.T[:b1], s.T[:b1]
    fbi = np.block([[ct, -st], [st, ct]]) / b2
    ph = 2 * np.pi * ((np.arange(a1)[:, None] * np.arange(nka)[None, :]) % a2) / a2
    wgt = np.where((np.arange(nka) == 0) | (np.arange(nka) == a1), 1.0, 2.0)[None, :] / a2
    ainv = np.zeros((a1, 2 * kap))
    ainv[:, 0:2 * nka:2] = wgt * np.cos(ph)
    ainv[:, 1:2 * nka:2] = -wgt * np.sin(ph)
    fa2 = fa(a2)
    fa_filt2 = np.stack([fa2[:, 0:a1], fa2[:, 1:a1 + 1]], axis=1).reshape(4 * kap, a1)
    fb_conj = np.concatenate([fb_full[:b2], -fb_full[b2:]], axis=0)
    f32 = lambda v: np.asarray(v, np.float32)
    return dict(nka=nka, kap=kap, fa_data=f32(fa(a1)), fa_filt2=f32(fa_filt2), fb_full=f32(fb_full),
                fb_conj=f32(fb_conj), fb_half=f32(fb_half), fbi=f32(fbi), ainv=f32(ainv))


def _hyena_filter_spectra(seq, hy, w1, b1, w2, b2, wo, fr, tabs):
    nbands = (FILTER_EMB - 1) // 2
    t = jnp.linspace(0.0, 1.0, seq, dtype=F32)[:, None]
    ang = 2.0 * math.pi * jnp.arange(seq, dtype=F32)[:, None] / seq
    f = jnp.linspace(1e-4, nbands - 1, nbands, dtype=F32)[None, :]
    zemb = jnp.concatenate([t, jnp.cos(f * ang), -jnp.sin(f * ang)], axis=-1)
    deltas = jnp.abs(jnp.linspace(math.log(DECAY_TARGET) / DECAY_FAST,
                                  math.log(DECAY_TARGET) / DECAY_SLOW, hy, dtype=F32))
    decay = jnp.exp(-t * deltas)
    dec = jnp.stack([decay, decay * (jnp.arange(seq) > 0)[:, None].astype(F32)])
    depth = w1.shape[0]
    h = _filter_mlp(zemb, w1, b1[:, None], w2, b2[:, None], wo, fr[:, None], dec, tl=min(512, seq))
    pieces = lambda m: jnp.stack(_split_bf16(jnp.asarray(m)))
    ya = _hyadft(pieces(tabs["fa_filt2"]), h.reshape(depth * 2, seq, hy), name="hyena_filter_adft")
    ya5 = ya.reshape(depth, 2, tabs["kap"], 4 * HY_B1, hy)
    return _filter_bdft(pieces(tabs["fb_full"]), pieces(tabs["fb_conj"]), ya5, tabs["nka"])


def _hyena(z, x0, skip, hhat, layer, tabs):
    b, seq, hy = z.shape
    kap, nka = tabs["kap"], tabs["nka"]
    ya = _hyadft(jnp.asarray(tabs["fa_data"], BF16)[None], z, out_dtype=BF16)
    y4 = ya.reshape(b, kap, 2 * HY_B1, hy)
    w4 = _hyfreq(y4, jnp.asarray(tabs["fb_half"], BF16), hhat, jnp.asarray(tabs["fbi"], BF16), nka, layer)
    w3 = w4.reshape(b, 2 * kap * HY_B1, hy)
    return _hyout(jnp.asarray(tabs["ainv"], BF16), w3, z, x0, skip[None])


def _na_geometry(rows):
    gr, kr_n, kh = NA_GROUP_ROWS, NA_KEY_ROWS, NA_KH_MAX
    n_g = rows // gr
    geo = []
    for g in (0, 1, n_g - 1):
        ks = min(max(gr * g - kh // 2, 0), rows - kr_n)
        per_q = []
        for qr in range(gr):
            r = gr * g + qr
            rs = min(max(r - kh // 2, 0), rows - kh)
            per_q.append([((rs <= ks + k < rs + kh), ks + k - r + NA_KH_MAX - 1) for k in range(kr_n)])
        geo.append(per_q)
    return geo


def _na_table_kernel(r_ref, t_ref, *, geo):
    w, kw = GRID_W, NA_KW
    qc = lax.broadcasted_iota(I32, (w, 1), 0)
    lane = lax.broadcasted_iota(I32, (1, LANES), 1)
    kc = lane % w
    cs = jnp.clip(qc - kw // 2, 0, w - kw)
    colvalid = (kc >= cs) & (kc < cs + kw)
    left = lane < w
    neg = jnp.full((w, LANES), NEG, F32)
    shift = LANES - (kw - 1)

    def toeplitz(dr, lane_off):
        row = r_ref[0, 0, dr:dr + 1, :]
        if lane_off:
            row = pltpu.roll(row, lane_off, 1)
        return pltpu.roll(jnp.broadcast_to(row, (w, LANES)), shift, 1, stride=1, stride_axis=0)

    for v, per_q in enumerate(geo):
        for qr, per_k in enumerate(per_q):
            for pair in range(len(per_k) // 2):
                (ok0, dr0), (ok1, dr1) = per_k[2 * pair], per_k[2 * pair + 1]
                tile = neg
                if ok0:
                    tile = jnp.where(left & colvalid, toeplitz(dr0, 0), tile)
                if ok1:
                    tile = jnp.where((~left) & colvalid, toeplitz(dr1, w), tile)
                t_ref[0, v, 0, qr * w:(qr + 1) * w, pair * LANES:(pair + 1) * LANES] = tile


def _na_tables(rpb_all, rows):
    depth, heads, nr, nc = rpb_all.shape
    rp = jnp.pad(rpb_all.astype(F32), ((0, 0), (0, 0), (0, 16 - nr), (0, LANES - nc)))
    tq, tk = NA_GROUP_ROWS * GRID_W, NA_KEY_ROWS * GRID_W
    return pl.pallas_call(
        functools.partial(_na_table_kernel, geo=_na_geometry(rows)),
        grid=(depth, heads),
        in_specs=[pl.BlockSpec((1, 1, 16, LANES), lambda l, h: (l, h, 0, 0))],
        out_specs=pl.BlockSpec((1, 3, 1, tq, tk), lambda l, h: (l, 0, h, 0, 0)),
        out_shape=jax.ShapeDtypeStruct((depth, 3, heads, tq, tk), F32),
        compiler_params=_cparams(("parallel", "parallel")),
        name="natten_tables",
    )(rp)


def _natten_kernel(q_ref, k_ref, v_ref, t_ref, o_ref, *, heads, n_g, rows):
    g = pl.program_id(1)
    tq = q_ref.shape[1]
    tk = t_ref.shape[4]
    ks = jnp.clip(NA_GROUP_ROWS * g - NA_KH_MAX // 2, 0, rows - NA_KEY_ROWS)
    kstart = pl.multiple_of(ks * GRID_W, GRID_W)
    per_tile = LANES // NA_HEAD_DIM
    lane = lax.broadcasted_iota(I32, (1, LANES), 1)
    ones = jnp.ones((tk, LANES), BF16)
    for j in range(heads // per_tile):
        lanes = slice(j * LANES, (j + 1) * LANES)
        q2 = q_ref[0, :, lanes].astype(F32) * (NA_HEAD_DIM ** -0.5)
        k2 = k_ref[0, pl.ds(kstart, tk), lanes]
        vaug = jnp.concatenate([v_ref[0, pl.ds(kstart, tk), lanes], ones], axis=1)
        o2 = None
        for hh in range(per_tile):
            own = (lane >= hh * NA_HEAD_DIM) & (lane < (hh + 1) * NA_HEAD_DIM)
            qm = jnp.where(own, q2, 0.0).astype(BF16)
            s = lax.dot_general(qm, k2, (((1,), (1,)), ((), ())), preferred_element_type=F32)
            s = s + t_ref[0, 0, j * per_tile + hh]
            m = jnp.max(s, axis=-1, keepdims=True)
            p = jnp.exp((s - m).astype(BF16))
            r = jnp.dot(p, vaug, preferred_element_type=F32)
            o = r[:, :LANES] / r[:, LANES:]
            o2 = o if o2 is None else jnp.where(own, o, o2)
        o_ref[0, :, lanes] = o2.astype(o_ref.dtype)


def _natten(qkv, tables, layer, b, seq, naw):
    heads = naw // NA_HEAD_DIM
    rows = seq // GRID_W
    n_g = rows // NA_GROUP_ROWS
    tq = NA_GROUP_ROWS * GRID_W
    tk = NA_KEY_ROWS * GRID_W

    def tmap(i, g):
        return (layer, jnp.where(g == 0, 0, jnp.where(g == n_g - 1, 2, 1)), 0, 0, 0)

    return pl.pallas_call(
        functools.partial(_natten_kernel, heads=heads, n_g=n_g, rows=rows),
        grid=(b, n_g),
        in_specs=[pl.BlockSpec((1, tq, naw), lambda i, g: (i, g, 0)),
                  pl.BlockSpec((1, seq, naw), lambda i, g: (i, 0, 1)),
                  pl.BlockSpec((1, seq, naw), lambda i, g: (i, 0, 2)),
                  pl.BlockSpec((1, 1, heads, tq, tk), tmap)],
        out_specs=pl.BlockSpec((1, tq, naw), lambda i, g: (i, g, 0)),
        out_shape=jax.ShapeDtypeStruct((b, seq, naw), F32),
        compiler_params=_cparams(("parallel", "arbitrary")),
        name="natten",
    )(qkv, qkv, qkv, tables)


def _pool_kernel(u_ref, w_ref, sc_ref, o_ref, pad_ref, *, pw):
    seq = u_ref.shape[1]
    lp = seq + 2 * POOL_PAD
    j = pl.program_id(1)
    u = u_ref[0]
    zeros = jnp.zeros((POOL_PAD, LANES), F32)
    pad_ref[0:POOL_PAD, :] = zeros
    pad_ref[POOL_PAD + seq:lp, :] = zeros
    pad_ref[POOL_PAD:POOL_PAD + seq, :] = u
    xp = pad_ref[...]
    dn = lambda a, k: pltpu.roll(a, k, 0)
    up = lambda a, k: pltpu.roll(a, lp - k, 0)
    s2 = xp + dn(xp, 1)
    s4 = dn(s2, 1) + up(s2, 1)
    s8 = dn(s4, 2) + up(s4, 2)
    s16 = dn(s8, 4) + up(s8, 4)
    t = lax.broadcasted_iota(I32, (seq, 1), 0)
    lane = lax.broadcasted_iota(I32, (1, LANES), 1) + j * LANES
    gdim = pw // len(POOL_WINDOWS)
    grp = lane // gdim
    sums = (s2, s4, s8, s16)
    pooled = jnp.zeros((seq, LANES), F32)
    for gi, w in enumerate(POOL_WINDOWS):
        cnt = (jnp.minimum(t + w // 2, seq) - jnp.maximum(t - w // 2, 0)).astype(F32)
        mean = sums[gi][POOL_PAD:POOL_PAD + seq] / cnt
        pooled = jnp.where(grp == gi, mean, pooled)
    pooled = pooled - u
    y = jnp.dot(pooled.astype(BF16), w_ref[0], preferred_element_type=F32)
    o_ref[0] = y * sc_ref[...]


def _pool(u, pool_w, pool_scale):
    b, seq, pw = u.shape
    ng, gd, _ = pool_w.shape
    nh = pw // LANES
    per = LANES // gd
    wbd = jnp.zeros((nh, LANES, LANES), F32)
    for gi in range(ng):
        hh, k = divmod(gi, per)
        wbd = wbd.at[hh, k * gd:(k + 1) * gd, k * gd:(k + 1) * gd].set(pool_w[gi])
    return pl.pallas_call(
        functools.partial(_pool_kernel, pw=pw),
        grid=(b, nh),
        in_specs=[pl.BlockSpec((1, seq, LANES), lambda i, j: (i, 0, j)),
                  pl.BlockSpec((1, LANES, LANES), lambda i, j: (j, 0, 0)),
                  pl.BlockSpec((1, LANES), lambda i, j: (0, j))],
        out_specs=pl.BlockSpec((1, seq, LANES), lambda i, j: (i, 0, j)),
        out_shape=jax.ShapeDtypeStruct((b, seq, pw), F32),
        scratch_shapes=[pltpu.VMEM((seq + 2 * POOL_PAD, LANES), F32)],
        compiler_params=_cparams(("parallel", "parallel")),
        name="pool_mixer",
    )(u, wbd.astype(BF16), pool_scale[None])


def _outproj_kernel(x_ref, yh_ref, yn_ref, yp_ref, gm_ref, w_ref, g2_ref, wr_ref,
                    xo_ref, h_ref, aff_ref, *, hy, naw):
    gm = gm_ref[...]
    m1 = _rms(yh_ref[...], gm[:, :hy]).astype(BF16)
    m2 = _rms(yn_ref[...], gm[:, hy:hy + naw]).astype(BF16)
    m3 = _rms(yp_ref[...], gm[:, hy + naw:]).astype(BF16)
    acc = jnp.dot(m1, w_ref[:hy, :], preferred_element_type=F32)
    acc += jnp.dot(m2, w_ref[hy:hy + naw, :], preferred_element_type=F32)
    acc += jnp.dot(m3, w_ref[hy + naw:, :], preferred_element_type=F32)
    xn = x_ref[...] + acc
    xo_ref[...] = xn
    h = _rms(xn, g2_ref[...])
    h_ref[...] = h.reshape(h_ref.shape)
    logits = _dot_x3(wr_ref[0], wr_ref[1], h, (((1,), (1,)), ((), ())))
    mx = jnp.max(logits, axis=0, keepdims=True)
    ex = jnp.exp(logits - mx)
    aff_ref[...] = ex / jnp.sum(ex, axis=0, keepdims=True)


def _outproj(x2, yh, yn, yp, gm, w_bf, g2, wr_t, tm=512):
    n, d = x2.shape
    hy, naw, pw = yh.shape[1], yn.shape[1], yp.shape[1]
    e = wr_t.shape[1]
    row = lambda c: pl.BlockSpec((tm, c), lambda i: (i, 0))
    full = lambda s: pl.BlockSpec(s, lambda i: (0, 0))
    return pl.pallas_call(
        functools.partial(_outproj_kernel, hy=hy, naw=naw),
        grid=(n // tm,),
        in_specs=[row(d), row(hy), row(naw), row(pw), full((1, d)), full((d, d)), full((1, d)),
                  pl.BlockSpec((2, e, d), lambda i: (0, 0, 0))],
        out_specs=[row(d), pl.BlockSpec((tm, d // LANES, LANES), lambda i: (i, 0, 0)),
                   pl.BlockSpec((e, tm), lambda i: (0, i))],
        out_shape=[jax.ShapeDtypeStruct((n, d), F32), jax.ShapeDtypeStruct((n, d // LANES, LANES), F32),
                   jax.ShapeDtypeStruct((e, n), F32)],
        compiler_params=_cparams(("parallel",)),
        name="outproj_router",
    )(x2, yh, yn, yp, gm, w_bf, g2, wr_t)


def _block_cumsum(x, tri):
    r, n = x.shape
    cls, offs = [], []
    off = jnp.zeros((r, 1), F32)
    for j in range(n // LANES):
        c = jnp.dot(x[:, j * LANES:(j + 1) * LANES], tri, preferred_element_type=F32)
        cls.append(c)
        off = off + c[:, LANES - 1:LANES]
        offs.append(off)
    return cls, offs


def _route_kernel(aff_ref, tri_ref, bci_ref, bcx_ref, idx_ref, gate_ref, blk_ref, *, cap):
    aff = aff_ref[...]
    e, seq = aff.shape
    nblk = seq // LANES
    bits = pltpu.bitcast(aff, I32)
    capf = jnp.float32(cap)

    def radix(i, prefix):
        cand = prefix | jnp.left_shift(jnp.int32(1), 30 - i)
        cnt = jnp.sum((bits >= cand).astype(F32), axis=1, keepdims=True)
        return jnp.where(cnt >= capf, cand, prefix)

    tau = lax.fori_loop(0, 31, radix, jnp.zeros((e, 1), I32))
    gt = bits > tau
    eq = bits == tau
    need = capf - jnp.sum(gt.astype(F32), axis=1, keepdims=True)
    tri = tri_ref[...]
    cls, offs = _block_cumsum(jnp.where(eq, 1.0, 0.0).astype(BF16), tri)
    tie_rank = jnp.concatenate([c if j == 0 else c + offs[j - 1] for j, c in enumerate(cls)], axis=1)
    sel = gt | (eq & (tie_rank <= need))
    self32 = jnp.where(sel, 1.0, 0.0)
    selb = self32.astype(BF16)
    cls, _ = _block_cumsum(selb, tri)
    bend = jnp.dot(selb, bci_ref[...], preferred_element_type=F32)
    bstart = jnp.dot(selb, bcx_ref[...], preferred_element_type=F32)
    for j in range(nblk):
        rows = slice(j * e, (j + 1) * e)
        lanes = slice(j * LANES, (j + 1) * LANES)
        blk_ref[0, rows, :] = cls[j]
        blk_ref[1, rows, :] = self32[:, lanes]
        blk_ref[2, rows, :] = aff[:, lanes]
    slot = lax.broadcasted_iota(I32, (cap, 1), 0).astype(F32)
    lane = lax.broadcasted_iota(I32, (1, LANES), 1).astype(F32)
    for ei in range(e):
        bs, be = bstart[ei:ei + 1, :], bend[ei:ei + 1, :]
        inblk = (bs <= slot) & (slot < be)
        local = slot - jnp.sum(jnp.where(inblk, bs, 0.0), axis=1, keepdims=True)
        jcol = jnp.sum(jnp.where(inblk, lane, 0.0), axis=1, keepdims=True)
        pick = jnp.where(inblk, 1.0, 0.0)[:, :nblk].astype(BF16)
        rows = pl.ds(ei, nblk, stride=e)
        a = blk_ref[2, rows, :]
        a_hi = a.astype(BF16)
        r1 = a - a_hi.astype(F32)
        a_mid = r1.astype(BF16)
        a_lo = (r1 - a_mid.astype(F32)).astype(BF16)
        take = lambda v: jnp.dot(pick, v, preferred_element_type=F32)
        g_cl = take(blk_ref[0, rows, :].astype(BF16))
        g_sel = take(blk_ref[1, rows, :].astype(BF16))
        g_aff = take(a_hi) + take(a_mid) + take(a_lo)
        hit = (g_cl == local + 1.0) & (g_sel > 0.5)
        idx = jcol * LANES + jnp.sum(jnp.where(hit, lane, 0.0), axis=1, keepdims=True)
        gate = jnp.sum(jnp.where(hit, g_aff, 0.0), axis=1, keepdims=True)
        idx_ref[0, :, ei:ei + 1] = idx.astype(I32)
        gate_ref[0, ei] = jnp.broadcast_to(gate, (cap, LANES))


def _route(aff_t, b, seq, cap):
    e = aff_t.shape[0]
    nblk = seq // LANES
    tri = jnp.asarray(np.triu(np.ones((LANES, LANES), np.float32)), BF16)
    tblk = np.arange(seq)[:, None] // LANES
    bci = jnp.asarray(tblk <= np.arange(LANES)[None, :], BF16)
    bcx = jnp.asarray(tblk < np.arange(LANES)[None, :], BF16)
    full = lambda s: pl.BlockSpec(s, lambda i: (0, 0))
    idx, gate = pl.pallas_call(
        functools.partial(_route_kernel, cap=cap),
        grid=(b,),
        in_specs=[pl.BlockSpec((e, seq), lambda i: (0, i)), full((LANES, LANES)),
                  full((seq, LANES)), full((seq, LANES))],
        out_specs=[pl.BlockSpec((1, cap, e), lambda i: (i, 0, 0)),
                   pl.BlockSpec((1, e, cap, LANES), lambda i: (i, 0, 0, 0))],
        out_shape=[jax.ShapeDtypeStruct((b, cap, e), I32), jax.ShapeDtypeStruct((b, e, cap, LANES), F32)],
        scratch_shapes=[pltpu.VMEM((3, nblk * e, LANES), F32)],
        compiler_params=_cparams(("parallel",)),
        name="ec_route",
    )(aff_t, tri, bci, bcx)
    return idx.transpose(0, 2, 1), gate


def _expert_kernel(rows_ref, h_hbm, wg_ref, wu_ref, wd_ref, y_ref, xbuf, xb, acc, sem, *, tm, nm, nf, ne):
    e = pl.program_id(0)
    m = pl.program_id(1)
    f = pl.program_id(2)
    tile = e * nm + m
    ntiles = ne * nm
    slot = tile % 2
    chunk = tm // nf

    def row_copy(src_row, dst_slot, dst_chunk, dst_row):
        return pltpu.make_async_copy(h_hbm.at[pl.ds(src_row, 1)],
                                     xbuf.at[dst_slot, dst_chunk, pl.ds(dst_row, 1)], sem.at[dst_slot])

    def tile_wait(dst_slot):
        for k in range(nf):
            pltpu.make_async_copy(h_hbm.at[pl.ds(0, chunk)], xbuf.at[dst_slot, k], sem.at[dst_slot]).wait()

    @pl.when((tile == 0) & (f == 0))
    def _():
        for k in range(nf):
            def issue(i, c, k=k):
                row_copy(rows_ref[k * chunk + i], 0, k, i).start()
                return c

            lax.fori_loop(0, chunk, issue, 0, unroll=8)

    @pl.when(f == 0)
    def _():
        tile_wait(slot)
        rb = min(EXPERT_RELAYOUT_ROWS, chunk)
        for k in range(nf):
            def to_rows(c, carry, k=k):
                r0 = pl.multiple_of(c * rb, rb)
                xb[pl.ds(k * chunk + r0, rb), :] = xbuf[slot, k, pl.ds(r0, rb)].reshape(rb, xb.shape[1]).astype(BF16)
                return carry

            lax.fori_loop(0, chunk // rb, to_rows, 0)
        acc[...] = jnp.zeros(acc.shape, F32)

    nxt = jnp.minimum(tile + 1, ntiles - 1)
    base = nxt * tm + f * chunk
    for i in range(chunk):
        row_copy(rows_ref[base + i], 1 - slot, f, i).start()

    x = xb[...]
    a = jnp.dot(x, wg_ref[0, 0].astype(BF16), preferred_element_type=F32)
    u = jnp.dot(x, wu_ref[0, 0].astype(BF16), preferred_element_type=F32)
    hh = (a * jax.nn.sigmoid(a) * u).astype(BF16)
    acc[...] += jnp.dot(hh, wd_ref[0, 0].astype(BF16), preferred_element_type=F32)

    @pl.when(f == nf - 1)
    def _():
        y_ref[0] = acc[...].astype(y_ref.dtype)

    @pl.when((tile == ntiles - 1) & (f == nf - 1))
    def _():
        tile_wait(1 - slot)


EXPERT_RELAYOUT_ROWS = 64


def _experts(rows_flat, h3, w_gate, w_up, w_down, layer, mtot, tm=1024, tf=512):
    _, e, d, ff = w_gate.shape
    tm = min(tm, mtot)
    tf = min(tf, ff)
    nm, nf = mtot // tm, ff // tf
    grid_spec = pltpu.PrefetchScalarGridSpec(
        num_scalar_prefetch=1,
        grid=(e, nm, nf),
        in_specs=[pl.BlockSpec(memory_space=pl.ANY),
                  pl.BlockSpec((1, 1, d, tf), lambda i, m, f, r: (layer, i, 0, f)),
                  pl.BlockSpec((1, 1, d, tf), lambda i, m, f, r: (layer, i, 0, f)),
                  pl.BlockSpec((1, 1, tf, d), lambda i, m, f, r: (layer, i, f, 0))],
        out_specs=pl.BlockSpec((1, tm, d), lambda i, m, f, r: (i, m, 0)),
        scratch_shapes=[pltpu.VMEM((2, nf, tm // nf, d // LANES, LANES), F32), pltpu.VMEM((tm, d), BF16),
                        pltpu.VMEM((tm, d), F32),
                        pltpu.SemaphoreType.DMA((2,))],
    )
    return pl.pallas_call(
        functools.partial(_expert_kernel, tm=tm, nm=nm, nf=nf, ne=e),
        grid_spec=grid_spec,
        out_shape=jax.ShapeDtypeStruct((e, mtot, d), BF16),
        compiler_params=_cparams(("arbitrary", "arbitrary", "arbitrary")),
        name="ec_experts",
    )(rows_flat, h3, w_gate, w_up, w_down)


COMBINE_ROWS = 64
COMBINE_UNROLL = 8


def _combine_kernel(idx_ref, split_ref, x_ref, y_ref, gl, *rest, ne, cap, span, final):
    if final:
        g_ref, o_ref, acc3, y3 = rest
    else:
        o_ref, acc3, y3 = rest
    b = pl.program_id(0)
    sp = pl.program_id(1)
    e = pl.program_id(2)
    d = x_ref.shape[2]
    sub = d // LANES
    rb = COMBINE_ROWS

    @pl.when(e == 0)
    def _():
        def load(c, carry):
            r0 = pl.multiple_of(c * rb, rb)
            acc3[pl.ds(pl.multiple_of(r0 * sub, rb * sub), rb * sub), :] = (
                x_ref[0, pl.ds(r0, rb), :].reshape(rb * sub, LANES))
            return carry

        lax.fori_loop(0, span // rb, load, 0)

    lo = split_ref[(b * ne + e) * 3 + sp]
    hi = split_ref[(b * ne + e) * 3 + sp + 1]

    def relayout(c, carry):
        r0 = pl.multiple_of(c * rb, rb)
        y3[pl.ds(r0, rb)] = y_ref[0, pl.ds(r0, rb), :].astype(F32).reshape(rb, sub, LANES)
        return carry

    lax.fori_loop(lo // rb, (hi + rb - 1) // rb, relayout, 0)

    base = (b * ne + e) * cap
    nu = COMBINE_UNROLL

    def tokens(first, count):
        return tuple(idx_ref[base + first + u] for u in range(count))

    def add_rows(first, toks):
        rows = [pl.ds(pl.multiple_of(t, sub), sub) for t in toks]
        vals = [acc3[r, :] + gl[0, 0, pl.ds(first + u, 1), :] * y3[first + u] for u, r in enumerate(rows)]
        for r, v in zip(rows, vals):
            acc3[r, :] = v

    def group(k, toks):
        first = lo + k * nu
        nxt = tokens(jnp.minimum(first + nu, cap - nu), nu)
        add_rows(first, toks)
        return nxt

    ngroups = (hi - lo) // nu
    lax.fori_loop(0, ngroups, group, tokens(jnp.minimum(lo, cap - nu), nu))

    def tail(i, carry):
        add_rows(i, tokens(i, 1))
        return carry

    lax.fori_loop(lo + ngroups * nu, hi, tail, 0)

    @pl.when(e == ne - 1)
    def _():
        def store(c, carry):
            r0 = pl.multiple_of(c * rb, rb)
            v = acc3[pl.ds(pl.multiple_of(r0 * sub, rb * sub), rb * sub), :].reshape(rb, d)
            if final:
                v = _rms(v, g_ref[...])
            o_ref[0, pl.ds(r0, rb), :] = v
            return carry

        lax.fori_loop(0, span // rb, store, 0)


def _combine(idx, gate, x3, y, cap, final_g=None):
    b, seq, d = x3.shape
    ne = y.shape[0]
    span = seq // 2
    n_lower = jnp.sum((idx < span).astype(I32), axis=-1)
    split_flat = jnp.stack([jnp.zeros_like(n_lower), n_lower, jnp.full_like(n_lower, cap)], axis=-1).reshape(-1)
    final = final_g is not None
    in_specs = [pl.BlockSpec((1, span, d), lambda i, s, e, *_: (i, s, 0)),
                pl.BlockSpec((1, cap, d), lambda i, s, e, *_: (e, i, 0)),
                pl.BlockSpec((1, 1, cap, LANES), lambda i, s, e, *_: (i, e, 0, 0))]
    args = [x3, y, gate]
    if final:
        in_specs.append(pl.BlockSpec((1, d), lambda i, s, e, *_: (0, 0)))
        args.append(final_g)
    grid_spec = pltpu.PrefetchScalarGridSpec(
        num_scalar_prefetch=2,
        grid=(b, 2, ne),
        in_specs=in_specs,
        out_specs=pl.BlockSpec((1, span, d), lambda i, s, e, *_: (i, s, 0)),
        scratch_shapes=[pltpu.VMEM((span * (d // LANES), LANES), F32), pltpu.VMEM((cap, d // LANES, LANES), F32)],
    )
    return pl.pallas_call(
        functools.partial(_combine_kernel, ne=ne, cap=cap, span=span, final=final),
        grid_spec=grid_spec,
        out_shape=jax.ShapeDtypeStruct((b, seq, d), F32),
        compiler_params=_cparams(("arbitrary", "arbitrary", "arbitrary")),
        name="ec_combine",
    )(((idx % span) * (d // LANES)).reshape(-1), split_flat, *args)


def _moe(x3, h2, aff_t, w_gate, w_up, w_down, layer, final_g=None):
    b, seq, d = x3.shape
    ne = w_gate.shape[1]
    cap = EC_CAPACITY * seq // ne
    idx, gate = _route(aff_t, b, seq, cap)
    rows = idx + (jnp.arange(b, dtype=I32) * seq)[:, None, None]
    rows_flat = rows.transpose(1, 0, 2).reshape(-1)
    y = _experts(rows_flat, h2, w_gate, w_up, w_down, layer, b * cap)
    return _combine(idx, gate, x3, y, cap, final_g)


def kernel(x, norm1_g, w_in, hy_short_w, hy_short_b, hy_f_w1, hy_f_b1, hy_f_w2, hy_f_b2, hy_f_wout, hy_f_freq, hy_skip, na_rpb, pool_w, pool_scale, mix_norm_g, w_out, norm2_g, w_router, w_gate, w_up, w_down, final_g):
    b, seq, d = x.shape
    depth = w_in.shape[0]
    hy = hy_skip.shape[1]
    pw = pool_scale.shape[1]
    naw = d - hy - pw
    n = b * seq
    rows = seq // GRID_W
    tabs = _dft_tables(seq // HY_B1)
    x2 = x.reshape(n, d)
    hhat = _hyena_filter_spectra(seq, hy, hy_f_w1, hy_f_b1, hy_f_w2, hy_f_b2, hy_f_wout, hy_f_freq, tabs)
    na_tables = _na_tables(na_rpb, rows)
    for i in range(depth):
        z, x0, qkv, pool_in = _inproj(x2, norm1_g[i][None], w_in[i].astype(BF16), hy_short_w[i],
                                      hy_short_b[i][None], 3 * hy, 3 * naw, seq)
        y_hy = _hyena(z.reshape(b, seq, hy), x0.reshape(b, seq, hy), hy_skip[i], hhat, i, tabs)
        y_na = _natten(qkv.reshape(b, seq, 3 * naw), na_tables, i, b, seq, naw)
        y_pool = _pool(pool_in.reshape(b, seq, pw), pool_w[i], pool_scale[i])
        x2, h2, aff_t = _outproj(x2, y_hy.reshape(n, hy), y_na.reshape(n, naw), y_pool.reshape(n, pw),
                                 mix_norm_g[i][None], w_out[i].astype(BF16), norm2_g[i][None],
                                 jnp.stack(_split_bf16(w_router[i].T)))
        last = final_g[None] if i == depth - 1 else None
        x2 = _moe(x2.reshape(b, seq, d), h2, aff_t, w_gate, w_up, w_down, i, last).reshape(n, d)
    return x2.reshape(b, seq, d)
```

```python
import functools
import math

import numpy as np
import jax
import jax.numpy as jnp
from jax import lax
from jax.experimental import pallas as pl
from jax.experimental.pallas import tpu as pltpu

F32 = jnp.float32
BF16 = jnp.bfloat16
I32 = jnp.int32
EPS = 1e-6
HIGHEST = lax.Precision.HIGHEST

GRID_W = 64
NA_HEAD_DIM = 64
NA_KH_MAX = 8
NA_KW = 16
NA_GROUP_ROWS = 4
NA_KEY_ROWS = 12
POOL_WINDOWS = (2, 4, 8, 16)
POOL_PAD = 16
FILTER_EMB = 33
DECAY_FAST, DECAY_SLOW, DECAY_TARGET = 0.3, 1.5, 1e-2
EC_CAPACITY = 2
HY_B1 = 128
NEG = -1e30
LANES = 128
V7X_VMEM_BYTES = 64 * 1024 * 1024
VMEM_LIMIT = V7X_VMEM_BYTES - 8 * 1024 * 1024


def _cparams(sem, vmem=VMEM_LIMIT):
    return pltpu.CompilerParams(dimension_semantics=sem, vmem_limit_bytes=vmem)


def _rms(v, g):
    return v * lax.rsqrt(jnp.mean(v * v, axis=-1, keepdims=True) + EPS) * g


def _split_bf16(v):
    hi = v.astype(BF16)
    return hi, (v - hi.astype(F32)).astype(BF16)


def _dot_x3(a_hi, a_lo, x, dims=(((1,), (0,)), ((), ()))):
    x_hi, x_lo = _split_bf16(x)
    dg = lambda p, q: lax.dot_general(p, q, dims, preferred_element_type=F32)
    return dg(a_hi, x_hi) + dg(a_lo, x_hi) + dg(a_hi, x_lo)


HALO = 16


def _inproj_kernel(x_ref, xp_ref, xn_ref, g_ref, w_ref, sw_ref, sb_ref, z_ref, x0_ref, qkv_ref, pool_ref,
                   *, hyw, naw, seq):
    g = g_ref[...]
    tm = x_ref.shape[0]
    h = _rms(x_ref[...], g).astype(BF16)
    hall = jnp.concatenate([_rms(xp_ref[...], g).astype(BF16), h, _rms(xn_ref[...], g).astype(BF16)], axis=0)
    u = jnp.dot(hall, w_ref[:, :hyw], preferred_element_type=F32)
    t = lax.rem(pl.program_id(0) * tm + lax.broadcasted_iota(I32, (tm, 1), 0), seq)
    prev = jnp.where(t == 0, 0.0, u[HALO - 1:HALO - 1 + tm])
    nxt = jnp.where(t == seq - 1, 0.0, u[HALO + 1:HALO + 1 + tm])
    conv = prev * sw_ref[0:1, :] + u[HALO:HALO + tm] * sw_ref[1:2, :] + nxt * sw_ref[2:3, :] + sb_ref[...]
    hy = hyw // 3
    x0_ref[...] = conv[:, :hy]
    z_ref[...] = conv[:, 2 * hy:] * conv[:, hy:2 * hy]
    qkv_ref[...] = jnp.dot(h, w_ref[:, hyw:hyw + naw], preferred_element_type=F32).astype(BF16)
    pool_ref[...] = jnp.dot(h, w_ref[:, hyw + naw:], preferred_element_type=F32)


def _inproj(x2, g, w_bf, sw, sb, hyw, naw, seq, tm=512):
    n, d = x2.shape
    inw = w_bf.shape[1]
    pw = inw - hyw - naw
    hy = hyw // 3
    per = tm // HALO
    last = n // HALO - 1
    return pl.pallas_call(
        functools.partial(_inproj_kernel, hyw=hyw, naw=naw, seq=seq),
        grid=(n // tm,),
        in_specs=[pl.BlockSpec((tm, d), lambda i: (i, 0)),
                  pl.BlockSpec((HALO, d), lambda i: (jnp.maximum(i * per - 1, 0), 0)),
                  pl.BlockSpec((HALO, d), lambda i: (jnp.minimum((i + 1) * per, last), 0)),
                  pl.BlockSpec((1, d), lambda i: (0, 0)),
                  pl.BlockSpec((d, inw), lambda i: (0, 0)),
                  pl.BlockSpec((3, hyw), lambda i: (0, 0)),
                  pl.BlockSpec((1, hyw), lambda i: (0, 0))],
        out_specs=[pl.BlockSpec((tm, hy), lambda i: (i, 0)),
                   pl.BlockSpec((tm, hy), lambda i: (i, 0)),
                   pl.BlockSpec((tm, naw), lambda i: (i, 0)),
                   pl.BlockSpec((tm, pw), lambda i: (i, 0))],
        out_shape=[jax.ShapeDtypeStruct((n, hy), F32),
                   jax.ShapeDtypeStruct((n, hy), F32),
                   jax.ShapeDtypeStruct((n, naw), BF16),
                   jax.ShapeDtypeStruct((n, pw), F32)],
        compiler_params=_cparams(("parallel",)),
        name="inproj",
    )(x2, x2, x2, g, w_bf, sw, sb)


def _filter_kernel(z_ref, w1_ref, b1_ref, w2_ref, b2_ref, wo_ref, fr_ref, dec_ref, o_ref):
    fr = fr_ref[0]
    h = jnp.sin(fr * (jnp.dot(z_ref[...], w1_ref[0], preferred_element_type=F32, precision=HIGHEST) + b1_ref[0]))
    h = jnp.sin(fr * (jnp.dot(h, w2_ref[0], preferred_element_type=F32, precision=HIGHEST) + b2_ref[0]))
    hw = jnp.dot(h, wo_ref[0], preferred_element_type=F32, precision=HIGHEST)
    c = dec_ref.shape[2]
    o_ref[0, 0] = hw[:, :c] * dec_ref[0]
    o_ref[0, 1] = hw[:, c:] * dec_ref[1]


def _filter_mlp(zemb, w1, b1, w2, b2, wo, fr, dec, tl=512):
    seq, emb = zemb.shape
    depth, _, hid = w1.shape
    ow = wo.shape[2]
    c = ow // 2
    lay = lambda s: pl.BlockSpec((1,) + s, lambda l, i: (l, 0, 0))
    return pl.pallas_call(
        _filter_kernel,
        grid=(depth, seq // tl),
        in_specs=[pl.BlockSpec((tl, emb), lambda l, i: (i, 0)), lay((emb, hid)), lay((1, hid)),
                  lay((hid, hid)), lay((1, hid)), lay((hid, ow)), lay((1, hid)),
                  pl.BlockSpec((2, tl, c), lambda l, i: (0, i, 0))],
        out_specs=pl.BlockSpec((1, 2, tl, c), lambda l, i: (l, 0, i, 0)),
        out_shape=jax.ShapeDtypeStruct((depth, 2, seq, c), F32),
        compiler_params=_cparams(("parallel", "parallel")),
        name="hyena_filter_mlp",
    )(zemb, w1, b1, w2, b2, wo, fr, dec)


def _filter_bdft_kernel(ff_ref, fc_ref, yf_ref, yb_ref, o_ref):
    o_ref[0, 0] = (_dot_x3(ff_ref[0], ff_ref[1], yf_ref[0, 0, 0]) + _dot_x3(fc_ref[0], fc_ref[1], yb_ref[0, 0, 0]))


def _filter_bdft(ff, fc, ya5, nka):
    depth, _, kap, r, c = ya5.shape
    blk = lambda p: pl.BlockSpec((1, 1, 1, r, c), lambda l, k, p=p: (l, p, k, 0, 0))
    return pl.pallas_call(
        _filter_bdft_kernel,
        grid=(depth, nka),
        in_specs=[pl.BlockSpec((2, r, r), lambda l, k: (0, 0, 0)), pl.BlockSpec((2, r, r), lambda l, k: (0, 0, 0)),
                  blk(0), blk(1)],
        out_specs=pl.BlockSpec((1, 1, r, c), lambda l, k: (l, k, 0, 0)),
        out_shape=jax.ShapeDtypeStruct((depth, nka, r, c), F32),
        compiler_params=_cparams(("parallel", "parallel")),
        name="hyena_filter_bdft",
    )(ff, fc, ya5, ya5)


def _hyadft_kernel(fa_ref, z_ref, y_ref, zt_ref, yt_ref):
    pieces, k2, a1 = fa_ref.shape
    zt_ref[...] = pltpu.einshape("abl->bal", z_ref[0].reshape(a1, HY_B1, LANES))

    def body(i, c):
        b0 = 2 * i
        zz = jnp.concatenate([zt_ref[b0], zt_ref[b0 + 1]], axis=1)
        if pieces == 1:
            r = jnp.dot(fa_ref[0], zz.astype(BF16), preferred_element_type=F32)
        else:
            r = _dot_x3(fa_ref[0], fa_ref[1], zz)
        yt_ref[b0] = r[:, :LANES]
        yt_ref[b0 + 1] = r[:, LANES:]
        return c

    lax.fori_loop(0, HY_B1 // 2, body, 0, unroll=8)
    y_ref[0] = pltpu.einshape("bkl->kbl", yt_ref[...]).reshape(k2 * HY_B1, LANES).astype(y_ref.dtype)


def _hyadft(fa, z, out_dtype=F32, name="hyena_adft"):
    b, seq, hy = z.shape
    pieces, k2, a1 = fa.shape
    return pl.pallas_call(
        _hyadft_kernel,
        grid=(b, hy // LANES),
        in_specs=[pl.BlockSpec((pieces, k2, a1), lambda i, j: (0, 0, 0)),
                  pl.BlockSpec((1, seq, LANES), lambda i, j: (i, 0, j))],
        out_specs=pl.BlockSpec((1, k2 * HY_B1, LANES), lambda i, j: (i, 0, j)),
        out_shape=jax.ShapeDtypeStruct((b, k2 * HY_B1, hy), out_dtype),
        scratch_shapes=[pltpu.VMEM((HY_B1, a1, LANES), F32), pltpu.VMEM((HY_B1, k2, LANES), F32)],
        compiler_params=_cparams(("parallel", "parallel")),
        name=name,
    )(fa, z)


def _hyfreq_kernel(y_ref, fb_ref, h_ref, fbi_ref, w_ref, *, nka):
    ka = pl.program_id(0)
    nb = fb_ref.shape[0] // 2
    nbatch = y_ref.shape[0]

    @pl.when(ka < nka)
    def _():
        hr, hi = h_ref[0, 0, :nb], h_ref[0, 0, nb:]
        for bi in range(nbatch):
            p = jnp.dot(fb_ref[...], y_ref[bi, 0].astype(BF16), preferred_element_type=F32)
            pr, pi = p[:nb], p[nb:]
            q = jnp.concatenate([pr * hr - pi * hi, pr * hi + pi * hr], axis=0).astype(BF16)
            w_ref[bi, 0] = jnp.dot(fbi_ref[...], q, preferred_element_type=F32).astype(w_ref.dtype)

    @pl.when(ka >= nka)
    def _():
        w_ref[...] = jnp.zeros(w_ref.shape, w_ref.dtype)


def _hyfreq(y4, fb, hhat, fbi, nka, layer):
    b, kap, r2, hy = y4.shape
    nb2 = fb.shape[0]
    return pl.pallas_call(
        functools.partial(_hyfreq_kernel, nka=nka),
        grid=(kap,),
        in_specs=[pl.BlockSpec((b, 1, r2, hy), lambda k: (0, k, 0, 0)),
                  pl.BlockSpec((nb2, r2), lambda k: (0, 0)),
                  pl.BlockSpec((1, 1, nb2, hy), lambda k: (layer, jnp.minimum(k, nka - 1), 0, 0)),
                  pl.BlockSpec((r2, nb2), lambda k: (0, 0))],
        out_specs=pl.BlockSpec((b, 1, r2, hy), lambda k: (0, k, 0, 0)),
        out_shape=jax.ShapeDtypeStruct((b, kap, r2, hy), BF16),
        compiler_params=_cparams(("parallel",)),
        name="hyena_freq",
    )(y4, fb, hhat, fbi)


def _hyout_kernel(a_ref, w_ref, z_ref, x0_ref, sk_ref, o_ref, wt_ref, zt_ref, xt_ref, ot_ref):
    a1, k2 = a_ref.shape
    ainv = a_ref[...]
    skip = sk_ref[...]
    wt_ref[...] = pltpu.einshape("kbl->bkl", w_ref[0].astype(F32).reshape(k2, HY_B1, LANES))
    zt_ref[...] = pltpu.einshape("abl->bal", z_ref[0].reshape(a1, HY_B1, LANES))
    xt_ref[...] = pltpu.einshape("abl->bal", x0_ref[0].reshape(a1, HY_B1, LANES))

    def body(i, c):
        b0 = 2 * i
        ww = jnp.concatenate([wt_ref[b0], wt_ref[b0 + 1]], axis=1).astype(BF16)
        y = jnp.dot(ainv, ww, preferred_element_type=F32)
        ot_ref[b0] = (y[:, :LANES] + zt_ref[b0] * skip) * xt_ref[b0]
        ot_ref[b0 + 1] = (y[:, LANES:] + zt_ref[b0 + 1] * skip) * xt_ref[b0 + 1]
        return c

    lax.fori_loop(0, HY_B1 // 2, body, 0, unroll=8)
    o_ref[0] = pltpu.einshape("bal->abl", ot_ref[...]).reshape(a1 * HY_B1, LANES)


def _hyout(ainv, w3, z, x0, skip):
    b, seq, hy = z.shape
    a1, k2 = ainv.shape
    blk = pl.BlockSpec((1, seq, LANES), lambda i, j: (i, 0, j))
    return pl.pallas_call(
        _hyout_kernel,
        grid=(b, hy // LANES),
        in_specs=[pl.BlockSpec((a1, k2), lambda i, j: (0, 0)),
                  pl.BlockSpec((1, k2 * HY_B1, LANES), lambda i, j: (i, 0, j)),
                  blk, blk,
                  pl.BlockSpec((1, LANES), lambda i, j: (0, j))],
        out_specs=blk,
        out_shape=jax.ShapeDtypeStruct((b, seq, hy), F32),
        scratch_shapes=[pltpu.VMEM((HY_B1, k2, LANES), F32)] + [pltpu.VMEM((HY_B1, a1, LANES), F32)] * 3,
        compiler_params=_cparams(("parallel", "parallel")),
        name="hyena_out",
    )(ainv, w3, z, x0, skip)


def _dft_tables(a1):
    a2, b2, b1 = 2 * a1, 2 * HY_B1, HY_B1
    nka = a1 + 1
    kap = -(-nka // 8) * 8
    ka = np.arange(nka)[:, None]
    def fa(na):
        ph = 2 * np.pi * ((ka * np.arange(na)[None, :]) % a2) / a2
        m = np.zeros((2 * kap, na))
        m[0:2 * nka:2] = np.cos(ph)
        m[1:2 * nka:2] = -np.sin(ph)
        return m
    kb = np.arange(b2)[:, None]
    th = 2 * np.pi * ((kb * np.arange(b2)[None, :]) % b2) / b2
    c, s = np.cos(th), np.sin(th)
    fb_full = np.block([[c, s], [-s, c]])
    fb_half = np.block([[c[:, :b1], s[:, :b1]], [-s[:, :b1], c[:, :b1]]])
    ct, st = c.T[:b1], s.T[:b1]
    fbi = np.block([[ct, -st], [st, ct]]) / b2
    ph = 2 * np.pi * ((np.arange(a1)[:, None] * np.arange(nka)[None, :]) % a2) / a2
    wgt = np.where((np.arange(nka) == 0) | (np.arange(nka) == a1), 1.0, 2.0)[None, :] / a2
    ainv = np.zeros((a1, 2 * kap))
    ainv[:, 0:2 * nka:2] = wgt * np.cos(ph)
    ainv[:, 1:2 * nka:2] = -wgt * np.sin(ph)
    fa2 = fa(a2)
    fa_filt2 = np.stack([fa2[:, 0:a1], fa2[:, 1:a1 + 1]], axis=1).reshape(4 * kap, a1)
    fb_conj = np.concatenate([fb_full[:b2], -fb_full[b2:]], axis=0)
    f32 = lambda v: np.asarray(v, np.float32)
    return dict(nka=nka, kap=kap, fa_data=f32(fa(a1)), fa_filt2=f32(fa_filt2), fb_full=f32(fb_full),
                fb_conj=f32(fb_conj), fb_half=f32(fb_half), fbi=f32(fbi), ainv=f32(ainv))


def _hyena_filter_spectra(seq, hy, w1, b1, w2, b2, wo, fr, tabs):
    nbands = (FILTER_EMB - 1) // 2
    t = jnp.linspace(0.0, 1.0, seq, dtype=F32)[:, None]
    ang = 2.0 * math.pi * jnp.arange(seq, dtype=F32)[:, None] / seq
    f = jnp.linspace(1e-4, nbands - 1, nbands, dtype=F32)[None, :]
    zemb = jnp.concatenate([t, jnp.cos(f * ang), -jnp.sin(f * ang)], axis=-1)
    deltas = jnp.abs(jnp.linspace(math.log(DECAY_TARGET) / DECAY_FAST,
                                  math.log(DECAY_TARGET) / DECAY_SLOW, hy, dtype=F32))
    decay = jnp.exp(-t * deltas)
    dec = jnp.stack([decay, decay * (jnp.arange(seq) > 0)[:, None].astype(F32)])
    depth = w1.shape[0]
    h = _filter_mlp(zemb, w1, b1[:, None], w2, b2[:, None], wo, fr[:, None], dec, tl=min(512, seq))
    pieces = lambda m: jnp.stack(_split_bf16(jnp.asarray(m)))
    ya = _hyadft(pieces(tabs["fa_filt2"]), h.reshape(depth * 2, seq, hy), name="hyena_filter_adft")
    ya5 = ya.reshape(depth, 2, tabs["kap"], 4 * HY_B1, hy)
    return _filter_bdft(pieces(tabs["fb_full"]), pieces(tabs["fb_conj"]), ya5, tabs["nka"])


def _hyena(z, x0, skip, hhat, layer, tabs):
    b, seq, hy = z.shape
    kap, nka = tabs["kap"], tabs["nka"]
    ya = _hyadft(jnp.asarray(tabs["fa_data"], BF16)[None], z, out_dtype=BF16)
    y4 = ya.reshape(b, kap, 2 * HY_B1, hy)
    w4 = _hyfreq(y4, jnp.asarray(tabs["fb_half"], BF16), hhat, jnp.asarray(tabs["fbi"], BF16), nka, layer)
    w3 = w4.reshape(b, 2 * kap * HY_B1, hy)
    return _hyout(jnp.asarray(tabs["ainv"], BF16), w3, z, x0, skip[None])


def _na_geometry(rows):
    gr, kr_n, kh = NA_GROUP_ROWS, NA_KEY_ROWS, NA_KH_MAX
    n_g = rows // gr
    geo = []
    for g in (0, 1, n_g - 1):
        ks = min(max(gr * g - kh // 2, 0), rows - kr_n)
        per_q = []
        for qr in range(gr):
            r = gr * g + qr
            rs = min(max(r - kh // 2, 0), rows - kh)
            per_q.append([((rs <= ks + k < rs + kh), ks + k - r + NA_KH_MAX - 1) for k in range(kr_n)])
        geo.append(per_q)
    return geo


def _na_table_kernel(r_ref, t_ref, *, geo):
    w, kw = GRID_W, NA_KW
    qc = lax.broadcasted_iota(I32, (w, 1), 0)
    lane = lax.broadcasted_iota(I32, (1, LANES), 1)
    kc = lane % w
    cs = jnp.clip(qc - kw // 2, 0, w - kw)
    colvalid = (kc >= cs) & (kc < cs + kw)
    left = lane < w
    neg = jnp.full((w, LANES), NEG, F32)
    shift = LANES - (kw - 1)

    cache = {}

    def toeplitz(dr, lane_off):
        if (dr, lane_off) not in cache:
            row = r_ref[0, 0, dr:dr + 1, :]
            if lane_off:
                row = pltpu.roll(row, lane_off, 1)
            tz = pltpu.roll(jnp.broadcast_to(row, (w, LANES)), shift, 1, stride=1, stride_axis=0)
            cache[(dr, lane_off)] = jnp.where(colvalid, tz, neg)
        return cache[(dr, lane_off)]

    for v, per_q in enumerate(geo):
        for qr, per_k in enumerate(per_q):
            for pair in range(len(per_k) // 2):
                (ok0, dr0), (ok1, dr1) = per_k[2 * pair], per_k[2 * pair + 1]
                lo = toeplitz(dr0, 0) if ok0 else neg
                hi = toeplitz(dr1, w) if ok1 else neg
                t_ref[0, v, 0, qr * w:(qr + 1) * w, pair * LANES:(pair + 1) * LANES] = jnp.where(left, lo, hi)


def _na_tables(rpb_all, rows):
    depth, heads, nr, nc = rpb_all.shape
    rp = jnp.pad(rpb_all.astype(F32), ((0, 0), (0, 0), (0, 16 - nr), (0, LANES - nc)))
    tq, tk = NA_GROUP_ROWS * GRID_W, NA_KEY_ROWS * GRID_W
    return pl.pallas_call(
        functools.partial(_na_table_kernel, geo=_na_geometry(rows)),
        grid=(depth, heads),
        in_specs=[pl.BlockSpec((1, 1, 16, LANES), lambda l, h: (l, h, 0, 0))],
        out_specs=pl.BlockSpec((1, 3, 1, tq, tk), lambda l, h: (l, 0, h, 0, 0)),
        out_shape=jax.ShapeDtypeStruct((depth, 3, heads, tq, tk), F32),
        compiler_params=_cparams(("parallel", "parallel")),
        name="natten_tables",
    )(rp)


def _natten_kernel(q_ref, k_ref, v_ref, t_ref, o_ref, *, heads, n_g, rows):
    g = pl.program_id(1)
    tq = q_ref.shape[1]
    tk = t_ref.shape[4]
    ks = jnp.clip(NA_GROUP_ROWS * g - NA_KH_MAX // 2, 0, rows - NA_KEY_ROWS)
    kstart = pl.multiple_of(ks * GRID_W, GRID_W)
    per_tile = LANES // NA_HEAD_DIM
    lane = lax.broadcasted_iota(I32, (1, LANES), 1)
    ones = jnp.ones((tk, LANES), BF16)
    for j in range(heads // per_tile):
        lanes = slice(j * LANES, (j + 1) * LANES)
        q2 = q_ref[0, :, lanes].astype(F32) * (NA_HEAD_DIM ** -0.5)
        k2 = k_ref[0, pl.ds(kstart, tk), lanes]
        vaug = jnp.concatenate([v_ref[0, pl.ds(kstart, tk), lanes], ones], axis=1)
        o2 = None
        for hh in range(per_tile):
            own = (lane >= hh * NA_HEAD_DIM) & (lane < (hh + 1) * NA_HEAD_DIM)
            qm = jnp.where(own, q2, 0.0).astype(BF16)
            s = lax.dot_general(qm, k2, (((1,), (1,)), ((), ())), preferred_element_type=F32)
            s = s + t_ref[0, 0, j * per_tile + hh]
            m = jnp.max(s, axis=-1, keepdims=True)
            p = jnp.exp((s - m).astype(BF16))
            r = jnp.dot(p, vaug, preferred_element_type=F32)
            o = r[:, :LANES] / r[:, LANES:]
            o2 = o if o2 is None else jnp.where(own, o, o2)
        o_ref[0, :, lanes] = o2.astype(o_ref.dtype)


def _natten(qkv, tables, layer, b, seq, naw):
    heads = naw // NA_HEAD_DIM
    rows = seq // GRID_W
    n_g = rows // NA_GROUP_ROWS
    tq = NA_GROUP_ROWS * GRID_W
    tk = NA_KEY_ROWS * GRID_W

    def tmap(i, g):
        return (layer, jnp.where(g == 0, 0, jnp.where(g == n_g - 1, 2, 1)), 0, 0, 0)

    return pl.pallas_call(
        functools.partial(_natten_kernel, heads=heads, n_g=n_g, rows=rows),
        grid=(b, n_g),
        in_specs=[pl.BlockSpec((1, tq, naw), lambda i, g: (i, g, 0)),
                  pl.BlockSpec((1, seq, naw), lambda i, g: (i, 0, 1)),
                  pl.BlockSpec((1, seq, naw), lambda i, g: (i, 0, 2)),
                  pl.BlockSpec((1, 1, heads, tq, tk), tmap)],
        out_specs=pl.BlockSpec((1, tq, naw), lambda i, g: (i, g, 0)),
        out_shape=jax.ShapeDtypeStruct((b, seq, naw), F32),
        compiler_params=_cparams(("parallel", "arbitrary")),
        name="natten",
    )(qkv, qkv, qkv, tables)


def _pool_kernel(u_ref, w_ref, sc_ref, o_ref, pad_ref, *, pw):
    seq = u_ref.shape[1]
    lp = seq + 2 * POOL_PAD
    j = pl.program_id(1)
    u = u_ref[0]
    zeros = jnp.zeros((POOL_PAD, LANES), F32)
    pad_ref[0:POOL_PAD, :] = zeros
    pad_ref[POOL_PAD + seq:lp, :] = zeros
    pad_ref[POOL_PAD:POOL_PAD + seq, :] = u
    xp = pad_ref[...]
    dn = lambda a, k: pltpu.roll(a, k, 0)
    up = lambda a, k: pltpu.roll(a, lp - k, 0)
    s2 = xp + dn(xp, 1)
    s4 = dn(s2, 1) + up(s2, 1)
    s8 = dn(s4, 2) + up(s4, 2)
    s16 = dn(s8, 4) + up(s8, 4)
    lane = lax.broadcasted_iota(I32, (1, LANES), 1) + j * LANES
    gdim = pw // len(POOL_WINDOWS)
    grp = lane // gdim
    sums = (s2, s4, s8, s16)
    wsum = sums[0][POOL_PAD:POOL_PAD + seq]
    half = jnp.full((1, LANES), POOL_WINDOWS[0] // 2, F32)
    for gi in range(1, len(POOL_WINDOWS)):
        wsum = jnp.where(grp == gi, sums[gi][POOL_PAD:POOL_PAD + seq], wsum)
        half = jnp.where(grp == gi, float(POOL_WINDOWS[gi] // 2), half)
    t = lax.broadcasted_iota(I32, (seq, LANES), 0).astype(F32)
    cnt = jnp.minimum(t + half, float(seq)) - jnp.maximum(t - half, 0.0)
    pooled = wsum / cnt - u
    y = jnp.dot(pooled.astype(BF16), w_ref[0], preferred_element_type=F32)
    o_ref[0] = y * sc_ref[...]


def _pool(u, pool_w, pool_scale):
    b, seq, pw = u.shape
    ng, gd, _ = pool_w.shape
    nh = pw // LANES
    per = LANES // gd
    wbd = jnp.zeros((nh, LANES, LANES), F32)
    for gi in range(ng):
        hh, k = divmod(gi, per)
        wbd = wbd.at[hh, k * gd:(k + 1) * gd, k * gd:(k + 1) * gd].set(pool_w[gi])
    return pl.pallas_call(
        functools.partial(_pool_kernel, pw=pw),
        grid=(b, nh),
        in_specs=[pl.BlockSpec((1, seq, LANES), lambda i, j: (i, 0, j)),
                  pl.BlockSpec((1, LANES, LANES), lambda i, j: (j, 0, 0)),
                  pl.BlockSpec((1, LANES), lambda i, j: (0, j))],
        out_specs=pl.BlockSpec((1, seq, LANES), lambda i, j: (i, 0, j)),
        out_shape=jax.ShapeDtypeStruct((b, seq, pw), F32),
        scratch_shapes=[pltpu.VMEM((seq + 2 * POOL_PAD, LANES), F32)],
        compiler_params=_cparams(("parallel", "parallel")),
        name="pool_mixer",
    )(u, wbd.astype(BF16), pool_scale[None])


def _outproj_kernel(x_ref, yh_ref, yn_ref, yp_ref, gm_ref, w_ref, g2_ref, wr_ref,
                    xo_ref, h_ref, aff_ref, *, hy, naw):
    gm = gm_ref[...]
    m1 = _rms(yh_ref[...], gm[:, :hy]).astype(BF16)
    m2 = _rms(yn_ref[...], gm[:, hy:hy + naw]).astype(BF16)
    m3 = _rms(yp_ref[...], gm[:, hy + naw:]).astype(BF16)
    acc = jnp.dot(m1, w_ref[:hy, :], preferred_element_type=F32)
    acc += jnp.dot(m2, w_ref[hy:hy + naw, :], preferred_element_type=F32)
    acc += jnp.dot(m3, w_ref[hy + naw:, :], preferred_element_type=F32)
    xn = x_ref[...] + acc
    xo_ref[...] = xn
    h = _rms(xn, g2_ref[...])
    h_ref[...] = h.reshape(h_ref.shape)
    logits = _dot_x3(wr_ref[0], wr_ref[1], h, (((1,), (1,)), ((), ())))
    mx = jnp.max(logits, axis=0, keepdims=True)
    ex = jnp.exp(logits - mx)
    aff_ref[...] = ex / jnp.sum(ex, axis=0, keepdims=True)


def _outproj(x2, yh, yn, yp, gm, w_bf, g2, wr_t, tm=512):
    n, d = x2.shape
    hy, naw, pw = yh.shape[1], yn.shape[1], yp.shape[1]
    e = wr_t.shape[1]
    row = lambda c: pl.BlockSpec((tm, c), lambda i: (i, 0))
    full = lambda s: pl.BlockSpec(s, lambda i: (0, 0))
    return pl.pallas_call(
        functools.partial(_outproj_kernel, hy=hy, naw=naw),
        grid=(n // tm,),
        in_specs=[row(d), row(hy), row(naw), row(pw), full((1, d)), full((d, d)), full((1, d)),
                  pl.BlockSpec((2, e, d), lambda i: (0, 0, 0))],
        out_specs=[row(d), pl.BlockSpec((tm, d // LANES, LANES), lambda i: (i, 0, 0)),
                   pl.BlockSpec((e, tm), lambda i: (0, i))],
        out_shape=[jax.ShapeDtypeStruct((n, d), F32), jax.ShapeDtypeStruct((n, d // LANES, LANES), F32),
                   jax.ShapeDtypeStruct((e, n), F32)],
        compiler_params=_cparams(("parallel",)),
        name="outproj_router",
    )(x2, yh, yn, yp, gm, w_bf, g2, wr_t)


def _block_cumsum(x, tri):
    r, n = x.shape
    cls, offs = [], []
    off = jnp.zeros((r, 1), F32)
    for j in range(n // LANES):
        c = jnp.dot(x[:, j * LANES:(j + 1) * LANES], tri, preferred_element_type=F32)
        cls.append(c)
        off = off + c[:, LANES - 1:LANES]
        offs.append(off)
    return cls, offs


def _route_kernel(aff_ref, tri_ref, bci_ref, bcx_ref, idx_ref, gate_ref, blk_ref, *, cap):
    aff = aff_ref[...]
    e, seq = aff.shape
    nblk = seq // LANES
    bits = pltpu.bitcast(aff, I32)
    capf = jnp.float32(cap)

    def radix(i, prefix):
        cand = prefix | jnp.left_shift(jnp.int32(1), 30 - i)
        cnt = jnp.sum((bits >= cand).astype(F32), axis=1, keepdims=True)
        return jnp.where(cnt >= capf, cand, prefix)

    tau = lax.fori_loop(0, 31, radix, jnp.zeros((e, 1), I32))
    gt = bits > tau
    eq = bits == tau
    need = capf - jnp.sum(gt.astype(F32), axis=1, keepdims=True)
    tri = tri_ref[...]
    cls, offs = _block_cumsum(jnp.where(eq, 1.0, 0.0).astype(BF16), tri)
    tie_rank = jnp.concatenate([c if j == 0 else c + offs[j - 1] for j, c in enumerate(cls)], axis=1)
    sel = gt | (eq & (tie_rank <= need))
    self32 = jnp.where(sel, 1.0, 0.0)
    selb = self32.astype(BF16)
    cls, _ = _block_cumsum(selb, tri)
    bend = jnp.dot(selb, bci_ref[...], preferred_element_type=F32)
    bstart = jnp.dot(selb, bcx_ref[...], preferred_element_type=F32)
    for j in range(nblk):
        rows = slice(j * e, (j + 1) * e)
        lanes = slice(j * LANES, (j + 1) * LANES)
        blk_ref[0, rows, :] = cls[j]
        blk_ref[1, rows, :] = self32[:, lanes]
        blk_ref[2, rows, :] = aff[:, lanes]
    slot = lax.broadcasted_iota(I32, (cap, 1), 0).astype(F32)
    lane = lax.broadcasted_iota(I32, (1, LANES), 1).astype(F32)
    for ei in range(e):
        bs, be = bstart[ei:ei + 1, :], bend[ei:ei + 1, :]
        inblk = (bs <= slot) & (slot < be)
        local = slot - jnp.sum(jnp.where(inblk, bs, 0.0), axis=1, keepdims=True)
        jcol = jnp.sum(jnp.where(inblk, lane, 0.0), axis=1, keepdims=True)
        pick = jnp.where(inblk, 1.0, 0.0)[:, :nblk].astype(BF16)
        rows = pl.ds(ei, nblk, stride=e)
        a = blk_ref[2, rows, :]
        a_hi = a.astype(BF16)
        r1 = a - a_hi.astype(F32)
        a_mid = r1.astype(BF16)
        a_lo = (r1 - a_mid.astype(F32)).astype(BF16)
        take = lambda v: jnp.dot(pick, v, preferred_element_type=F32)
        g_cl = take(blk_ref[0, rows, :].astype(BF16))
        g_sel = take(blk_ref[1, rows, :].astype(BF16))
        g_aff = take(a_hi) + take(a_mid) + take(a_lo)
        hit = (g_cl == local + 1.0) & (g_sel > 0.5)
        idx = jcol * LANES + jnp.sum(jnp.where(hit, lane, 0.0), axis=1, keepdims=True)
        gate = jnp.sum(jnp.where(hit, g_aff, 0.0), axis=1, keepdims=True)
        idx_ref[0, :, ei:ei + 1] = idx.astype(I32)
        gate_ref[0, ei] = jnp.broadcast_to(gate, (cap, LANES))


def _route(aff_t, b, seq, cap):
    e = aff_t.shape[0]
    nblk = seq // LANES
    tri = jnp.asarray(np.triu(np.ones((LANES, LANES), np.float32)), BF16)
    tblk = np.arange(seq)[:, None] // LANES
    bci = jnp.asarray(tblk <= np.arange(LANES)[None, :], BF16)
    bcx = jnp.asarray(tblk < np.arange(LANES)[None, :], BF16)
    full = lambda s: pl.BlockSpec(s, lambda i: (0, 0))
    idx, gate = pl.pallas_call(
        functools.partial(_route_kernel, cap=cap),
        grid=(b,),
        in_specs=[pl.BlockSpec((e, seq), lambda i: (0, i)), full((LANES, LANES)),
                  full((seq, LANES)), full((seq, LANES))],
        out_specs=[pl.BlockSpec((1, cap, e), lambda i: (i, 0, 0)),
                   pl.BlockSpec((1, e, cap, LANES), lambda i: (i, 0, 0, 0))],
        out_shape=[jax.ShapeDtypeStruct((b, cap, e), I32), jax.ShapeDtypeStruct((b, e, cap, LANES), F32)],
        scratch_shapes=[pltpu.VMEM((3, nblk * e, LANES), F32)],
        compiler_params=_cparams(("parallel",)),
        name="ec_route",
    )(aff_t, tri, bci, bcx)
    return idx.transpose(0, 2, 1), gate


def _expert_kernel(rows_ref, h_hbm, wg_ref, wu_ref, wd_ref, y_ref, xbuf, xb, acc, sem, *, tm, nm, nf, ne):
    e = pl.program_id(0)
    m = pl.program_id(1)
    f = pl.program_id(2)
    tile = e * nm + m
    ntiles = ne * nm
    slot = tile % 2
    chunk = tm // nf

    def row_copy(src_row, dst_slot, dst_chunk, dst_row):
        return pltpu.make_async_copy(h_hbm.at[pl.ds(src_row, 1)],
                                     xbuf.at[dst_slot, dst_chunk, pl.ds(dst_row, 1)], sem.at[dst_slot])

    def tile_wait(dst_slot):
        for k in range(nf):
            pltpu.make_async_copy(h_hbm.at[pl.ds(0, chunk)], xbuf.at[dst_slot, k], sem.at[dst_slot]).wait()

    @pl.when((tile == 0) & (f == 0))
    def _():
        for k in range(nf):
            def issue(i, c, k=k):
                row_copy(rows_ref[k * chunk + i], 0, k, i).start()
                return c

            lax.fori_loop(0, chunk, issue, 0, unroll=8)

    @pl.when(f == 0)
    def _():
        tile_wait(slot)
        rb = min(EXPERT_RELAYOUT_ROWS, chunk)
        for k in range(nf):
            def to_rows(c, carry, k=k):
                r0 = pl.multiple_of(c * rb, rb)
                xb[pl.ds(k * chunk + r0, rb), :] = xbuf[slot, k, pl.ds(r0, rb)].reshape(rb, xb.shape[1]).astype(BF16)
                return carry

            lax.fori_loop(0, chunk // rb, to_rows, 0)
        acc[...] = jnp.zeros(acc.shape, F32)

    nxt = jnp.minimum(tile + 1, ntiles - 1)
    base = nxt * tm + f * chunk
    for i in range(chunk):
        row_copy(rows_ref[base + i], 1 - slot, f, i).start()

    x = xb[...]
    a = jnp.dot(x, wg_ref[0, 0].astype(BF16), preferred_element_type=F32)
    u = jnp.dot(x, wu_ref[0, 0].astype(BF16), preferred_element_type=F32)
    hh = (a * jax.nn.sigmoid(a) * u).astype(BF16)
    acc[...] += jnp.dot(hh, wd_ref[0, 0].astype(BF16), preferred_element_type=F32)

    @pl.when(f == nf - 1)
    def _():
        y_ref[0] = acc[...].astype(y_ref.dtype)

    @pl.when((tile == ntiles - 1) & (f == nf - 1))
    def _():
        tile_wait(1 - slot)


EXPERT_RELAYOUT_ROWS = 64


def _experts(rows_flat, h3, w_gate, w_up, w_down, layer, mtot, tm=1024, tf=512):
    _, e, d, ff = w_gate.shape
    tm = min(tm, mtot)
    tf = min(tf, ff)
    nm, nf = mtot // tm, ff // tf
    grid_spec = pltpu.PrefetchScalarGridSpec(
        num_scalar_prefetch=1,
        grid=(e, nm, nf),
        in_specs=[pl.BlockSpec(memory_space=pl.ANY),
                  pl.BlockSpec((1, 1, d, tf), lambda i, m, f, r: (layer, i, 0, f)),
                  pl.BlockSpec((1, 1, d, tf), lambda i, m, f, r: (layer, i, 0, f)),
                  pl.BlockSpec((1, 1, tf, d), lambda i, m, f, r: (layer, i, f, 0))],
        out_specs=pl.BlockSpec((1, tm, d), lambda i, m, f, r: (i, m, 0)),
        scratch_shapes=[pltpu.VMEM((2, nf, tm // nf, d // LANES, LANES), F32), pltpu.VMEM((tm, d), BF16),
                        pltpu.VMEM((tm, d), F32),
                        pltpu.SemaphoreType.DMA((2,))],
    )
    return pl.pallas_call(
        functools.partial(_expert_kernel, tm=tm, nm=nm, nf=nf, ne=e),
        grid_spec=grid_spec,
        out_shape=jax.ShapeDtypeStruct((e, mtot, d), BF16),
        compiler_params=_cparams(("arbitrary", "arbitrary", "arbitrary")),
        name="ec_experts",
    )(rows_flat, h3, w_gate, w_up, w_down)


COMBINE_ROWS = 64
COMBINE_UNROLL = 8


def _combine_kernel(idx_ref, split_ref, x_ref, y_ref, gl, *rest, ne, cap, span, final):
    if final:
        g_ref, o_ref, acc3, y3 = rest
    else:
        o_ref, acc3, y3 = rest
    b = pl.program_id(0)
    sp = pl.program_id(1)
    e = pl.program_id(2)
    d = x_ref.shape[2]
    sub = d // LANES
    rb = COMBINE_ROWS

    @pl.when(e == 0)
    def _():
        def load(c, carry):
            r0 = pl.multiple_of(c * rb, rb)
            acc3[pl.ds(pl.multiple_of(r0 * sub, rb * sub), rb * sub), :] = (
                x_ref[0, pl.ds(r0, rb), :].reshape(rb * sub, LANES))
            return carry

        lax.fori_loop(0, span // rb, load, 0)

    lo = split_ref[(b * ne + e) * 3 + sp]
    hi = split_ref[(b * ne + e) * 3 + sp + 1]

    def relayout(c, carry):
        r0 = pl.multiple_of(c * rb, rb)
        y3[pl.ds(r0, rb)] = y_ref[0, pl.ds(r0, rb), :].astype(F32).reshape(rb, sub, LANES)
        return carry

    lax.fori_loop(lo // rb, (hi + rb - 1) // rb, relayout, 0)

    base = (b * ne + e) * cap
    nu = COMBINE_UNROLL

    def tokens(first, count):
        return tuple(idx_ref[base + first + u] for u in range(count))

    def add_rows(first, toks):
        rows = [pl.ds(pl.multiple_of(t, sub), sub) for t in toks]
        vals = [acc3[r, :] + gl[0, 0, pl.ds(first + u, 1), :] * y3[first + u] for u, r in enumerate(rows)]
        for r, v in zip(rows, vals):
            acc3[r, :] = v

    def group(k, toks):
        first = lo + k * nu
        nxt = tokens(jnp.minimum(first + nu, cap - nu), nu)
        add_rows(first, toks)
        return nxt

    ngroups = (hi - lo) // nu
    lax.fori_loop(0, ngroups, group, tokens(jnp.minimum(lo, cap - nu), nu))

    def tail(i, carry):
        add_rows(i, tokens(i, 1))
        return carry

    lax.fori_loop(lo + ngroups * nu, hi, tail, 0)

    @pl.when(e == ne - 1)
    def _():
        def store(c, carry):
            r0 = pl.multiple_of(c * rb, rb)
            v = acc3[pl.ds(pl.multiple_of(r0 * sub, rb * sub), rb * sub), :].reshape(rb, d)
            if final:
                v = _rms(v, g_ref[...])
            o_ref[0, pl.ds(r0, rb), :] = v
            return carry

        lax.fori_loop(0, span // rb, store, 0)


def _combine(idx, gate, x3, y, cap, final_g=None):
    b, seq, d = x3.shape
    ne = y.shape[0]
    span = seq // 2
    n_lower = jnp.sum((idx < span).astype(I32), axis=-1)
    split_flat = jnp.stack([jnp.zeros_like(n_lower), n_lower, jnp.full_like(n_lower, cap)], axis=-1).reshape(-1)
    final = final_g is not None
    in_specs = [pl.BlockSpec((1, span, d), lambda i, s, e, *_: (i, s, 0)),
                pl.BlockSpec((1, cap, d), lambda i, s, e, *_: (e, i, 0)),
                pl.BlockSpec((1, 1, cap, LANES), lambda i, s, e, *_: (i, e, 0, 0))]
    args = [x3, y, gate]
    if final:
        in_specs.append(pl.BlockSpec((1, d), lambda i, s, e, *_: (0, 0)))
        args.append(final_g)
    grid_spec = pltpu.PrefetchScalarGridSpec(
        num_scalar_prefetch=2,
        grid=(b, 2, ne),
        in_specs=in_specs,
        out_specs=pl.BlockSpec((1, span, d), lambda i, s, e, *_: (i, s, 0)),
        scratch_shapes=[pltpu.VMEM((span * (d // LANES), LANES), F32), pltpu.VMEM((cap, d // LANES, LANES), F32)],
    )
    return pl.pallas_call(
        functools.partial(_combine_kernel, ne=ne, cap=cap, span=span, final=final),
        grid_spec=grid_spec,
        out_shape=jax.ShapeDtypeStruct((b, seq, d), F32),
        compiler_params=_cparams(("arbitrary", "arbitrary", "arbitrary")),
        name="ec_combine",
    )(((idx % span) * (d // LANES)).reshape(-1), split_flat, *args)


def _moe(x3, h2, aff_t, w_gate, w_up, w_down, layer, final_g=None):
    b, seq, d = x3.shape
    ne = w_gate.shape[1]
    cap = EC_CAPACITY * seq // ne
    idx, gate = _route(aff_t, b, seq, cap)
    rows = idx + (jnp.arange(b, dtype=I32) * seq)[:, None, None]
    rows_flat = rows.transpose(1, 0, 2).reshape(-1)
    y = _experts(rows_flat, h2, w_gate, w_up, w_down, layer, b * cap)
    return _combine(idx, gate, x3, y, cap, final_g)


def kernel(x, norm1_g, w_in, hy_short_w, hy_short_b, hy_f_w1, hy_f_b1, hy_f_w2, hy_f_b2, hy_f_wout, hy_f_freq, hy_skip, na_rpb, pool_w, pool_scale, mix_norm_g, w_out, norm2_g, w_router, w_gate, w_up, w_down, final_g):
    b, seq, d = x.shape
    depth = w_in.shape[0]
    hy = hy_skip.shape[1]
    pw = pool_scale.shape[1]
    naw = d - hy - pw
    n = b * seq
    rows = seq // GRID_W
    tabs = _dft_tables(seq // HY_B1)
    x2 = x.reshape(n, d)
    hhat = _hyena_filter_spectra(seq, hy, hy_f_w1, hy_f_b1, hy_f_w2, hy_f_b2, hy_f_wout, hy_f_freq, tabs)
    na_tables = _na_tables(na_rpb, rows)
    for i in range(depth):
        z, x0, qkv, pool_in = _inproj(x2, norm1_g[i][None], w_in[i].astype(BF16), hy_short_w[i],
                                      hy_short_b[i][None], 3 * hy, 3 * naw, seq)
        y_hy = _hyena(z.reshape(b, seq, hy), x0.reshape(b, seq, hy), hy_skip[i], hhat, i, tabs)
        y_na = _natten(qkv.reshape(b, seq, 3 * naw), na_tables, i, b, seq, naw)
        y_pool = _pool(pool_in.reshape(b, seq, pw), pool_w[i], pool_scale[i])
        x2, h2, aff_t = _outproj(x2, y_hy.reshape(n, hy), y_na.reshape(n, naw), y_pool.reshape(n, pw),
                                 mix_norm_g[i][None], w_out[i].astype(BF16), norm2_g[i][None],
                                 jnp.stack(_split_bf16(w_router[i].T)))
        last = final_g[None] if i == depth - 1 else None
        x2 = _moe(x2.reshape(b, seq, d), h2, aff_t, w_gate, w_up, w_down, i, last).reshape(n, d)
    return x2.reshape(b, seq, d)
```

```python
import functools
import math

import numpy as np
import jax
import jax.numpy as jnp
from jax import lax
from jax.experimental import pallas as pl
from jax.experimental.pallas import tpu as pltpu

F32 = jnp.float32
BF16 = jnp.bfloat16
I32 = jnp.int32
EPS = 1e-6
HIGHEST = lax.Precision.HIGHEST

GRID_W = 64
NA_HEAD_DIM = 64
NA_KH_MAX = 8
NA_KW = 16
NA_GROUP_ROWS = 4
NA_KEY_ROWS = 12
POOL_WINDOWS = (2, 4, 8, 16)
POOL_PAD = 16
FILTER_EMB = 33
DECAY_FAST, DECAY_SLOW, DECAY_TARGET = 0.3, 1.5, 1e-2
EC_CAPACITY = 2
HY_B1 = 128
NEG = -1e30
LANES = 128
V7X_VMEM_BYTES = 64 * 1024 * 1024
VMEM_LIMIT = V7X_VMEM_BYTES - 8 * 1024 * 1024


def _cparams(sem, vmem=VMEM_LIMIT):
    return pltpu.CompilerParams(dimension_semantics=sem, vmem_limit_bytes=vmem)


def _rms(v, g):
    return v * lax.rsqrt(jnp.mean(v * v, axis=-1, keepdims=True) + EPS) * g


def _split_bf16(v):
    hi = v.astype(BF16)
    return hi, (v - hi.astype(F32)).astype(BF16)


def _dot_x3(a_hi, a_lo, x, dims=(((1,), (0,)), ((), ()))):
    x_hi, x_lo = _split_bf16(x)
    dg = lambda p, q: lax.dot_general(p, q, dims, preferred_element_type=F32)
    return dg(a_hi, x_hi) + dg(a_lo, x_hi) + dg(a_hi, x_lo)


HALO = 16


def _inproj_kernel(x_ref, xp_ref, xn_ref, g_ref, w_ref, sw_ref, sb_ref, z_ref, x0_ref, qkv_ref, pool_ref,
                   *, hyw, naw, seq):
    g = g_ref[...]
    tm = x_ref.shape[0]
    h = _rms(x_ref[...], g).astype(BF16)
    hall = jnp.concatenate([_rms(xp_ref[...], g).astype(BF16), h, _rms(xn_ref[...], g).astype(BF16)], axis=0)
    u = jnp.dot(hall, w_ref[:, :hyw], preferred_element_type=F32)
    t = lax.rem(pl.program_id(0) * tm + lax.broadcasted_iota(I32, (tm, 1), 0), seq)
    prev = jnp.where(t == 0, 0.0, u[HALO - 1:HALO - 1 + tm])
    nxt = jnp.where(t == seq - 1, 0.0, u[HALO + 1:HALO + 1 + tm])
    conv = prev * sw_ref[0:1, :] + u[HALO:HALO + tm] * sw_ref[1:2, :] + nxt * sw_ref[2:3, :] + sb_ref[...]
    hy = hyw // 3
    x0_ref[...] = conv[:, :hy]
    z_ref[...] = conv[:, 2 * hy:] * conv[:, hy:2 * hy]
    qkv_ref[...] = jnp.dot(h, w_ref[:, hyw:hyw + naw], preferred_element_type=F32).astype(BF16)
    pool_ref[...] = jnp.dot(h, w_ref[:, hyw + naw:], preferred_element_type=F32)


def _inproj(x2, g, w_bf, sw, sb, hyw, naw, seq, tm=512):
    n, d = x2.shape
    inw = w_bf.shape[1]
    pw = inw - hyw - naw
    hy = hyw // 3
    per = tm // HALO
    last = n // HALO - 1
    return pl.pallas_call(
        functools.partial(_inproj_kernel, hyw=hyw, naw=naw, seq=seq),
        grid=(n // tm,),
        in_specs=[pl.BlockSpec((tm, d), lambda i: (i, 0)),
                  pl.BlockSpec((HALO, d), lambda i: (jnp.maximum(i * per - 1, 0), 0)),
                  pl.BlockSpec((HALO, d), lambda i: (jnp.minimum((i + 1) * per, last), 0)),
                  pl.BlockSpec((1, d), lambda i: (0, 0)),
                  pl.BlockSpec((d, inw), lambda i: (0, 0)),
                  pl.BlockSpec((3, hyw), lambda i: (0, 0)),
                  pl.BlockSpec((1, hyw), lambda i: (0, 0))],
        out_specs=[pl.BlockSpec((tm, hy), lambda i: (i, 0)),
                   pl.BlockSpec((tm, hy), lambda i: (i, 0)),
                   pl.BlockSpec((tm, naw), lambda i: (i, 0)),
                   pl.BlockSpec((tm, pw), lambda i: (i, 0))],
        out_shape=[jax.ShapeDtypeStruct((n, hy), F32),
                   jax.ShapeDtypeStruct((n, hy), F32),
                   jax.ShapeDtypeStruct((n, naw), BF16),
                   jax.ShapeDtypeStruct((n, pw), F32)],
        compiler_params=_cparams(("parallel",)),
        name="inproj",
    )(x2, x2, x2, g, w_bf, sw, sb)


def _filter_kernel(z_ref, w1_ref, b1_ref, w2_ref, b2_ref, wo_ref, fr_ref, dec_ref, o_ref):
    fr = fr_ref[0]
    h = jnp.sin(fr * (jnp.dot(z_ref[...], w1_ref[0], preferred_element_type=F32, precision=HIGHEST) + b1_ref[0]))
    h = jnp.sin(fr * (jnp.dot(h, w2_ref[0], preferred_element_type=F32, precision=HIGHEST) + b2_ref[0]))
    hw = jnp.dot(h, wo_ref[0], preferred_element_type=F32, precision=HIGHEST)
    c = dec_ref.shape[2]
    o_ref[0, 0] = hw[:, :c] * dec_ref[0]
    o_ref[0, 1] = hw[:, c:] * dec_ref[1]


def _filter_mlp(zemb, w1, b1, w2, b2, wo, fr, dec, tl=512):
    seq, emb = zemb.shape
    depth, _, hid = w1.shape
    ow = wo.shape[2]
    c = ow // 2
    lay = lambda s: pl.BlockSpec((1,) + s, lambda l, i: (l, 0, 0))
    return pl.pallas_call(
        _filter_kernel,
        grid=(depth, seq // tl),
        in_specs=[pl.BlockSpec((tl, emb), lambda l, i: (i, 0)), lay((emb, hid)), lay((1, hid)),
                  lay((hid, hid)), lay((1, hid)), lay((hid, ow)), lay((1, hid)),
                  pl.BlockSpec((2, tl, c), lambda l, i: (0, i, 0))],
        out_specs=pl.BlockSpec((1, 2, tl, c), lambda l, i: (l, 0, i, 0)),
        out_shape=jax.ShapeDtypeStruct((depth, 2, seq, c), F32),
        compiler_params=_cparams(("parallel", "parallel")),
        name="hyena_filter_mlp",
    )(zemb, w1, b1, w2, b2, wo, fr, dec)


def _filter_bdft_kernel(ff_ref, fc_ref, yf_ref, yb_ref, o_ref):
    o_ref[0, 0] = (_dot_x3(ff_ref[0], ff_ref[1], yf_ref[0, 0, 0]) + _dot_x3(fc_ref[0], fc_ref[1], yb_ref[0, 0, 0]))


def _filter_bdft(ff, fc, ya5, nka):
    depth, _, kap, r, c = ya5.shape
    blk = lambda p: pl.BlockSpec((1, 1, 1, r, c), lambda l, k, p=p: (l, p, k, 0, 0))
    return pl.pallas_call(
        _filter_bdft_kernel,
        grid=(depth, nka),
        in_specs=[pl.BlockSpec((2, r, r), lambda l, k: (0, 0, 0)), pl.BlockSpec((2, r, r), lambda l, k: (0, 0, 0)),
                  blk(0), blk(1)],
        out_specs=pl.BlockSpec((1, 1, r, c), lambda l, k: (l, k, 0, 0)),
        out_shape=jax.ShapeDtypeStruct((depth, nka, r, c), F32),
        compiler_params=_cparams(("parallel", "parallel")),
        name="hyena_filter_bdft",
    )(ff, fc, ya5, ya5)


def _hyadft_kernel(fa_ref, z_ref, y_ref, zt_ref, yt_ref):
    pieces, k2, a1 = fa_ref.shape
    zt_ref[...] = pltpu.einshape("abl->bal", z_ref[0].reshape(a1, HY_B1, LANES))

    def body(i, c):
        b0 = 2 * i
        zz = jnp.concatenate([zt_ref[b0], zt_ref[b0 + 1]], axis=1)
        if pieces == 1:
            r = jnp.dot(fa_ref[0], zz.astype(BF16), preferred_element_type=F32)
        else:
            r = _dot_x3(fa_ref[0], fa_ref[1], zz)
        yt_ref[b0] = r[:, :LANES]
        yt_ref[b0 + 1] = r[:, LANES:]
        return c

    lax.fori_loop(0, HY_B1 // 2, body, 0, unroll=8)
    y_ref[0] = pltpu.einshape("bkl->kbl", yt_ref[...]).reshape(k2 * HY_B1, LANES).astype(y_ref.dtype)


def _hyadft(fa, z, out_dtype=F32, name="hyena_adft"):
    b, seq, hy = z.shape
    pieces, k2, a1 = fa.shape
    return pl.pallas_call(
        _hyadft_kernel,
        grid=(b, hy // LANES),
        in_specs=[pl.BlockSpec((pieces, k2, a1), lambda i, j: (0, 0, 0)),
                  pl.BlockSpec((1, seq, LANES), lambda i, j: (i, 0, j))],
        out_specs=pl.BlockSpec((1, k2 * HY_B1, LANES), lambda i, j: (i, 0, j)),
        out_shape=jax.ShapeDtypeStruct((b, k2 * HY_B1, hy), out_dtype),
        scratch_shapes=[pltpu.VMEM((HY_B1, a1, LANES), F32), pltpu.VMEM((HY_B1, k2, LANES), F32)],
        compiler_params=_cparams(("parallel", "parallel")),
        name=name,
    )(fa, z)


def _hyfreq_kernel(y_ref, fb_ref, h_ref, fbi_ref, w_ref, *, nka):
    ka = pl.program_id(0)
    nb = fb_ref.shape[0] // 2
    nbatch = y_ref.shape[0]

    @pl.when(ka < nka)
    def _():
        hr, hi = h_ref[0, 0, :nb], h_ref[0, 0, nb:]
        for bi in range(nbatch):
            p = jnp.dot(fb_ref[...], y_ref[bi, 0].astype(BF16), preferred_element_type=F32)
            pr, pi = p[:nb], p[nb:]
            q = jnp.concatenate([pr * hr - pi * hi, pr * hi + pi * hr], axis=0).astype(BF16)
            w_ref[bi, 0] = jnp.dot(fbi_ref[...], q, preferred_element_type=F32).astype(w_ref.dtype)

    @pl.when(ka >= nka)
    def _():
        w_ref[...] = jnp.zeros(w_ref.shape, w_ref.dtype)


def _hyfreq(y4, fb, hhat, fbi, nka, layer):
    b, kap, r2, hy = y4.shape
    nb2 = fb.shape[0]
    return pl.pallas_call(
        functools.partial(_hyfreq_kernel, nka=nka),
        grid=(kap,),
        in_specs=[pl.BlockSpec((b, 1, r2, hy), lambda k: (0, k, 0, 0)),
                  pl.BlockSpec((nb2, r2), lambda k: (0, 0)),
                  pl.BlockSpec((1, 1, nb2, hy), lambda k: (layer, jnp.minimum(k, nka - 1), 0, 0)),
                  pl.BlockSpec((r2, nb2), lambda k: (0, 0))],
        out_specs=pl.BlockSpec((b, 1, r2, hy), lambda k: (0, k, 0, 0)),
        out_shape=jax.ShapeDtypeStruct((b, kap, r2, hy), BF16),
        compiler_params=_cparams(("parallel",)),
        name="hyena_freq",
    )(y4, fb, hhat, fbi)


def _hyout_kernel(a_ref, w_ref, z_ref, x0_ref, sk_ref, o_ref, wt_ref, zt_ref, xt_ref, ot_ref):
    a1, k2 = a_ref.shape
    ainv = a_ref[...]
    skip = sk_ref[...]
    wt_ref[...] = pltpu.einshape("kbl->bkl", w_ref[0].astype(F32).reshape(k2, HY_B1, LANES))
    zt_ref[...] = pltpu.einshape("abl->bal", z_ref[0].reshape(a1, HY_B1, LANES))
    xt_ref[...] = pltpu.einshape("abl->bal", x0_ref[0].reshape(a1, HY_B1, LANES))

    def body(i, c):
        b0 = 2 * i
        ww = jnp.concatenate([wt_ref[b0], wt_ref[b0 + 1]], axis=1).astype(BF16)
        y = jnp.dot(ainv, ww, preferred_element_type=F32)
        ot_ref[b0] = (y[:, :LANES] + zt_ref[b0] * skip) * xt_ref[b0]
        ot_ref[b0 + 1] = (y[:, LANES:] + zt_ref[b0 + 1] * skip) * xt_ref[b0 + 1]
        return c

    lax.fori_loop(0, HY_B1 // 2, body, 0, unroll=8)
    o_ref[0] = pltpu.einshape("bal->abl", ot_ref[...]).reshape(a1 * HY_B1, LANES)


def _hyout(ainv, w3, z, x0, skip):
    b, seq, hy = z.shape
    a1, k2 = ainv.shape
    blk = pl.BlockSpec((1, seq, LANES), lambda i, j: (i, 0, j))
    return pl.pallas_call(
        _hyout_kernel,
        grid=(b, hy // LANES),
        in_specs=[pl.BlockSpec((a1, k2), lambda i, j: (0, 0)),
                  pl.BlockSpec((1, k2 * HY_B1, LANES), lambda i, j: (i, 0, j)),
                  blk, blk,
                  pl.BlockSpec((1, LANES), lambda i, j: (0, j))],
        out_specs=blk,
        out_shape=jax.ShapeDtypeStruct((b, seq, hy), F32),
        scratch_shapes=[pltpu.VMEM((HY_B1, k2, LANES), F32)] + [pltpu.VMEM((HY_B1, a1, LANES), F32)] * 3,
        compiler_params=_cparams(("parallel", "parallel")),
        name="hyena_out",
    )(ainv, w3, z, x0, skip)


def _dft_tables(a1):
    a2, b2, b1 = 2 * a1, 2 * HY_B1, HY_B1
    nka = a1 + 1
    kap = -(-nka // 8) * 8
    ka = np.arange(nka)[:, None]
    def fa(na):
        ph = 2 * np.pi * ((ka * np.arange(na)[None, :]) % a2) / a2
        m = np.zeros((2 * kap, na))
        m[0:2 * nka:2] = np.cos(ph)
        m[1:2 * nka:2] = -np.sin(ph)
        return m
    kb = np.arange(b2)[:, None]
    th = 2 * np.pi * ((kb * np.arange(b2)[None, :]) % b2) / b2
    c, s = np.cos(th), np.sin(th)
    fb_full = np.block([[c, s], [-s, c]])
    fb_half = np.block([[c[:, :b1], s[:, :b1]], [-s[:, :b1], c[:, :b1]]])
    ct, st = c.T[:b1], s.T[:b1]
    fbi = np.block([[ct, -st], [st, ct]]) / b2
    ph = 2 * np.pi * ((np.arange(a1)[:, None] * np.arange(nka)[None, :]) % a2) / a2
    wgt = np.where((np.arange(nka) == 0) | (np.arange(nka) == a1), 1.0, 2.0)[None, :] / a2
    ainv = np.zeros((a1, 2 * kap))
    ainv[:, 0:2 * nka:2] = wgt * np.cos(ph)
    ainv[:, 1:2 * nka:2] = -wgt * np.sin(ph)
    fa2 = fa(a2)
    fa_filt2 = np.stack([fa2[:, 0:a1], fa2[:, 1:a1 + 1]], axis=1).reshape(4 * kap, a1)
    fb_conj = np.concatenate([fb_full[:b2], -fb_full[b2:]], axis=0)
    f32 = lambda v: np.asarray(v, np.float32)
    return dict(nka=nka, kap=kap, fa_data=f32(fa(a1)), fa_filt2=f32(fa_filt2), fb_full=f32(fb_full),
                fb_conj=f32(fb_conj), fb_half=f32(fb_half), fbi=f32(fbi), ainv=f32(ainv))


def _hyena_filter_spectra(seq, hy, w1, b1, w2, b2, wo, fr, tabs):
    nbands = (FILTER_EMB - 1) // 2
    t = jnp.linspace(0.0, 1.0, seq, dtype=F32)[:, None]
    ang = 2.0 * math.pi * jnp.arange(seq, dtype=F32)[:, None] / seq
    f = jnp.linspace(1e-4, nbands - 1, nbands, dtype=F32)[None, :]
    zemb = jnp.concatenate([t, jnp.cos(f * ang), -jnp.sin(f * ang)], axis=-1)
    deltas = jnp.abs(jnp.linspace(math.log(DECAY_TARGET) / DECAY_FAST,
                                  math.log(DECAY_TARGET) / DECAY_SLOW, hy, dtype=F32))
    decay = jnp.exp(-t * deltas)
    dec = jnp.stack([decay, decay * (jnp.arange(seq) > 0)[:, None].astype(F32)])
    depth = w1.shape[0]
    h = _filter_mlp(zemb, w1, b1[:, None], w2, b2[:, None], wo, fr[:, None], dec, tl=min(512, seq))
    pieces = lambda m: jnp.stack(_split_bf16(jnp.asarray(m)))
    ya = _hyadft(pieces(tabs["fa_filt2"]), h.reshape(depth * 2, seq, hy), name="hyena_filter_adft")
    ya5 = ya.reshape(depth, 2, tabs["kap"], 4 * HY_B1, hy)
    return _filter_bdft(pieces(tabs["fb_full"]), pieces(tabs["fb_conj"]), ya5, tabs["nka"])


def _hyena(z, x0, skip, hhat, layer, tabs):
    b, seq, hy = z.shape
    kap, nka = tabs["kap"], tabs["nka"]
    ya = _hyadft(jnp.asarray(tabs["fa_data"], BF16)[None], z, out_dtype=BF16)
    y4 = ya.reshape(b, kap, 2 * HY_B1, hy)
    w4 = _hyfreq(y4, jnp.asarray(tabs["fb_half"], BF16), hhat, jnp.asarray(tabs["fbi"], BF16), nka, layer)
    w3 = w4.reshape(b, 2 * kap * HY_B1, hy)
    return _hyout(jnp.asarray(tabs["ainv"], BF16), w3, z, x0, skip[None])


def _na_geometry(rows):
    gr, kr_n, kh = NA_GROUP_ROWS, NA_KEY_ROWS, NA_KH_MAX
    n_g = rows // gr
    geo = []
    for g in (0, 1, n_g - 1):
        ks = min(max(gr * g - kh // 2, 0), rows - kr_n)
        per_q = []
        for qr in range(gr):
            r = gr * g + qr
            rs = min(max(r - kh // 2, 0), rows - kh)
            per_q.append([((rs <= ks + k < rs + kh), ks + k - r + NA_KH_MAX - 1) for k in range(kr_n)])
        geo.append(per_q)
    return geo


def _na_table_kernel(r_ref, t_ref, *, geo):
    w, kw = GRID_W, NA_KW
    qc = lax.broadcasted_iota(I32, (w, 1), 0)
    lane = lax.broadcasted_iota(I32, (1, LANES), 1)
    kc = lane % w
    cs = jnp.clip(qc - kw // 2, 0, w - kw)
    colvalid = (kc >= cs) & (kc < cs + kw)
    left = lane < w
    neg = jnp.full((w, LANES), NEG, F32)
    shift = LANES - (kw - 1)

    cache = {}

    def toeplitz(dr, lane_off):
        if (dr, lane_off) not in cache:
            row = r_ref[0, 0, dr:dr + 1, :]
            if lane_off:
                row = pltpu.roll(row, lane_off, 1)
            tz = pltpu.roll(jnp.broadcast_to(row, (w, LANES)), shift, 1, stride=1, stride_axis=0)
            cache[(dr, lane_off)] = jnp.where(colvalid, tz, neg)
        return cache[(dr, lane_off)]

    for v, per_q in enumerate(geo):
        for qr, per_k in enumerate(per_q):
            for pair in range(len(per_k) // 2):
                (ok0, dr0), (ok1, dr1) = per_k[2 * pair], per_k[2 * pair + 1]
                lo = toeplitz(dr0, 0) if ok0 else neg
                hi = toeplitz(dr1, w) if ok1 else neg
                t_ref[0, v, 0, qr * w:(qr + 1) * w, pair * LANES:(pair + 1) * LANES] = jnp.where(left, lo, hi)


def _na_tables(rpb_all, rows):
    depth, heads, nr, nc = rpb_all.shape
    rp = jnp.pad(rpb_all.astype(F32), ((0, 0), (0, 0), (0, 16 - nr), (0, LANES - nc)))
    tq, tk = NA_GROUP_ROWS * GRID_W, NA_KEY_ROWS * GRID_W
    return pl.pallas_call(
        functools.partial(_na_table_kernel, geo=_na_geometry(rows)),
        grid=(depth, heads),
        in_specs=[pl.BlockSpec((1, 1, 16, LANES), lambda l, h: (l, h, 0, 0))],
        out_specs=pl.BlockSpec((1, 3, 1, tq, tk), lambda l, h: (l, 0, h, 0, 0)),
        out_shape=jax.ShapeDtypeStruct((depth, 3, heads, tq, tk), F32),
        compiler_params=_cparams(("parallel", "parallel")),
        name="natten_tables",
    )(rp)


def _natten_kernel(q_ref, k_ref, v_ref, t_ref, o_ref, *, heads, n_g, rows):
    g = pl.program_id(1)
    tq = q_ref.shape[1]
    tk = t_ref.shape[4]
    ks = jnp.clip(NA_GROUP_ROWS * g - NA_KH_MAX // 2, 0, rows - NA_KEY_ROWS)
    kstart = pl.multiple_of(ks * GRID_W, GRID_W)
    per_tile = LANES // NA_HEAD_DIM
    lane = lax.broadcasted_iota(I32, (1, LANES), 1)
    ones = jnp.ones((tk, LANES), BF16)
    for j in range(heads // per_tile):
        lanes = slice(j * LANES, (j + 1) * LANES)
        q2 = q_ref[0, :, lanes].astype(F32) * (NA_HEAD_DIM ** -0.5)
        k2 = k_ref[0, pl.ds(kstart, tk), lanes]
        vaug = jnp.concatenate([v_ref[0, pl.ds(kstart, tk), lanes], ones], axis=1)
        o2 = None
        for hh in range(per_tile):
            own = (lane >= hh * NA_HEAD_DIM) & (lane < (hh + 1) * NA_HEAD_DIM)
            qm = jnp.where(own, q2, 0.0).astype(BF16)
            s = lax.dot_general(qm, k2, (((1,), (1,)), ((), ())), preferred_element_type=F32)
            s = s + t_ref[0, 0, j * per_tile + hh]
            m = jnp.max(s, axis=-1, keepdims=True)
            p = jnp.exp((s - m).astype(BF16))
            r = jnp.dot(p, vaug, preferred_element_type=F32)
            o = r[:, :LANES] / r[:, LANES:]
            o2 = o if o2 is None else jnp.where(own, o, o2)
        o_ref[0, :, lanes] = o2.astype(o_ref.dtype)


def _natten(qkv, tables, layer, b, seq, naw):
    heads = naw // NA_HEAD_DIM
    rows = seq // GRID_W
    n_g = rows // NA_GROUP_ROWS
    tq = NA_GROUP_ROWS * GRID_W
    tk = NA_KEY_ROWS * GRID_W

    def tmap(i, g):
        return (layer, jnp.where(g == 0, 0, jnp.where(g == n_g - 1, 2, 1)), 0, 0, 0)

    return pl.pallas_call(
        functools.partial(_natten_kernel, heads=heads, n_g=n_g, rows=rows),
        grid=(b, n_g),
        in_specs=[pl.BlockSpec((1, tq, naw), lambda i, g: (i, g, 0)),
                  pl.BlockSpec((1, seq, naw), lambda i, g: (i, 0, 1)),
                  pl.BlockSpec((1, seq, naw), lambda i, g: (i, 0, 2)),
                  pl.BlockSpec((1, 1, heads, tq, tk), tmap)],
        out_specs=pl.BlockSpec((1, tq, naw), lambda i, g: (i, g, 0)),
        out_shape=jax.ShapeDtypeStruct((b, seq, naw), F32),
        compiler_params=_cparams(("parallel", "arbitrary")),
        name="natten",
    )(qkv, qkv, qkv, tables)


def _pool_kernel(u_ref, w_ref, sc_ref, o_ref, pad_ref, *, pw):
    seq = u_ref.shape[1]
    lp = seq + 2 * POOL_PAD
    j = pl.program_id(1)
    u = u_ref[0]
    zeros = jnp.zeros((POOL_PAD, LANES), F32)
    pad_ref[0:POOL_PAD, :] = zeros
    pad_ref[POOL_PAD + seq:lp, :] = zeros
    pad_ref[POOL_PAD:POOL_PAD + seq, :] = u
    xp = pad_ref[...]
    dn = lambda a, k: pltpu.roll(a, k, 0)
    up = lambda a, k: pltpu.roll(a, lp - k, 0)
    s2 = xp + dn(xp, 1)
    s4 = dn(s2, 1) + up(s2, 1)
    s8 = dn(s4, 2) + up(s4, 2)
    s16 = dn(s8, 4) + up(s8, 4)
    lane = lax.broadcasted_iota(I32, (1, LANES), 1) + j * LANES
    gdim = pw // len(POOL_WINDOWS)
    grp = lane // gdim
    sums = (s2, s4, s8, s16)
    wsum = sums[0][POOL_PAD:POOL_PAD + seq]
    half = jnp.full((1, LANES), POOL_WINDOWS[0] // 2, F32)
    for gi in range(1, len(POOL_WINDOWS)):
        wsum = jnp.where(grp == gi, sums[gi][POOL_PAD:POOL_PAD + seq], wsum)
        half = jnp.where(grp == gi, float(POOL_WINDOWS[gi] // 2), half)
    t = lax.broadcasted_iota(I32, (seq, LANES), 0).astype(F32)
    cnt = jnp.minimum(t + half, float(seq)) - jnp.maximum(t - half, 0.0)
    pooled = wsum / cnt - u
    y = jnp.dot(pooled.astype(BF16), w_ref[0], preferred_element_type=F32)
    o_ref[0] = y * sc_ref[...]


def _pool(u, pool_w, pool_scale):
    b, seq, pw = u.shape
    ng, gd, _ = pool_w.shape
    nh = pw // LANES
    per = LANES // gd
    wbd = jnp.zeros((nh, LANES, LANES), F32)
    for gi in range(ng):
        hh, k = divmod(gi, per)
        wbd = wbd.at[hh, k * gd:(k + 1) * gd, k * gd:(k + 1) * gd].set(pool_w[gi])
    return pl.pallas_call(
        functools.partial(_pool_kernel, pw=pw),
        grid=(b, nh),
        in_specs=[pl.BlockSpec((1, seq, LANES), lambda i, j: (i, 0, j)),
                  pl.BlockSpec((1, LANES, LANES), lambda i, j: (j, 0, 0)),
                  pl.BlockSpec((1, LANES), lambda i, j: (0, j))],
        out_specs=pl.BlockSpec((1, seq, LANES), lambda i, j: (i, 0, j)),
        out_shape=jax.ShapeDtypeStruct((b, seq, pw), F32),
        scratch_shapes=[pltpu.VMEM((seq + 2 * POOL_PAD, LANES), F32)],
        compiler_params=_cparams(("parallel", "parallel")),
        name="pool_mixer",
    )(u, wbd.astype(BF16), pool_scale[None])


def _outproj_kernel(x_ref, yh_ref, yn_ref, yp_ref, gm_ref, w_ref, g2_ref, wr_ref,
                    xo_ref, h_ref, aff_ref, *, hy, naw):
    gm = gm_ref[...]
    m1 = _rms(yh_ref[...], gm[:, :hy]).astype(BF16)
    m2 = _rms(yn_ref[...], gm[:, hy:hy + naw]).astype(BF16)
    m3 = _rms(yp_ref[...], gm[:, hy + naw:]).astype(BF16)
    acc = jnp.dot(m1, w_ref[:hy, :], preferred_element_type=F32)
    acc += jnp.dot(m2, w_ref[hy:hy + naw, :], preferred_element_type=F32)
    acc += jnp.dot(m3, w_ref[hy + naw:, :], preferred_element_type=F32)
    xn = x_ref[...] + acc
    xo_ref[...] = xn
    h = _rms(xn, g2_ref[...])
    h_ref[...] = h.reshape(h_ref.shape)
    logits = _dot_x3(wr_ref[0], wr_ref[1], h, (((1,), (1,)), ((), ())))
    mx = jnp.max(logits, axis=0, keepdims=True)
    ex = jnp.exp(logits - mx)
    aff_ref[...] = ex / jnp.sum(ex, axis=0, keepdims=True)


def _outproj(x2, yh, yn, yp, gm, w_bf, g2, wr_t, tm=512):
    n, d = x2.shape
    hy, naw, pw = yh.shape[1], yn.shape[1], yp.shape[1]
    e = wr_t.shape[1]
    row = lambda c: pl.BlockSpec((tm, c), lambda i: (i, 0))
    full = lambda s: pl.BlockSpec(s, lambda i: (0, 0))
    return pl.pallas_call(
        functools.partial(_outproj_kernel, hy=hy, naw=naw),
        grid=(n // tm,),
        in_specs=[row(d), row(hy), row(naw), row(pw), full((1, d)), full((d, d)), full((1, d)),
                  pl.BlockSpec((2, e, d), lambda i: (0, 0, 0))],
        out_specs=[row(d), pl.BlockSpec((tm, d // LANES, LANES), lambda i: (i, 0, 0)),
                   pl.BlockSpec((e, tm), lambda i: (0, i))],
        out_shape=[jax.ShapeDtypeStruct((n, d), F32), jax.ShapeDtypeStruct((n, d // LANES, LANES), F32),
                   jax.ShapeDtypeStruct((e, n), F32)],
        compiler_params=_cparams(("parallel",)),
        name="outproj_router",
    )(x2, yh, yn, yp, gm, w_bf, g2, wr_t)


def _block_cumsum(x, tri):
    r, n = x.shape
    cls, offs = [], []
    off = jnp.zeros((r, 1), F32)
    for j in range(n // LANES):
        c = jnp.dot(x[:, j * LANES:(j + 1) * LANES], tri, preferred_element_type=F32)
        cls.append(c)
        off = off + c[:, LANES - 1:LANES]
        offs.append(off)
    return cls, offs


def _route_kernel(aff_ref, tri_ref, bci_ref, bcx_ref, idx_ref, gate_ref, blk_ref, *, cap):
    aff = aff_ref[...]
    e, seq = aff.shape
    nblk = seq // LANES
    bits = pltpu.bitcast(aff, I32)
    capf = jnp.float32(cap)

    def radix(i, prefix):
        cand = prefix | jnp.left_shift(jnp.int32(1), 30 - i)
        cnt = jnp.sum((bits >= cand).astype(F32), axis=1, keepdims=True)
        return jnp.where(cnt >= capf, cand, prefix)

    tau = lax.fori_loop(0, 31, radix, jnp.zeros((e, 1), I32))
    gt = bits > tau
    eq = bits == tau
    need = capf - jnp.sum(gt.astype(F32), axis=1, keepdims=True)
    tri = tri_ref[...]
    cls, offs = _block_cumsum(jnp.where(eq, 1.0, 0.0).astype(BF16), tri)
    tie_rank = jnp.concatenate([c if j == 0 else c + offs[j - 1] for j, c in enumerate(cls)], axis=1)
    sel = gt | (eq & (tie_rank <= need))
    self32 = jnp.where(sel, 1.0, 0.0)
    selb = self32.astype(BF16)
    cls, _ = _block_cumsum(selb, tri)
    bend = jnp.dot(selb, bci_ref[...], preferred_element_type=F32)
    bstart = jnp.dot(selb, bcx_ref[...], preferred_element_type=F32)
    for j in range(nblk):
        rows = slice(j * e, (j + 1) * e)
        lanes = slice(j * LANES, (j + 1) * LANES)
        blk_ref[0, rows, :] = cls[j]
        blk_ref[1, rows, :] = self32[:, lanes]
        blk_ref[2, rows, :] = aff[:, lanes]
    slot = lax.broadcasted_iota(I32, (cap, 1), 0).astype(F32)
    lane = lax.broadcasted_iota(I32, (1, LANES), 1).astype(F32)
    for ei in range(e):
        bs, be = bstart[ei:ei + 1, :], bend[ei:ei + 1, :]
        inblk = (bs <= slot) & (slot < be)
        local = slot - jnp.sum(jnp.where(inblk, bs, 0.0), axis=1, keepdims=True)
        jcol = jnp.sum(jnp.where(inblk, lane, 0.0), axis=1, keepdims=True)
        pick = jnp.where(inblk, 1.0, 0.0)[:, :nblk].astype(BF16)
        rows = pl.ds(ei, nblk, stride=e)
        a = blk_ref[2, rows, :]
        a_hi = a.astype(BF16)
        r1 = a - a_hi.astype(F32)
        a_mid = r1.astype(BF16)
        a_lo = (r1 - a_mid.astype(F32)).astype(BF16)
        take = lambda v: jnp.dot(pick, v, preferred_element_type=F32)
        g_cl = take(blk_ref[0, rows, :].astype(BF16))
        g_sel = take(blk_ref[1, rows, :].astype(BF16))
        g_aff = take(a_hi) + take(a_mid) + take(a_lo)
        hit = (g_cl == local + 1.0) & (g_sel > 0.5)
        idx = jcol * LANES + jnp.sum(jnp.where(hit, lane, 0.0), axis=1, keepdims=True)
        gate = jnp.sum(jnp.where(hit, g_aff, 0.0), axis=1, keepdims=True)
        idx_ref[0, :, ei:ei + 1] = idx.astype(I32)
        gate_ref[0, ei] = jnp.broadcast_to(gate, (cap, LANES))


def _route(aff_t, b, seq, cap):
    e = aff_t.shape[0]
    nblk = seq // LANES
    tri = jnp.asarray(np.triu(np.ones((LANES, LANES), np.float32)), BF16)
    tblk = np.arange(seq)[:, None] // LANES
    bci = jnp.asarray(tblk <= np.arange(LANES)[None, :], BF16)
    bcx = jnp.asarray(tblk < np.arange(LANES)[None, :], BF16)
    full = lambda s: pl.BlockSpec(s, lambda i: (0, 0))
    idx, gate = pl.pallas_call(
        functools.partial(_route_kernel, cap=cap),
        grid=(b,),
        in_specs=[pl.BlockSpec((e, seq), lambda i: (0, i)), full((LANES, LANES)),
                  full((seq, LANES)), full((seq, LANES))],
        out_specs=[pl.BlockSpec((1, cap, e), lambda i: (i, 0, 0)),
                   pl.BlockSpec((1, e, cap, LANES), lambda i: (i, 0, 0, 0))],
        out_shape=[jax.ShapeDtypeStruct((b, cap, e), I32), jax.ShapeDtypeStruct((b, e, cap, LANES), F32)],
        scratch_shapes=[pltpu.VMEM((3, nblk * e, LANES), F32)],
        compiler_params=_cparams(("parallel",)),
        name="ec_route",
    )(aff_t, tri, bci, bcx)
    return idx.transpose(0, 2, 1), gate


def _expert_kernel(rows_ref, h_hbm, wg_ref, wu_ref, wd_ref, y_ref, xbuf, xb, acc, sem, *, tm, nm, nf, ne):
    e = pl.program_id(0)
    m = pl.program_id(1)
    f = pl.program_id(2)
    tile = e * nm + m
    last = ne * nm - 1
    chunk = tm // nf
    d = xb.shape[2]
    rb = min(EXPERT_RELAYOUT_ROWS, chunk)

    def row_copy(src_row, p, c, dst_row):
        return pltpu.make_async_copy(h_hbm.at[pl.ds(src_row, 1)], xbuf.at[p, c, pl.ds(dst_row, 1)], sem.at[p, c])

    def chunk_wait(t, c):
        pltpu.make_async_copy(h_hbm.at[pl.ds(0, chunk)], xbuf.at[t % 2, c], sem.at[t % 2, c]).wait()

    def chunk_convert(t, c):
        for j in range(chunk // rb):
            rows = xbuf[t % 2, c, j * rb:(j + 1) * rb]
            xb[t % 2, pl.ds(c * chunk + j * rb, rb), :] = rows.reshape(rb, d).astype(BF16)

    @pl.when((tile == 0) & (f == 0))
    def _():
        nxt0 = min(1, last)
        for t, c in [(0, k) for k in range(nf)] + [(nxt0, 0)]:
            def issue(i, carry, t=t, c=c):
                row_copy(rows_ref[t * tm + c * chunk + i], t % 2, c, i).start()
                return carry

            lax.fori_loop(0, chunk, issue, 0, unroll=8)
        for k in range(nf):
            chunk_wait(0, k)
            chunk_convert(0, k)

    @pl.when(f == 0)
    def _():
        acc[...] = jnp.zeros(acc.shape, F32)

    nxt = jnp.minimum(tile + 1, last)
    wrap = f + 1 == nf
    t_issue = jnp.where(wrap, jnp.minimum(tile + 2, last), nxt)
    c_issue = jnp.where(wrap, 0, f + 1)
    base = t_issue * tm + c_issue * chunk
    for i in range(chunk):
        row_copy(rows_ref[base + i], t_issue % 2, c_issue, i).start()

    x = xb[tile % 2]
    a = jnp.dot(x, wg_ref[0, 0].astype(BF16), preferred_element_type=F32)
    u = jnp.dot(x, wu_ref[0, 0].astype(BF16), preferred_element_type=F32)
    hh = (a * jax.nn.sigmoid(a) * u).astype(BF16)
    acc[...] += jnp.dot(hh, wd_ref[0, 0].astype(BF16), preferred_element_type=F32)

    chunk_wait(nxt, f)
    chunk_convert(nxt, f)

    @pl.when(f == nf - 1)
    def _():
        y_ref[0] = acc[...].astype(y_ref.dtype)

    @pl.when((tile == last) & (f == nf - 1))
    def _():
        chunk_wait(last, 0)


EXPERT_RELAYOUT_ROWS = 64


def _experts(rows_flat, h3, w_gate, w_up, w_down, layer, mtot, tm=1024, tf=512):
    _, e, d, ff = w_gate.shape
    tm = min(tm, mtot)
    tf = min(tf, ff)
    nm, nf = mtot // tm, ff // tf
    grid_spec = pltpu.PrefetchScalarGridSpec(
        num_scalar_prefetch=1,
        grid=(e, nm, nf),
        in_specs=[pl.BlockSpec(memory_space=pl.ANY),
                  pl.BlockSpec((1, 1, d, tf), lambda i, m, f, r: (layer, i, 0, f)),
                  pl.BlockSpec((1, 1, d, tf), lambda i, m, f, r: (layer, i, 0, f)),
                  pl.BlockSpec((1, 1, tf, d), lambda i, m, f, r: (layer, i, f, 0))],
        out_specs=pl.BlockSpec((1, tm, d), lambda i, m, f, r: (i, m, 0)),
        scratch_shapes=[pltpu.VMEM((2, nf, tm // nf, d // LANES, LANES), F32), pltpu.VMEM((2, tm, d), BF16),
                        pltpu.VMEM((tm, d), F32),
                        pltpu.SemaphoreType.DMA((2, nf))],
    )
    return pl.pallas_call(
        functools.partial(_expert_kernel, tm=tm, nm=nm, nf=nf, ne=e),
        grid_spec=grid_spec,
        out_shape=jax.ShapeDtypeStruct((e, mtot, d), BF16),
        compiler_params=_cparams(("arbitrary", "arbitrary", "arbitrary")),
        name="ec_experts",
    )(rows_flat, h3, w_gate, w_up, w_down)


COMBINE_ROWS = 64
COMBINE_UNROLL = 8


def _combine_kernel(idx_ref, split_ref, x_ref, y_ref, gl, *rest, ne, cap, span, final):
    if final:
        g_ref, o_ref, acc3, y3 = rest
    else:
        o_ref, acc3, y3 = rest
    b = pl.program_id(0)
    sp = pl.program_id(1)
    e = pl.program_id(2)
    d = x_ref.shape[2]
    sub = d // LANES
    rb = COMBINE_ROWS

    @pl.when(e == 0)
    def _():
        def load(c, carry):
            r0 = pl.multiple_of(c * rb, rb)
            acc3[pl.ds(pl.multiple_of(r0 * sub, rb * sub), rb * sub), :] = (
                x_ref[0, pl.ds(r0, rb), :].reshape(rb * sub, LANES))
            return carry

        lax.fori_loop(0, span // rb, load, 0)

    lo = split_ref[(b * ne + e) * 3 + sp]
    hi = split_ref[(b * ne + e) * 3 + sp + 1]

    def relayout(c, carry):
        r0 = pl.multiple_of(c * rb, rb)
        y3[pl.ds(r0, rb)] = y_ref[0, pl.ds(r0, rb), :].astype(F32).reshape(rb, sub, LANES)
        return carry

    lax.fori_loop(lo // rb, (hi + rb - 1) // rb, relayout, 0)

    base = (b * ne + e) * cap
    nu = COMBINE_UNROLL

    def tokens(first, count):
        return tuple(idx_ref[base + first + u] for u in range(count))

    def add_rows(first, toks):
        rows = [pl.ds(pl.multiple_of(t, sub), sub) for t in toks]
        vals = [acc3[r, :] + gl[0, 0, pl.ds(first + u, 1), :] * y3[first + u] for u, r in enumerate(rows)]
        for r, v in zip(rows, vals):
            acc3[r, :] = v

    def group(k, toks):
        first = lo + k * nu
        nxt = tokens(jnp.minimum(first + nu, cap - nu), nu)
        add_rows(first, toks)
        return nxt

    ngroups = (hi - lo) // nu
    lax.fori_loop(0, ngroups, group, tokens(jnp.minimum(lo, cap - nu), nu))

    def tail(i, carry):
        add_rows(i, tokens(i, 1))
        return carry

    lax.fori_loop(lo + ngroups * nu, hi, tail, 0)

    @pl.when(e == ne - 1)
    def _():
        def store(c, carry):
            r0 = pl.multiple_of(c * rb, rb)
            v = acc3[pl.ds(pl.multiple_of(r0 * sub, rb * sub), rb * sub), :].reshape(rb, d)
            if final:
                v = _rms(v, g_ref[...])
            o_ref[0, pl.ds(r0, rb), :] = v
            return carry

        lax.fori_loop(0, span // rb, store, 0)


def _combine(idx, gate, x3, y, cap, final_g=None):
    b, seq, d = x3.shape
    ne = y.shape[0]
    span = seq // 2
    n_lower = jnp.sum((idx < span).astype(I32), axis=-1)
    split_flat = jnp.stack([jnp.zeros_like(n_lower), n_lower, jnp.full_like(n_lower, cap)], axis=-1).reshape(-1)
    final = final_g is not None
    in_specs = [pl.BlockSpec((1, span, d), lambda i, s, e, *_: (i, s, 0)),
                pl.BlockSpec((1, cap, d), lambda i, s, e, *_: (e, i, 0)),
                pl.BlockSpec((1, 1, cap, LANES), lambda i, s, e, *_: (i, e, 0, 0))]
    args = [x3, y, gate]
    if final:
        in_specs.append(pl.BlockSpec((1, d), lambda i, s, e, *_: (0, 0)))
        args.append(final_g)
    grid_spec = pltpu.PrefetchScalarGridSpec(
        num_scalar_prefetch=2,
        grid=(b, 2, ne),
        in_specs=in_specs,
        out_specs=pl.BlockSpec((1, span, d), lambda i, s, e, *_: (i, s, 0)),
        scratch_shapes=[pltpu.VMEM((span * (d // LANES), LANES), F32), pltpu.VMEM((cap, d // LANES, LANES), F32)],
    )
    return pl.pallas_call(
        functools.partial(_combine_kernel, ne=ne, cap=cap, span=span, final=final),
        grid_spec=grid_spec,
        out_shape=jax.ShapeDtypeStruct((b, seq, d), F32),
        compiler_params=_cparams(("arbitrary", "arbitrary", "arbitrary")),
        name="ec_combine",
    )(((idx % span) * (d // LANES)).reshape(-1), split_flat, *args)


def _moe(x3, h2, aff_t, w_gate, w_up, w_down, layer, final_g=None):
    b, seq, d = x3.shape
    ne = w_gate.shape[1]
    cap = EC_CAPACITY * seq // ne
    idx, gate = _route(aff_t, b, seq, cap)
    rows = idx + (jnp.arange(b, dtype=I32) * seq)[:, None, None]
    rows_flat = rows.transpose(1, 0, 2).reshape(-1)
    y = _experts(rows_flat, h2, w_gate, w_up, w_down, layer, b * cap)
    return _combine(idx, gate, x3, y, cap, final_g)


def kernel(x, norm1_g, w_in, hy_short_w, hy_short_b, hy_f_w1, hy_f_b1, hy_f_w2, hy_f_b2, hy_f_wout, hy_f_freq, hy_skip, na_rpb, pool_w, pool_scale, mix_norm_g, w_out, norm2_g, w_router, w_gate, w_up, w_down, final_g):
    b, seq, d = x.shape
    depth = w_in.shape[0]
    hy = hy_skip.shape[1]
    pw = pool_scale.shape[1]
    naw = d - hy - pw
    n = b * seq
    rows = seq // GRID_W
    tabs = _dft_tables(seq // HY_B1)
    x2 = x.reshape(n, d)
    hhat = _hyena_filter_spectra(seq, hy, hy_f_w1, hy_f_b1, hy_f_w2, hy_f_b2, hy_f_wout, hy_f_freq, tabs)
    na_tables = _na_tables(na_rpb, rows)
    for i in range(depth):
        z, x0, qkv, pool_in = _inproj(x2, norm1_g[i][None], w_in[i].astype(BF16), hy_short_w[i],
                                      hy_short_b[i][None], 3 * hy, 3 * naw, seq)
        y_hy = _hyena(z.reshape(b, seq, hy), x0.reshape(b, seq, hy), hy_skip[i], hhat, i, tabs)
        y_na = _natten(qkv.reshape(b, seq, 3 * naw), na_tables, i, b, seq, naw)
        y_pool = _pool(pool_in.reshape(b, seq, pw), pool_w[i], pool_scale[i])
        x2, h2, aff_t = _outproj(x2, y_hy.reshape(n, hy), y_na.reshape(n, naw), y_pool.reshape(n, pw),
                                 mix_norm_g[i][None], w_out[i].astype(BF16), norm2_g[i][None],
                                 jnp.stack(_split_bf16(w_router[i].T)))
        last = final_g[None] if i == depth - 1 else None
        x2 = _moe(x2.reshape(b, seq, d), h2, aff_t, w_gate, w_up, w_down, i, last).reshape(n, d)
    return x2.reshape(b, seq, d)
```

```python
import functools
import math

import numpy as np
import jax
import jax.numpy as jnp
from jax import lax
from jax.experimental import pallas as pl
from jax.experimental.pallas import tpu as pltpu

F32 = jnp.float32
BF16 = jnp.bfloat16
I32 = jnp.int32
EPS = 1e-6
HIGHEST = lax.Precision.HIGHEST

GRID_W = 64
NA_HEAD_DIM = 64
NA_KH_MAX = 8
NA_KW = 16
NA_GROUP_ROWS = 4
NA_KEY_ROWS = 12
POOL_WINDOWS = (2, 4, 8, 16)
POOL_PAD = 16
FILTER_EMB = 33
DECAY_FAST, DECAY_SLOW, DECAY_TARGET = 0.3, 1.5, 1e-2
EC_CAPACITY = 2
HY_B1 = 128
NEG = -1e30
LANES = 128
V7X_VMEM_BYTES = 64 * 1024 * 1024
VMEM_LIMIT = V7X_VMEM_BYTES - 8 * 1024 * 1024


def _cparams(sem, vmem=VMEM_LIMIT):
    return pltpu.CompilerParams(dimension_semantics=sem, vmem_limit_bytes=vmem)


def _rms(v, g):
    return v * lax.rsqrt(jnp.mean(v * v, axis=-1, keepdims=True) + EPS) * g


def _split_bf16(v):
    hi = v.astype(BF16)
    return hi, (v - hi.astype(F32)).astype(BF16)


def _dot_x3(a_hi, a_lo, x, dims=(((1,), (0,)), ((), ()))):
    x_hi, x_lo = _split_bf16(x)
    dg = lambda p, q: lax.dot_general(p, q, dims, preferred_element_type=F32)
    return dg(a_hi, x_hi) + dg(a_lo, x_hi) + dg(a_hi, x_lo)


HALO = 16


def _inproj_kernel(x_ref, xp_ref, xn_ref, g_ref, w_ref, sw_ref, sb_ref, z_ref, x0_ref, qkv_ref, pool_ref,
                   *, hyw, naw, seq):
    g = g_ref[...]
    tm = x_ref.shape[0]
    h = _rms(x_ref[...], g).astype(BF16)
    hall = jnp.concatenate([_rms(xp_ref[...], g).astype(BF16), h, _rms(xn_ref[...], g).astype(BF16)], axis=0)
    u = jnp.dot(hall, w_ref[:, :hyw], preferred_element_type=F32)
    t = lax.rem(pl.program_id(0) * tm + lax.broadcasted_iota(I32, (tm, 1), 0), seq)
    prev = jnp.where(t == 0, 0.0, u[HALO - 1:HALO - 1 + tm])
    nxt = jnp.where(t == seq - 1, 0.0, u[HALO + 1:HALO + 1 + tm])
    conv = prev * sw_ref[0:1, :] + u[HALO:HALO + tm] * sw_ref[1:2, :] + nxt * sw_ref[2:3, :] + sb_ref[...]
    hy = hyw // 3
    x0_ref[...] = conv[:, :hy]
    z_ref[...] = conv[:, 2 * hy:] * conv[:, hy:2 * hy]
    qkv_ref[...] = jnp.dot(h, w_ref[:, hyw:hyw + naw], preferred_element_type=F32).astype(BF16)
    pool_ref[...] = jnp.dot(h, w_ref[:, hyw + naw:], preferred_element_type=F32)


def _inproj(x2, g, w_bf, sw, sb, hyw, naw, seq, tm=512):
    n, d = x2.shape
    inw = w_bf.shape[1]
    pw = inw - hyw - naw
    hy = hyw // 3
    per = tm // HALO
    last = n // HALO - 1
    return pl.pallas_call(
        functools.partial(_inproj_kernel, hyw=hyw, naw=naw, seq=seq),
        grid=(n // tm,),
        in_specs=[pl.BlockSpec((tm, d), lambda i: (i, 0)),
                  pl.BlockSpec((HALO, d), lambda i: (jnp.maximum(i * per - 1, 0), 0)),
                  pl.BlockSpec((HALO, d), lambda i: (jnp.minimum((i + 1) * per, last), 0)),
                  pl.BlockSpec((1, d), lambda i: (0, 0)),
                  pl.BlockSpec((d, inw), lambda i: (0, 0)),
                  pl.BlockSpec((3, hyw), lambda i: (0, 0)),
                  pl.BlockSpec((1, hyw), lambda i: (0, 0))],
        out_specs=[pl.BlockSpec((tm, hy), lambda i: (i, 0)),
                   pl.BlockSpec((tm, hy), lambda i: (i, 0)),
                   pl.BlockSpec((tm, naw), lambda i: (i, 0)),
                   pl.BlockSpec((tm, pw), lambda i: (i, 0))],
        out_shape=[jax.ShapeDtypeStruct((n, hy), F32),
                   jax.ShapeDtypeStruct((n, hy), F32),
                   jax.ShapeDtypeStruct((n, naw), BF16),
                   jax.ShapeDtypeStruct((n, pw), F32)],
        compiler_params=_cparams(("parallel",)),
        name="inproj",
    )(x2, x2, x2, g, w_bf, sw, sb)


def _filter_kernel(z_ref, w1_ref, b1_ref, w2_ref, b2_ref, wo_ref, fr_ref, dec_ref, o_ref):
    fr = fr_ref[0]
    h = jnp.sin(fr * (jnp.dot(z_ref[...], w1_ref[0], preferred_element_type=F32, precision=HIGHEST) + b1_ref[0]))
    h = jnp.sin(fr * (jnp.dot(h, w2_ref[0], preferred_element_type=F32, precision=HIGHEST) + b2_ref[0]))
    hw = jnp.dot(h, wo_ref[0], preferred_element_type=F32, precision=HIGHEST)
    c = dec_ref.shape[2]
    o_ref[0, 0] = hw[:, :c] * dec_ref[0]
    o_ref[0, 1] = hw[:, c:] * dec_ref[1]


def _filter_mlp(zemb, w1, b1, w2, b2, wo, fr, dec, tl=512):
    seq, emb = zemb.shape
    depth, _, hid = w1.shape
    ow = wo.shape[2]
    c = ow // 2
    lay = lambda s: pl.BlockSpec((1,) + s, lambda l, i: (l, 0, 0))
    return pl.pallas_call(
        _filter_kernel,
        grid=(depth, seq // tl),
        in_specs=[pl.BlockSpec((tl, emb), lambda l, i: (i, 0)), lay((emb, hid)), lay((1, hid)),
                  lay((hid, hid)), lay((1, hid)), lay((hid, ow)), lay((1, hid)),
                  pl.BlockSpec((2, tl, c), lambda l, i: (0, i, 0))],
        out_specs=pl.BlockSpec((1, 2, tl, c), lambda l, i: (l, 0, i, 0)),
        out_shape=jax.ShapeDtypeStruct((depth, 2, seq, c), F32),
        compiler_params=_cparams(("parallel", "parallel")),
        name="hyena_filter_mlp",
    )(zemb, w1, b1, w2, b2, wo, fr, dec)


def _filter_bdft_kernel(ff_ref, fc_ref, yf_ref, yb_ref, o_ref):
    o_ref[0, 0] = (_dot_x3(ff_ref[0], ff_ref[1], yf_ref[0, 0, 0]) + _dot_x3(fc_ref[0], fc_ref[1], yb_ref[0, 0, 0]))


def _filter_bdft(ff, fc, ya5, nka):
    depth, _, kap, r, c = ya5.shape
    blk = lambda p: pl.BlockSpec((1, 1, 1, r, c), lambda l, k, p=p: (l, p, k, 0, 0))
    return pl.pallas_call(
        _filter_bdft_kernel,
        grid=(depth, nka),
        in_specs=[pl.BlockSpec((2, r, r), lambda l, k: (0, 0, 0)), pl.BlockSpec((2, r, r), lambda l, k: (0, 0, 0)),
                  blk(0), blk(1)],
        out_specs=pl.BlockSpec((1, 1, r, c), lambda l, k: (l, k, 0, 0)),
        out_shape=jax.ShapeDtypeStruct((depth, nka, r, c), F32),
        compiler_params=_cparams(("parallel", "parallel")),
        name="hyena_filter_bdft",
    )(ff, fc, ya5, ya5)


def _hyadft_kernel(fa_ref, z_ref, y_ref, zt_ref, yt_ref):
    pieces, k2, a1 = fa_ref.shape
    zt_ref[...] = pltpu.einshape("abl->bal", z_ref[0].reshape(a1, HY_B1, LANES))

    def body(i, c):
        b0 = 2 * i
        zz = jnp.concatenate([zt_ref[b0], zt_ref[b0 + 1]], axis=1)
        if pieces == 1:
            r = jnp.dot(fa_ref[0], zz.astype(BF16), preferred_element_type=F32)
        else:
            r = _dot_x3(fa_ref[0], fa_ref[1], zz)
        yt_ref[b0] = r[:, :LANES]
        yt_ref[b0 + 1] = r[:, LANES:]
        return c

    lax.fori_loop(0, HY_B1 // 2, body, 0, unroll=8)
    y_ref[0] = pltpu.einshape("bkl->kbl", yt_ref[...]).reshape(k2 * HY_B1, LANES).astype(y_ref.dtype)


def _hyadft(fa, z, out_dtype=F32, name="hyena_adft"):
    b, seq, hy = z.shape
    pieces, k2, a1 = fa.shape
    return pl.pallas_call(
        _hyadft_kernel,
        grid=(b, hy // LANES),
        in_specs=[pl.BlockSpec((pieces, k2, a1), lambda i, j: (0, 0, 0)),
                  pl.BlockSpec((1, seq, LANES), lambda i, j: (i, 0, j))],
        out_specs=pl.BlockSpec((1, k2 * HY_B1, LANES), lambda i, j: (i, 0, j)),
        out_shape=jax.ShapeDtypeStruct((b, k2 * HY_B1, hy), out_dtype),
        scratch_shapes=[pltpu.VMEM((HY_B1, a1, LANES), F32), pltpu.VMEM((HY_B1, k2, LANES), F32)],
        compiler_params=_cparams(("parallel", "parallel")),
        name=name,
    )(fa, z)


def _hyfreq_kernel(y_ref, fb_ref, h_ref, fbi_ref, w_ref, *, nka):
    ka = pl.program_id(0)
    nb = fb_ref.shape[0] // 2
    nbatch = y_ref.shape[0]

    @pl.when(ka < nka)
    def _():
        hr, hi = h_ref[0, 0, :nb], h_ref[0, 0, nb:]
        for bi in range(nbatch):
            p = jnp.dot(fb_ref[...], y_ref[bi, 0].astype(BF16), preferred_element_type=F32)
            pr, pi = p[:nb], p[nb:]
            q = jnp.concatenate([pr * hr - pi * hi, pr * hi + pi * hr], axis=0).astype(BF16)
            w_ref[bi, 0] = jnp.dot(fbi_ref[...], q, preferred_element_type=F32).astype(w_ref.dtype)

    @pl.when(ka >= nka)
    def _():
        w_ref[...] = jnp.zeros(w_ref.shape, w_ref.dtype)


def _hyfreq(y4, fb, hhat, fbi, nka, layer):
    b, kap, r2, hy = y4.shape
    nb2 = fb.shape[0]
    return pl.pallas_call(
        functools.partial(_hyfreq_kernel, nka=nka),
        grid=(kap,),
        in_specs=[pl.BlockSpec((b, 1, r2, hy), lambda k: (0, k, 0, 0)),
                  pl.BlockSpec((nb2, r2), lambda k: (0, 0)),
                  pl.BlockSpec((1, 1, nb2, hy), lambda k: (layer, jnp.minimum(k, nka - 1), 0, 0)),
                  pl.BlockSpec((r2, nb2), lambda k: (0, 0))],
        out_specs=pl.BlockSpec((b, 1, r2, hy), lambda k: (0, k, 0, 0)),
        out_shape=jax.ShapeDtypeStruct((b, kap, r2, hy), BF16),
        compiler_params=_cparams(("parallel",)),
        name="hyena_freq",
    )(y4, fb, hhat, fbi)


def _hyout_kernel(a_ref, w_ref, z_ref, x0_ref, sk_ref, o_ref, wt_ref, zt_ref, xt_ref, ot_ref):
    a1, k2 = a_ref.shape
    ainv = a_ref[...]
    skip = sk_ref[...]
    wt_ref[...] = pltpu.einshape("kbl->bkl", w_ref[0].astype(F32).reshape(k2, HY_B1, LANES))
    zt_ref[...] = pltpu.einshape("abl->bal", z_ref[0].reshape(a1, HY_B1, LANES))
    xt_ref[...] = pltpu.einshape("abl->bal", x0_ref[0].reshape(a1, HY_B1, LANES))

    def body(i, c):
        b0 = 2 * i
        ww = jnp.concatenate([wt_ref[b0], wt_ref[b0 + 1]], axis=1).astype(BF16)
        y = jnp.dot(ainv, ww, preferred_element_type=F32)
        ot_ref[b0] = (y[:, :LANES] + zt_ref[b0] * skip) * xt_ref[b0]
        ot_ref[b0 + 1] = (y[:, LANES:] + zt_ref[b0 + 1] * skip) * xt_ref[b0 + 1]
        return c

    lax.fori_loop(0, HY_B1 // 2, body, 0, unroll=8)
    o_ref[0] = pltpu.einshape("bal->abl", ot_ref[...]).reshape(a1 * HY_B1, LANES)


def _hyout(ainv, w3, z, x0, skip):
    b, seq, hy = z.shape
    a1, k2 = ainv.shape
    blk = pl.BlockSpec((1, seq, LANES), lambda i, j: (i, 0, j))
    return pl.pallas_call(
        _hyout_kernel,
        grid=(b, hy // LANES),
        in_specs=[pl.BlockSpec((a1, k2), lambda i, j: (0, 0)),
                  pl.BlockSpec((1, k2 * HY_B1, LANES), lambda i, j: (i, 0, j)),
                  blk, blk,
                  pl.BlockSpec((1, LANES), lambda i, j: (0, j))],
        out_specs=blk,
        out_shape=jax.ShapeDtypeStruct((b, seq, hy), F32),
        scratch_shapes=[pltpu.VMEM((HY_B1, k2, LANES), F32)] + [pltpu.VMEM((HY_B1, a1, LANES), F32)] * 3,
        compiler_params=_cparams(("parallel", "parallel")),
        name="hyena_out",
    )(ainv, w3, z, x0, skip)


def _dft_tables(a1):
    a2, b2, b1 = 2 * a1, 2 * HY_B1, HY_B1
    nka = a1 + 1
    kap = -(-nka // 8) * 8
    ka = np.arange(nka)[:, None]
    def fa(na):
        ph = 2 * np.pi * ((ka * np.arange(na)[None, :]) % a2) / a2
        m = np.zeros((2 * kap, na))
        m[0:2 * nka:2] = np.cos(ph)
        m[1:2 * nka:2] = -np.sin(ph)
        return m
    kb = np.arange(b2)[:, None]
    th = 2 * np.pi * ((kb * np.arange(b2)[None, :]) % b2) / b2
    c, s = np.cos(th), np.sin(th)
    fb_full = np.block([[c, s], [-s, c]])
    fb_half = np.block([[c[:, :b1], s[:, :b1]], [-s[:, :b1], c[:, :b1]]])
    ct, st = c.T[:b1], s.T[:b1]
    fbi = np.block([[ct, -st], [st, ct]]) / b2
    ph = 2 * np.pi * ((np.arange(a1)[:, None] * np.arange(nka)[None, :]) % a2) / a2
    wgt = np.where((np.arange(nka) == 0) | (np.arange(nka) == a1), 1.0, 2.0)[None, :] / a2
    ainv = np.zeros((a1, 2 * kap))
    ainv[:, 0:2 * nka:2] = wgt * np.cos(ph)
    ainv[:, 1:2 * nka:2] = -wgt * np.sin(ph)
    fa2 = fa(a2)
    fa_filt2 = np.stack([fa2[:, 0:a1], fa2[:, 1:a1 + 1]], axis=1).reshape(4 * kap, a1)
    fb_conj = np.concatenate([fb_full[:b2], -fb_full[b2:]], axis=0)
    f32 = lambda v: np.asarray(v, np.float32)
    return dict(nka=nka, kap=kap, fa_data=f32(fa(a1)), fa_filt2=f32(fa_filt2), fb_full=f32(fb_full),
                fb_conj=f32(fb_conj), fb_half=f32(fb_half), fbi=f32(fbi), ainv=f32(ainv))


def _hyena_filter_spectra(seq, hy, w1, b1, w2, b2, wo, fr, tabs):
    nbands = (FILTER_EMB - 1) // 2
    t = jnp.linspace(0.0, 1.0, seq, dtype=F32)[:, None]
    ang = 2.0 * math.pi * jnp.arange(seq, dtype=F32)[:, None] / seq
    f = jnp.linspace(1e-4, nbands - 1, nbands, dtype=F32)[None, :]
    zemb = jnp.concatenate([t, jnp.cos(f * ang), -jnp.sin(f * ang)], axis=-1)
    deltas = jnp.abs(jnp.linspace(math.log(DECAY_TARGET) / DECAY_FAST,
                                  math.log(DECAY_TARGET) / DECAY_SLOW, hy, dtype=F32))
    decay = jnp.exp(-t * deltas)
    dec = jnp.stack([decay, decay * (jnp.arange(seq) > 0)[:, None].astype(F32)])
    depth = w1.shape[0]
    h = _filter_mlp(zemb, w1, b1[:, None], w2, b2[:, None], wo, fr[:, None], dec, tl=min(512, seq))
    pieces = lambda m: jnp.stack(_split_bf16(jnp.asarray(m)))
    ya = _hyadft(pieces(tabs["fa_filt2"]), h.reshape(depth * 2, seq, hy), name="hyena_filter_adft")
    ya5 = ya.reshape(depth, 2, tabs["kap"], 4 * HY_B1, hy)
    return _filter_bdft(pieces(tabs["fb_full"]), pieces(tabs["fb_conj"]), ya5, tabs["nka"])


def _hyena(z, x0, skip, hhat, layer, tabs):
    b, seq, hy = z.shape
    kap, nka = tabs["kap"], tabs["nka"]
    ya = _hyadft(jnp.asarray(tabs["fa_data"], BF16)[None], z, out_dtype=BF16)
    y4 = ya.reshape(b, kap, 2 * HY_B1, hy)
    w4 = _hyfreq(y4, jnp.asarray(tabs["fb_half"], BF16), hhat, jnp.asarray(tabs["fbi"], BF16), nka, layer)
    w3 = w4.reshape(b, 2 * kap * HY_B1, hy)
    return _hyout(jnp.asarray(tabs["ainv"], BF16), w3, z, x0, skip[None])


def _na_geometry(rows):
    gr, kr_n, kh = NA_GROUP_ROWS, NA_KEY_ROWS, NA_KH_MAX
    n_g = rows // gr
    geo = []
    for g in (0, 1, n_g - 1):
        ks = min(max(gr * g - kh // 2, 0), rows - kr_n)
        per_q = []
        for qr in range(gr):
            r = gr * g + qr
            rs = min(max(r - kh // 2, 0), rows - kh)
            per_q.append([((rs <= ks + k < rs + kh), ks + k - r + NA_KH_MAX - 1) for k in range(kr_n)])
        geo.append(per_q)
    return geo


def _na_table_kernel(r_ref, t_ref, *, geo):
    w, kw = GRID_W, NA_KW
    qc = lax.broadcasted_iota(I32, (w, 1), 0)
    lane = lax.broadcasted_iota(I32, (1, LANES), 1)
    kc = lane % w
    cs = jnp.clip(qc - kw // 2, 0, w - kw)
    colvalid = (kc >= cs) & (kc < cs + kw)
    left = lane < w
    neg = jnp.full((w, LANES), NEG, F32)
    shift = LANES - (kw - 1)

    cache = {}

    def toeplitz(dr, lane_off):
        if (dr, lane_off) not in cache:
            row = r_ref[0, 0, dr:dr + 1, :]
            if lane_off:
                row = pltpu.roll(row, lane_off, 1)
            tz = pltpu.roll(jnp.broadcast_to(row, (w, LANES)), shift, 1, stride=1, stride_axis=0)
            cache[(dr, lane_off)] = jnp.where(colvalid, tz, neg)
        return cache[(dr, lane_off)]

    for v, per_q in enumerate(geo):
        for qr, per_k in enumerate(per_q):
            for pair in range(len(per_k) // 2):
                (ok0, dr0), (ok1, dr1) = per_k[2 * pair], per_k[2 * pair + 1]
                lo = toeplitz(dr0, 0) if ok0 else neg
                hi = toeplitz(dr1, w) if ok1 else neg
                t_ref[0, v, 0, qr * w:(qr + 1) * w, pair * LANES:(pair + 1) * LANES] = jnp.where(left, lo, hi)


def _na_tables(rpb_all, rows):
    depth, heads, nr, nc = rpb_all.shape
    rp = jnp.pad(rpb_all.astype(F32), ((0, 0), (0, 0), (0, 16 - nr), (0, LANES - nc)))
    tq, tk = NA_GROUP_ROWS * GRID_W, NA_KEY_ROWS * GRID_W
    return pl.pallas_call(
        functools.partial(_na_table_kernel, geo=_na_geometry(rows)),
        grid=(depth, heads),
        in_specs=[pl.BlockSpec((1, 1, 16, LANES), lambda l, h: (l, h, 0, 0))],
        out_specs=pl.BlockSpec((1, 3, 1, tq, tk), lambda l, h: (l, 0, h, 0, 0)),
        out_shape=jax.ShapeDtypeStruct((depth, 3, heads, tq, tk), F32),
        compiler_params=_cparams(("parallel", "parallel")),
        name="natten_tables",
    )(rp)


def _natten_kernel(q_ref, k_ref, v_ref, t_ref, o_ref, *, heads, n_g, rows):
    g = pl.program_id(1)
    tq = q_ref.shape[1]
    tk = t_ref.shape[4]
    ks = jnp.clip(NA_GROUP_ROWS * g - NA_KH_MAX // 2, 0, rows - NA_KEY_ROWS)
    kstart = pl.multiple_of(ks * GRID_W, GRID_W)
    per_tile = LANES // NA_HEAD_DIM
    lane = lax.broadcasted_iota(I32, (1, LANES), 1)
    ones = jnp.ones((tk, LANES), BF16)
    for j in range(heads // per_tile):
        lanes = slice(j * LANES, (j + 1) * LANES)
        q2 = q_ref[0, :, lanes].astype(F32) * (NA_HEAD_DIM ** -0.5)
        k2 = k_ref[0, pl.ds(kstart, tk), lanes]
        vaug = jnp.concatenate([v_ref[0, pl.ds(kstart, tk), lanes], ones], axis=1)
        o2 = None
        for hh in range(per_tile):
            own = (lane >= hh * NA_HEAD_DIM) & (lane < (hh + 1) * NA_HEAD_DIM)
            qm = jnp.where(own, q2, 0.0).astype(BF16)
            s = lax.dot_general(qm, k2, (((1,), (1,)), ((), ())), preferred_element_type=F32)
            s = s + t_ref[0, 0, j * per_tile + hh]
            m = jnp.max(s, axis=-1, keepdims=True)
            p = jnp.exp((s - m).astype(BF16))
            r = jnp.dot(p, vaug, preferred_element_type=F32)
            o = r[:, :LANES] / r[:, LANES:]
            o2 = o if o2 is None else jnp.where(own, o, o2)
        o_ref[0, :, lanes] = o2.astype(o_ref.dtype)


def _natten(qkv, tables, layer, b, seq, naw):
    heads = naw // NA_HEAD_DIM
    rows = seq // GRID_W
    n_g = rows // NA_GROUP_ROWS
    tq = NA_GROUP_ROWS * GRID_W
    tk = NA_KEY_ROWS * GRID_W

    def tmap(i, g):
        return (layer, jnp.where(g == 0, 0, jnp.where(g == n_g - 1, 2, 1)), 0, 0, 0)

    return pl.pallas_call(
        functools.partial(_natten_kernel, heads=heads, n_g=n_g, rows=rows),
        grid=(b, n_g),
        in_specs=[pl.BlockSpec((1, tq, naw), lambda i, g: (i, g, 0)),
                  pl.BlockSpec((1, seq, naw), lambda i, g: (i, 0, 1)),
                  pl.BlockSpec((1, seq, naw), lambda i, g: (i, 0, 2)),
                  pl.BlockSpec((1, 1, heads, tq, tk), tmap)],
        out_specs=pl.BlockSpec((1, tq, naw), lambda i, g: (i, g, 0)),
        out_shape=jax.ShapeDtypeStruct((b, seq, naw), F32),
        compiler_params=_cparams(("parallel", "arbitrary")),
        name="natten",
    )(qkv, qkv, qkv, tables)


def _pool_kernel(u_ref, w_ref, sc_ref, o_ref, pad_ref, *, pw):
    seq = u_ref.shape[1]
    lp = seq + 2 * POOL_PAD
    j = pl.program_id(1)
    u = u_ref[0]
    zeros = jnp.zeros((POOL_PAD, LANES), F32)
    pad_ref[0:POOL_PAD, :] = zeros
    pad_ref[POOL_PAD + seq:lp, :] = zeros
    pad_ref[POOL_PAD:POOL_PAD + seq, :] = u
    xp = pad_ref[...]
    dn = lambda a, k: pltpu.roll(a, k, 0)
    up = lambda a, k: pltpu.roll(a, lp - k, 0)
    s2 = xp + dn(xp, 1)
    s4 = dn(s2, 1) + up(s2, 1)
    s8 = dn(s4, 2) + up(s4, 2)
    s16 = dn(s8, 4) + up(s8, 4)
    lane = lax.broadcasted_iota(I32, (1, LANES), 1) + j * LANES
    gdim = pw // len(POOL_WINDOWS)
    grp = lane // gdim
    sums = (s2, s4, s8, s16)
    wsum = sums[0][POOL_PAD:POOL_PAD + seq]
    half = jnp.full((1, LANES), POOL_WINDOWS[0] // 2, F32)
    for gi in range(1, len(POOL_WINDOWS)):
        wsum = jnp.where(grp == gi, sums[gi][POOL_PAD:POOL_PAD + seq], wsum)
        half = jnp.where(grp == gi, float(POOL_WINDOWS[gi] // 2), half)
    t = lax.broadcasted_iota(I32, (seq, LANES), 0).astype(F32)
    cnt = jnp.minimum(t + half, float(seq)) - jnp.maximum(t - half, 0.0)
    pooled = wsum / cnt - u
    y = jnp.dot(pooled.astype(BF16), w_ref[0], preferred_element_type=F32)
    o_ref[0] = y * sc_ref[...]


def _pool(u, pool_w, pool_scale):
    b, seq, pw = u.shape
    ng, gd, _ = pool_w.shape
    nh = pw // LANES
    per = LANES // gd
    wbd = jnp.zeros((nh, LANES, LANES), F32)
    for gi in range(ng):
        hh, k = divmod(gi, per)
        wbd = wbd.at[hh, k * gd:(k + 1) * gd, k * gd:(k + 1) * gd].set(pool_w[gi])
    return pl.pallas_call(
        functools.partial(_pool_kernel, pw=pw),
        grid=(b, nh),
        in_specs=[pl.BlockSpec((1, seq, LANES), lambda i, j: (i, 0, j)),
                  pl.BlockSpec((1, LANES, LANES), lambda i, j: (j, 0, 0)),
                  pl.BlockSpec((1, LANES), lambda i, j: (0, j))],
        out_specs=pl.BlockSpec((1, seq, LANES), lambda i, j: (i, 0, j)),
        out_shape=jax.ShapeDtypeStruct((b, seq, pw), F32),
        scratch_shapes=[pltpu.VMEM((seq + 2 * POOL_PAD, LANES), F32)],
        compiler_params=_cparams(("parallel", "parallel")),
        name="pool_mixer",
    )(u, wbd.astype(BF16), pool_scale[None])


def _outproj_kernel(x_ref, yh_ref, yn_ref, yp_ref, gm_ref, w_ref, g2_ref, wr_ref,
                    xo_ref, h_ref, aff_ref, *, hy, naw):
    gm = gm_ref[...]
    m1 = _rms(yh_ref[...], gm[:, :hy]).astype(BF16)
    m2 = _rms(yn_ref[...], gm[:, hy:hy + naw]).astype(BF16)
    m3 = _rms(yp_ref[...], gm[:, hy + naw:]).astype(BF16)
    acc = jnp.dot(m1, w_ref[:hy, :], preferred_element_type=F32)
    acc += jnp.dot(m2, w_ref[hy:hy + naw, :], preferred_element_type=F32)
    acc += jnp.dot(m3, w_ref[hy + naw:, :], preferred_element_type=F32)
    xn = x_ref[...] + acc
    xo_ref[...] = xn
    h = _rms(xn, g2_ref[...])
    h_ref[...] = h.reshape(h_ref.shape)
    logits = _dot_x3(wr_ref[0], wr_ref[1], h, (((1,), (1,)), ((), ())))
    mx = jnp.max(logits, axis=0, keepdims=True)
    ex = jnp.exp(logits - mx)
    aff_ref[...] = ex / jnp.sum(ex, axis=0, keepdims=True)


def _outproj(x2, yh, yn, yp, gm, w_bf, g2, wr_t, tm=512):
    n, d = x2.shape
    hy, naw, pw = yh.shape[1], yn.shape[1], yp.shape[1]
    e = wr_t.shape[1]
    row = lambda c: pl.BlockSpec((tm, c), lambda i: (i, 0))
    full = lambda s: pl.BlockSpec(s, lambda i: (0, 0))
    return pl.pallas_call(
        functools.partial(_outproj_kernel, hy=hy, naw=naw),
        grid=(n // tm,),
        in_specs=[row(d), row(hy), row(naw), row(pw), full((1, d)), full((d, d)), full((1, d)),
                  pl.BlockSpec((2, e, d), lambda i: (0, 0, 0))],
        out_specs=[row(d), pl.BlockSpec((tm, d // LANES, LANES), lambda i: (i, 0, 0)),
                   pl.BlockSpec((e, tm), lambda i: (0, i))],
        out_shape=[jax.ShapeDtypeStruct((n, d), F32), jax.ShapeDtypeStruct((n, d // LANES, LANES), F32),
                   jax.ShapeDtypeStruct((e, n), F32)],
        compiler_params=_cparams(("parallel",)),
        name="outproj_router",
    )(x2, yh, yn, yp, gm, w_bf, g2, wr_t)


def _block_cumsum(x, tri):
    r, n = x.shape
    cls, offs = [], []
    off = jnp.zeros((r, 1), F32)
    for j in range(n // LANES):
        c = jnp.dot(x[:, j * LANES:(j + 1) * LANES], tri, preferred_element_type=F32)
        cls.append(c)
        off = off + c[:, LANES - 1:LANES]
        offs.append(off)
    return cls, offs


def _route_kernel(aff_ref, tri_ref, bci_ref, bcx_ref, idx_ref, gate_ref, blk_ref, *, cap):
    aff = aff_ref[...]
    e, seq = aff.shape
    nblk = seq // LANES
    bits = pltpu.bitcast(aff, I32)
    capf = jnp.float32(cap)

    def radix(i, prefix):
        cand = prefix | jnp.left_shift(jnp.int32(1), 30 - i)
        cnt = jnp.sum((bits >= cand).astype(F32), axis=1, keepdims=True)
        return jnp.where(cnt >= capf, cand, prefix)

    tau = lax.fori_loop(0, 31, radix, jnp.zeros((e, 1), I32))
    gt = bits > tau
    eq = bits == tau
    need = capf - jnp.sum(gt.astype(F32), axis=1, keepdims=True)
    tri = tri_ref[...]
    cls, offs = _block_cumsum(jnp.where(eq, 1.0, 0.0).astype(BF16), tri)
    tie_rank = jnp.concatenate([c if j == 0 else c + offs[j - 1] for j, c in enumerate(cls)], axis=1)
    sel = gt | (eq & (tie_rank <= need))
    self32 = jnp.where(sel, 1.0, 0.0)
    selb = self32.astype(BF16)
    cls, _ = _block_cumsum(selb, tri)
    bend = jnp.dot(selb, bci_ref[...], preferred_element_type=F32)
    bstart = jnp.dot(selb, bcx_ref[...], preferred_element_type=F32)
    for j in range(nblk):
        rows = slice(j * e, (j + 1) * e)
        lanes = slice(j * LANES, (j + 1) * LANES)
        blk_ref[0, rows, :] = cls[j]
        blk_ref[1, rows, :] = self32[:, lanes]
        blk_ref[2, rows, :] = aff[:, lanes]
    slot = lax.broadcasted_iota(I32, (cap, 1), 0).astype(F32)
    lane = lax.broadcasted_iota(I32, (1, LANES), 1).astype(F32)
    for ei in range(e):
        bs, be = bstart[ei:ei + 1, :], bend[ei:ei + 1, :]
        inblk = (bs <= slot) & (slot < be)
        local = slot - jnp.sum(jnp.where(inblk, bs, 0.0), axis=1, keepdims=True)
        jcol = jnp.sum(jnp.where(inblk, lane, 0.0), axis=1, keepdims=True)
        pick = jnp.where(inblk, 1.0, 0.0)[:, :nblk].astype(BF16)
        rows = pl.ds(ei, nblk, stride=e)
        a = blk_ref[2, rows, :]
        a_hi = a.astype(BF16)
        r1 = a - a_hi.astype(F32)
        a_mid = r1.astype(BF16)
        a_lo = (r1 - a_mid.astype(F32)).astype(BF16)
        take = lambda v: jnp.dot(pick, v, preferred_element_type=F32)
        g_cl = take(blk_ref[0, rows, :].astype(BF16))
        g_sel = take(blk_ref[1, rows, :].astype(BF16))
        g_aff = take(a_hi) + take(a_mid) + take(a_lo)
        hit = (g_cl == local + 1.0) & (g_sel > 0.5)
        idx = jcol * LANES + jnp.sum(jnp.where(hit, lane, 0.0), axis=1, keepdims=True)
        gate = jnp.sum(jnp.where(hit, g_aff, 0.0), axis=1, keepdims=True)
        idx_ref[0, :, ei:ei + 1] = idx.astype(I32)
        gate_ref[0, ei] = jnp.broadcast_to(gate, (cap, LANES))


def _route(aff_t, b, seq, cap):
    e = aff_t.shape[0]
    nblk = seq // LANES
    tri = jnp.asarray(np.triu(np.ones((LANES, LANES), np.float32)), BF16)
    tblk = np.arange(seq)[:, None] // LANES
    bci = jnp.asarray(tblk <= np.arange(LANES)[None, :], BF16)
    bcx = jnp.asarray(tblk < np.arange(LANES)[None, :], BF16)
    full = lambda s: pl.BlockSpec(s, lambda i: (0, 0))
    idx, gate = pl.pallas_call(
        functools.partial(_route_kernel, cap=cap),
        grid=(b,),
        in_specs=[pl.BlockSpec((e, seq), lambda i: (0, i)), full((LANES, LANES)),
                  full((seq, LANES)), full((seq, LANES))],
        out_specs=[pl.BlockSpec((1, cap, e), lambda i: (i, 0, 0)),
                   pl.BlockSpec((1, e, cap, LANES), lambda i: (i, 0, 0, 0))],
        out_shape=[jax.ShapeDtypeStruct((b, cap, e), I32), jax.ShapeDtypeStruct((b, e, cap, LANES), F32)],
        scratch_shapes=[pltpu.VMEM((3, nblk * e, LANES), F32)],
        compiler_params=_cparams(("parallel",)),
        name="ec_route",
    )(aff_t, tri, bci, bcx)
    return idx.transpose(0, 2, 1), gate


def _expert_kernel(rows_ref, h_hbm, wg_ref, wu_ref, wd_ref, y_ref, xbuf, xb, acc, sem, *, tm, nm, nf, ne):
    e = pl.program_id(0)
    m = pl.program_id(1)
    f = pl.program_id(2)
    tile = e * nm + m
    last = ne * nm - 1
    chunk = tm // nf
    d = xb.shape[2]
    rb = min(EXPERT_RELAYOUT_ROWS, chunk)

    def row_copy(src_row, p, c, dst_row):
        return pltpu.make_async_copy(h_hbm.at[pl.ds(src_row, 1)], xbuf.at[p, c, pl.ds(dst_row, 1)], sem.at[p, c])

    def chunk_wait(t, c):
        pltpu.make_async_copy(h_hbm.at[pl.ds(0, chunk)], xbuf.at[t % 2, c], sem.at[t % 2, c]).wait()

    def chunk_convert(t, c):
        for j in range(chunk // rb):
            rows = xbuf[t % 2, c, j * rb:(j + 1) * rb]
            xb[t % 2, pl.ds(c * chunk + j * rb, rb), :] = rows.reshape(rb, d).astype(BF16)

    @pl.when((tile == 0) & (f == 0))
    def _():
        nxt0 = min(1, last)
        for t, c in [(0, k) for k in range(nf)] + [(nxt0, 0)]:
            def issue(i, carry, t=t, c=c):
                row_copy(rows_ref[t * tm + c * chunk + i], t % 2, c, i).start()
                return carry

            lax.fori_loop(0, chunk, issue, 0, unroll=8)
        for k in range(nf):
            chunk_wait(0, k)
            chunk_convert(0, k)

    @pl.when(f == 0)
    def _():
        acc[...] = jnp.zeros(acc.shape, F32)

    nxt = jnp.minimum(tile + 1, last)
    wrap = f + 1 == nf
    t_issue = jnp.where(wrap, jnp.minimum(tile + 2, last), nxt)
    c_issue = jnp.where(wrap, 0, f + 1)
    base = t_issue * tm + c_issue * chunk
    for i in range(chunk):
        row_copy(rows_ref[base + i], t_issue % 2, c_issue, i).start()

    x = xb[tile % 2]
    a = jnp.dot(x, wg_ref[0, 0].astype(BF16), preferred_element_type=F32)
    u = jnp.dot(x, wu_ref[0, 0].astype(BF16), preferred_element_type=F32)
    hh = (a * jax.nn.sigmoid(a) * u).astype(BF16)
    acc[...] += jnp.dot(hh, wd_ref[0, 0].astype(BF16), preferred_element_type=F32)

    chunk_wait(nxt, f)
    chunk_convert(nxt, f)

    @pl.when(f == nf - 1)
    def _():
        y_ref[0] = acc[...].astype(y_ref.dtype)

    @pl.when((tile == last) & (f == nf - 1))
    def _():
        chunk_wait(last, 0)


EXPERT_RELAYOUT_ROWS = 64


def _experts(rows_flat, h3, w_gate, w_up, w_down, layer, mtot, tm=1024, tf=512):
    _, e, d, ff = w_gate.shape
    tm = min(tm, mtot)
    tf = min(tf, ff)
    nm, nf = mtot // tm, ff // tf
    grid_spec = pltpu.PrefetchScalarGridSpec(
        num_scalar_prefetch=1,
        grid=(e, nm, nf),
        in_specs=[pl.BlockSpec(memory_space=pl.ANY),
                  pl.BlockSpec((1, 1, d, tf), lambda i, m, f, r: (layer, i, 0, f)),
                  pl.BlockSpec((1, 1, d, tf), lambda i, m, f, r: (layer, i, 0, f)),
                  pl.BlockSpec((1, 1, tf, d), lambda i, m, f, r: (layer, i, f, 0))],
        out_specs=pl.BlockSpec((1, tm, d), lambda i, m, f, r: (i, m, 0)),
        scratch_shapes=[pltpu.VMEM((2, nf, tm // nf, d // LANES, LANES), F32), pltpu.VMEM((2, tm, d), BF16),
                        pltpu.VMEM((tm, d), F32),
                        pltpu.SemaphoreType.DMA((2, nf))],
    )
    return pl.pallas_call(
        functools.partial(_expert_kernel, tm=tm, nm=nm, nf=nf, ne=e),
        grid_spec=grid_spec,
        out_shape=jax.ShapeDtypeStruct((e, mtot, d), BF16),
        compiler_params=_cparams(("arbitrary", "arbitrary", "arbitrary")),
        name="ec_experts",
    )(rows_flat, h3, w_gate, w_up, w_down)


COMBINE_ROWS = 64
COMBINE_UNROLL = 8
COMBINE_EXPERTS = 4


def _combine_kernel(idx_ref, split_ref, x_ref, y_ref, gl, *rest, ne, cap, span, final):
    if final:
        g_ref, o_ref, acc3, y3 = rest
    else:
        o_ref, acc3, y3 = rest
    b = pl.program_id(0)
    sp = pl.program_id(1)
    eg = pl.program_id(2)
    d = x_ref.shape[2]
    sub = d // LANES
    rb = COMBINE_ROWS
    nu = COMBINE_UNROLL

    @pl.when(eg == 0)
    def _():
        def load(c, carry):
            r0 = pl.multiple_of(c * rb, rb)
            acc3[pl.ds(pl.multiple_of(r0 * sub, rb * sub), rb * sub), :] = (
                x_ref[0, pl.ds(r0, rb), :].reshape(rb * sub, LANES))
            return carry

        lax.fori_loop(0, span // rb, load, 0)

    def one_expert(k):
        e = eg * COMBINE_EXPERTS + k
        lo = split_ref[(b * ne + e) * 3 + sp]
        hi = split_ref[(b * ne + e) * 3 + sp + 1]

        def relayout(c, carry):
            r0 = pl.multiple_of(c * rb, rb)
            y3[pl.ds(r0, rb)] = y_ref[k, pl.ds(r0, rb), :].astype(F32).reshape(rb, sub, LANES)
            return carry

        lax.fori_loop(lo // rb, (hi + rb - 1) // rb, relayout, 0)
        base = (b * ne + e) * cap

        def tokens(first, count):
            return tuple(idx_ref[base + first + u] for u in range(count))

        def add_rows(first, toks):
            rows = [pl.ds(pl.multiple_of(t, sub), sub) for t in toks]
            vals = [acc3[r, :] + gl[0, k, pl.ds(first + u, 1), :] * y3[first + u] for u, r in enumerate(rows)]
            for r, v in zip(rows, vals):
                acc3[r, :] = v

        def group(g, toks):
            first = lo + g * nu
            nxt = tokens(jnp.minimum(first + nu, cap - nu), nu)
            add_rows(first, toks)
            return nxt

        ngroups = (hi - lo) // nu
        lax.fori_loop(0, ngroups, group, tokens(jnp.minimum(lo, cap - nu), nu))

        def tail(i, carry):
            add_rows(i, tokens(i, 1))
            return carry

        lax.fori_loop(lo + ngroups * nu, hi, tail, 0)

    for k in range(COMBINE_EXPERTS):
        one_expert(k)

    @pl.when(eg == ne // COMBINE_EXPERTS - 1)
    def _():
        def store(c, carry):
            r0 = pl.multiple_of(c * rb, rb)
            v = acc3[pl.ds(pl.multiple_of(r0 * sub, rb * sub), rb * sub), :].reshape(rb, d)
            if final:
                v = _rms(v, g_ref[...])
            o_ref[0, pl.ds(r0, rb), :] = v
            return carry

        lax.fori_loop(0, span // rb, store, 0)


def _combine(idx, gate, x3, y, cap, final_g=None):
    b, seq, d = x3.shape
    ne = y.shape[0]
    span = seq // 2
    n_lower = jnp.sum((idx < span).astype(I32), axis=-1)
    split_flat = jnp.stack([jnp.zeros_like(n_lower), n_lower, jnp.full_like(n_lower, cap)], axis=-1).reshape(-1)
    final = final_g is not None
    ge = COMBINE_EXPERTS
    in_specs = [pl.BlockSpec((1, span, d), lambda i, s, e, *_: (i, s, 0)),
                pl.BlockSpec((ge, cap, d), lambda i, s, e, *_: (e, i, 0)),
                pl.BlockSpec((1, ge, cap, LANES), lambda i, s, e, *_: (i, e, 0, 0))]
    args = [x3, y, gate]
    if final:
        in_specs.append(pl.BlockSpec((1, d), lambda i, s, e, *_: (0, 0)))
        args.append(final_g)
    grid_spec = pltpu.PrefetchScalarGridSpec(
        num_scalar_prefetch=2,
        grid=(b, 2, ne // ge),
        in_specs=in_specs,
        out_specs=pl.BlockSpec((1, span, d), lambda i, s, e, *_: (i, s, 0)),
        scratch_shapes=[pltpu.VMEM((span * (d // LANES), LANES), F32), pltpu.VMEM((cap, d // LANES, LANES), F32)],
    )
    return pl.pallas_call(
        functools.partial(_combine_kernel, ne=ne, cap=cap, span=span, final=final),
        grid_spec=grid_spec,
        out_shape=jax.ShapeDtypeStruct((b, seq, d), F32),
        compiler_params=_cparams(("arbitrary", "arbitrary", "arbitrary")),
        name="ec_combine",
    )(((idx % span) * (d // LANES)).reshape(-1), split_flat, *args)


def _moe(x3, h2, aff_t, w_gate, w_up, w_down, layer, final_g=None):
    b, seq, d = x3.shape
    ne = w_gate.shape[1]
    cap = EC_CAPACITY * seq // ne
    idx, gate = _route(aff_t, b, seq, cap)
    rows = idx + (jnp.arange(b, dtype=I32) * seq)[:, None, None]
    rows_flat = rows.transpose(1, 0, 2).reshape(-1)
    y = _experts(rows_flat, h2, w_gate, w_up, w_down, layer, b * cap)
    return _combine(idx, gate, x3, y, cap, final_g)


def kernel(x, norm1_g, w_in, hy_short_w, hy_short_b, hy_f_w1, hy_f_b1, hy_f_w2, hy_f_b2, hy_f_wout, hy_f_freq, hy_skip, na_rpb, pool_w, pool_scale, mix_norm_g, w_out, norm2_g, w_router, w_gate, w_up, w_down, final_g):
    b, seq, d = x.shape
    depth = w_in.shape[0]
    hy = hy_skip.shape[1]
    pw = pool_scale.shape[1]
    naw = d - hy - pw
    n = b * seq
    rows = seq // GRID_W
    tabs = _dft_tables(seq // HY_B1)
    x2 = x.reshape(n, d)
    hhat = _hyena_filter_spectra(seq, hy, hy_f_w1, hy_f_b1, hy_f_w2, hy_f_b2, hy_f_wout, hy_f_freq, tabs)
    na_tables = _na_tables(na_rpb, rows)
    for i in range(depth):
        z, x0, qkv, pool_in = _inproj(x2, norm1_g[i][None], w_in[i].astype(BF16), hy_short_w[i],
                                      hy_short_b[i][None], 3 * hy, 3 * naw, seq)
        y_hy = _hyena(z.reshape(b, seq, hy), x0.reshape(b, seq, hy), hy_skip[i], hhat, i, tabs)
        y_na = _natten(qkv.reshape(b, seq, 3 * naw), na_tables, i, b, seq, naw)
        y_pool = _pool(pool_in.reshape(b, seq, pw), pool_w[i], pool_scale[i])
        x2, h2, aff_t = _outproj(x2, y_hy.reshape(n, hy), y_na.reshape(n, naw), y_pool.reshape(n, pw),
                                 mix_norm_g[i][None], w_out[i].astype(BF16), norm2_g[i][None],
                                 jnp.stack(_split_bf16(w_router[i].T)))
        last = final_g[None] if i == depth - 1 else None
        x2 = _moe(x2.reshape(b, seq, d), h2, aff_t, w_gate, w_up, w_down, i, last).reshape(n, d)
    return x2.reshape(b, seq, d)
```

```python
import functools
import math

import numpy as np
import jax
import jax.numpy as jnp
from jax import lax
from jax.experimental import pallas as pl
from jax.experimental.pallas import tpu as pltpu

F32 = jnp.float32
BF16 = jnp.bfloat16
I32 = jnp.int32
EPS = 1e-6
HIGHEST = lax.Precision.HIGHEST

GRID_W = 64
NA_HEAD_DIM = 64
NA_KH_MAX = 8
NA_KW = 16
NA_GROUP_ROWS = 4
NA_KEY_ROWS = 12
POOL_WINDOWS = (2, 4, 8, 16)
POOL_PAD = 16
FILTER_EMB = 33
DECAY_FAST, DECAY_SLOW, DECAY_TARGET = 0.3, 1.5, 1e-2
EC_CAPACITY = 2
HY_B1 = 128
NEG = -1e30
LANES = 128
V7X_VMEM_BYTES = 64 * 1024 * 1024
VMEM_LIMIT = V7X_VMEM_BYTES - 8 * 1024 * 1024


def _cparams(sem, vmem=VMEM_LIMIT):
    return pltpu.CompilerParams(dimension_semantics=sem, vmem_limit_bytes=vmem)


def _rms(v, g):
    return v * lax.rsqrt(jnp.mean(v * v, axis=-1, keepdims=True) + EPS) * g


def _split_bf16(v):
    hi = v.astype(BF16)
    return hi, (v - hi.astype(F32)).astype(BF16)


def _dot_x3(a_hi, a_lo, x, dims=(((1,), (0,)), ((), ()))):
    x_hi, x_lo = _split_bf16(x)
    dg = lambda p, q: lax.dot_general(p, q, dims, preferred_element_type=F32)
    return dg(a_hi, x_hi) + dg(a_lo, x_hi) + dg(a_hi, x_lo)


HALO = 16


def _inproj_kernel(x_ref, xp_ref, xn_ref, g_ref, w_ref, sw_ref, sb_ref, z_ref, x0_ref, qkv_ref, pool_ref,
                   *, hyw, naw, seq):
    g = g_ref[...]
    tm = x_ref.shape[0]
    h = _rms(x_ref[...], g).astype(BF16)
    hall = jnp.concatenate([_rms(xp_ref[...], g).astype(BF16), h, _rms(xn_ref[...], g).astype(BF16)], axis=0)
    u = jnp.dot(hall, w_ref[:, :hyw], preferred_element_type=F32)
    t = lax.rem(pl.program_id(0) * tm + lax.broadcasted_iota(I32, (tm, 1), 0), seq)
    prev = jnp.where(t == 0, 0.0, u[HALO - 1:HALO - 1 + tm])
    nxt = jnp.where(t == seq - 1, 0.0, u[HALO + 1:HALO + 1 + tm])
    conv = prev * sw_ref[0:1, :] + u[HALO:HALO + tm] * sw_ref[1:2, :] + nxt * sw_ref[2:3, :] + sb_ref[...]
    hy = hyw // 3
    x0_ref[...] = conv[:, :hy]
    z_ref[...] = conv[:, 2 * hy:] * conv[:, hy:2 * hy]
    qkv_ref[...] = jnp.dot(h, w_ref[:, hyw:hyw + naw], preferred_element_type=F32).astype(BF16)
    pool_ref[...] = jnp.dot(h, w_ref[:, hyw + naw:], preferred_element_type=F32)


def _inproj(x2, g, w_bf, sw, sb, hyw, naw, seq, tm=1024):
    n, d = x2.shape
    inw = w_bf.shape[1]
    pw = inw - hyw - naw
    hy = hyw // 3
    per = tm // HALO
    last = n // HALO - 1
    return pl.pallas_call(
        functools.partial(_inproj_kernel, hyw=hyw, naw=naw, seq=seq),
        grid=(n // tm,),
        in_specs=[pl.BlockSpec((tm, d), lambda i: (i, 0)),
                  pl.BlockSpec((HALO, d), lambda i: (jnp.maximum(i * per - 1, 0), 0)),
                  pl.BlockSpec((HALO, d), lambda i: (jnp.minimum((i + 1) * per, last), 0)),
                  pl.BlockSpec((1, d), lambda i: (0, 0)),
                  pl.BlockSpec((d, inw), lambda i: (0, 0)),
                  pl.BlockSpec((3, hyw), lambda i: (0, 0)),
                  pl.BlockSpec((1, hyw), lambda i: (0, 0))],
        out_specs=[pl.BlockSpec((tm, hy), lambda i: (i, 0)),
                   pl.BlockSpec((tm, hy), lambda i: (i, 0)),
                   pl.BlockSpec((tm, naw), lambda i: (i, 0)),
                   pl.BlockSpec((tm, pw), lambda i: (i, 0))],
        out_shape=[jax.ShapeDtypeStruct((n, hy), F32),
                   jax.ShapeDtypeStruct((n, hy), F32),
                   jax.ShapeDtypeStruct((n, naw), BF16),
                   jax.ShapeDtypeStruct((n, pw), F32)],
        compiler_params=_cparams(("parallel",)),
        name="inproj",
    )(x2, x2, x2, g, w_bf, sw, sb)


def _filter_kernel(z_ref, w1_ref, b1_ref, w2_ref, b2_ref, wo_ref, fr_ref, dec_ref, o_ref):
    fr = fr_ref[0]
    h = jnp.sin(fr * (jnp.dot(z_ref[...], w1_ref[0], preferred_element_type=F32, precision=HIGHEST) + b1_ref[0]))
    h = jnp.sin(fr * (jnp.dot(h, w2_ref[0], preferred_element_type=F32, precision=HIGHEST) + b2_ref[0]))
    hw = jnp.dot(h, wo_ref[0], preferred_element_type=F32, precision=HIGHEST)
    c = dec_ref.shape[2]
    o_ref[0, 0] = hw[:, :c] * dec_ref[0]
    o_ref[0, 1] = hw[:, c:] * dec_ref[1]


def _filter_mlp(zemb, w1, b1, w2, b2, wo, fr, dec, tl=512):
    seq, emb = zemb.shape
    depth, _, hid = w1.shape
    ow = wo.shape[2]
    c = ow // 2
    lay = lambda s: pl.BlockSpec((1,) + s, lambda l, i: (l, 0, 0))
    return pl.pallas_call(
        _filter_kernel,
        grid=(depth, seq // tl),
        in_specs=[pl.BlockSpec((tl, emb), lambda l, i: (i, 0)), lay((emb, hid)), lay((1, hid)),
                  lay((hid, hid)), lay((1, hid)), lay((hid, ow)), lay((1, hid)),
                  pl.BlockSpec((2, tl, c), lambda l, i: (0, i, 0))],
        out_specs=pl.BlockSpec((1, 2, tl, c), lambda l, i: (l, 0, i, 0)),
        out_shape=jax.ShapeDtypeStruct((depth, 2, seq, c), F32),
        compiler_params=_cparams(("parallel", "parallel")),
        name="hyena_filter_mlp",
    )(zemb, w1, b1, w2, b2, wo, fr, dec)


def _filter_bdft_kernel(ff_ref, fc_ref, yf_ref, yb_ref, o_ref):
    o_ref[0, 0] = (_dot_x3(ff_ref[0], ff_ref[1], yf_ref[0, 0, 0]) + _dot_x3(fc_ref[0], fc_ref[1], yb_ref[0, 0, 0]))


def _filter_bdft(ff, fc, ya5, nka):
    depth, _, kap, r, c = ya5.shape
    blk = lambda p: pl.BlockSpec((1, 1, 1, r, c), lambda l, k, p=p: (l, p, k, 0, 0))
    return pl.pallas_call(
        _filter_bdft_kernel,
        grid=(depth, nka),
        in_specs=[pl.BlockSpec((2, r, r), lambda l, k: (0, 0, 0)), pl.BlockSpec((2, r, r), lambda l, k: (0, 0, 0)),
                  blk(0), blk(1)],
        out_specs=pl.BlockSpec((1, 1, r, c), lambda l, k: (l, k, 0, 0)),
        out_shape=jax.ShapeDtypeStruct((depth, nka, r, c), F32),
        compiler_params=_cparams(("parallel", "parallel")),
        name="hyena_filter_bdft",
    )(ff, fc, ya5, ya5)


def _hyadft_kernel(fa_ref, z_ref, y_ref, zt_ref, yt_ref):
    pieces, k2, a1 = fa_ref.shape
    zt_ref[...] = pltpu.einshape("abl->bal", z_ref[0].reshape(a1, HY_B1, LANES))

    def body(i, c):
        b0 = 2 * i
        zz = jnp.concatenate([zt_ref[b0], zt_ref[b0 + 1]], axis=1)
        if pieces == 1:
            r = jnp.dot(fa_ref[0], zz.astype(BF16), preferred_element_type=F32)
        else:
            r = _dot_x3(fa_ref[0], fa_ref[1], zz)
        yt_ref[b0] = r[:, :LANES]
        yt_ref[b0 + 1] = r[:, LANES:]
        return c

    lax.fori_loop(0, HY_B1 // 2, body, 0, unroll=8)
    y_ref[0] = pltpu.einshape("bkl->kbl", yt_ref[...]).reshape(k2 * HY_B1, LANES).astype(y_ref.dtype)


def _hyadft(fa, z, out_dtype=F32, name="hyena_adft"):
    b, seq, hy = z.shape
    pieces, k2, a1 = fa.shape
    return pl.pallas_call(
        _hyadft_kernel,
        grid=(b, hy // LANES),
        in_specs=[pl.BlockSpec((pieces, k2, a1), lambda i, j: (0, 0, 0)),
                  pl.BlockSpec((1, seq, LANES), lambda i, j: (i, 0, j))],
        out_specs=pl.BlockSpec((1, k2 * HY_B1, LANES), lambda i, j: (i, 0, j)),
        out_shape=jax.ShapeDtypeStruct((b, k2 * HY_B1, hy), out_dtype),
        scratch_shapes=[pltpu.VMEM((HY_B1, a1, LANES), F32), pltpu.VMEM((HY_B1, k2, LANES), F32)],
        compiler_params=_cparams(("parallel", "parallel")),
        name=name,
    )(fa, z)


def _hyfreq_kernel(y_ref, fb_ref, h_ref, fbi_ref, w_ref, *, nka):
    ka = pl.program_id(0)
    nb = fb_ref.shape[0] // 2
    nbatch = y_ref.shape[0]

    @pl.when(ka < nka)
    def _():
        hr, hi = h_ref[0, 0, :nb], h_ref[0, 0, nb:]
        for bi in range(nbatch):
            p = jnp.dot(fb_ref[...], y_ref[bi, 0].astype(BF16), preferred_element_type=F32)
            pr, pi = p[:nb], p[nb:]
            q = jnp.concatenate([pr * hr - pi * hi, pr * hi + pi * hr], axis=0).astype(BF16)
            w_ref[bi, 0] = jnp.dot(fbi_ref[...], q, preferred_element_type=F32).astype(w_ref.dtype)

    @pl.when(ka >= nka)
    def _():
        w_ref[...] = jnp.zeros(w_ref.shape, w_ref.dtype)


def _hyfreq(y4, fb, hhat, fbi, nka, layer):
    b, kap, r2, hy = y4.shape
    nb2 = fb.shape[0]
    return pl.pallas_call(
        functools.partial(_hyfreq_kernel, nka=nka),
        grid=(kap,),
        in_specs=[pl.BlockSpec((b, 1, r2, hy), lambda k: (0, k, 0, 0)),
                  pl.BlockSpec((nb2, r2), lambda k: (0, 0)),
                  pl.BlockSpec((1, 1, nb2, hy), lambda k: (layer, jnp.minimum(k, nka - 1), 0, 0)),
                  pl.BlockSpec((r2, nb2), lambda k: (0, 0))],
        out_specs=pl.BlockSpec((b, 1, r2, hy), lambda k: (0, k, 0, 0)),
        out_shape=jax.ShapeDtypeStruct((b, kap, r2, hy), BF16),
        compiler_params=_cparams(("parallel",)),
        name="hyena_freq",
    )(y4, fb, hhat, fbi)


def _hyout_kernel(a_ref, w_ref, z_ref, x0_ref, sk_ref, o_ref, wt_ref, zt_ref, xt_ref, ot_ref):
    a1, k2 = a_ref.shape
    ainv = a_ref[...]
    skip = sk_ref[...]
    wt_ref[...] = pltpu.einshape("kbl->bkl", w_ref[0].astype(F32).reshape(k2, HY_B1, LANES))
    zt_ref[...] = pltpu.einshape("abl->bal", z_ref[0].reshape(a1, HY_B1, LANES))
    xt_ref[...] = pltpu.einshape("abl->bal", x0_ref[0].reshape(a1, HY_B1, LANES))

    def body(i, c):
        b0 = 2 * i
        ww = jnp.concatenate([wt_ref[b0], wt_ref[b0 + 1]], axis=1).astype(BF16)
        y = jnp.dot(ainv, ww, preferred_element_type=F32)
        ot_ref[b0] = (y[:, :LANES] + zt_ref[b0] * skip) * xt_ref[b0]
        ot_ref[b0 + 1] = (y[:, LANES:] + zt_ref[b0 + 1] * skip) * xt_ref[b0 + 1]
        return c

    lax.fori_loop(0, HY_B1 // 2, body, 0, unroll=8)
    o_ref[0] = pltpu.einshape("bal->abl", ot_ref[...]).reshape(a1 * HY_B1, LANES)


def _hyout(ainv, w3, z, x0, skip):
    b, seq, hy = z.shape
    a1, k2 = ainv.shape
    blk = pl.BlockSpec((1, seq, LANES), lambda i, j: (i, 0, j))
    return pl.pallas_call(
        _hyout_kernel,
        grid=(b, hy // LANES),
        in_specs=[pl.BlockSpec((a1, k2), lambda i, j: (0, 0)),
                  pl.BlockSpec((1, k2 * HY_B1, LANES), lambda i, j: (i, 0, j)),
                  blk, blk,
                  pl.BlockSpec((1, LANES), lambda i, j: (0, j))],
        out_specs=blk,
        out_shape=jax.ShapeDtypeStruct((b, seq, hy), F32),
        scratch_shapes=[pltpu.VMEM((HY_B1, k2, LANES), F32)] + [pltpu.VMEM((HY_B1, a1, LANES), F32)] * 3,
        compiler_params=_cparams(("parallel", "parallel")),
        name="hyena_out",
    )(ainv, w3, z, x0, skip)


def _dft_tables(a1):
    a2, b2, b1 = 2 * a1, 2 * HY_B1, HY_B1
    nka = a1 + 1
    kap = -(-nka // 8) * 8
    ka = np.arange(nka)[:, None]
    def fa(na):
        ph = 2 * np.pi * ((ka * np.arange(na)[None, :]) % a2) / a2
        m = np.zeros((2 * kap, na))
        m[0:2 * nka:2] = np.cos(ph)
        m[1:2 * nka:2] = -np.sin(ph)
        return m
    kb = np.arange(b2)[:, None]
    th = 2 * np.pi * ((kb * np.arange(b2)[None, :]) % b2) / b2
    c, s = np.cos(th), np.sin(th)
    fb_full = np.block([[c, s], [-s, c]])
    fb_half = np.block([[c[:, :b1], s[:, :b1]], [-s[:, :b1], c[:, :b1]]])
    ct, st = c.T[:b1], s.T[:b1]
    fbi = np.block([[ct, -st], [st, ct]]) / b2
    ph = 2 * np.pi * ((np.arange(a1)[:, None] * np.arange(nka)[None, :]) % a2) / a2
    wgt = np.where((np.arange(nka) == 0) | (np.arange(nka) == a1), 1.0, 2.0)[None, :] / a2
    ainv = np.zeros((a1, 2 * kap))
    ainv[:, 0:2 * nka:2] = wgt * np.cos(ph)
    ainv[:, 1:2 * nka:2] = -wgt * np.sin(ph)
    fa2 = fa(a2)
    fa_filt2 = np.stack([fa2[:, 0:a1], fa2[:, 1:a1 + 1]], axis=1).reshape(4 * kap, a1)
    fb_conj = np.concatenate([fb_full[:b2], -fb_full[b2:]], axis=0)
    f32 = lambda v: np.asarray(v, np.float32)
    return dict(nka=nka, kap=kap, fa_data=f32(fa(a1)), fa_filt2=f32(fa_filt2), fb_full=f32(fb_full),
                fb_conj=f32(fb_conj), fb_half=f32(fb_half), fbi=f32(fbi), ainv=f32(ainv))


def _hyena_filter_spectra(seq, hy, w1, b1, w2, b2, wo, fr, tabs):
    nbands = (FILTER_EMB - 1) // 2
    t = jnp.linspace(0.0, 1.0, seq, dtype=F32)[:, None]
    ang = 2.0 * math.pi * jnp.arange(seq, dtype=F32)[:, None] / seq
    f = jnp.linspace(1e-4, nbands - 1, nbands, dtype=F32)[None, :]
    zemb = jnp.concatenate([t, jnp.cos(f * ang), -jnp.sin(f * ang)], axis=-1)
    deltas = jnp.abs(jnp.linspace(math.log(DECAY_TARGET) / DECAY_FAST,
                                  math.log(DECAY_TARGET) / DECAY_SLOW, hy, dtype=F32))
    decay = jnp.exp(-t * deltas)
    dec = jnp.stack([decay, decay * (jnp.arange(seq) > 0)[:, None].astype(F32)])
    depth = w1.shape[0]
    h = _filter_mlp(zemb, w1, b1[:, None], w2, b2[:, None], wo, fr[:, None], dec, tl=min(512, seq))
    pieces = lambda m: jnp.stack(_split_bf16(jnp.asarray(m)))
    ya = _hyadft(pieces(tabs["fa_filt2"]), h.reshape(depth * 2, seq, hy), name="hyena_filter_adft")
    ya5 = ya.reshape(depth, 2, tabs["kap"], 4 * HY_B1, hy)
    return _filter_bdft(pieces(tabs["fb_full"]), pieces(tabs["fb_conj"]), ya5, tabs["nka"])


def _hyena(z, x0, skip, hhat, layer, tabs):
    b, seq, hy = z.shape
    kap, nka = tabs["kap"], tabs["nka"]
    ya = _hyadft(jnp.asarray(tabs["fa_data"], BF16)[None], z, out_dtype=BF16)
    y4 = ya.reshape(b, kap, 2 * HY_B1, hy)
    w4 = _hyfreq(y4, jnp.asarray(tabs["fb_half"], BF16), hhat, jnp.asarray(tabs["fbi"], BF16), nka, layer)
    w3 = w4.reshape(b, 2 * kap * HY_B1, hy)
    return _hyout(jnp.asarray(tabs["ainv"], BF16), w3, z, x0, skip[None])


def _na_geometry(rows):
    gr, kr_n, kh = NA_GROUP_ROWS, NA_KEY_ROWS, NA_KH_MAX
    n_g = rows // gr
    geo = []
    for g in (0, 1, n_g - 1):
        ks = min(max(gr * g - kh // 2, 0), rows - kr_n)
        per_q = []
        for qr in range(gr):
            r = gr * g + qr
            rs = min(max(r - kh // 2, 0), rows - kh)
            per_q.append([((rs <= ks + k < rs + kh), ks + k - r + NA_KH_MAX - 1) for k in range(kr_n)])
        geo.append(per_q)
    return geo


def _na_table_kernel(r_ref, t_ref, *, geo):
    w, kw = GRID_W, NA_KW
    qc = lax.broadcasted_iota(I32, (w, 1), 0)
    lane = lax.broadcasted_iota(I32, (1, LANES), 1)
    kc = lane % w
    cs = jnp.clip(qc - kw // 2, 0, w - kw)
    colvalid = (kc >= cs) & (kc < cs + kw)
    left = lane < w
    neg = jnp.full((w, LANES), NEG, F32)
    shift = LANES - (kw - 1)

    cache = {}

    def toeplitz(dr, lane_off):
        if (dr, lane_off) not in cache:
            row = r_ref[0, 0, dr:dr + 1, :]
            if lane_off:
                row = pltpu.roll(row, lane_off, 1)
            tz = pltpu.roll(jnp.broadcast_to(row, (w, LANES)), shift, 1, stride=1, stride_axis=0)
            cache[(dr, lane_off)] = jnp.where(colvalid, tz, neg)
        return cache[(dr, lane_off)]

    for v, per_q in enumerate(geo):
        for qr, per_k in enumerate(per_q):
            for pair in range(len(per_k) // 2):
                (ok0, dr0), (ok1, dr1) = per_k[2 * pair], per_k[2 * pair + 1]
                lo = toeplitz(dr0, 0) if ok0 else neg
                hi = toeplitz(dr1, w) if ok1 else neg
                t_ref[0, v, 0, qr * w:(qr + 1) * w, pair * LANES:(pair + 1) * LANES] = jnp.where(left, lo, hi)


def _na_tables(rpb_all, rows):
    depth, heads, nr, nc = rpb_all.shape
    rp = jnp.pad(rpb_all.astype(F32), ((0, 0), (0, 0), (0, 16 - nr), (0, LANES - nc)))
    tq, tk = NA_GROUP_ROWS * GRID_W, NA_KEY_ROWS * GRID_W
    return pl.pallas_call(
        functools.partial(_na_table_kernel, geo=_na_geometry(rows)),
        grid=(depth, heads),
        in_specs=[pl.BlockSpec((1, 1, 16, LANES), lambda l, h: (l, h, 0, 0))],
        out_specs=pl.BlockSpec((1, 3, 1, tq, tk), lambda l, h: (l, 0, h, 0, 0)),
        out_shape=jax.ShapeDtypeStruct((depth, 3, heads, tq, tk), F32),
        compiler_params=_cparams(("parallel", "parallel")),
        name="natten_tables",
    )(rp)


def _natten_kernel(q_ref, k_ref, v_ref, t_ref, o_ref, *, heads, n_g, rows):
    g = pl.program_id(1)
    tq = q_ref.shape[1]
    tk = t_ref.shape[4]
    ks = jnp.clip(NA_GROUP_ROWS * g - NA_KH_MAX // 2, 0, rows - NA_KEY_ROWS)
    kstart = pl.multiple_of(ks * GRID_W, GRID_W)
    per_tile = LANES // NA_HEAD_DIM
    lane = lax.broadcasted_iota(I32, (1, LANES), 1)
    ones = jnp.ones((tk, LANES), BF16)
    for j in range(heads // per_tile):
        lanes = slice(j * LANES, (j + 1) * LANES)
        q2 = q_ref[0, :, lanes].astype(F32) * (NA_HEAD_DIM ** -0.5)
        k2 = k_ref[0, pl.ds(kstart, tk), lanes]
        vaug = jnp.concatenate([v_ref[0, pl.ds(kstart, tk), lanes], ones], axis=1)
        o2 = None
        for hh in range(per_tile):
            own = (lane >= hh * NA_HEAD_DIM) & (lane < (hh + 1) * NA_HEAD_DIM)
            qm = jnp.where(own, q2, 0.0).astype(BF16)
            s = lax.dot_general(qm, k2, (((1,), (1,)), ((), ())), preferred_element_type=F32)
            s = s + t_ref[0, 0, j * per_tile + hh]
            m = jnp.max(s, axis=-1, keepdims=True)
            p = jnp.exp((s - m).astype(BF16))
            r = jnp.dot(p, vaug, preferred_element_type=F32)
            o = r[:, :LANES] / r[:, LANES:]
            o2 = o if o2 is None else jnp.where(own, o, o2)
        o_ref[0, :, lanes] = o2.astype(o_ref.dtype)


def _natten(qkv, tables, layer, b, seq, naw):
    heads = naw // NA_HEAD_DIM
    rows = seq // GRID_W
    n_g = rows // NA_GROUP_ROWS
    tq = NA_GROUP_ROWS * GRID_W
    tk = NA_KEY_ROWS * GRID_W

    def tmap(i, g):
        return (layer, jnp.where(g == 0, 0, jnp.where(g == n_g - 1, 2, 1)), 0, 0, 0)

    return pl.pallas_call(
        functools.partial(_natten_kernel, heads=heads, n_g=n_g, rows=rows),
        grid=(b, n_g),
        in_specs=[pl.BlockSpec((1, tq, naw), lambda i, g: (i, g, 0)),
                  pl.BlockSpec((1, seq, naw), lambda i, g: (i, 0, 1)),
                  pl.BlockSpec((1, seq, naw), lambda i, g: (i, 0, 2)),
                  pl.BlockSpec((1, 1, heads, tq, tk), tmap)],
        out_specs=pl.BlockSpec((1, tq, naw), lambda i, g: (i, g, 0)),
        out_shape=jax.ShapeDtypeStruct((b, seq, naw), F32),
        compiler_params=_cparams(("parallel", "arbitrary")),
        name="natten",
    )(qkv, qkv, qkv, tables)


def _pool_kernel(u_ref, w_ref, sc_ref, o_ref, pad_ref, *, pw):
    seq = u_ref.shape[1]
    lp = seq + 2 * POOL_PAD
    j = pl.program_id(1)
    u = u_ref[0]
    zeros = jnp.zeros((POOL_PAD, LANES), F32)
    pad_ref[0:POOL_PAD, :] = zeros
    pad_ref[POOL_PAD + seq:lp, :] = zeros
    pad_ref[POOL_PAD:POOL_PAD + seq, :] = u
    xp = pad_ref[...]
    dn = lambda a, k: pltpu.roll(a, k, 0)
    up = lambda a, k: pltpu.roll(a, lp - k, 0)
    s2 = xp + dn(xp, 1)
    s4 = dn(s2, 1) + up(s2, 1)
    s8 = dn(s4, 2) + up(s4, 2)
    s16 = dn(s8, 4) + up(s8, 4)
    lane = lax.broadcasted_iota(I32, (1, LANES), 1) + j * LANES
    gdim = pw // len(POOL_WINDOWS)
    grp = lane // gdim
    sums = (s2, s4, s8, s16)
    wsum = sums[0][POOL_PAD:POOL_PAD + seq]
    half = jnp.full((1, LANES), POOL_WINDOWS[0] // 2, F32)
    for gi in range(1, len(POOL_WINDOWS)):
        wsum = jnp.where(grp == gi, sums[gi][POOL_PAD:POOL_PAD + seq], wsum)
        half = jnp.where(grp == gi, float(POOL_WINDOWS[gi] // 2), half)
    t = lax.broadcasted_iota(I32, (seq, LANES), 0).astype(F32)
    cnt = jnp.minimum(t + half, float(seq)) - jnp.maximum(t - half, 0.0)
    pooled = wsum / cnt - u
    y = jnp.dot(pooled.astype(BF16), w_ref[0], preferred_element_type=F32)
    o_ref[0] = y * sc_ref[...]


def _pool(u, pool_w, pool_scale):
    b, seq, pw = u.shape
    ng, gd, _ = pool_w.shape
    nh = pw // LANES
    per = LANES // gd
    wbd = jnp.zeros((nh, LANES, LANES), F32)
    for gi in range(ng):
        hh, k = divmod(gi, per)
        wbd = wbd.at[hh, k * gd:(k + 1) * gd, k * gd:(k + 1) * gd].set(pool_w[gi])
    return pl.pallas_call(
        functools.partial(_pool_kernel, pw=pw),
        grid=(b, nh),
        in_specs=[pl.BlockSpec((1, seq, LANES), lambda i, j: (i, 0, j)),
                  pl.BlockSpec((1, LANES, LANES), lambda i, j: (j, 0, 0)),
                  pl.BlockSpec((1, LANES), lambda i, j: (0, j))],
        out_specs=pl.BlockSpec((1, seq, LANES), lambda i, j: (i, 0, j)),
        out_shape=jax.ShapeDtypeStruct((b, seq, pw), F32),
        scratch_shapes=[pltpu.VMEM((seq + 2 * POOL_PAD, LANES), F32)],
        compiler_params=_cparams(("parallel", "parallel")),
        name="pool_mixer",
    )(u, wbd.astype(BF16), pool_scale[None])


def _outproj_kernel(x_ref, yh_ref, yn_ref, yp_ref, gm_ref, w_ref, g2_ref, wr_ref,
                    xo_ref, h_ref, aff_ref, *, hy, naw):
    gm = gm_ref[...]
    m1 = _rms(yh_ref[...], gm[:, :hy]).astype(BF16)
    m2 = _rms(yn_ref[...], gm[:, hy:hy + naw]).astype(BF16)
    m3 = _rms(yp_ref[...], gm[:, hy + naw:]).astype(BF16)
    acc = jnp.dot(m1, w_ref[:hy, :], preferred_element_type=F32)
    acc += jnp.dot(m2, w_ref[hy:hy + naw, :], preferred_element_type=F32)
    acc += jnp.dot(m3, w_ref[hy + naw:, :], preferred_element_type=F32)
    xn = x_ref[...] + acc
    xo_ref[...] = xn
    h = _rms(xn, g2_ref[...])
    h_ref[...] = h.reshape(h_ref.shape)
    logits = _dot_x3(wr_ref[0], wr_ref[1], h, (((1,), (1,)), ((), ())))
    mx = jnp.max(logits, axis=0, keepdims=True)
    ex = jnp.exp(logits - mx)
    aff_ref[...] = ex / jnp.sum(ex, axis=0, keepdims=True)


def _outproj(x2, yh, yn, yp, gm, w_bf, g2, wr_t, tm=1024):
    n, d = x2.shape
    hy, naw, pw = yh.shape[1], yn.shape[1], yp.shape[1]
    e = wr_t.shape[1]
    row = lambda c: pl.BlockSpec((tm, c), lambda i: (i, 0))
    full = lambda s: pl.BlockSpec(s, lambda i: (0, 0))
    return pl.pallas_call(
        functools.partial(_outproj_kernel, hy=hy, naw=naw),
        grid=(n // tm,),
        in_specs=[row(d), row(hy), row(naw), row(pw), full((1, d)), full((d, d)), full((1, d)),
                  pl.BlockSpec((2, e, d), lambda i: (0, 0, 0))],
        out_specs=[row(d), pl.BlockSpec((tm, d // LANES, LANES), lambda i: (i, 0, 0)),
                   pl.BlockSpec((e, tm), lambda i: (0, i))],
        out_shape=[jax.ShapeDtypeStruct((n, d), F32), jax.ShapeDtypeStruct((n, d // LANES, LANES), F32),
                   jax.ShapeDtypeStruct((e, n), F32)],
        compiler_params=_cparams(("parallel",)),
        name="outproj_router",
    )(x2, yh, yn, yp, gm, w_bf, g2, wr_t)


def _block_cumsum(x, tri):
    r, n = x.shape
    cls, offs = [], []
    off = jnp.zeros((r, 1), F32)
    for j in range(n // LANES):
        c = jnp.dot(x[:, j * LANES:(j + 1) * LANES], tri, preferred_element_type=F32)
        cls.append(c)
        off = off + c[:, LANES - 1:LANES]
        offs.append(off)
    return cls, offs


def _route_kernel(aff_ref, tri_ref, bci_ref, bcx_ref, idx_ref, gate_ref, blk_ref, *, cap):
    aff = aff_ref[...]
    e, seq = aff.shape
    nblk = seq // LANES
    bits = pltpu.bitcast(aff, I32)
    capf = jnp.float32(cap)

    def radix(i, prefix):
        cand = prefix | jnp.left_shift(jnp.int32(1), 30 - i)
        cnt = jnp.sum((bits >= cand).astype(F32), axis=1, keepdims=True)
        return jnp.where(cnt >= capf, cand, prefix)

    tau = lax.fori_loop(0, 31, radix, jnp.zeros((e, 1), I32))
    gt = bits > tau
    eq = bits == tau
    need = capf - jnp.sum(gt.astype(F32), axis=1, keepdims=True)
    tri = tri_ref[...]
    cls, offs = _block_cumsum(jnp.where(eq, 1.0, 0.0).astype(BF16), tri)
    tie_rank = jnp.concatenate([c if j == 0 else c + offs[j - 1] for j, c in enumerate(cls)], axis=1)
    sel = gt | (eq & (tie_rank <= need))
    self32 = jnp.where(sel, 1.0, 0.0)
    selb = self32.astype(BF16)
    cls, _ = _block_cumsum(selb, tri)
    bend = jnp.dot(selb, bci_ref[...], preferred_element_type=F32)
    bstart = jnp.dot(selb, bcx_ref[...], preferred_element_type=F32)
    for j in range(nblk):
        rows = slice(j * e, (j + 1) * e)
        lanes = slice(j * LANES, (j + 1) * LANES)
        blk_ref[0, rows, :] = cls[j]
        blk_ref[1, rows, :] = self32[:, lanes]
        blk_ref[2, rows, :] = aff[:, lanes]
    slot = lax.broadcasted_iota(I32, (cap, 1), 0).astype(F32)
    lane = lax.broadcasted_iota(I32, (1, LANES), 1).astype(F32)
    for ei in range(e):
        bs, be = bstart[ei:ei + 1, :], bend[ei:ei + 1, :]
        inblk = (bs <= slot) & (slot < be)
        local = slot - jnp.sum(jnp.where(inblk, bs, 0.0), axis=1, keepdims=True)
        jcol = jnp.sum(jnp.where(inblk, lane, 0.0), axis=1, keepdims=True)
        pick = jnp.where(inblk, 1.0, 0.0)[:, :nblk].astype(BF16)
        rows = pl.ds(ei, nblk, stride=e)
        a = blk_ref[2, rows, :]
        a_hi = a.astype(BF16)
        r1 = a - a_hi.astype(F32)
        a_mid = r1.astype(BF16)
        a_lo = (r1 - a_mid.astype(F32)).astype(BF16)
        take = lambda v: jnp.dot(pick, v, preferred_element_type=F32)
        g_cl = take(blk_ref[0, rows, :].astype(BF16))
        g_sel = take(blk_ref[1, rows, :].astype(BF16))
        g_aff = take(a_hi) + take(a_mid) + take(a_lo)
        hit = (g_cl == local + 1.0) & (g_sel > 0.5)
        idx = jcol * LANES + jnp.sum(jnp.where(hit, lane, 0.0), axis=1, keepdims=True)
        gate = jnp.sum(jnp.where(hit, g_aff, 0.0), axis=1, keepdims=True)
        idx_ref[0, :, ei:ei + 1] = idx.astype(I32)
        gate_ref[0, ei] = jnp.broadcast_to(gate, (cap, LANES))


def _route(aff_t, b, seq, cap):
    e = aff_t.shape[0]
    nblk = seq // LANES
    tri = jnp.asarray(np.triu(np.ones((LANES, LANES), np.float32)), BF16)
    tblk = np.arange(seq)[:, None] // LANES
    bci = jnp.asarray(tblk <= np.arange(LANES)[None, :], BF16)
    bcx = jnp.asarray(tblk < np.arange(LANES)[None, :], BF16)
    full = lambda s: pl.BlockSpec(s, lambda i: (0, 0))
    idx, gate = pl.pallas_call(
        functools.partial(_route_kernel, cap=cap),
        grid=(b,),
        in_specs=[pl.BlockSpec((e, seq), lambda i: (0, i)), full((LANES, LANES)),
                  full((seq, LANES)), full((seq, LANES))],
        out_specs=[pl.BlockSpec((1, cap, e), lambda i: (i, 0, 0)),
                   pl.BlockSpec((1, e, cap, LANES), lambda i: (i, 0, 0, 0))],
        out_shape=[jax.ShapeDtypeStruct((b, cap, e), I32), jax.ShapeDtypeStruct((b, e, cap, LANES), F32)],
        scratch_shapes=[pltpu.VMEM((3, nblk * e, LANES), F32)],
        compiler_params=_cparams(("parallel",)),
        name="ec_route",
    )(aff_t, tri, bci, bcx)
    return idx.transpose(0, 2, 1), gate


def _expert_kernel(rows_ref, h_hbm, wg_ref, wu_ref, wd_ref, y_ref, xbuf, xb, acc, sem, *, tm, nm, nf, ne):
    e = pl.program_id(0)
    m = pl.program_id(1)
    f = pl.program_id(2)
    tile = e * nm + m
    last = ne * nm - 1
    chunk = tm // nf
    d = xb.shape[2]
    rb = min(EXPERT_RELAYOUT_ROWS, chunk)

    def row_copy(src_row, p, c, dst_row):
        return pltpu.make_async_copy(h_hbm.at[pl.ds(src_row, 1)], xbuf.at[p, c, pl.ds(dst_row, 1)], sem.at[p, c])

    def chunk_wait(t, c):
        pltpu.make_async_copy(h_hbm.at[pl.ds(0, chunk)], xbuf.at[t % 2, c], sem.at[t % 2, c]).wait()

    def chunk_convert(t, c):
        for j in range(chunk // rb):
            rows = xbuf[t % 2, c, j * rb:(j + 1) * rb]
            xb[t % 2, pl.ds(c * chunk + j * rb, rb), :] = rows.reshape(rb, d).astype(BF16)

    @pl.when((tile == 0) & (f == 0))
    def _():
        nxt0 = min(1, last)
        for t, c in [(0, k) for k in range(nf)] + [(nxt0, 0)]:
            def issue(i, carry, t=t, c=c):
                row_copy(rows_ref[t * tm + c * chunk + i], t % 2, c, i).start()
                return carry

            lax.fori_loop(0, chunk, issue, 0, unroll=8)
        for k in range(nf):
            chunk_wait(0, k)
            chunk_convert(0, k)

    @pl.when(f == 0)
    def _():
        acc[...] = jnp.zeros(acc.shape, F32)

    nxt = jnp.minimum(tile + 1, last)
    wrap = f + 1 == nf
    t_issue = jnp.where(wrap, jnp.minimum(tile + 2, last), nxt)
    c_issue = jnp.where(wrap, 0, f + 1)
    base = t_issue * tm + c_issue * chunk
    for i in range(chunk):
        row_copy(rows_ref[base + i], t_issue % 2, c_issue, i).start()

    x = xb[tile % 2]
    a = jnp.dot(x, wg_ref[0, 0].astype(BF16), preferred_element_type=F32)
    u = jnp.dot(x, wu_ref[0, 0].astype(BF16), preferred_element_type=F32)
    hh = (a * jax.nn.sigmoid(a) * u).astype(BF16)
    acc[...] += jnp.dot(hh, wd_ref[0, 0].astype(BF16), preferred_element_type=F32)

    chunk_wait(nxt, f)
    chunk_convert(nxt, f)

    @pl.when(f == nf - 1)
    def _():
        y_ref[0] = acc[...].astype(y_ref.dtype)

    @pl.when((tile == last) & (f == nf - 1))
    def _():
        chunk_wait(last, 0)


EXPERT_RELAYOUT_ROWS = 64


def _experts(rows_flat, h3, w_gate, w_up, w_down, layer, mtot, tm=1024, tf=512):
    _, e, d, ff = w_gate.shape
    tm = min(tm, mtot)
    tf = min(tf, ff)
    nm, nf = mtot // tm, ff // tf
    grid_spec = pltpu.PrefetchScalarGridSpec(
        num_scalar_prefetch=1,
        grid=(e, nm, nf),
        in_specs=[pl.BlockSpec(memory_space=pl.ANY),
                  pl.BlockSpec((1, 1, d, tf), lambda i, m, f, r: (layer, i, 0, f)),
                  pl.BlockSpec((1, 1, d, tf), lambda i, m, f, r: (layer, i, 0, f)),
                  pl.BlockSpec((1, 1, tf, d), lambda i, m, f, r: (layer, i, f, 0))],
        out_specs=pl.BlockSpec((1, tm, d), lambda i, m, f, r: (i, m, 0)),
        scratch_shapes=[pltpu.VMEM((2, nf, tm // nf, d // LANES, LANES), F32), pltpu.VMEM((2, tm, d), BF16),
                        pltpu.VMEM((tm, d), F32),
                        pltpu.SemaphoreType.DMA((2, nf))],
    )
    return pl.pallas_call(
        functools.partial(_expert_kernel, tm=tm, nm=nm, nf=nf, ne=e),
        grid_spec=grid_spec,
        out_shape=jax.ShapeDtypeStruct((e, mtot, d), BF16),
        compiler_params=_cparams(("arbitrary", "arbitrary", "arbitrary")),
        name="ec_experts",
    )(rows_flat, h3, w_gate, w_up, w_down)


COMBINE_ROWS = 64
COMBINE_UNROLL = 8
COMBINE_EXPERTS = 4


def _combine_kernel(idx_ref, split_ref, x_ref, y_ref, gl, *rest, ne, cap, span, final):
    if final:
        g_ref, o_ref, acc3, y3 = rest
    else:
        o_ref, acc3, y3 = rest
    b = pl.program_id(0)
    sp = pl.program_id(1)
    eg = pl.program_id(2)
    d = x_ref.shape[2]
    sub = d // LANES
    rb = COMBINE_ROWS
    nu = COMBINE_UNROLL

    @pl.when(eg == 0)
    def _():
        def load(c, carry):
            r0 = pl.multiple_of(c * rb, rb)
            acc3[pl.ds(pl.multiple_of(r0 * sub, rb * sub), rb * sub), :] = (
                x_ref[0, pl.ds(r0, rb), :].reshape(rb * sub, LANES))
            return carry

        lax.fori_loop(0, span // rb, load, 0)

    def one_expert(k):
        e = eg * COMBINE_EXPERTS + k
        lo = split_ref[(b * ne + e) * 3 + sp]
        hi = split_ref[(b * ne + e) * 3 + sp + 1]

        def relayout(c, carry):
            r0 = pl.multiple_of(c * rb, rb)
            y3[pl.ds(r0, rb)] = y_ref[k, pl.ds(r0, rb), :].astype(F32).reshape(rb, sub, LANES)
            return carry

        lax.fori_loop(lo // rb, (hi + rb - 1) // rb, relayout, 0)
        base = (b * ne + e) * cap

        def tokens(first, count):
            return tuple(idx_ref[base + first + u] for u in range(count))

        def add_rows(first, toks):
            rows = [pl.ds(pl.multiple_of(t, sub), sub) for t in toks]
            vals = [acc3[r, :] + gl[0, k, pl.ds(first + u, 1), :] * y3[first + u] for u, r in enumerate(rows)]
            for r, v in zip(rows, vals):
                acc3[r, :] = v

        def group(g, toks):
            first = lo + g * nu
            nxt = tokens(jnp.minimum(first + nu, cap - nu), nu)
            add_rows(first, toks)
            return nxt

        ngroups = (hi - lo) // nu
        lax.fori_loop(0, ngroups, group, tokens(jnp.minimum(lo, cap - nu), nu))

        def tail(i, carry):
            add_rows(i, tokens(i, 1))
            return carry

        lax.fori_loop(lo + ngroups * nu, hi, tail, 0)

    for k in range(COMBINE_EXPERTS):
        one_expert(k)

    @pl.when(eg == ne // COMBINE_EXPERTS - 1)
    def _():
        def store(c, carry):
            r0 = pl.multiple_of(c * rb, rb)
            v = acc3[pl.ds(pl.multiple_of(r0 * sub, rb * sub), rb * sub), :].reshape(rb, d)
            if final:
                v = _rms(v, g_ref[...])
            o_ref[0, pl.ds(r0, rb), :] = v
            return carry

        lax.fori_loop(0, span // rb, store, 0)


def _combine(idx, gate, x3, y, cap, final_g=None):
    b, seq, d = x3.shape
    ne = y.shape[0]
    span = seq // 2
    n_lower = jnp.sum((idx < span).astype(I32), axis=-1)
    split_flat = jnp.stack([jnp.zeros_like(n_lower), n_lower, jnp.full_like(n_lower, cap)], axis=-1).reshape(-1)
    final = final_g is not None
    ge = COMBINE_EXPERTS
    in_specs = [pl.BlockSpec((1, span, d), lambda i, s, e, *_: (i, s, 0)),
                pl.BlockSpec((ge, cap, d), lambda i, s, e, *_: (e, i, 0)),
                pl.BlockSpec((1, ge, cap, LANES), lambda i, s, e, *_: (i, e, 0, 0))]
    args = [x3, y, gate]
    if final:
        in_specs.append(pl.BlockSpec((1, d), lambda i, s, e, *_: (0, 0)))
        args.append(final_g)
    grid_spec = pltpu.PrefetchScalarGridSpec(
        num_scalar_prefetch=2,
        grid=(b, 2, ne // ge),
        in_specs=in_specs,
        out_specs=pl.BlockSpec((1, span, d), lambda i, s, e, *_: (i, s, 0)),
        scratch_shapes=[pltpu.VMEM((span * (d // LANES), LANES), F32), pltpu.VMEM((cap, d // LANES, LANES), F32)],
    )
    return pl.pallas_call(
        functools.partial(_combine_kernel, ne=ne, cap=cap, span=span, final=final),
        grid_spec=grid_spec,
        out_shape=jax.ShapeDtypeStruct((b, seq, d), F32),
        compiler_params=_cparams(("arbitrary", "arbitrary", "arbitrary")),
        name="ec_combine",
    )(((idx % span) * (d // LANES)).reshape(-1), split_flat, *args)


def _moe(x3, h2, aff_t, w_gate, w_up, w_down, layer, final_g=None):
    b, seq, d = x3.shape
    ne = w_gate.shape[1]
    cap = EC_CAPACITY * seq // ne
    idx, gate = _route(aff_t, b, seq, cap)
    rows = idx + (jnp.arange(b, dtype=I32) * seq)[:, None, None]
    rows_flat = rows.transpose(1, 0, 2).reshape(-1)
    y = _experts(rows_flat, h2, w_gate, w_up, w_down, layer, b * cap)
    return _combine(idx, gate, x3, y, cap, final_g)


def kernel(x, norm1_g, w_in, hy_short_w, hy_short_b, hy_f_w1, hy_f_b1, hy_f_w2, hy_f_b2, hy_f_wout, hy_f_freq, hy_skip, na_rpb, pool_w, pool_scale, mix_norm_g, w_out, norm2_g, w_router, w_gate, w_up, w_down, final_g):
    b, seq, d = x.shape
    depth = w_in.shape[0]
    hy = hy_skip.shape[1]
    pw = pool_scale.shape[1]
    naw = d - hy - pw
    n = b * seq
    rows = seq // GRID_W
    tabs = _dft_tables(seq // HY_B1)
    x2 = x.reshape(n, d)
    hhat = _hyena_filter_spectra(seq, hy, hy_f_w1, hy_f_b1, hy_f_w2, hy_f_b2, hy_f_wout, hy_f_freq, tabs)
    na_tables = _na_tables(na_rpb, rows)
    for i in range(depth):
        z, x0, qkv, pool_in = _inproj(x2, norm1_g[i][None], w_in[i].astype(BF16), hy_short_w[i],
                                      hy_short_b[i][None], 3 * hy, 3 * naw, seq)
        y_hy = _hyena(z.reshape(b, seq, hy), x0.reshape(b, seq, hy), hy_skip[i], hhat, i, tabs)
        y_na = _natten(qkv.reshape(b, seq, 3 * naw), na_tables, i, b, seq, naw)
        y_pool = _pool(pool_in.reshape(b, seq, pw), pool_w[i], pool_scale[i])
        x2, h2, aff_t = _outproj(x2, y_hy.reshape(n, hy), y_na.reshape(n, naw), y_pool.reshape(n, pw),
                                 mix_norm_g[i][None], w_out[i].astype(BF16), norm2_g[i][None],
                                 jnp.stack(_split_bf16(w_router[i].T)))
        last = final_g[None] if i == depth - 1 else None
        x2 = _moe(x2.reshape(b, seq, d), h2, aff_t, w_gate, w_up, w_down, i, last).reshape(n, d)
    return x2.reshape(b, seq, d)
```

```python
import functools
import math

import numpy as np
import jax
import jax.numpy as jnp
from jax import lax
from jax.experimental import pallas as pl
from jax.experimental.pallas import tpu as pltpu

F32 = jnp.float32
BF16 = jnp.bfloat16
I32 = jnp.int32
EPS = 1e-6
HIGHEST = lax.Precision.HIGHEST

GRID_W = 64
NA_HEAD_DIM = 64
NA_KH_MAX = 8
NA_KW = 16
NA_GROUP_ROWS = 4
NA_KEY_ROWS = 12
NA_GROUPS_PER_STEP = 2
POOL_WINDOWS = (2, 4, 8, 16)
POOL_PAD = 16
FILTER_EMB = 33
DECAY_FAST, DECAY_SLOW, DECAY_TARGET = 0.3, 1.5, 1e-2
EC_CAPACITY = 2
HY_B1 = 128
NEG = -1e30
LANES = 128
V7X_VMEM_BYTES = 64 * 1024 * 1024
VMEM_LIMIT = V7X_VMEM_BYTES - 8 * 1024 * 1024


def _cparams(sem, vmem=VMEM_LIMIT):
    return pltpu.CompilerParams(dimension_semantics=sem, vmem_limit_bytes=vmem)


def _rms(v, g):
    return v * lax.rsqrt(jnp.mean(v * v, axis=-1, keepdims=True) + EPS) * g


def _split_bf16(v):
    hi = v.astype(BF16)
    return hi, (v - hi.astype(F32)).astype(BF16)


def _dot_x3(a_hi, a_lo, x, dims=(((1,), (0,)), ((), ()))):
    x_hi, x_lo = _split_bf16(x)
    dg = lambda p, q: lax.dot_general(p, q, dims, preferred_element_type=F32)
    return dg(a_hi, x_hi) + dg(a_lo, x_hi) + dg(a_hi, x_lo)


HALO = 16


def _inproj_kernel(x_ref, xp_ref, xn_ref, g_ref, w_ref, sw_ref, sb_ref, z_ref, x0_ref, qkv_ref, pool_ref,
                   *, hyw, naw, seq):
    g = g_ref[...]
    tm = x_ref.shape[0]
    h = _rms(x_ref[...], g).astype(BF16)
    hall = jnp.concatenate([_rms(xp_ref[...], g).astype(BF16), h, _rms(xn_ref[...], g).astype(BF16)], axis=0)
    u = jnp.dot(hall, w_ref[:, :hyw], preferred_element_type=F32)
    t = lax.rem(pl.program_id(0) * tm + lax.broadcasted_iota(I32, (tm, 1), 0), seq)
    prev = jnp.where(t == 0, 0.0, u[HALO - 1:HALO - 1 + tm])
    nxt = jnp.where(t == seq - 1, 0.0, u[HALO + 1:HALO + 1 + tm])
    conv = prev * sw_ref[0:1, :] + u[HALO:HALO + tm] * sw_ref[1:2, :] + nxt * sw_ref[2:3, :] + sb_ref[...]
    hy = hyw // 3
    x0_ref[...] = conv[:, :hy]
    z_ref[...] = conv[:, 2 * hy:] * conv[:, hy:2 * hy]
    qkv_ref[...] = jnp.dot(h, w_ref[:, hyw:hyw + naw], preferred_element_type=F32).astype(BF16)
    pool_ref[...] = jnp.dot(h, w_ref[:, hyw + naw:], preferred_element_type=F32)


def _inproj(x2, g, w_bf, sw, sb, hyw, naw, seq, tm=1024):
    n, d = x2.shape
    inw = w_bf.shape[1]
    pw = inw - hyw - naw
    hy = hyw // 3
    per = tm // HALO
    last = n // HALO - 1
    return pl.pallas_call(
        functools.partial(_inproj_kernel, hyw=hyw, naw=naw, seq=seq),
        grid=(n // tm,),
        in_specs=[pl.BlockSpec((tm, d), lambda i: (i, 0)),
                  pl.BlockSpec((HALO, d), lambda i: (jnp.maximum(i * per - 1, 0), 0)),
                  pl.BlockSpec((HALO, d), lambda i: (jnp.minimum((i + 1) * per, last), 0)),
                  pl.BlockSpec((1, d), lambda i: (0, 0)),
                  pl.BlockSpec((d, inw), lambda i: (0, 0)),
                  pl.BlockSpec((3, hyw), lambda i: (0, 0)),
                  pl.BlockSpec((1, hyw), lambda i: (0, 0))],
        out_specs=[pl.BlockSpec((tm, hy), lambda i: (i, 0)),
                   pl.BlockSpec((tm, hy), lambda i: (i, 0)),
                   pl.BlockSpec((tm, naw), lambda i: (i, 0)),
                   pl.BlockSpec((tm, pw), lambda i: (i, 0))],
        out_shape=[jax.ShapeDtypeStruct((n, hy), F32),
                   jax.ShapeDtypeStruct((n, hy), F32),
                   jax.ShapeDtypeStruct((n, naw), BF16),
                   jax.ShapeDtypeStruct((n, pw), F32)],
        compiler_params=_cparams(("parallel",)),
        name="inproj",
    )(x2, x2, x2, g, w_bf, sw, sb)


def _filter_kernel(z_ref, w1_ref, b1_ref, w2_ref, b2_ref, wo_ref, fr_ref, dec_ref, o_ref):
    fr = fr_ref[0]
    h = jnp.sin(fr * (jnp.dot(z_ref[...], w1_ref[0], preferred_element_type=F32, precision=HIGHEST) + b1_ref[0]))
    h = jnp.sin(fr * (jnp.dot(h, w2_ref[0], preferred_element_type=F32, precision=HIGHEST) + b2_ref[0]))
    hw = jnp.dot(h, wo_ref[0], preferred_element_type=F32, precision=HIGHEST)
    c = dec_ref.shape[2]
    o_ref[0, 0] = hw[:, :c] * dec_ref[0]
    o_ref[0, 1] = hw[:, c:] * dec_ref[1]


def _filter_mlp(zemb, w1, b1, w2, b2, wo, fr, dec, tl=512):
    seq, emb = zemb.shape
    depth, _, hid = w1.shape
    ow = wo.shape[2]
    c = ow // 2
    lay = lambda s: pl.BlockSpec((1,) + s, lambda l, i: (l, 0, 0))
    return pl.pallas_call(
        _filter_kernel,
        grid=(depth, seq // tl),
        in_specs=[pl.BlockSpec((tl, emb), lambda l, i: (i, 0)), lay((emb, hid)), lay((1, hid)),
                  lay((hid, hid)), lay((1, hid)), lay((hid, ow)), lay((1, hid)),
                  pl.BlockSpec((2, tl, c), lambda l, i: (0, i, 0))],
        out_specs=pl.BlockSpec((1, 2, tl, c), lambda l, i: (l, 0, i, 0)),
        out_shape=jax.ShapeDtypeStruct((depth, 2, seq, c), F32),
        compiler_params=_cparams(("parallel", "parallel")),
        name="hyena_filter_mlp",
    )(zemb, w1, b1, w2, b2, wo, fr, dec)


def _filter_bdft_kernel(ff_ref, fc_ref, yf_ref, yb_ref, o_ref):
    o_ref[0, 0] = (_dot_x3(ff_ref[0], ff_ref[1], yf_ref[0, 0, 0]) + _dot_x3(fc_ref[0], fc_ref[1], yb_ref[0, 0, 0]))


def _filter_bdft(ff, fc, ya5, nka):
    depth, _, kap, r, c = ya5.shape
    blk = lambda p: pl.BlockSpec((1, 1, 1, r, c), lambda l, k, p=p: (l, p, k, 0, 0))
    return pl.pallas_call(
        _filter_bdft_kernel,
        grid=(depth, nka),
        in_specs=[pl.BlockSpec((2, r, r), lambda l, k: (0, 0, 0)), pl.BlockSpec((2, r, r), lambda l, k: (0, 0, 0)),
                  blk(0), blk(1)],
        out_specs=pl.BlockSpec((1, 1, r, c), lambda l, k: (l, k, 0, 0)),
        out_shape=jax.ShapeDtypeStruct((depth, nka, r, c), F32),
        compiler_params=_cparams(("parallel", "parallel")),
        name="hyena_filter_bdft",
    )(ff, fc, ya5, ya5)


def _hyadft_kernel(fa_ref, z_ref, y_ref, zt_ref, yt_ref):
    pieces, k2, a1 = fa_ref.shape
    zt_ref[...] = pltpu.einshape("abl->bal", z_ref[0].reshape(a1, HY_B1, LANES))

    def body(i, c):
        b0 = 2 * i
        zz = jnp.concatenate([zt_ref[b0], zt_ref[b0 + 1]], axis=1)
        if pieces == 1:
            r = jnp.dot(fa_ref[0], zz.astype(BF16), preferred_element_type=F32)
        else:
            r = _dot_x3(fa_ref[0], fa_ref[1], zz)
        yt_ref[b0] = r[:, :LANES]
        yt_ref[b0 + 1] = r[:, LANES:]
        return c

    lax.fori_loop(0, HY_B1 // 2, body, 0, unroll=8)
    y_ref[0] = pltpu.einshape("bkl->kbl", yt_ref[...]).reshape(k2 * HY_B1, LANES).astype(y_ref.dtype)


def _hyadft(fa, z, out_dtype=F32, name="hyena_adft"):
    b, seq, hy = z.shape
    pieces, k2, a1 = fa.shape
    return pl.pallas_call(
        _hyadft_kernel,
        grid=(b, hy // LANES),
        in_specs=[pl.BlockSpec((pieces, k2, a1), lambda i, j: (0, 0, 0)),
                  pl.BlockSpec((1, seq, LANES), lambda i, j: (i, 0, j))],
        out_specs=pl.BlockSpec((1, k2 * HY_B1, LANES), lambda i, j: (i, 0, j)),
        out_shape=jax.ShapeDtypeStruct((b, k2 * HY_B1, hy), out_dtype),
        scratch_shapes=[pltpu.VMEM((HY_B1, a1, LANES), F32), pltpu.VMEM((HY_B1, k2, LANES), F32)],
        compiler_params=_cparams(("parallel", "parallel")),
        name=name,
    )(fa, z)


def _hyfreq_kernel(y_ref, fb_ref, h_ref, fbi_ref, w_ref, *, nka):
    ka = pl.program_id(0)
    nb = fb_ref.shape[0] // 2
    nbatch = y_ref.shape[0]

    @pl.when(ka < nka)
    def _():
        hr, hi = h_ref[0, 0, :nb], h_ref[0, 0, nb:]
        for bi in range(nbatch):
            p = jnp.dot(fb_ref[...], y_ref[bi, 0].astype(BF16), preferred_element_type=F32)
            pr, pi = p[:nb], p[nb:]
            q = jnp.concatenate([pr * hr - pi * hi, pr * hi + pi * hr], axis=0).astype(BF16)
            w_ref[bi, 0] = jnp.dot(fbi_ref[...], q, preferred_element_type=F32).astype(w_ref.dtype)

    @pl.when(ka >= nka)
    def _():
        w_ref[...] = jnp.zeros(w_ref.shape, w_ref.dtype)


def _hyfreq(y4, fb, hhat, fbi, nka, layer):
    b, kap, r2, hy = y4.shape
    nb2 = fb.shape[0]
    return pl.pallas_call(
        functools.partial(_hyfreq_kernel, nka=nka),
        grid=(kap,),
        in_specs=[pl.BlockSpec((b, 1, r2, hy), lambda k: (0, k, 0, 0)),
                  pl.BlockSpec((nb2, r2), lambda k: (0, 0)),
                  pl.BlockSpec((1, 1, nb2, hy), lambda k: (layer, jnp.minimum(k, nka - 1), 0, 0)),
                  pl.BlockSpec((r2, nb2), lambda k: (0, 0))],
        out_specs=pl.BlockSpec((b, 1, r2, hy), lambda k: (0, k, 0, 0)),
        out_shape=jax.ShapeDtypeStruct((b, kap, r2, hy), BF16),
        compiler_params=_cparams(("parallel",)),
        name="hyena_freq",
    )(y4, fb, hhat, fbi)


def _hyout_kernel(a_ref, w_ref, z_ref, x0_ref, sk_ref, o_ref, wt_ref, zt_ref, xt_ref, ot_ref):
    a1, k2 = a_ref.shape
    ainv = a_ref[...]
    skip = sk_ref[...]
    wt_ref[...] = pltpu.einshape("kbl->bkl", w_ref[0].astype(F32).reshape(k2, HY_B1, LANES))
    zt_ref[...] = pltpu.einshape("abl->bal", z_ref[0].reshape(a1, HY_B1, LANES))
    xt_ref[...] = pltpu.einshape("abl->bal", x0_ref[0].reshape(a1, HY_B1, LANES))

    def body(i, c):
        b0 = 2 * i
        ww = jnp.concatenate([wt_ref[b0], wt_ref[b0 + 1]], axis=1).astype(BF16)
        y = jnp.dot(ainv, ww, preferred_element_type=F32)
        ot_ref[b0] = (y[:, :LANES] + zt_ref[b0] * skip) * xt_ref[b0]
        ot_ref[b0 + 1] = (y[:, LANES:] + zt_ref[b0 + 1] * skip) * xt_ref[b0 + 1]
        return c

    lax.fori_loop(0, HY_B1 // 2, body, 0, unroll=8)
    o_ref[0] = pltpu.einshape("bal->abl", ot_ref[...]).reshape(a1 * HY_B1, LANES)


def _hyout(ainv, w3, z, x0, skip):
    b, seq, hy = z.shape
    a1, k2 = ainv.shape
    blk = pl.BlockSpec((1, seq, LANES), lambda i, j: (i, 0, j))
    return pl.pallas_call(
        _hyout_kernel,
        grid=(b, hy // LANES),
        in_specs=[pl.BlockSpec((a1, k2), lambda i, j: (0, 0)),
                  pl.BlockSpec((1, k2 * HY_B1, LANES), lambda i, j: (i, 0, j)),
                  blk, blk,
                  pl.BlockSpec((1, LANES), lambda i, j: (0, j))],
        out_specs=blk,
        out_shape=jax.ShapeDtypeStruct((b, seq, hy), F32),
        scratch_shapes=[pltpu.VMEM((HY_B1, k2, LANES), F32)] + [pltpu.VMEM((HY_B1, a1, LANES), F32)] * 3,
        compiler_params=_cparams(("parallel", "parallel")),
        name="hyena_out",
    )(ainv, w3, z, x0, skip)


def _dft_tables(a1):
    a2, b2, b1 = 2 * a1, 2 * HY_B1, HY_B1
    nka = a1 + 1
    kap = -(-nka // 8) * 8
    ka = np.arange(nka)[:, None]
    def fa(na):
        ph = 2 * np.pi * ((ka * np.arange(na)[None, :]) % a2) / a2
        m = np.zeros((2 * kap, na))
        m[0:2 * nka:2] = np.cos(ph)
        m[1:2 * nka:2] = -np.sin(ph)
        return m
    kb = np.arange(b2)[:, None]
    th = 2 * np.pi * ((kb * np.arange(b2)[None, :]) % b2) / b2
    c, s = np.cos(th), np.sin(th)
    fb_full = np.block([[c, s], [-s, c]])
    fb_half = np.block([[c[:, :b1], s[:, :b1]], [-s[:, :b1], c[:, :b1]]])
    ct, st = c.T[:b1], s.T[:b1]
    fbi = np.block([[ct, -st], [st, ct]]) / b2
    ph = 2 * np.pi * ((np.arange(a1)[:, None] * np.arange(nka)[None, :]) % a2) / a2
    wgt = np.where((np.arange(nka) == 0) | (np.arange(nka) == a1), 1.0, 2.0)[None, :] / a2
    ainv = np.zeros((a1, 2 * kap))
    ainv[:, 0:2 * nka:2] = wgt * np.cos(ph)
    ainv[:, 1:2 * nka:2] = -wgt * np.sin(ph)
    fa2 = fa(a2)
    fa_filt2 = np.stack([fa2[:, 0:a1], fa2[:, 1:a1 + 1]], axis=1).reshape(4 * kap, a1)
    fb_conj = np.concatenate([fb_full[:b2], -fb_full[b2:]], axis=0)
    f32 = lambda v: np.asarray(v, np.float32)
    return dict(nka=nka, kap=kap, fa_data=f32(fa(a1)), fa_filt2=f32(fa_filt2), fb_full=f32(fb_full),
                fb_conj=f32(fb_conj), fb_half=f32(fb_half), fbi=f32(fbi), ainv=f32(ainv))


def _hyena_filter_spectra(seq, hy, w1, b1, w2, b2, wo, fr, tabs):
    nbands = (FILTER_EMB - 1) // 2
    t = jnp.linspace(0.0, 1.0, seq, dtype=F32)[:, None]
    ang = 2.0 * math.pi * jnp.arange(seq, dtype=F32)[:, None] / seq
    f = jnp.linspace(1e-4, nbands - 1, nbands, dtype=F32)[None, :]
    zemb = jnp.concatenate([t, jnp.cos(f * ang), -jnp.sin(f * ang)], axis=-1)
    deltas = jnp.abs(jnp.linspace(math.log(DECAY_TARGET) / DECAY_FAST,
                                  math.log(DECAY_TARGET) / DECAY_SLOW, hy, dtype=F32))
    decay = jnp.exp(-t * deltas)
    dec = jnp.stack([decay, decay * (jnp.arange(seq) > 0)[:, None].astype(F32)])
    depth = w1.shape[0]
    h = _filter_mlp(zemb, w1, b1[:, None], w2, b2[:, None], wo, fr[:, None], dec, tl=min(512, seq))
    pieces = lambda m: jnp.stack(_split_bf16(jnp.asarray(m)))
    ya = _hyadft(pieces(tabs["fa_filt2"]), h.reshape(depth * 2, seq, hy), name="hyena_filter_adft")
    ya5 = ya.reshape(depth, 2, tabs["kap"], 4 * HY_B1, hy)
    return _filter_bdft(pieces(tabs["fb_full"]), pieces(tabs["fb_conj"]), ya5, tabs["nka"])


def _hyena(z, x0, skip, hhat, layer, tabs):
    b, seq, hy = z.shape
    kap, nka = tabs["kap"], tabs["nka"]
    ya = _hyadft(jnp.asarray(tabs["fa_data"], BF16)[None], z, out_dtype=BF16)
    y4 = ya.reshape(b, kap, 2 * HY_B1, hy)
    w4 = _hyfreq(y4, jnp.asarray(tabs["fb_half"], BF16), hhat, jnp.asarray(tabs["fbi"], BF16), nka, layer)
    w3 = w4.reshape(b, 2 * kap * HY_B1, hy)
    return _hyout(jnp.asarray(tabs["ainv"], BF16), w3, z, x0, skip[None])


def _na_geometry(rows):
    gr, kr_n, kh = NA_GROUP_ROWS, NA_KEY_ROWS, NA_KH_MAX
    n_g = rows // gr
    geo = []
    for g in (0, 1, n_g - 1):
        ks = min(max(gr * g - kh // 2, 0), rows - kr_n)
        per_q = []
        for qr in range(gr):
            r = gr * g + qr
            rs = min(max(r - kh // 2, 0), rows - kh)
            per_q.append([((rs <= ks + k < rs + kh), ks + k - r + NA_KH_MAX - 1) for k in range(kr_n)])
        geo.append(per_q)
    return geo


def _na_table_kernel(r_ref, t_ref, *, geo):
    w, kw = GRID_W, NA_KW
    qc = lax.broadcasted_iota(I32, (w, 1), 0)
    lane = lax.broadcasted_iota(I32, (1, LANES), 1)
    kc = lane % w
    cs = jnp.clip(qc - kw // 2, 0, w - kw)
    colvalid = (kc >= cs) & (kc < cs + kw)
    left = lane < w
    neg = jnp.full((w, LANES), NEG, F32)
    shift = LANES - (kw - 1)

    cache = {}

    def toeplitz(dr, lane_off):
        if (dr, lane_off) not in cache:
            row = r_ref[0, 0, dr:dr + 1, :]
            if lane_off:
                row = pltpu.roll(row, lane_off, 1)
            tz = pltpu.roll(jnp.broadcast_to(row, (w, LANES)), shift, 1, stride=1, stride_axis=0)
            cache[(dr, lane_off)] = jnp.where(colvalid, tz, neg)
        return cache[(dr, lane_off)]

    for v, per_q in enumerate(geo):
        for qr, per_k in enumerate(per_q):
            for pair in range(len(per_k) // 2):
                (ok0, dr0), (ok1, dr1) = per_k[2 * pair], per_k[2 * pair + 1]
                lo = toeplitz(dr0, 0) if ok0 else neg
                hi = toeplitz(dr1, w) if ok1 else neg
                t_ref[0, v, 0, qr * w:(qr + 1) * w, pair * LANES:(pair + 1) * LANES] = jnp.where(left, lo, hi)


def _na_tables(rpb_all, rows):
    depth, heads, nr, nc = rpb_all.shape
    rp = jnp.pad(rpb_all.astype(F32), ((0, 0), (0, 0), (0, 16 - nr), (0, LANES - nc)))
    tq, tk = NA_GROUP_ROWS * GRID_W, NA_KEY_ROWS * GRID_W
    return pl.pallas_call(
        functools.partial(_na_table_kernel, geo=_na_geometry(rows)),
        grid=(depth, heads),
        in_specs=[pl.BlockSpec((1, 1, 16, LANES), lambda l, h: (l, h, 0, 0))],
        out_specs=pl.BlockSpec((1, 3, 1, tq, tk), lambda l, h: (l, 0, h, 0, 0)),
        out_shape=jax.ShapeDtypeStruct((depth, 3, heads, tq, tk), F32),
        compiler_params=_cparams(("parallel", "parallel")),
        name="natten_tables",
    )(rp)


def _natten_kernel(q_ref, k_ref, v_ref, *rest, heads, n_g, rows):
    t_refs, o_ref = rest[:-1], rest[-1]
    tq = NA_GROUP_ROWS * GRID_W
    tk = t_refs[0].shape[4]
    per_tile = LANES // NA_HEAD_DIM
    lane = lax.broadcasted_iota(I32, (1, LANES), 1)
    ones = jnp.ones((tk, LANES), BF16)
    for sub, t_ref in enumerate(t_refs):
        g = pl.program_id(1) * len(t_refs) + sub
        qrows = slice(sub * tq, (sub + 1) * tq)
        ks = jnp.clip(NA_GROUP_ROWS * g - NA_KH_MAX // 2, 0, rows - NA_KEY_ROWS)
        kstart = pl.multiple_of(ks * GRID_W, GRID_W)
        for j in range(heads // per_tile):
            lanes = slice(j * LANES, (j + 1) * LANES)
            q2 = q_ref[0, qrows, lanes].astype(F32) * (NA_HEAD_DIM ** -0.5)
            k2 = k_ref[0, pl.ds(kstart, tk), lanes]
            vaug = jnp.concatenate([v_ref[0, pl.ds(kstart, tk), lanes], ones], axis=1)
            o2 = None
            for hh in range(per_tile):
                own = (lane >= hh * NA_HEAD_DIM) & (lane < (hh + 1) * NA_HEAD_DIM)
                qm = jnp.where(own, q2, 0.0).astype(BF16)
                s = lax.dot_general(qm, k2, (((1,), (1,)), ((), ())), preferred_element_type=F32)
                s = s + t_ref[0, 0, j * per_tile + hh]
                m = jnp.max(s, axis=-1, keepdims=True)
                p = jnp.exp((s - m).astype(BF16))
                r = jnp.dot(p, vaug, preferred_element_type=F32)
                o = r[:, :LANES] / r[:, LANES:]
                o2 = o if o2 is None else jnp.where(own, o, o2)
            o_ref[0, qrows, lanes] = o2.astype(o_ref.dtype)


def _natten(qkv, tables, layer, b, seq, naw):
    heads = naw // NA_HEAD_DIM
    rows = seq // GRID_W
    n_g = rows // NA_GROUP_ROWS
    tq = NA_GROUP_ROWS * GRID_W
    tk = NA_KEY_ROWS * GRID_W

    per = NA_GROUPS_PER_STEP if n_g % NA_GROUPS_PER_STEP == 0 else 1

    def tmap(sub):
        def index(i, s):
            g = s * per + sub
            return (layer, jnp.where(g == 0, 0, jnp.where(g == n_g - 1, 2, 1)), 0, 0, 0)
        return index

    return pl.pallas_call(
        functools.partial(_natten_kernel, heads=heads, n_g=n_g, rows=rows),
        grid=(b, n_g // per),
        in_specs=[pl.BlockSpec((1, per * tq, naw), lambda i, s: (i, s, 0)),
                  pl.BlockSpec((1, seq, naw), lambda i, s: (i, 0, 1)),
                  pl.BlockSpec((1, seq, naw), lambda i, s: (i, 0, 2))]
                 + [pl.BlockSpec((1, 1, heads, tq, tk), tmap(sub)) for sub in range(per)],
        out_specs=pl.BlockSpec((1, per * tq, naw), lambda i, s: (i, s, 0)),
        out_shape=jax.ShapeDtypeStruct((b, seq, naw), F32),
        compiler_params=_cparams(("parallel", "arbitrary")),
        name="natten",
    )(qkv, qkv, qkv, *([tables] * per))


def _pool_kernel(u_ref, w_ref, sc_ref, o_ref, pad_ref, *, pw):
    seq = u_ref.shape[1]
    lp = seq + 2 * POOL_PAD
    j = pl.program_id(1)
    u = u_ref[0]
    zeros = jnp.zeros((POOL_PAD, LANES), F32)
    pad_ref[0:POOL_PAD, :] = zeros
    pad_ref[POOL_PAD + seq:lp, :] = zeros
    pad_ref[POOL_PAD:POOL_PAD + seq, :] = u
    xp = pad_ref[...]
    dn = lambda a, k: pltpu.roll(a, k, 0)
    up = lambda a, k: pltpu.roll(a, lp - k, 0)
    s2 = xp + dn(xp, 1)
    s4 = dn(s2, 1) + up(s2, 1)
    s8 = dn(s4, 2) + up(s4, 2)
    s16 = dn(s8, 4) + up(s8, 4)
    lane = lax.broadcasted_iota(I32, (1, LANES), 1) + j * LANES
    gdim = pw // len(POOL_WINDOWS)
    grp = lane // gdim
    sums = (s2, s4, s8, s16)
    wsum = sums[0][POOL_PAD:POOL_PAD + seq]
    half = jnp.full((1, LANES), POOL_WINDOWS[0] // 2, F32)
    for gi in range(1, len(POOL_WINDOWS)):
        wsum = jnp.where(grp == gi, sums[gi][POOL_PAD:POOL_PAD + seq], wsum)
        half = jnp.where(grp == gi, float(POOL_WINDOWS[gi] // 2), half)
    t = lax.broadcasted_iota(I32, (seq, LANES), 0).astype(F32)
    cnt = jnp.minimum(t + half, float(seq)) - jnp.maximum(t - half, 0.0)
    pooled = wsum / cnt - u
    y = jnp.dot(pooled.astype(BF16), w_ref[0], preferred_element_type=F32)
    o_ref[0] = y * sc_ref[...]


def _pool(u, pool_w, pool_scale):
    b, seq, pw = u.shape
    ng, gd, _ = pool_w.shape
    nh = pw // LANES
    per = LANES // gd
    wbd = jnp.zeros((nh, LANES, LANES), F32)
    for gi in range(ng):
        hh, k = divmod(gi, per)
        wbd = wbd.at[hh, k * gd:(k + 1) * gd, k * gd:(k + 1) * gd].set(pool_w[gi])
    return pl.pallas_call(
        functools.partial(_pool_kernel, pw=pw),
        grid=(b, nh),
        in_specs=[pl.BlockSpec((1, seq, LANES), lambda i, j: (i, 0, j)),
                  pl.BlockSpec((1, LANES, LANES), lambda i, j: (j, 0, 0)),
                  pl.BlockSpec((1, LANES), lambda i, j: (0, j))],
        out_specs=pl.BlockSpec((1, seq, LANES), lambda i, j: (i, 0, j)),
        out_shape=jax.ShapeDtypeStruct((b, seq, pw), F32),
        scratch_shapes=[pltpu.VMEM((seq + 2 * POOL_PAD, LANES), F32)],
        compiler_params=_cparams(("parallel", "parallel")),
        name="pool_mixer",
    )(u, wbd.astype(BF16), pool_scale[None])


def _outproj_kernel(x_ref, yh_ref, yn_ref, yp_ref, gm_ref, w_ref, g2_ref, wr_ref,
                    xo_ref, h_ref, aff_ref, *, hy, naw):
    gm = gm_ref[...]
    m1 = _rms(yh_ref[...], gm[:, :hy]).astype(BF16)
    m2 = _rms(yn_ref[...], gm[:, hy:hy + naw]).astype(BF16)
    m3 = _rms(yp_ref[...], gm[:, hy + naw:]).astype(BF16)
    acc = jnp.dot(m1, w_ref[:hy, :], preferred_element_type=F32)
    acc += jnp.dot(m2, w_ref[hy:hy + naw, :], preferred_element_type=F32)
    acc += jnp.dot(m3, w_ref[hy + naw:, :], preferred_element_type=F32)
    xn = x_ref[...] + acc
    xo_ref[...] = xn
    h = _rms(xn, g2_ref[...])
    h_ref[...] = h.reshape(h_ref.shape)
    logits = _dot_x3(wr_ref[0], wr_ref[1], h, (((1,), (1,)), ((), ())))
    mx = jnp.max(logits, axis=0, keepdims=True)
    ex = jnp.exp(logits - mx)
    aff_ref[...] = ex / jnp.sum(ex, axis=0, keepdims=True)


def _outproj(x2, yh, yn, yp, gm, w_bf, g2, wr_t, tm=1024):
    n, d = x2.shape
    hy, naw, pw = yh.shape[1], yn.shape[1], yp.shape[1]
    e = wr_t.shape[1]
    row = lambda c: pl.BlockSpec((tm, c), lambda i: (i, 0))
    full = lambda s: pl.BlockSpec(s, lambda i: (0, 0))
    return pl.pallas_call(
        functools.partial(_outproj_kernel, hy=hy, naw=naw),
        grid=(n // tm,),
        in_specs=[row(d), row(hy), row(naw), row(pw), full((1, d)), full((d, d)), full((1, d)),
                  pl.BlockSpec((2, e, d), lambda i: (0, 0, 0))],
        out_specs=[row(d), pl.BlockSpec((tm, d // LANES, LANES), lambda i: (i, 0, 0)),
                   pl.BlockSpec((e, tm), lambda i: (0, i))],
        out_shape=[jax.ShapeDtypeStruct((n, d), F32), jax.ShapeDtypeStruct((n, d // LANES, LANES), F32),
                   jax.ShapeDtypeStruct((e, n), F32)],
        compiler_params=_cparams(("parallel",)),
        name="outproj_router",
    )(x2, yh, yn, yp, gm, w_bf, g2, wr_t)


def _block_cumsum(x, tri):
    r, n = x.shape
    cls, offs = [], []
    off = jnp.zeros((r, 1), F32)
    for j in range(n // LANES):
        c = jnp.dot(x[:, j * LANES:(j + 1) * LANES], tri, preferred_element_type=F32)
        cls.append(c)
        off = off + c[:, LANES - 1:LANES]
        offs.append(off)
    return cls, offs


def _route_kernel(aff_ref, tri_ref, bci_ref, bcx_ref, idx_ref, gate_ref, blk_ref, *, cap):
    aff = aff_ref[...]
    e, seq = aff.shape
    nblk = seq // LANES
    bits = pltpu.bitcast(aff, I32)
    capf = jnp.float32(cap)

    def radix(i, prefix):
        cand = prefix | jnp.left_shift(jnp.int32(1), 30 - i)
        cnt = jnp.sum((bits >= cand).astype(F32), axis=1, keepdims=True)
        return jnp.where(cnt >= capf, cand, prefix)

    tau = lax.fori_loop(0, 31, radix, jnp.zeros((e, 1), I32))
    gt = bits > tau
    eq = bits == tau
    need = capf - jnp.sum(gt.astype(F32), axis=1, keepdims=True)
    tri = tri_ref[...]
    cls, offs = _block_cumsum(jnp.where(eq, 1.0, 0.0).astype(BF16), tri)
    tie_rank = jnp.concatenate([c if j == 0 else c + offs[j - 1] for j, c in enumerate(cls)], axis=1)
    sel = gt | (eq & (tie_rank <= need))
    self32 = jnp.where(sel, 1.0, 0.0)
    selb = self32.astype(BF16)
    cls, _ = _block_cumsum(selb, tri)
    bend = jnp.dot(selb, bci_ref[...], preferred_element_type=F32)
    bstart = jnp.dot(selb, bcx_ref[...], preferred_element_type=F32)
    for j in range(nblk):
        rows = slice(j * e, (j + 1) * e)
        lanes = slice(j * LANES, (j + 1) * LANES)
        blk_ref[0, rows, :] = cls[j]
        blk_ref[1, rows, :] = self32[:, lanes]
        blk_ref[2, rows, :] = aff[:, lanes]
    slot = lax.broadcasted_iota(I32, (cap, 1), 0).astype(F32)
    lane = lax.broadcasted_iota(I32, (1, LANES), 1).astype(F32)
    for ei in range(e):
        bs, be = bstart[ei:ei + 1, :], bend[ei:ei + 1, :]
        inblk = (bs <= slot) & (slot < be)
        local = slot - jnp.sum(jnp.where(inblk, bs, 0.0), axis=1, keepdims=True)
        jcol = jnp.sum(jnp.where(inblk, lane, 0.0), axis=1, keepdims=True)
        pick = jnp.where(inblk, 1.0, 0.0)[:, :nblk].astype(BF16)
        rows = pl.ds(ei, nblk, stride=e)
        a = blk_ref[2, rows, :]
        a_hi = a.astype(BF16)
        r1 = a - a_hi.astype(F32)
        a_mid = r1.astype(BF16)
        a_lo = (r1 - a_mid.astype(F32)).astype(BF16)
        take = lambda v: jnp.dot(pick, v, preferred_element_type=F32)
        g_cl = take(blk_ref[0, rows, :].astype(BF16))
        g_sel = take(blk_ref[1, rows, :].astype(BF16))
        g_aff = take(a_hi) + take(a_mid) + take(a_lo)
        hit = (g_cl == local + 1.0) & (g_sel > 0.5)
        idx = jcol * LANES + jnp.sum(jnp.where(hit, lane, 0.0), axis=1, keepdims=True)
        gate = jnp.sum(jnp.where(hit, g_aff, 0.0), axis=1, keepdims=True)
        idx_ref[0, :, ei:ei + 1] = idx.astype(I32)
        gate_ref[0, ei] = jnp.broadcast_to(gate, (cap, LANES))


def _route(aff_t, b, seq, cap):
    e = aff_t.shape[0]
    nblk = seq // LANES
    tri = jnp.asarray(np.triu(np.ones((LANES, LANES), np.float32)), BF16)
    tblk = np.arange(seq)[:, None] // LANES
    bci = jnp.asarray(tblk <= np.arange(LANES)[None, :], BF16)
    bcx = jnp.asarray(tblk < np.arange(LANES)[None, :], BF16)
    full = lambda s: pl.BlockSpec(s, lambda i: (0, 0))
    idx, gate = pl.pallas_call(
        functools.partial(_route_kernel, cap=cap),
        grid=(b,),
        in_specs=[pl.BlockSpec((e, seq), lambda i: (0, i)), full((LANES, LANES)),
                  full((seq, LANES)), full((seq, LANES))],
        out_specs=[pl.BlockSpec((1, cap, e), lambda i: (i, 0, 0)),
                   pl.BlockSpec((1, e, cap, LANES), lambda i: (i, 0, 0, 0))],
        out_shape=[jax.ShapeDtypeStruct((b, cap, e), I32), jax.ShapeDtypeStruct((b, e, cap, LANES), F32)],
        scratch_shapes=[pltpu.VMEM((3, nblk * e, LANES), F32)],
        compiler_params=_cparams(("parallel",)),
        name="ec_route",
    )(aff_t, tri, bci, bcx)
    return idx.transpose(0, 2, 1), gate


def _expert_kernel(rows_ref, h_hbm, wg_ref, wu_ref, wd_ref, y_ref, xbuf, xb, acc, sem, *, tm, nm, nf, ne):
    e = pl.program_id(0)
    m = pl.program_id(1)
    f = pl.program_id(2)
    tile = e * nm + m
    last = ne * nm - 1
    chunk = tm // nf
    d = xb.shape[2]
    rb = min(EXPERT_RELAYOUT_ROWS, chunk)

    def row_copy(src_row, p, c, dst_row):
        return pltpu.make_async_copy(h_hbm.at[pl.ds(src_row, 1)], xbuf.at[p, c, pl.ds(dst_row, 1)], sem.at[p, c])

    def chunk_wait(t, c):
        pltpu.make_async_copy(h_hbm.at[pl.ds(0, chunk)], xbuf.at[t % 2, c], sem.at[t % 2, c]).wait()

    def chunk_convert(t, c):
        for j in range(chunk // rb):
            rows = xbuf[t % 2, c, j * rb:(j + 1) * rb]
            xb[t % 2, pl.ds(c * chunk + j * rb, rb), :] = rows.reshape(rb, d).astype(BF16)

    @pl.when((tile == 0) & (f == 0))
    def _():
        nxt0 = min(1, last)
        for t, c in [(0, k) for k in range(nf)] + [(nxt0, 0)]:
            def issue(i, carry, t=t, c=c):
                row_copy(rows_ref[t * tm + c * chunk + i], t % 2, c, i).start()
                return carry

            lax.fori_loop(0, chunk, issue, 0, unroll=8)
        for k in range(nf):
            chunk_wait(0, k)
            chunk_convert(0, k)

    @pl.when(f == 0)
    def _():
        acc[...] = jnp.zeros(acc.shape, F32)

    nxt = jnp.minimum(tile + 1, last)
    wrap = f + 1 == nf
    t_issue = jnp.where(wrap, jnp.minimum(tile + 2, last), nxt)
    c_issue = jnp.where(wrap, 0, f + 1)
    base = t_issue * tm + c_issue * chunk
    for i in range(chunk):
        row_copy(rows_ref[base + i], t_issue % 2, c_issue, i).start()

    x = xb[tile % 2]
    a = jnp.dot(x, wg_ref[0, 0].astype(BF16), preferred_element_type=F32)
    u = jnp.dot(x, wu_ref[0, 0].astype(BF16), preferred_element_type=F32)
    hh = (a * jax.nn.sigmoid(a) * u).astype(BF16)
    acc[...] += jnp.dot(hh, wd_ref[0, 0].astype(BF16), preferred_element_type=F32)

    chunk_wait(nxt, f)
    chunk_convert(nxt, f)

    @pl.when(f == nf - 1)
    def _():
        y_ref[0] = acc[...].astype(y_ref.dtype)

    @pl.when((tile == last) & (f == nf - 1))
    def _():
        chunk_wait(last, 0)


EXPERT_RELAYOUT_ROWS = 64


def _experts(rows_flat, h3, w_gate, w_up, w_down, layer, mtot, tm=1024, tf=512):
    _, e, d, ff = w_gate.shape
    tm = min(tm, mtot)
    tf = min(tf, ff)
    nm, nf = mtot // tm, ff // tf
    grid_spec = pltpu.PrefetchScalarGridSpec(
        num_scalar_prefetch=1,
        grid=(e, nm, nf),
        in_specs=[pl.BlockSpec(memory_space=pl.ANY),
                  pl.BlockSpec((1, 1, d, tf), lambda i, m, f, r: (layer, i, 0, f)),
                  pl.BlockSpec((1, 1, d, tf), lambda i, m, f, r: (layer, i, 0, f)),
                  pl.BlockSpec((1, 1, tf, d), lambda i, m, f, r: (layer, i, f, 0))],
        out_specs=pl.BlockSpec((1, tm, d), lambda i, m, f, r: (i, m, 0)),
        scratch_shapes=[pltpu.VMEM((2, nf, tm // nf, d // LANES, LANES), F32), pltpu.VMEM((2, tm, d), BF16),
                        pltpu.VMEM((tm, d), F32),
                        pltpu.SemaphoreType.DMA((2, nf))],
    )
    return pl.pallas_call(
        functools.partial(_expert_kernel, tm=tm, nm=nm, nf=nf, ne=e),
        grid_spec=grid_spec,
        out_shape=jax.ShapeDtypeStruct((e, mtot, d), BF16),
        compiler_params=_cparams(("arbitrary", "arbitrary", "arbitrary")),
        name="ec_experts",
    )(rows_flat, h3, w_gate, w_up, w_down)


COMBINE_ROWS = 64
COMBINE_UNROLL = 8
COMBINE_EXPERTS = 4


def _combine_kernel(idx_ref, split_ref, x_ref, y_ref, gl, *rest, ne, cap, span, final):
    if final:
        g_ref, o_ref, acc3, y3 = rest
    else:
        o_ref, acc3, y3 = rest
    b = pl.program_id(0)
    sp = pl.program_id(1)
    eg = pl.program_id(2)
    d = x_ref.shape[2]
    sub = d // LANES
    rb = COMBINE_ROWS
    nu = COMBINE_UNROLL

    @pl.when(eg == 0)
    def _():
        def load(c, carry):
            r0 = pl.multiple_of(c * rb, rb)
            acc3[pl.ds(pl.multiple_of(r0 * sub, rb * sub), rb * sub), :] = (
                x_ref[0, pl.ds(r0, rb), :].reshape(rb * sub, LANES))
            return carry

        lax.fori_loop(0, span // rb, load, 0)

    def one_expert(k):
        e = eg * COMBINE_EXPERTS + k
        lo = split_ref[(b * ne + e) * 3 + sp]
        hi = split_ref[(b * ne + e) * 3 + sp + 1]

        def relayout(c, carry):
            r0 = pl.multiple_of(c * rb, rb)
            y3[pl.ds(r0, rb)] = y_ref[k, pl.ds(r0, rb), :].astype(F32).reshape(rb, sub, LANES)
            return carry

        lax.fori_loop(lo // rb, (hi + rb - 1) // rb, relayout, 0)
        base = (b * ne + e) * cap

        def tokens(first, count):
            return tuple(idx_ref[base + first + u] for u in range(count))

        def add_rows(first, toks):
            rows = [pl.ds(pl.multiple_of(t, sub), sub) for t in toks]
            vals = [acc3[r, :] + gl[0, k, pl.ds(first + u, 1), :] * y3[first + u] for u, r in enumerate(rows)]
            for r, v in zip(rows, vals):
                acc3[r, :] = v

        def group(g, toks):
            first = lo + g * nu
            nxt = tokens(jnp.minimum(first + nu, cap - nu), nu)
            add_rows(first, toks)
            return nxt

        ngroups = (hi - lo) // nu
        lax.fori_loop(0, ngroups, group, tokens(jnp.minimum(lo, cap - nu), nu))

        def tail(i, carry):
            add_rows(i, tokens(i, 1))
            return carry

        lax.fori_loop(lo + ngroups * nu, hi, tail, 0)

    for k in range(COMBINE_EXPERTS):
        one_expert(k)

    @pl.when(eg == ne // COMBINE_EXPERTS - 1)
    def _():
        def store(c, carry):
            r0 = pl.multiple_of(c * rb, rb)
            v = acc3[pl.ds(pl.multiple_of(r0 * sub, rb * sub), rb * sub), :].reshape(rb, d)
            if final:
                v = _rms(v, g_ref[...])
            o_ref[0, pl.ds(r0, rb), :] = v
            return carry

        lax.fori_loop(0, span // rb, store, 0)


def _combine(idx, gate, x3, y, cap, final_g=None):
    b, seq, d = x3.shape
    ne = y.shape[0]
    span = seq // 2
    n_lower = jnp.sum((idx < span).astype(I32), axis=-1)
    split_flat = jnp.stack([jnp.zeros_like(n_lower), n_lower, jnp.full_like(n_lower, cap)], axis=-1).reshape(-1)
    final = final_g is not None
    ge = COMBINE_EXPERTS
    in_specs = [pl.BlockSpec((1, span, d), lambda i, s, e, *_: (i, s, 0)),
                pl.BlockSpec((ge, cap, d), lambda i, s, e, *_: (e, i, 0)),
                pl.BlockSpec((1, ge, cap, LANES), lambda i, s, e, *_: (i, e, 0, 0))]
    args = [x3, y, gate]
    if final:
        in_specs.append(pl.BlockSpec((1, d), lambda i, s, e, *_: (0, 0)))
        args.append(final_g)
    grid_spec = pltpu.PrefetchScalarGridSpec(
        num_scalar_prefetch=2,
        grid=(b, 2, ne // ge),
        in_specs=in_specs,
        out_specs=pl.BlockSpec((1, span, d), lambda i, s, e, *_: (i, s, 0)),
        scratch_shapes=[pltpu.VMEM((span * (d // LANES), LANES), F32), pltpu.VMEM((cap, d // LANES, LANES), F32)],
    )
    return pl.pallas_call(
        functools.partial(_combine_kernel, ne=ne, cap=cap, span=span, final=final),
        grid_spec=grid_spec,
        out_shape=jax.ShapeDtypeStruct((b, seq, d), F32),
        compiler_params=_cparams(("arbitrary", "arbitrary", "arbitrary")),
        name="ec_combine",
    )(((idx % span) * (d // LANES)).reshape(-1), split_flat, *args)


def _moe(x3, h2, aff_t, w_gate, w_up, w_down, layer, final_g=None):
    b, seq, d = x3.shape
    ne = w_gate.shape[1]
    cap = EC_CAPACITY * seq // ne
    idx, gate = _route(aff_t, b, seq, cap)
    rows = idx + (jnp.arange(b, dtype=I32) * seq)[:, None, None]
    rows_flat = rows.transpose(1, 0, 2).reshape(-1)
    y = _experts(rows_flat, h2, w_gate, w_up, w_down, layer, b * cap)
    return _combine(idx, gate, x3, y, cap, final_g)


def kernel(x, norm1_g, w_in, hy_short_w, hy_short_b, hy_f_w1, hy_f_b1, hy_f_w2, hy_f_b2, hy_f_wout, hy_f_freq, hy_skip, na_rpb, pool_w, pool_scale, mix_norm_g, w_out, norm2_g, w_router, w_gate, w_up, w_down, final_g):
    b, seq, d = x.shape
    depth = w_in.shape[0]
    hy = hy_skip.shape[1]
    pw = pool_scale.shape[1]
    naw = d - hy - pw
    n = b * seq
    rows = seq // GRID_W
    tabs = _dft_tables(seq // HY_B1)
    x2 = x.reshape(n, d)
    hhat = _hyena_filter_spectra(seq, hy, hy_f_w1, hy_f_b1, hy_f_w2, hy_f_b2, hy_f_wout, hy_f_freq, tabs)
    na_tables = _na_tables(na_rpb, rows)
    for i in range(depth):
        z, x0, qkv, pool_in = _inproj(x2, norm1_g[i][None], w_in[i].astype(BF16), hy_short_w[i],
                                      hy_short_b[i][None], 3 * hy, 3 * naw, seq)
        y_hy = _hyena(z.reshape(b, seq, hy), x0.reshape(b, seq, hy), hy_skip[i], hhat, i, tabs)
        y_na = _natten(qkv.reshape(b, seq, 3 * naw), na_tables, i, b, seq, naw)
        y_pool = _pool(pool_in.reshape(b, seq, pw), pool_w[i], pool_scale[i])
        x2, h2, aff_t = _outproj(x2, y_hy.reshape(n, hy), y_na.reshape(n, naw), y_pool.reshape(n, pw),
                                 mix_norm_g[i][None], w_out[i].astype(BF16), norm2_g[i][None],
                                 jnp.stack(_split_bf16(w_router[i].T)))
        last = final_g[None] if i == depth - 1 else None
        x2 = _moe(x2.reshape(b, seq, d), h2, aff_t, w_gate, w_up, w_down, i, last).reshape(n, d)
    return x2.reshape(b, seq, d)
```
